```python
import jax
import jax.numpy as jnp
from jax import lax
import numpy as np

D_MODEL = 2048
BATCH = 2
SEQ = 4096
DEPTH = 4

GRID_W = 64
CTX_LEN = 256
N_BRANCH = 4
BRANCH_W = D_MODEL // 4
NORM_EPS = 1e-6

NA_HEADS = 8
NA_HEAD_DIM = BRANCH_W // NA_HEADS
NA_WIN_R = 8
NA_WIN_C = 16
NA_QBLOCK_C = 16
NA_KBLOCK_C = NA_QBLOCK_C + NA_WIN_C

RW_HEAD_DIM = 64
RW_HEADS = BRANCH_W // RW_HEAD_DIM
RW_DECAY_LORA = 64
RW_ICL_LORA = 64
RW_GATE_LORA = 128
RW_GN_EPS = 64e-5

MLA_HEADS = 8
MLA_Q_RANK = 512
MLA_KV_RANK = 256
MLA_NOPE_DIM = 64
MLA_ROPE_DIM = 32
MLA_V_DIM = BRANCH_W // MLA_HEADS
MLA_QK_DIM = MLA_NOPE_DIM + MLA_ROPE_DIM
MLA_QBLOCK = 128
ROPE_THETA = 10000.0

S5_GROUP = 16
S5_GROUPS = BRANCH_W // S5_GROUP
S5_STATE = 64

N_EXPERTS = 32
TOP_K = 4
D_EXPERT = 512
SWIGLU_LIMIT = 7.0
SWIGLU_ALPHA = 1.702
MOE_BLOCK = 128

NA_COLS = 3 * BRANCH_W
RW_COLS = 3 * BRANCH_W + 2 * RW_DECAY_LORA + 2 * RW_ICL_LORA + RW_GATE_LORA
MLA_COLS = MLA_Q_RANK + MLA_KV_RANK + MLA_ROPE_DIM
S5_COLS = BRANCH_W
IN_COLS = NA_COLS + RW_COLS + MLA_COLS + S5_COLS

kernel_name = 'hybrid_diffusion_trunk'


def rms_norm(x, g):
    xf = x.astype(jnp.float32)
    y = xf * lax.rsqrt(jnp.mean(xf * xf, axis=-1, keepdims=True) + NORM_EPS)
    return y.astype(x.dtype) * g


def softmax_f32(s):
    return jax.nn.softmax(s.astype(jnp.float32), axis=-1)


def to_heads(t, n_heads, head_dim):
    return t.reshape(t.shape[:-1] + (n_heads, head_dim))


def split_columns(t):
    o1 = NA_COLS
    o2 = o1 + RW_COLS
    o3 = o2 + MLA_COLS
    return t[..., :o1], t[..., o1:o2], t[..., o2:o3], t[..., o3:]


def context_attention(q, k, v, scale):
    s = jnp.einsum('bqhd,bkhd->bhqk', q, k).astype(jnp.float32) * scale
    o = jnp.einsum('bhqk,bkhd->bqhd', softmax_f32(s).astype(v.dtype), v)
    return o.reshape(o.shape[:2] + (-1,))


def natten_geometry(rows, rpb):
    win_r = min(NA_WIN_R, rows)
    n_cb = GRID_W // NA_QBLOCK_C
    r = jnp.arange(rows)
    row_start = jnp.clip(r - win_r // 2, 0, rows - win_r)
    key_row = row_start[:, None] + jnp.arange(win_r)[None, :]
    cb = jnp.arange(n_cb)
    key_col0 = jnp.clip(cb * NA_QBLOCK_C - NA_WIN_C // 2, 0, GRID_W - NA_KBLOCK_C)
    key_col = key_col0[:, None] + jnp.arange(NA_KBLOCK_C)[None, :]
    q_col = cb[:, None] * NA_QBLOCK_C + jnp.arange(NA_QBLOCK_C)[None, :]
    col_start = jnp.clip(q_col - NA_WIN_C // 2, 0, GRID_W - NA_WIN_C)
    rel_c = key_col[:, None, :] - col_start[:, :, None]
    in_win = (rel_c >= 0) & (rel_c < NA_WIN_C)
    d_row = key_row - r[:, None] + NA_WIN_R - 1
    d_col = jnp.clip(key_col[:, None, :] - q_col[:, :, None] + NA_WIN_C - 1, 0, 2 * NA_WIN_C - 2)
    bias = rpb[:, d_row[:, None, None, :, None], d_col[None, :, :, None, :]]
    bias = jnp.where(in_win[None, None, :, :, None, :], bias.astype(jnp.float32), -jnp.inf)
    bias = bias.reshape(bias.shape[:4] + (win_r * NA_KBLOCK_C,))
    key_idx = key_row[:, None, :, None] * GRID_W + key_col[None, :, None, :]
    return key_idx.reshape(rows, n_cb, win_r * NA_KBLOCK_C), bias


def neighbourhood_attention(q, k, v, qc, kc, vc, qn_g, kn_g, rpb, need_ctx):
    bsz, seq = q.shape[:2]
    rows = seq // GRID_W
    n_cb = GRID_W // NA_QBLOCK_C
    scale = NA_HEAD_DIM ** -0.5
    q, k = rms_norm(q, qn_g), rms_norm(k, kn_g)
    qc, kc = rms_norm(qc, qn_g), rms_norm(kc, kn_g)
    key_idx, bias = natten_geometry(rows, rpb)
    k_band, v_band = k[:, key_idx], v[:, key_idx]
    q_blk = q.reshape(bsz, rows, n_cb, NA_QBLOCK_C, NA_HEADS, NA_HEAD_DIM)
    s_loc = jnp.einsum('brnqhd,brnkhd->bhrnqk', q_blk, k_band).astype(jnp.float32) * scale + bias[None]
    s_ctx = jnp.einsum('brnqhd,bkhd->bhrnqk', q_blk, kc).astype(jnp.float32) * scale
    p = softmax_f32(jnp.concatenate([s_loc, s_ctx], axis=-1)).astype(v.dtype)
    kw = key_idx.shape[-1]
    o = (jnp.einsum('bhrnqk,brnkhd->brnqhd', p[..., :kw], v_band)
         + jnp.einsum('bhrnqk,bkhd->brnqhd', p[..., kw:], vc))
    o_ctx = context_attention(qc, kc, vc, scale) if need_ctx else None
    return o.reshape(bsz, seq, BRANCH_W), o_ctx


def token_shift_mix(f, mu):
    prev = jnp.pad(f[:, :-1], ((0, 0), (1, 0), (0, 0)))
    nxt = jnp.pad(f[:, 1:], ((0, 0), (0, 1), (0, 0)))
    return f + mu[0] * (prev - f) + mu[1] * (nxt - f)


def rwkv_features(f, mu, w0, w2, a0, a2, g2, k_k, k_a):
    f = token_shift_mix(f, mu).astype(jnp.float32)
    lead = f.shape[:2]
    n = BRANCH_W
    o_w = 3 * n
    o_a = o_w + 2 * RW_DECAY_LORA
    o_g = o_a + 2 * RW_ICL_LORA
    r, k, v = f[..., :n], f[..., n:2 * n], f[..., 2 * n:o_w]
    w_low = f[..., o_w:o_a].reshape(lead + (2, RW_DECAY_LORA))
    a_low = f[..., o_a:o_g].reshape(lead + (2, RW_ICL_LORA))
    g_low = f[..., o_g:]
    kk = to_heads(k * k_k, RW_HEADS, RW_HEAD_DIM)
    kk = kk / jnp.maximum(jnp.sqrt(jnp.sum(kk * kk, axis=-1, keepdims=True)), 1e-12)
    log_w = -jax.nn.softplus(-(w0 + jnp.einsum('bldr,drn->bldn', jnp.tanh(w_low), w2))) - 0.5
    decay = jnp.exp(-jnp.exp(log_w))
    a = jax.nn.sigmoid(a0 + jnp.einsum('bldr,drn->bldn', a_low, a2))
    k_dir = k[:, :, None, :] * (1.0 + (a - 1.0) * k_a)
    g = jax.nn.sigmoid(g_low) @ g2
    return (to_heads(r, RW_HEADS, RW_HEAD_DIM), to_heads(v, RW_HEADS, RW_HEAD_DIM), kk,
            to_heads(decay, RW_HEADS, RW_HEAD_DIM), to_heads(k_dir, RW_HEADS, RW_HEAD_DIM),
            to_heads(a, RW_HEADS, RW_HEAD_DIM), g)


def rwkv_scan(state0, r, w, k, v, kk, a, reverse):
    def step(S, inp):
        r_t, w_t, k_t, v_t, kk_t, a_t = inp
        s_kk = jnp.einsum('bhvk,bhk->bhv', S, kk_t)
        S = (S * w_t[:, :, None, :] - s_kk[..., None] * (kk_t * a_t)[:, :, None, :]
             + v_t[..., None] * k_t[:, :, None, :])
        return S, jnp.einsum('bhvk,bhk->bhv', S, r_t)
    xs = tuple(jnp.moveaxis(t, 1, 0) for t in (r, w, k, v, kk, a))
    state, y = lax.scan(step, state0, xs, reverse=reverse)
    return state, jnp.moveaxis(y, 0, 1)


def rwkv_readout(y, r, k_sum, v, r_k, g, gn_g, gn_b):
    mean = jnp.mean(y, axis=-1, keepdims=True)
    var = jnp.mean(jnp.square(y - mean), axis=-1, keepdims=True)
    yn = ((y - mean) * lax.rsqrt(var + RW_GN_EPS)).reshape(y.shape[:2] + (BRANCH_W,))
    bonus = (jnp.sum(r * k_sum * r_k, axis=-1, keepdims=True) * v).reshape(y.shape[:2] + (BRANCH_W,))
    return (yn * gn_g + gn_b + bonus) * g


def rwkv_time_mix(f, fc, mu, w0, w2, a0, a2, g2, k_k, k_a, r_k, gn_g, gn_b, need_ctx):
    rx, vx, kkx, wx, kx, ax, gx = rwkv_features(f, mu, w0, w2, a0, a2, g2, k_k, k_a)
    rc, vc, kkc, wc, kc, ac, gc = rwkv_features(fc, mu, w0, w2, a0, a2, g2, k_k, k_a)
    state0 = jnp.zeros((f.shape[0], RW_HEADS, RW_HEAD_DIM, RW_HEAD_DIM), jnp.float32)
    ys_x, ys_c = [], []
    for d, reverse in enumerate((False, True)):
        state_ctx, y_c = rwkv_scan(state0, rc, wc[:, :, d], kc[:, :, d], vc, kkc, ac[:, :, d], reverse)
        _, y_x = rwkv_scan(state_ctx, rx, wx[:, :, d], kx[:, :, d], vx, kkx, ax[:, :, d], reverse)
        ys_x.append(y_x)
        ys_c.append(y_c)
    out_x = rwkv_readout(ys_x[0] + ys_x[1], rx, kx[:, :, 0] + kx[:, :, 1], vx, r_k, gx, gn_g, gn_b).astype(f.dtype)
    out_c = None
    if need_ctx:
        out_c = rwkv_readout(ys_c[0] + ys_c[1], rc, kc[:, :, 0] + kc[:, :, 1], vc, r_k, gc, gn_g, gn_b).astype(fc.dtype)
    return out_x, out_c


def rope_2d_tables(seq):
    n_freq = MLA_ROPE_DIM // 4
    inv_freq = ROPE_THETA ** (-jnp.arange(n_freq, dtype=jnp.float32) / n_freq)
    t = jnp.arange(seq)
    row = (t // GRID_W).astype(jnp.float32)[:, None] * inv_freq
    col = (t % GRID_W).astype(jnp.float32)[:, None] * inv_freq
    ang = jnp.concatenate([row, col], axis=-1)
    return jnp.cos(ang), jnp.sin(ang)


def apply_rope_2d(x, cos, sin):
    n_freq = MLA_ROPE_DIM // 4
    xs = x.reshape(x.shape[:-1] + (2, 2, n_freq))
    x1, x2 = xs[..., 0, :], xs[..., 1, :]
    cs = cos.reshape(cos.shape[0], 1, 2, n_freq)
    sn = sin.reshape(sin.shape[0], 1, 2, n_freq)
    out = jnp.stack([x1 * cs - x2 * sn, x1 * sn + x2 * cs], axis=-2)
    return out.reshape(x.shape).astype(x.dtype)


def add_rope(t, cos, sin):
    return jnp.concatenate([t[..., :MLA_NOPE_DIM], apply_rope_2d(t[..., MLA_NOPE_DIM:], cos, sin)], axis=-1)


def mla_heads(t, qn_g, wq_up, kvn_g, wkv_up, qkn_q, qkn_k):
    lead = t.shape[:2]
    cq = t[..., :MLA_Q_RANK]
    ckv = t[..., MLA_Q_RANK:MLA_Q_RANK + MLA_KV_RANK]
    k_rope = t[..., MLA_Q_RANK + MLA_KV_RANK:]
    q = to_heads(rms_norm(cq, qn_g) @ wq_up, MLA_HEADS, MLA_QK_DIM)
    kv = to_heads(rms_norm(ckv, kvn_g) @ wkv_up, MLA_HEADS, MLA_NOPE_DIM + MLA_V_DIM)
    k_rope = jnp.broadcast_to(k_rope[:, :, None, :], lead + (MLA_HEADS, MLA_ROPE_DIM))
    k = jnp.concatenate([kv[..., :MLA_NOPE_DIM], k_rope], axis=-1)
    return rms_norm(q, qkn_q), rms_norm(k, qkn_k), kv[..., MLA_NOPE_DIM:]


def latent_attention(q, k, v, qc, kc, vc, cos, sin, need_ctx):
    bsz, seq = q.shape[:2]
    scale = MLA_QK_DIM ** -0.5
    q, k = add_rope(q, cos, sin), add_rope(k, cos, sin)
    k_all = jnp.concatenate([kc, k], axis=1)
    v_all = jnp.concatenate([vc, v], axis=1)
    q_blocks = jnp.moveaxis(q.reshape(bsz, seq // MLA_QBLOCK, MLA_QBLOCK, MLA_HEADS, MLA_QK_DIM), 1, 0)

    def attend(q_blk):
        s = jnp.einsum('bqhd,bkhd->bhqk', q_blk, k_all).astype(jnp.float32) * scale
        return jnp.einsum('bhqk,bkhd->bqhd', softmax_f32(s).astype(v_all.dtype), v_all)

    o = jnp.moveaxis(lax.map(attend, q_blocks), 0, 1).reshape(bsz, seq, BRANCH_W)
    o_ctx = context_attention(qc, kc, vc, scale) if need_ctx else None
    return o, o_ctx


def s5_discretize(a_re, a_im, log_dt, b_re, b_im):
    a_re, a_im = a_re.astype(jnp.float32), a_im.astype(jnp.float32)
    dt = jnp.exp(log_dt.astype(jnp.float32))[:, None]
    mag = jnp.exp(a_re * dt)
    lam_re, lam_im = mag * jnp.cos(a_im * dt), mag * jnp.sin(a_im * dt)
    den = a_re * a_re + a_im * a_im
    f_re = ((lam_re - 1.0) * a_re + lam_im * a_im) / den
    f_im = (lam_im * a_re - (lam_re - 1.0) * a_im) / den
    b_re, b_im = b_re.astype(jnp.float32), b_im.astype(jnp.float32)
    bb_re = f_re[..., None] * b_re - f_im[..., None] * b_im
    bb_im = f_re[..., None] * b_im + f_im[..., None] * b_re
    return lam_re, lam_im, bb_re, bb_im


def complex_linear_recurrence_op(e1, e2):
    a1r, a1i, b1r, b1i = e1
    a2r, a2i, b2r, b2i = e2
    return (a1r * a2r - a1i * a2i, a1r * a2i + a1i * a2r,
            a2r * b1r - a2i * b1i + b2r, a2r * b1i + a2i * b1r + b2i)


def s5_scan(h0_re, h0_im, lam_re, lam_im, bu_re, bu_im, reverse):
    first, last = (-1, 0) if reverse else (0, -1)
    bu_re = bu_re.at[:, first].add(lam_re * h0_re - lam_im * h0_im)
    bu_im = bu_im.at[:, first].add(lam_re * h0_im + lam_im * h0_re)
    a_re = jnp.broadcast_to(lam_re, bu_re.shape)
    a_im = jnp.broadcast_to(lam_im, bu_im.shape)
    _, _, h_re, h_im = lax.associative_scan(complex_linear_recurrence_op, (a_re, a_im, bu_re, bu_im),
                                            reverse=reverse, axis=1)
    return h_re, h_im, h_re[:, last], h_im[:, last]


def s5_mixer(u, uc, a_re, a_im, log_dt, b_re, b_im, c_re, c_im, d_skip, w_glu, b_glu, need_ctx):
    ux = u.reshape(u.shape[:2] + (S5_GROUPS, S5_GROUP)).astype(jnp.float32)
    ucx = uc.reshape(uc.shape[:2] + (S5_GROUPS, S5_GROUP)).astype(jnp.float32)
    d_g = d_skip.reshape(S5_GROUPS, S5_GROUP)
    y_x, y_c = ux * d_g, ucx * d_g
    h0 = jnp.zeros((u.shape[0], S5_GROUPS, S5_STATE), jnp.float32)
    for d, reverse in enumerate((False, True)):
        lam_re, lam_im, bb_re, bb_im = s5_discretize(a_re[d], a_im[d], log_dt[d], b_re[d], b_im[d])
        hc_re, hc_im, fin_re, fin_im = s5_scan(
            h0, h0, lam_re, lam_im,
            jnp.einsum('blgj,gpj->blgp', ucx, bb_re), jnp.einsum('blgj,gpj->blgp', ucx, bb_im), reverse)
        hx_re, hx_im, _, _ = s5_scan(
            fin_re, fin_im, lam_re, lam_im,
            jnp.einsum('blgj,gpj->blgp', ux, bb_re), jnp.einsum('blgj,gpj->blgp', ux, bb_im), reverse)
        y_x = y_x + jnp.einsum('blgp,gjp->blgj', hx_re, c_re[d]) - jnp.einsum('blgp,gjp->blgj', hx_im, c_im[d])
        if need_ctx:
            y_c = y_c + jnp.einsum('blgp,gjp->blgj', hc_re, c_re[d]) - jnp.einsum('blgp,gjp->blgj', hc_im, c_im[d])

    def gelu_glu(y):
        y = jax.nn.gelu(y.reshape(y.shape[:2] + (BRANCH_W,)))
        return y * jax.nn.sigmoid(y @ w_glu + b_glu)

    out_c = gelu_glu(y_c).astype(uc.dtype) if need_ctx else None
    return gelu_glu(y_x).astype(u.dtype), out_c


def branch_merge(ys, xn, w_gate, b_gate, w_branch, w_out):
    y = jnp.stack(ys, axis=2)
    up = jnp.einsum('blnw,nwd->blnd', y, w_branch)
    gates = jax.nn.sigmoid(xn @ w_gate + b_gate).reshape(up.shape)
    return jnp.sum(gates * up, axis=2) @ w_out


def moe_ffn(h, w_router, b_router, w_gu, b_gu, w_dn, b_dn):
    n_tok, d_model = h.shape
    logits = (h @ w_router + b_router).astype(jnp.float32)
    top_v, top_i = lax.top_k(logits, TOP_K)
    gate = jax.nn.softmax(top_v, axis=-1)
    flat_e = top_i.reshape(-1)
    order = jnp.argsort(flat_e)
    sorted_e = flat_e[order]
    sorted_tok = order // TOP_K
    sorted_gate = gate.reshape(-1)[order]
    counts = jax.ops.segment_sum(jnp.ones_like(flat_e), flat_e, num_segments=N_EXPERTS)
    padded = (counts + MOE_BLOCK - 1) // MOE_BLOCK * MOE_BLOCK
    pad_end = jnp.cumsum(padded)
    pad_start = pad_end - padded
    start = jnp.cumsum(counts) - counts
    dest = pad_start[sorted_e] + jnp.arange(n_tok * TOP_K) - start[sorted_e]
    n_blocks = -(-(n_tok * TOP_K) // MOE_BLOCK) + N_EXPERTS
    n_slots = n_blocks * MOE_BLOCK
    slot_tok = jnp.full((n_slots,), n_tok, jnp.int32).at[dest].set(sorted_tok)
    slot_gate = jnp.zeros((n_slots,), jnp.float32).at[dest].set(sorted_gate)
    block_e = jnp.minimum(jnp.searchsorted(pad_end, jnp.arange(n_blocks) * MOE_BLOCK, side='right'), N_EXPERTS - 1)
    h_pad = jnp.concatenate([h, jnp.zeros((1, d_model), h.dtype)], axis=0)

    def expert_block(args):
        tok, e = args
        gu = h_pad[tok] @ w_gu[e] + b_gu[e]
        g_lin, u_lin = gu[:, :D_EXPERT], gu[:, D_EXPERT:]
        g_lin = jnp.minimum(g_lin, SWIGLU_LIMIT)
        u_lin = jnp.clip(u_lin, -SWIGLU_LIMIT, SWIGLU_LIMIT)
        act = (u_lin + 1.0) * g_lin * jax.nn.sigmoid(SWIGLU_ALPHA * g_lin)
        return act @ w_dn[e] + b_dn[e]

    y_blocks = lax.map(expert_block, (slot_tok.reshape(n_blocks, MOE_BLOCK), block_e))
    contrib = y_blocks.reshape(n_slots, d_model) * slot_gate[:, None].astype(y_blocks.dtype)
    out = jnp.zeros((n_tok + 1, d_model), y_blocks.dtype).at[slot_tok].add(contrib)
    return out[:n_tok].astype(h.dtype)


def trunk_layer(x, s, mod_x, mod_s, cos, sin, need_ctx, p):
    sh1, sc1, g1, sh2, sc2, g2 = jnp.split(mod_x[:, None, :], 6, axis=-1)
    sh1c, sc1c, g1c, sh2c, sc2c, g2c = jnp.split(mod_s, 6, axis=-1)
    xn = rms_norm(x, p['norm1']) * (1.0 + sc1) + sh1
    sn = rms_norm(s, p['norm1']) * (1.0 + sc1c) + sh1c
    na_x, rw_x, mla_x, s5_x = split_columns(xn @ p['w_in'])
    na_s, rw_s, mla_s, s5_s = split_columns(sn @ p['w_in'])

    def na_qkv(t):
        return [to_heads(u, NA_HEADS, NA_HEAD_DIM) for u in jnp.split(t, 3, axis=-1)]

    ya_x, ya_s = neighbourhood_attention(*na_qkv(na_x), *na_qkv(na_s),
                                         p['na_qn_g'], p['na_kn_g'], p['na_rpb'], need_ctx)
    yb_x, yb_s = rwkv_time_mix(rw_x, rw_s, p['rw_mu'], p['rw_w0'], p['rw_w2'], p['rw_a0'], p['rw_a2'],
                               p['rw_g2'], p['rw_kk'], p['rw_ka'], p['rw_rk'], p['rw_gn_g'], p['rw_gn_b'],
                               need_ctx)
    mla_w = (p['mla_qn_g'], p['mla_wq_up'], p['mla_kvn_g'], p['mla_wkv_up'], p['mla_qkn_q'], p['mla_qkn_k'])
    yc_x, yc_s = latent_attention(*mla_heads(mla_x, *mla_w), *mla_heads(mla_s, *mla_w), cos, sin, need_ctx)
    yd_x, yd_s = s5_mixer(s5_x, s5_s, p['s5_a_re'], p['s5_a_im'], p['s5_log_dt'], p['s5_b_re'], p['s5_b_im'],
                          p['s5_c_re'], p['s5_c_im'], p['s5_d'], p['s5_w_glu'], p['s5_b_glu'], need_ctx)

    merge_w = (p['w_gate'], p['b_gate'], p['w_branch'], p['w_out'])
    moe_w = (p['moe_wr'], p['moe_br'], p['moe_w_gu'], p['moe_b_gu'], p['moe_w_dn'], p['moe_b_dn'])
    x = x + g1 * branch_merge([ya_x, yb_x, yc_x, yd_x], xn, *merge_w)
    hx = rms_norm(x, p['norm2']) * (1.0 + sc2) + sh2
    n_x = hx.shape[0] * hx.shape[1]
    if need_ctx:
        s = s + g1c * branch_merge([ya_s, yb_s, yc_s, yd_s], sn, *merge_w)
        hs = rms_norm(s, p['norm2']) * (1.0 + sc2c) + sh2c
        f = moe_ffn(jnp.concatenate([hx.reshape(n_x, -1), hs.reshape(-1, hs.shape[-1])], axis=0), *moe_w)
        s = s + g2c * f[n_x:].reshape(s.shape)
        fx = f[:n_x]
    else:
        fx = moe_ffn(hx.reshape(n_x, -1), *moe_w)
    x = x + g2 * fx.reshape(x.shape)
    return x, s


def setup_inputs(seed: int = 0) -> dict:
    key = jax.random.key(seed)
    ks = iter(jax.random.split(key, 64))

    def nrm(shape, scale=1.0):
        return scale * jax.random.normal(next(ks), shape, jnp.float32)

    def gain(shape):
        return 1.0 + nrm(shape, 0.02)

    L, D, W = DEPTH, D_MODEL, BRANCH_W
    G, P, J = S5_GROUPS, S5_STATE, S5_GROUP
    return {
        'x': nrm((BATCH, SEQ, D)),
        'c': nrm((BATCH, D)),
        'ctx': nrm((BATCH, CTX_LEN, D)),
        'c_ctx': nrm((D,)),
        'w_mod': nrm((L, D, 6 * D), 0.5 * D ** -0.5),
        'b_mod': nrm((L, 6 * D), 0.01),
        'norm1_g': gain((L, D)),
        'norm2_g': gain((L, D)),
        'w_in': nrm((L, D, IN_COLS), D ** -0.5),
        'w_gate': nrm((L, D, N_BRANCH * D), D ** -0.5),
        'b_gate': nrm((L, N_BRANCH * D), 0.01),
        'w_branch': nrm((L, N_BRANCH, W, D), W ** -0.5),
        'w_out': nrm((L, D, D), D ** -0.5),
        'na_qn_g': gain((L, NA_HEAD_DIM)),
        'na_kn_g': gain((L, NA_HEAD_DIM)),
        'na_rpb': nrm((L, NA_HEADS, 2 * NA_WIN_R - 1, 2 * NA_WIN_C - 1), 0.1),
        'rw_mu': jax.random.uniform(next(ks), (L, 2, RW_COLS), jnp.float32, 0.0, 0.5),
        'rw_w0': nrm((L, 2, W)) - 1.0,
        'rw_w2': nrm((L, 2, RW_DECAY_LORA, W), 0.1),
        'rw_a0': nrm((L, 2, W), 0.5),
        'rw_a2': nrm((L, 2, RW_ICL_LORA, W), 0.1),
        'rw_g2': nrm((L, RW_GATE_LORA, W), RW_GATE_LORA ** -0.5),
        'rw_kk': 0.85 + nrm((L, W), 0.02),
        'rw_ka': gain((L, W)),
        'rw_rk': nrm((L, RW_HEADS, RW_HEAD_DIM), 0.1),
        'rw_gn_g': gain((L, W)),
        'rw_gn_b': nrm((L, W), 0.01),
        'mla_qn_g': gain((L, MLA_Q_RANK)),
        'mla_wq_up': nrm((L, MLA_Q_RANK, MLA_HEADS * MLA_QK_DIM), MLA_Q_RANK ** -0.5),
        'mla_kvn_g': gain((L, MLA_KV_RANK)),
        'mla_wkv_up': nrm((L, MLA_KV_RANK, MLA_HEADS * (MLA_NOPE_DIM + MLA_V_DIM)), MLA_KV_RANK ** -0.5),
        'mla_qkn_q': gain((L, MLA_QK_DIM)),
        'mla_qkn_k': gain((L, MLA_QK_DIM)),
        's5_a_re': -0.5 + nrm((L, 2, G, P), 0.01),
        's5_a_im': jnp.pi * jnp.arange(P, dtype=jnp.float32) + nrm((L, 2, G, P), 0.01),
        's5_log_dt': jnp.log(jax.random.uniform(next(ks), (L, 2, G), jnp.float32, 1e-3, 1e-1)),
        's5_b_re': nrm((L, 2, G, P, J), (2 * J) ** -0.5),
        's5_b_im': nrm((L, 2, G, P, J), (2 * J) ** -0.5),
        's5_c_re': nrm((L, 2, G, J, P), (2 * P) ** -0.5),
        's5_c_im': nrm((L, 2, G, J, P), (2 * P) ** -0.5),
        's5_d': nrm((L, W)),
        's5_w_glu': nrm((L, W, W), W ** -0.5),
        's5_b_glu': nrm((L, W), 0.01),
        'moe_wr': nrm((L, D, N_EXPERTS), D ** -0.5),
        'moe_br': nrm((L, N_EXPERTS), 0.01),
        'moe_w_gu': nrm((L, N_EXPERTS, D, 2 * D_EXPERT), D ** -0.5),
        'moe_b_gu': nrm((L, N_EXPERTS, 2 * D_EXPERT), 0.01),
        'moe_w_dn': nrm((L, N_EXPERTS, D_EXPERT, D), D_EXPERT ** -0.5),
        'moe_b_dn': nrm((L, N_EXPERTS, D), 0.01),
    }


def reference(x, c, ctx, c_ctx, w_mod, b_mod, norm1_g, norm2_g, w_in, w_gate, b_gate, w_branch, w_out,
              na_qn_g, na_kn_g, na_rpb,
              rw_mu, rw_w0, rw_w2, rw_a0, rw_a2, rw_g2, rw_kk, rw_ka, rw_rk, rw_gn_g, rw_gn_b,
              mla_qn_g, mla_wq_up, mla_kvn_g, mla_wkv_up, mla_qkn_q, mla_qkn_k,
              s5_a_re, s5_a_im, s5_log_dt, s5_b_re, s5_b_im, s5_c_re, s5_c_im, s5_d, s5_w_glu, s5_b_glu,
              moe_wr, moe_br, moe_w_gu, moe_b_gu, moe_w_dn, moe_b_dn):
    cos, sin = rope_2d_tables(x.shape[1])
    cond_x = jax.nn.silu(c)
    cond_s = jax.nn.silu(c_ctx)
    s = ctx
    for l in range(DEPTH):
        need_ctx = l < DEPTH - 1
        p = dict(norm1=norm1_g[l], norm2=norm2_g[l], w_in=w_in[l], w_gate=w_gate[l], b_gate=b_gate[l],
                 w_branch=w_branch[l], w_out=w_out[l],
                 na_qn_g=na_qn_g[l], na_kn_g=na_kn_g[l], na_rpb=na_rpb[l],
                 rw_mu=rw_mu[l], rw_w0=rw_w0[l], rw_w2=rw_w2[l], rw_a0=rw_a0[l], rw_a2=rw_a2[l],
                 rw_g2=rw_g2[l], rw_kk=rw_kk[l], rw_ka=rw_ka[l], rw_rk=rw_rk[l],
                 rw_gn_g=rw_gn_g[l], rw_gn_b=rw_gn_b[l],
                 mla_qn_g=mla_qn_g[l], mla_wq_up=mla_wq_up[l], mla_kvn_g=mla_kvn_g[l],
                 mla_wkv_up=mla_wkv_up[l], mla_qkn_q=mla_qkn_q[l], mla_qkn_k=mla_qkn_k[l],
                 s5_a_re=s5_a_re[l], s5_a_im=s5_a_im[l], s5_log_dt=s5_log_dt[l], s5_b_re=s5_b_re[l],
                 s5_b_im=s5_b_im[l], s5_c_re=s5_c_re[l], s5_c_im=s5_c_im[l], s5_d=s5_d[l],
                 s5_w_glu=s5_w_glu[l], s5_b_glu=s5_b_glu[l],
                 moe_wr=moe_wr[l], moe_br=moe_br[l], moe_w_gu=moe_w_gu[l], moe_b_gu=moe_b_gu[l],
                 moe_w_dn=moe_w_dn[l], moe_b_dn=moe_b_dn[l])
        mod_x = cond_x @ w_mod[l] + b_mod[l]
        mod_s = cond_s @ w_mod[l] + b_mod[l]
        x, s = trunk_layer(x, s, mod_x, mod_s, cos, sin, need_ctx, p)
    return x
```

```python
import functools

import numpy as np
import jax
import jax.numpy as jnp
from jax import lax
from jax.experimental import pallas as pl
from jax.experimental.pallas import tpu as pltpu

F32 = jnp.float32
BF16 = jnp.bfloat16
I32 = jnp.int32

_VMEM_LIMIT_BYTES = 56 * 1024 * 1024
_LANES = 128

NORM_EPS = 1e-6
GRID_W = 64
NA_HEADS = 8
NA_HEAD_DIM = 64
NA_WIN_R = 8
NA_WIN_C = 16
NA_QROWS = 4
NA_BAND_ROWS = NA_QROWS + NA_WIN_R - 1
RW_HEAD_DIM = 64
RW_GN_EPS = 64e-5
RW_CHUNK = 64
MLA_HEADS = 8
MLA_NOPE = 64
MLA_ROPE = 32
MLA_QK = MLA_NOPE + MLA_ROPE
MLA_V = 64
MLA_Q_RANK = 512
MLA_KV_RANK = 256
ROPE_THETA = 10000.0
MLA_KEY_CHUNK = 512
S5_GROUP = 16
S5_STATE = 64
S5_CHUNK = 16
N_EXPERTS = 32
TOP_K = 4
D_EXPERT = 512
SWIGLU_LIMIT = 7.0
SWIGLU_ALPHA = 1.702
MOE_BLOCK = 256
MOE_COMBINE_ROWS = 128

_NT = (((1,), (1,)), ((), ()))
_TN = (((0,), (0,)), ((), ()))


def _cp(*sem):
    return pltpu.CompilerParams(dimension_semantics=sem, vmem_limit_bytes=_VMEM_LIMIT_BYTES)


def _dot(a, b):
    return jnp.dot(a.astype(BF16), b.astype(BF16), preferred_element_type=F32)


def _dot_nt(a, b):
    return lax.dot_general(a.astype(BF16), b.astype(BF16), _NT, preferred_element_type=F32)


def _dot_tn(a, b):
    return lax.dot_general(a.astype(BF16), b.astype(BF16), _TN, preferred_element_type=F32)


def _split(x, parts):
    out = []
    for _ in range(parts):
        h = x.astype(BF16)
        out.append(h)
        x = x - h.astype(F32)
    return out


def _dot01_right(x, m01, parts=2):
    return sum(jnp.dot(h, m01, preferred_element_type=F32) for h in _split(x, parts))


def _dot01_left(m01, x, parts=3):
    return sum(jnp.dot(m01, h, preferred_element_type=F32) for h in _split(x, parts))


def _sigmoid(x):
    return 1.0 / (1.0 + jnp.exp(-x))


def _softplus(x):
    return jnp.maximum(x, 0.0) + jnp.log(1.0 + jnp.exp(-jnp.abs(x)))


def _pick(n, cands):
    for c in cands:
        if n % c == 0:
            return c
    raise ValueError(f"no tile for {n} in {cands}")


def _head_blockdiag(width, head):
    i = np.arange(width)
    return jnp.asarray((i[:, None] // head) == (i[None, :] // head), BF16)


def _mm_kernel(*refs, has_bias, n_extra, epilogue):
    a_ref, w_ref = refs[0], refs[1]
    pos = 2
    bias_ref = None
    if has_bias:
        bias_ref = refs[pos]
        pos += 1
    extra = refs[pos:pos + n_extra]
    o_ref, wbf_ref = refs[pos + n_extra], refs[pos + n_extra + 1]

    @pl.when(pl.program_id(1) == 0)
    def _():
        wbf_ref[...] = w_ref[...].astype(BF16)

    acc = jnp.dot(a_ref[...].astype(BF16), wbf_ref[...], preferred_element_type=F32)
    if has_bias:
        acc = acc + bias_ref[...]
    if epilogue is not None:
        acc = epilogue(acc, *[e[...] for e in extra])
    o_ref[...] = acc.astype(o_ref.dtype)


def _mm(a, w, l, *, tm, tn, col0=0, ncols=None, rows=None, bias=None, extras=(), epilogue=None,
        out_dtype=F32):
    k = a.shape[1]
    rows = a.shape[0] if rows is None else rows
    ncols = w.shape[2] - col0 if ncols is None else ncols
    assert col0 % tn == 0 and ncols % tn == 0 and rows % tm == 0 and w.shape[1] == k
    c0 = col0 // tn
    in_specs = [pl.BlockSpec((tm, k), lambda j, i: (i, 0)),
                pl.BlockSpec((None, k, tn), lambda j, i: (l, 0, j + c0))]
    args = [a, w]
    if bias is not None:
        in_specs.append(pl.BlockSpec((None, 1, tn), lambda j, i: (l, 0, j + c0)))
        args.append(bias)
    for arr, spec in extras:
        in_specs.append(spec)
        args.append(arr)
    return pl.pallas_call(
        functools.partial(_mm_kernel, has_bias=bias is not None, n_extra=len(extras), epilogue=epilogue),
        grid=(ncols // tn, rows // tm),
        in_specs=in_specs,
        out_specs=pl.BlockSpec((tm, tn), lambda j, i: (i, j)),
        out_shape=jax.ShapeDtypeStruct((rows, ncols), out_dtype),
        scratch_shapes=[pltpu.VMEM((k, tn), BF16)],
        compiler_params=_cp("arbitrary", "arbitrary"),
    )(*args)


def _norm_mod_kernel(x_ref, g_ref, sc_ref, sh_ref, o_ref):
    x = x_ref[...]
    y = x * lax.rsqrt(jnp.mean(x * x, axis=-1, keepdims=True) + NORM_EPS) * g_ref[...]
    o_ref[...] = (y * (1.0 + sc_ref[...]) + sh_ref[...]).astype(o_ref.dtype)


def _norm_mod(x, gain3, l, mod, which_sh, which_sc, dims, rows, out_dtype):
    d = x.shape[1]
    tm = 256
    seg = dims.seg_fn(tm)
    return pl.pallas_call(
        _norm_mod_kernel,
        grid=(rows // tm,),
        in_specs=[pl.BlockSpec((tm, d), lambda i: (i, 0)),
                  pl.BlockSpec((None, 1, d), lambda i: (l, 0, 0)),
                  pl.BlockSpec((None, 1, d), lambda i: (seg(i), 0, which_sc)),
                  pl.BlockSpec((None, 1, d), lambda i: (seg(i), 0, which_sh))],
        out_specs=pl.BlockSpec((tm, d), lambda i: (i, 0)),
        out_shape=jax.ShapeDtypeStruct((rows, d), out_dtype),
        compiler_params=_cp("parallel"),
    )(x, gain3, mod, mod)


class _Dims:
    def __init__(self, b, s, c, d):
        self.B, self.S, self.C, self.D = b, s, c, d
        self.BS = b * s
        self.M = b * s + b * c
        self.W = d // 4

    def seg_fn(self, tm):
        assert self.S % tm == 0 and self.M % tm == 0
        s, b = self.S, self.B
        return lambda i: jnp.minimum((i * tm) // s, b)


def _attend(q, segs):
    m = l = acc = None
    for k, v, bias in segs:
        s = lax.dot_general(q, k, _NT, preferred_element_type=F32)
        if bias is not None:
            s = s + bias
        smax = jnp.max(s, axis=-1, keepdims=True)
        if m is None:
            m = smax
            p = jnp.exp(s - m)
            l = jnp.sum(p, axis=-1, keepdims=True)
            acc = jnp.dot(p.astype(BF16), v, preferred_element_type=F32)
        else:
            m_new = jnp.maximum(m, smax)
            alpha = jnp.exp(m - m_new)
            p = jnp.exp(s - m_new)
            l = alpha * l + jnp.sum(p, axis=-1, keepdims=True)
            acc = alpha * acc + jnp.dot(p.astype(BF16), v, preferred_element_type=F32)
            m = m_new
    return acc * (1.0 / l)


def _half_masks(dtype):
    lane = lax.broadcasted_iota(I32, (1, _LANES), 1)
    lo = lane < 64
    return lo, jnp.where(lo, 1.0, 0.0).astype(dtype), jnp.where(lo, 0.0, 1.0).astype(dtype)


def _na_prep_kernel(f_ref, gq_ref, gk_ref, bd_ref, o_ref):
    w = gq_ref.shape[1]
    bd = bd_ref[...]

    def head_norm(x, g):
        ss = _dot01_right(x * x, bd)
        return x * lax.rsqrt(ss * (1.0 / NA_HEAD_DIM) + NORM_EPS) * g

    o_ref[:, 0:w] = head_norm(f_ref[:, 0:w], gq_ref[...]).astype(o_ref.dtype)
    o_ref[:, w:2 * w] = head_norm(f_ref[:, w:2 * w], gk_ref[...]).astype(o_ref.dtype)
    o_ref[:, 2 * w:3 * w] = f_ref[:, 2 * w:3 * w].astype(o_ref.dtype)


def _na_prep(f_na, qn_g, kn_g, dims):
    m, w3 = f_na.shape
    w = w3 // 3
    tm = 256
    gq = (jnp.tile(qn_g, NA_HEADS) * NA_HEAD_DIM ** -0.5).reshape(1, w)
    gk = jnp.tile(kn_g, NA_HEADS).reshape(1, w)
    return pl.pallas_call(
        _na_prep_kernel,
        grid=(m // tm,),
        in_specs=[pl.BlockSpec((tm, w3), lambda i: (i, 0)),
                  pl.BlockSpec((1, w), lambda i: (0, 0)),
                  pl.BlockSpec((1, w), lambda i: (0, 0)),
                  pl.BlockSpec((w, w), lambda i: (0, 0))],
        out_specs=pl.BlockSpec((tm, w3), lambda i: (i, 0)),
        out_shape=jax.ShapeDtypeStruct((m, w3), BF16),
        compiler_params=_cp("parallel"),
    )(f_na, gq, gk, _head_blockdiag(w, NA_HEAD_DIM))


def _na_bias_tables(rpb, rows):
    n_tiles = rows // NA_QROWS
    assert rows % NA_QROWS == 0 and rows >= NA_BAND_ROWS + NA_QROWS and NA_QROWS * 2 <= NA_WIN_R
    tabs = []
    for g in (0, 1, n_tiles - 1):
        r = g * NA_QROWS + np.arange(NA_QROWS)
        band0 = int(np.clip(g * NA_QROWS - NA_WIN_R // 2, 0, rows - NA_BAND_ROWS))
        key_row = band0 + np.arange(NA_BAND_ROWS)
        row_start = np.clip(r - NA_WIN_R // 2, 0, rows - NA_WIN_R)
        valid_r = (key_row[None, :] >= row_start[:, None]) & (key_row[None, :] < row_start[:, None] + NA_WIN_R)
        d_row = np.clip(key_row[None, :] - r[:, None] + NA_WIN_R - 1, 0, 2 * NA_WIN_R - 2)
        qc = np.arange(GRID_W)
        col_start = np.clip(qc - NA_WIN_C // 2, 0, GRID_W - NA_WIN_C)
        kc = np.arange(GRID_W)
        valid_c = (kc[None, :] >= col_start[:, None]) & (kc[None, :] < col_start[:, None] + NA_WIN_C)
        d_col = np.clip(kc[None, :] - qc[:, None] + NA_WIN_C - 1, 0, 2 * NA_WIN_C - 2)
        b = rpb[:, d_row[:, None, :, None], d_col[None, :, None, :]]
        valid = valid_r[:, None, :, None] & valid_c[None, :, None, :]
        b = jnp.where(valid[None], b, -jnp.inf)
        tabs.append(b.reshape(NA_HEADS, NA_QROWS * GRID_W, NA_BAND_ROWS * GRID_W))
    return jnp.stack(tabs).astype(BF16)


def _na_attn_kernel(q_ref, k_ref, v_ref, kc_ref, vc_ref, bias_ref, o_ref, *, rows):
    g = pl.program_id(1)
    band0 = jnp.clip(g * NA_QROWS - NA_WIN_R // 2, 0, rows - NA_BAND_ROWS)
    off = pl.multiple_of(band0 * GRID_W, GRID_W)
    band = NA_BAND_ROWS * GRID_W
    lo, m_e, m_o = _half_masks(BF16)
    for p in range(q_ref.shape[1] // _LANES):
        sl = slice(p * _LANES, (p + 1) * _LANES)
        qp = q_ref[:, sl]
        kb = k_ref[pl.ds(off, band), sl]
        vb = v_ref[pl.ds(off, band), sl]
        kc = kc_ref[:, sl]
        vc = vc_ref[:, sl]
        o_e = _attend(qp * m_e, [(kb, vb, bias_ref[2 * p].astype(F32)), (kc, vc, None)])
        o_o = _attend(qp * m_o, [(kb, vb, bias_ref[2 * p + 1].astype(F32)), (kc, vc, None)])
        o_ref[:, sl] = jnp.where(lo, o_e, o_o).astype(o_ref.dtype)


def _na_ctx_kernel(q_ref, kc_ref, vc_ref, o_ref):
    lo, m_e, m_o = _half_masks(BF16)
    for p in range(q_ref.shape[1] // _LANES):
        sl = slice(p * _LANES, (p + 1) * _LANES)
        qp = q_ref[:, sl]
        seg = [(kc_ref[:, sl], vc_ref[:, sl], None)]
        o_ref[:, sl] = jnp.where(lo, _attend(qp * m_e, seg), _attend(qp * m_o, seg)).astype(o_ref.dtype)


def _na_attention(qkv, bias, dims, need_ctx):
    b, s, c, w = dims.B, dims.S, dims.C, dims.W
    rows = s // GRID_W
    n_tiles = rows // NA_QROWS
    tq = NA_QROWS * GRID_W
    band = NA_BAND_ROWS * GRID_W
    ctx_blk = dims.BS // c

    def cls(g):
        return (g > 0).astype(I32) + (g == n_tiles - 1).astype(I32)

    ya = pl.pallas_call(
        functools.partial(_na_attn_kernel, rows=rows),
        grid=(b, n_tiles),
        in_specs=[pl.BlockSpec((tq, w), lambda bi, g: (bi * n_tiles + g, 0)),
                  pl.BlockSpec((s, w), lambda bi, g: (bi, 1)),
                  pl.BlockSpec((s, w), lambda bi, g: (bi, 2)),
                  pl.BlockSpec((c, w), lambda bi, g: (ctx_blk + bi, 1)),
                  pl.BlockSpec((c, w), lambda bi, g: (ctx_blk + bi, 2)),
                  pl.BlockSpec((None, NA_HEADS, tq, band), lambda bi, g: (cls(g), 0, 0, 0))],
        out_specs=pl.BlockSpec((tq, w), lambda bi, g: (bi * n_tiles + g, 0)),
        out_shape=jax.ShapeDtypeStruct((dims.BS, w), BF16),
        compiler_params=_cp("parallel", "arbitrary"),
    )(qkv, qkv, qkv, qkv, qkv, bias)
    if not need_ctx:
        return ya
    yc = pl.pallas_call(
        _na_ctx_kernel,
        grid=(b,),
        in_specs=[pl.BlockSpec((c, w), lambda bi: (ctx_blk + bi, 0)),
                  pl.BlockSpec((c, w), lambda bi: (ctx_blk + bi, 1)),
                  pl.BlockSpec((c, w), lambda bi: (ctx_blk + bi, 2))],
        out_specs=pl.BlockSpec((c, w), lambda bi: (bi, 0)),
        out_shape=jax.ShapeDtypeStruct((b * c, w), BF16),
        compiler_params=_cp("parallel"),
    )(qkv, qkv, qkv)
    return jnp.concatenate([ya, yc], axis=0)


def _rope_tables(s, tm):
    n_freq = MLA_ROPE // 4
    inv_freq = ROPE_THETA ** (-jnp.arange(n_freq, dtype=F32) / n_freq)
    t = jnp.arange(s)
    row = (t // GRID_W).astype(F32)[:, None] * inv_freq
    col = (t % GRID_W).astype(F32)[:, None] * inv_freq
    zeros = jnp.zeros((s, n_freq), F32)

    def slab(row_x1, row_x2, col_x1, col_x2, fill):
        body = jnp.concatenate([row_x1, row_x2, col_x1, col_x2], axis=-1)
        return jnp.concatenate([jnp.full((s, MLA_NOPE), fill, F32), body,
                                jnp.zeros((s, _LANES - MLA_QK), F32)], axis=-1)

    cr, sr, cc, sc = jnp.cos(row), jnp.sin(row), jnp.cos(col), jnp.sin(col)
    c_tab = slab(cr, cr, cc, cc, 1.0)
    s1_tab = slab(-sr, zeros, -sc, zeros, 0.0)
    s2_tab = slab(zeros, sr, zeros, sc, 0.0)
    ident = jnp.concatenate([jnp.ones((tm, MLA_QK), F32), jnp.zeros((tm, _LANES - MLA_QK), F32)], axis=-1)
    zero = jnp.zeros((tm, _LANES), F32)
    return (jnp.concatenate([c_tab, ident]), jnp.concatenate([s1_tab, zero]),
            jnp.concatenate([s2_tab, zero]))


def _mla_prep_kernel(f_ref, wq_ref, wk_ref, wv_ref, place_ref, gcq_ref, gckv_ref, gq_ref, gk_ref,
                     ct_ref, s1_ref, s2_ref, q_o, k_o, v_o):
    cq = f_ref[:, 0:MLA_Q_RANK]
    ckv = f_ref[:, MLA_Q_RANK:MLA_Q_RANK + MLA_KV_RANK]
    kr = f_ref[:, MLA_Q_RANK + MLA_KV_RANK:MLA_Q_RANK + MLA_KV_RANK + MLA_ROPE]

    def rms(x, g):
        return x * lax.rsqrt(jnp.mean(x * x, axis=-1, keepdims=True) + NORM_EPS) * g

    cqn = rms(cq, gcq_ref[...]).astype(BF16)
    ckvn = rms(ckv, gckv_ref[...]).astype(BF16)
    q = jnp.dot(cqn, wq_ref[...], preferred_element_type=F32)
    kn = jnp.dot(ckvn, wk_ref[...], preferred_element_type=F32)
    v_o[...] = jnp.dot(ckvn, wv_ref[...], preferred_element_type=F32).astype(v_o.dtype)
    kr_slab = _dot01_right(kr, place_ref[...])
    ct, s1, s2 = ct_ref[...], s1_ref[...], s2_ref[...]

    def head(x, g):
        x = x * lax.rsqrt(jnp.sum(x * x, axis=-1, keepdims=True) * (1.0 / MLA_QK) + NORM_EPS) * g
        return x * ct + pltpu.roll(x, _LANES - MLA_ROPE // 4, 1) * s1 + pltpu.roll(x, MLA_ROPE // 4, 1) * s2

    for h in range(MLA_HEADS):
        sl = slice(h * _LANES, (h + 1) * _LANES)
        q_o[h] = head(q[:, sl], gq_ref[...]).astype(q_o.dtype)
        k_o[h] = head(kn[:, sl] + kr_slab, gk_ref[...]).astype(k_o.dtype)


def _mla_prep(f_mla, pw, l, ropes, dims):
    m = f_mla.shape[0]
    tm = 256
    n_lat = dims.S // tm
    bs = dims.BS

    def rope_blk(i):
        return jnp.where(i * tm < bs, i % n_lat, n_lat)

    hw = MLA_HEADS * _LANES
    full = lambda shape: pl.BlockSpec(shape, lambda i: tuple(0 for _ in shape))
    lyr = lambda shape: pl.BlockSpec((None,) + shape, lambda i: (l,) + tuple(0 for _ in shape))
    return pl.pallas_call(
        _mla_prep_kernel,
        grid=(m // tm,),
        in_specs=[pl.BlockSpec((tm, f_mla.shape[1]), lambda i: (i, 0)),
                  lyr((MLA_Q_RANK, hw)), lyr((MLA_KV_RANK, hw)), lyr((MLA_KV_RANK, MLA_HEADS * MLA_V)),
                  full((MLA_ROPE, _LANES)),
                  lyr((1, MLA_Q_RANK)), lyr((1, MLA_KV_RANK)), lyr((1, _LANES)), lyr((1, _LANES)),
                  pl.BlockSpec((tm, _LANES), lambda i: (rope_blk(i), 0)),
                  pl.BlockSpec((tm, _LANES), lambda i: (rope_blk(i), 0)),
                  pl.BlockSpec((tm, _LANES), lambda i: (rope_blk(i), 0))],
        out_specs=[pl.BlockSpec((MLA_HEADS, tm, _LANES), lambda i: (0, i, 0)),
                   pl.BlockSpec((MLA_HEADS, tm, _LANES), lambda i: (0, i, 0)),
                   pl.BlockSpec((tm, MLA_HEADS * MLA_V), lambda i: (i, 0))],
        out_shape=[jax.ShapeDtypeStruct((MLA_HEADS, m, _LANES), BF16),
                   jax.ShapeDtypeStruct((MLA_HEADS, m, _LANES), BF16),
                   jax.ShapeDtypeStruct((m, MLA_HEADS * MLA_V), BF16)],
        compiler_params=_cp("parallel"),
    )(f_mla, pw["mla_wq"], pw["mla_wk"], pw["mla_wv"], pw["mla_place"], pw["mla_gcq"], pw["mla_gckv"],
      pw["mla_gq"], pw["mla_gk"], *ropes)


def _mla_attn_kernel(qe_ref, qo_ref, kce_ref, kco_ref, kle_ref, klo_ref, vc_ref, vl_ref, o_ref):
    lo, _, _ = _half_masks(F32)
    n_chunks = kle_ref.shape[0] // MLA_KEY_CHUNK

    def segs(kc_ref, kl_ref):
        out = [(kc_ref[...], vc_ref[...], None)]
        for j in range(n_chunks):
            sl = slice(j * MLA_KEY_CHUNK, (j + 1) * MLA_KEY_CHUNK)
            out.append((kl_ref[sl, :], vl_ref[sl, :], None))
        return out

    o_e = _attend(qe_ref[...], segs(kce_ref, kle_ref))
    o_o = _attend(qo_ref[...], segs(kco_ref, klo_ref))
    o_ref[...] = jnp.where(lo, o_e, o_o).astype(o_ref.dtype)


def _mla_ctx_kernel(qe_ref, qo_ref, kce_ref, kco_ref, vc_ref, o_ref):
    lo, _, _ = _half_masks(F32)
    o_e = _attend(qe_ref[...], [(kce_ref[...], vc_ref[...], None)])
    o_o = _attend(qo_ref[...], [(kco_ref[...], vc_ref[...], None)])
    o_ref[...] = jnp.where(lo, o_e, o_o).astype(o_ref.dtype)


def _mla_attention(q, k, v, dims, need_ctx):
    b, s, c, w = dims.B, dims.S, dims.C, dims.W
    tq = 256
    nq = s // tq
    ctx_blk = dims.BS // c
    pairs = MLA_HEADS // 2
    hq = lambda half: pl.BlockSpec((None, tq, _LANES), lambda bi, p, i: (2 * p + half, bi * nq + i, 0))
    hkc = lambda half: pl.BlockSpec((None, c, _LANES), lambda bi, p, i: (2 * p + half, ctx_blk + bi, 0))
    hkl = lambda half: pl.BlockSpec((None, s, _LANES), lambda bi, p, i: (2 * p + half, bi, 0))
    yc_x = pl.pallas_call(
        _mla_attn_kernel,
        grid=(b, pairs, nq),
        in_specs=[hq(0), hq(1), hkc(0), hkc(1), hkl(0), hkl(1),
                  pl.BlockSpec((c, _LANES), lambda bi, p, i: (ctx_blk + bi, p)),
                  pl.BlockSpec((s, _LANES), lambda bi, p, i: (bi, p))],
        out_specs=pl.BlockSpec((tq, _LANES), lambda bi, p, i: (bi * nq + i, p)),
        out_shape=jax.ShapeDtypeStruct((dims.BS, w), BF16),
        compiler_params=_cp("parallel", "parallel", "arbitrary"),
    )(q, q, k, k, k, k, v, v)
    if not need_ctx:
        return yc_x
    cq = lambda half: pl.BlockSpec((None, c, _LANES), lambda bi, p: (2 * p + half, ctx_blk + bi, 0))
    yc_s = pl.pallas_call(
        _mla_ctx_kernel,
        grid=(b, pairs),
        in_specs=[cq(0), cq(1), cq(0), cq(1),
                  pl.BlockSpec((c, _LANES), lambda bi, p: (ctx_blk + bi, p))],
        out_specs=pl.BlockSpec((c, _LANES), lambda bi, p: (bi, p)),
        out_shape=jax.ShapeDtypeStruct((b * c, w), BF16),
        compiler_params=_cp("parallel", "parallel"),
    )(q, q, k, k, v)
    return jnp.concatenate([yc_x, yc_s], axis=0)


def _rw_prep_kernel(f_ref, fp_ref, fn_ref, mu_ref, w0_ref, w2_ref, a0_ref, a2_ref, g2_ref, kkg_ref,
                    ka_ref, bd_ref, r_o, v_o, kk_o, g_o, lw_o, kd_o, bb_o, *, s, c, bs, tm):
    w = r_o.shape[1]
    row0 = pl.program_id(0) * tm
    is_lat = row0 < bs
    pos = jnp.where(is_lat, row0 % s, (row0 - bs) % c)
    seglen = jnp.where(is_lat, s, c)
    f = f_ref[...]
    prev_row = jnp.where(pos == 0, 0.0, fp_ref[7:8, :])
    next_row = jnp.where(pos + tm == seglen, 0.0, fn_ref[0:1, :])
    ridx = lax.broadcasted_iota(I32, (tm, 1), 0)
    prev = jnp.where(ridx == 0, prev_row, pltpu.roll(f, 1, 0))
    nxt = jnp.where(ridx == tm - 1, next_row, pltpu.roll(f, tm - 1, 0))
    fm = f + mu_ref[0:1, :] * (prev - f) + mu_ref[1:2, :] * (nxt - f)
    r, k, v = fm[:, 0:w], fm[:, w:2 * w], fm[:, 2 * w:3 * w]
    o = 3 * w
    w_low = jnp.tanh(fm[:, o:o + 128])
    a_low = fm[:, o + 128:o + 256]
    g_low = _sigmoid(fm[:, o + 256:o + 384])
    log_w = -_softplus(-(w0_ref[...] + _dot(w_low, w2_ref[...]))) - 0.5
    lw = -jnp.exp(log_w)
    a = _sigmoid(a0_ref[...] + _dot(a_low, a2_ref[...]))
    g_o[...] = _dot(g_low, g2_ref[...])
    kk = k * kkg_ref[...]
    kk = kk / jnp.maximum(jnp.sqrt(_dot01_right(kk * kk, bd_ref[...])), 1e-12)
    r_o[...] = r
    v_o[...] = v
    kk_o[...] = kk
    for d in range(2):
        ad = a[:, d * w:(d + 1) * w]
        lw_o[d] = lw[:, d * w:(d + 1) * w]
        kd_o[d] = k * (1.0 + (ad - 1.0) * ka_ref[...])
        bb_o[d] = ad * kk


def _rw_prep(f_rw, pw, l, dims):
    m, cols = f_rw.shape
    w = dims.W
    tm = 256
    assert dims.C % tm == 0 and dims.S % tm == 0
    nb8 = m // 8
    lyr = lambda shape: pl.BlockSpec((None,) + shape, lambda i: (l,) + tuple(0 for _ in shape))
    out_w = pl.BlockSpec((tm, w), lambda i: (i, 0))
    out_2w = pl.BlockSpec((2, tm, w), lambda i: (0, i, 0))
    sd = jax.ShapeDtypeStruct
    return pl.pallas_call(
        functools.partial(_rw_prep_kernel, s=dims.S, c=dims.C, bs=dims.BS, tm=tm),
        grid=(m // tm,),
        in_specs=[pl.BlockSpec((tm, cols), lambda i: (i, 0)),
                  pl.BlockSpec((8, cols), lambda i: (jnp.maximum(i * (tm // 8) - 1, 0), 0)),
                  pl.BlockSpec((8, cols), lambda i: (jnp.minimum((i + 1) * (tm // 8), nb8 - 1), 0)),
                  lyr((2, cols)), lyr((1, 2 * w)), lyr((128, 2 * w)), lyr((1, 2 * w)), lyr((128, 2 * w)),
                  lyr((128, w)), lyr((1, w)), lyr((1, w)),
                  pl.BlockSpec((w, w), lambda i: (0, 0))],
        out_specs=[out_w, out_w, out_w, out_w, out_2w, out_2w, out_2w],
        out_shape=[sd((m, w), F32)] * 4 + [sd((2, m, w), F32)] * 3,
        compiler_params=_cp("parallel"),
    )(f_rw, f_rw, f_rw, pw["rw_mu"], pw["rw_w0"], pw["rw_w2"], pw["rw_a0"], pw["rw_a2"], pw["rw_g2"],
      pw["rw_kk"], pw["rw_ka"], _head_blockdiag(w, RW_HEAD_DIM))


def _rw_chunk_kernel(rf, vf, kkf, lwf, kdf, bbf, rb, vb, kkb, lwb, kdb, bbb, yf_ref, yb_ref, h_ref):
    c = RW_CHUNK

    @pl.when(pl.program_id(2) == 0)
    def _():
        h_ref[...] = jnp.zeros_like(h_ref)

    ii = lax.broadcasted_iota(I32, (c, c), 0)
    jj = lax.broadcasted_iota(I32, (c, c), 1)
    tri_f = jnp.where(jj <= ii, 1.0, 0.0).astype(BF16)
    tri_b = jnp.where(jj >= ii, 1.0, 0.0).astype(BF16)
    _, m_e, m_o = _half_masks(F32)

    def feats(r_ref, kk_ref, lw_ref, kd_ref, bb_ref, tri, last):
        lw = lw_ref[...]
        cum = _dot01_left(tri, lw)
        tot = cum[last:last + 1, :]
        w_inv = jnp.exp(-cum)
        w_end = jnp.exp(tot - cum)
        bb, kd = bb_ref[...], kd_ref[...]
        return dict(rt=r_ref[...] * jnp.exp(cum), at=-kk_ref[...] * jnp.exp(cum - lw),
                    bbar=bb * w_inv, kbar=kd * w_inv, bhat=bb * w_end, khat=kd * w_end, wtot=jnp.exp(tot))

    ff = feats(rf, kkf, lwf, kdf, bbf, tri_f, c - 1)
    fb = feats(rb, kkb, lwb, kdb, bbb, tri_b, 0)

    def stack(xf, xb):
        return jnp.concatenate([xf * m_e, xf * m_o, xb * m_e, xb * m_o], axis=0).astype(BF16)

    la, lr = stack(ff["at"], fb["at"]), stack(ff["rt"], fb["rt"])
    rbm, rkm = stack(ff["bbar"], fb["bbar"]), stack(ff["kbar"], fb["kbar"])
    vm = stack(vf[...], vb[...])
    n4 = 4 * c
    i4 = lax.broadcasted_iota(I32, (n4, n4), 0)
    j4 = lax.broadcasted_iota(I32, (n4, n4), 1)
    lag = jnp.where(i4 < 2 * c, i4 - j4, j4 - i4)
    lag = jnp.where((i4 // c) == (j4 // c), lag, -1)
    strict = lag > 0
    incl = lag >= 0
    nmat = jnp.where(strict, _dot_nt(la, rbm), 0.0)
    aak = jnp.where(strict, _dot_nt(la, rkm), 0.0)
    arb = jnp.where(incl, _dot_nt(lr, rbm), 0.0)
    ark = jnp.where(incl, _dot_nt(lr, rkm), 0.0)
    t_inv = jnp.where(i4 == j4, 1.0, 0.0) + nmat
    pw = nmat
    steps = int(np.log2(c)) - 1
    for _ in range(steps):
        pw = _dot(pw, pw)
        t_inv = t_inv + _dot(t_inv, pw)
    x = _dot(t_inv, la)
    u = _dot(t_inv, _dot(aak, vm))
    q = lr.astype(F32) + _dot(arb, x)
    y0 = _dot(arb, u) + _dot(ark, vm)
    eye = lax.broadcasted_iota(I32, (_LANES, _LANES), 0) == lax.broadcasted_iota(I32, (_LANES, _LANES), 1)
    for d, (fd, out_ref) in enumerate(((ff, yf_ref), (fb, yb_ref))):
        rows = slice(2 * c * d, 2 * c * (d + 1))
        h = h_ref[d]
        yst = _dot(q[rows], h) + y0[rows]
        out_ref[...] = yst[0:c] + yst[c:2 * c]
        bh = jnp.concatenate([fd["bhat"] * m_e, fd["bhat"] * m_o], axis=0)
        kh = jnp.concatenate([fd["khat"] * m_e, fd["khat"] * m_o], axis=0)
        trans = jnp.where(eye, fd["wtot"], 0.0) + _dot_tn(bh, x[rows])
        h_ref[d] = _dot(trans, h) + _dot_tn(bh, u[rows]) + _dot_tn(kh, vm[rows])


def _rw_scan(r, v, kk, lw, kd, bb, dims):
    b, s, c, w = dims.B, dims.S, dims.C, dims.W
    ch = RW_CHUNK
    n_c, n_s = c // ch, s // ch
    ctx0 = dims.BS // ch
    pairs = w // _LANES

    def cf(bi, t):
        return jnp.where(t < n_c, ctx0 + bi * n_c + t, bi * n_s + (t - n_c))

    def cb(bi, t):
        return jnp.where(t < n_c, ctx0 + bi * n_c + (n_c - 1 - t), bi * n_s + (n_s - 1 - (t - n_c)))

    sh = lambda fn: pl.BlockSpec((ch, _LANES), lambda bi, p, t: (fn(bi, t), p))
    dr = lambda fn, d: pl.BlockSpec((None, ch, _LANES), lambda bi, p, t: (d, fn(bi, t), p))
    m = dims.M
    return pl.pallas_call(
        _rw_chunk_kernel,
        grid=(b, pairs, n_c + n_s),
        in_specs=[sh(cf), sh(cf), sh(cf), dr(cf, 0), dr(cf, 0), dr(cf, 0),
                  sh(cb), sh(cb), sh(cb), dr(cb, 1), dr(cb, 1), dr(cb, 1)],
        out_specs=[sh(cf), sh(cb)],
        out_shape=[jax.ShapeDtypeStruct((m, w), F32)] * 2,
        scratch_shapes=[pltpu.VMEM((2, _LANES, _LANES), F32)],
        compiler_params=_cp("parallel", "parallel", "arbitrary"),
    )(r, v, kk, lw, kd, bb, r, v, kk, lw, kd, bb)


def _rw_readout_kernel(yf_ref, yb_ref, r_ref, v_ref, g_ref, kd_ref, rk_ref, gg_ref, gb_ref, bd_ref, o_ref):
    bd = bd_ref[...]
    inv = 1.0 / RW_HEAD_DIM
    y = yf_ref[...] + yb_ref[...]
    yc = y - _dot01_right(y, bd) * inv
    var = _dot01_right(yc * yc, bd) * inv
    yn = yc * lax.rsqrt(var + RW_GN_EPS)
    bonus = _dot01_right(r_ref[...] * (kd_ref[0] + kd_ref[1]) * rk_ref[...], bd) * v_ref[...]
    o_ref[...] = ((yn * gg_ref[...] + gb_ref[...] + bonus) * g_ref[...]).astype(o_ref.dtype)


def _rw_readout(yf, yb, r, v, g, kd, pw, l, dims, rows):
    w = dims.W
    tm = 256
    blk = pl.BlockSpec((tm, w), lambda i: (i, 0))
    lyr = pl.BlockSpec((None, 1, w), lambda i: (l, 0, 0))
    return pl.pallas_call(
        _rw_readout_kernel,
        grid=(rows // tm,),
        in_specs=[blk, blk, blk, blk, blk, pl.BlockSpec((2, tm, w), lambda i: (0, i, 0)), lyr, lyr, lyr,
                  pl.BlockSpec((w, w), lambda i: (0, 0))],
        out_specs=blk,
        out_shape=jax.ShapeDtypeStruct((rows, w), BF16),
        compiler_params=_cp("parallel"),
    )(yf, yb, r, v, g, kd, pw["rw_rk"], pw["rw_gn_g"], pw["rw_gn_b"], _head_blockdiag(w, RW_HEAD_DIM))


def _s5_matrices(a_re, a_im, log_dt, b_re, b_im, c_re, c_im, d_skip):
    hi = lax.Precision.HIGHEST
    n_g, n_p, n_j = b_re.shape[1], b_re.shape[2], b_re.shape[3]
    ct = S5_CHUNK
    dt = jnp.exp(log_dt)[..., None]
    mag = jnp.exp(a_re * dt)
    lam_re, lam_im = mag * jnp.cos(a_im * dt), mag * jnp.sin(a_im * dt)
    den = a_re * a_re + a_im * a_im
    f_re = ((lam_re - 1.0) * a_re + lam_im * a_im) / den
    f_im = (lam_im * a_re - (lam_re - 1.0) * a_im) / den
    bb_re = f_re[..., None] * b_re - f_im[..., None] * b_im
    bb_im = f_re[..., None] * b_im + f_im[..., None] * b_re
    p_re, p_im = [jnp.ones_like(lam_re)], [jnp.zeros_like(lam_im)]
    for _ in range(ct):
        p_re.append(p_re[-1] * lam_re - p_im[-1] * lam_im)
        p_im.append(p_re[-2] * lam_im + p_im[-1] * lam_re)
    p_re, p_im = jnp.stack(p_re), jnp.stack(p_im)
    cl_re = c_re[None] * p_re[:, :, :, None, :] - c_im[None] * p_im[:, :, :, None, :]
    cl_im = c_re[None] * p_im[:, :, :, None, :] + c_im[None] * p_re[:, :, :, None, :]
    kern = (jnp.einsum('ndgip,dgpm->ndgim', cl_re, bb_re, precision=hi)
            - jnp.einsum('ndgip,dgpm->ndgim', cl_im, bb_im, precision=hi))
    j_idx = np.arange(ct)[:, None]
    t_idx = np.arange(ct)[None, :]
    lag_f, lag_b = t_idx - j_idx, j_idx - t_idx
    toep_f = jnp.where((lag_f >= 0)[:, :, None, None, None], kern[np.clip(lag_f, 0, ct), 0], 0.0)
    toep_b = jnp.where((lag_b >= 0)[:, :, None, None, None], kern[np.clip(lag_b, 0, ct), 1], 0.0)
    toep = (toep_f + toep_b).transpose(2, 0, 4, 1, 3)
    skip = (jnp.eye(ct)[None, :, None, :, None] * jnp.eye(n_j)[None, None, :, None, :]
            * d_skip.reshape(n_g, 1, 1, 1, n_j))
    toep = (toep + skip).reshape(n_g, ct * n_j, ct * n_j)

    def state_in(n_of_j, d):
        pr, pi = p_re[n_of_j, d], p_im[n_of_j, d]
        re = pr[..., None] * bb_re[d][None] - pi[..., None] * bb_im[d][None]
        im = pr[..., None] * bb_im[d][None] + pi[..., None] * bb_re[d][None]
        fix = lambda z: z.transpose(1, 0, 3, 2).reshape(n_g, ct * n_j, n_p)
        return fix(re), fix(im)

    def state_out(n_of_t, d):
        fix = lambda z: z.transpose(1, 3, 0, 2).reshape(n_g, n_p, ct * n_j)
        return fix(cl_re[n_of_t, d]), fix(-cl_im[n_of_t, d])

    ar = np.arange(ct)
    rf_re, rf_im = state_in(ct - 1 - ar, 0)
    rb_re, rb_im = state_in(ar, 1)
    of_re, of_im = state_out(ar + 1, 0)
    ob_re, ob_im = state_out(ct - ar, 1)
    half = (np.arange(n_g) % 2)[:, None, None]

    def lane_half(z):
        zero = jnp.zeros_like(z)
        return jnp.concatenate([jnp.where(half == 0, z, zero), jnp.where(half == 1, z, zero)], axis=-1)

    def row_half(z):
        zero = jnp.zeros_like(z)
        return jnp.concatenate([jnp.where(half == 0, z, zero), jnp.where(half == 1, z, zero)], axis=1)

    r_re = jnp.stack([lane_half(rf_re), lane_half(rb_re)]).astype(BF16)
    r_im = jnp.stack([lane_half(rf_im), lane_half(rb_im)]).astype(BF16)
    o_re = jnp.stack([row_half(of_re), row_half(ob_re)]).astype(BF16)
    o_im = jnp.stack([row_half(of_im), row_half(ob_im)]).astype(BF16)
    lam_c = (p_re[ct].reshape(2, 1, n_g * n_p), p_im[ct].reshape(2, 1, n_g * n_p))
    return toep.astype(BF16), r_re, r_im, o_re, o_im, lam_c


def _s5_z_kernel(u_ref, rre_ref, rim_ref, zre_o, zim_o):
    u0, u1 = u_ref[0].astype(BF16), u_ref[1].astype(BF16)
    for d in range(2):
        zre_o[d] = (jnp.dot(u0, rre_ref[d, 0], preferred_element_type=F32)
                    + jnp.dot(u1, rre_ref[d, 1], preferred_element_type=F32))
        zim_o[d] = (jnp.dot(u0, rim_ref[d, 0], preferred_element_type=F32)
                    + jnp.dot(u1, rim_ref[d, 1], preferred_element_type=F32))


def _s5_scan_kernel(zre_ref, zim_ref, lre_ref, lim_ref, xre_o, xim_o, *, n_b, n_s, n_c):
    lanes = zre_ref.shape[2]
    chains = [(d, b) for d in range(2) for b in range(n_b)]
    lam = [(lre_ref[d], lim_ref[d]) for d in range(2)]

    def run(base_of, length, carry):
        def body(t, carry):
            new = []
            for (d, b), (xr, xi) in zip(chains, carry):
                row = base_of(b) + (t if d == 0 else length - 1 - t)
                xre_o[d, pl.ds(row, 1), :] = xr
                xim_o[d, pl.ds(row, 1), :] = xi
                lr, li = lam[d]
                new.append((lr * xr - li * xi + zre_ref[d, pl.ds(row, 1), :],
                            lr * xi + li * xr + zim_ref[d, pl.ds(row, 1), :]))
            return tuple(new)
        return lax.fori_loop(0, length, body, carry)

    zero = jnp.zeros((1, lanes), F32)
    carry = tuple((zero, zero) for _ in chains)
    carry = run(lambda b: n_b * n_s + b * n_c, n_c, carry)
    run(lambda b: b * n_s, n_s, carry)


def _s5_y_kernel(u_ref, toep_ref, xre_ref, xim_ref, ore_ref, oim_ref, y_o):
    xr = [xre_ref[d].astype(BF16) for d in range(2)]
    xi = [xim_ref[d].astype(BF16) for d in range(2)]
    for g in range(2):
        y = jnp.dot(u_ref[g].astype(BF16), toep_ref[g], preferred_element_type=F32)
        for d in range(2):
            y = y + jnp.dot(xr[d], ore_ref[d, g], preferred_element_type=F32)
            y = y + jnp.dot(xi[d], oim_ref[d, g], preferred_element_type=F32)
        y_o[g] = y


def _s5_glu_kernel(y_ref, w_ref, b_ref, o_ref, wbf_ref):
    @pl.when(pl.program_id(0) == 0)
    def _():
        wbf_ref[...] = w_ref[...].astype(BF16)

    y = y_ref[...]
    y = 0.5 * y * (1.0 + jnp.tanh(0.7978845608028654 * (y + 0.044715 * y * y * y)))
    z = jnp.dot(y.astype(BF16), wbf_ref[...], preferred_element_type=F32) + b_ref[...]
    o_ref[...] = (y * _sigmoid(z)).astype(o_ref.dtype)


def _s5_mixer(f_s5, mats, w_glu, b_glu3, l, dims, rows):
    toep, r_re, r_im, o_re, o_im, (lam_re, lam_im) = mats
    m, w = f_s5.shape
    ct = S5_CHUNK
    n_g = w // S5_GROUP
    n_ch = m // ct
    gw = ct * S5_GROUP
    u = f_s5.reshape(n_ch, ct, n_g, S5_GROUP).transpose(2, 0, 1, 3).reshape(n_g, n_ch, gw)
    sl = n_g * S5_STATE
    sd = jax.ShapeDtypeStruct
    zre, zim = pl.pallas_call(
        _s5_z_kernel,
        grid=(n_g // 2,),
        in_specs=[pl.BlockSpec((2, n_ch, gw), lambda g: (g, 0, 0)),
                  pl.BlockSpec((2, 2, gw, _LANES), lambda g: (0, g, 0, 0)),
                  pl.BlockSpec((2, 2, gw, _LANES), lambda g: (0, g, 0, 0))],
        out_specs=[pl.BlockSpec((2, n_ch, _LANES), lambda g: (0, 0, g))] * 2,
        out_shape=[sd((2, n_ch, sl), F32)] * 2,
        compiler_params=_cp("parallel"),
    )(u, r_re, r_im)
    lb = 512
    xre, xim = pl.pallas_call(
        functools.partial(_s5_scan_kernel, n_b=dims.B, n_s=dims.S // ct, n_c=dims.C // ct),
        grid=(sl // lb,),
        in_specs=[pl.BlockSpec((2, n_ch, lb), lambda j: (0, 0, j))] * 2
                 + [pl.BlockSpec((2, 1, lb), lambda j: (0, 0, j))] * 2,
        out_specs=[pl.BlockSpec((2, n_ch, lb), lambda j: (0, 0, j))] * 2,
        out_shape=[sd((2, n_ch, sl), F32)] * 2,
        compiler_params=_cp("parallel"),
    )(zre, zim, lam_re, lam_im)
    y = pl.pallas_call(
        _s5_y_kernel,
        grid=(n_g // 2,),
        in_specs=[pl.BlockSpec((2, n_ch, gw), lambda g: (g, 0, 0)),
                  pl.BlockSpec((2, gw, gw), lambda g: (g, 0, 0)),
                  pl.BlockSpec((2, n_ch, _LANES), lambda g: (0, 0, g)),
                  pl.BlockSpec((2, n_ch, _LANES), lambda g: (0, 0, g)),
                  pl.BlockSpec((2, 2, _LANES, gw), lambda g: (0, g, 0, 0)),
                  pl.BlockSpec((2, 2, _LANES, gw), lambda g: (0, g, 0, 0))],
        out_specs=pl.BlockSpec((2, n_ch, gw), lambda g: (g, 0, 0)),
        out_shape=sd((n_g, n_ch, gw), F32),
        compiler_params=_cp("parallel"),
    )(u, toep, xre, xim, o_re, o_im)
    y = y.reshape(n_g, n_ch, ct, S5_GROUP).transpose(1, 2, 0, 3).reshape(m, w)
    tm = 256
    return pl.pallas_call(
        _s5_glu_kernel,
        grid=(rows // tm,),
        in_specs=[pl.BlockSpec((tm, w), lambda i: (i, 0)),
                  pl.BlockSpec((None, w, w), lambda i: (l, 0, 0)),
                  pl.BlockSpec((None, 1, w), lambda i: (l, 0, 0))],
        out_specs=pl.BlockSpec((tm, w), lambda i: (i, 0)),
        out_shape=sd((rows, w), BF16),
        scratch_shapes=[pltpu.VMEM((w, w), BF16)],
        compiler_params=_cp("arbitrary"),
    )(y, w_glu, b_glu3)


def _merge_kernel(xn_ref, ya_ref, yb_ref, yc_ref, yd_ref, wg0, wg1, wg2, wg3, bg0, bg1, bg2, bg3, wb_ref,
                  o_ref, wg_bf, wb_bf):
    wgs = (wg0, wg1, wg2, wg3)

    @pl.when(pl.program_id(1) == 0)
    def _():
        for n in range(4):
            wg_bf[n] = wgs[n][...].astype(BF16)
            wb_bf[n] = wb_ref[n].astype(BF16)

    xn = xn_ref[...]
    acc = None
    for n, (y_ref, bg) in enumerate(zip((ya_ref, yb_ref, yc_ref, yd_ref), (bg0, bg1, bg2, bg3))):
        gate = _sigmoid(jnp.dot(xn, wg_bf[n], preferred_element_type=F32) + bg[...])
        up = jnp.dot(y_ref[...], wb_bf[n], preferred_element_type=F32)
        acc = gate * up if acc is None else acc + gate * up
    o_ref[...] = acc.astype(o_ref.dtype)


def _merge(xn, ys, w_gate, b_gate3, w_branch, l, rows):
    d = xn.shape[1]
    w = ys[0].shape[1]
    tn = 256
    tm = _pick(rows, (1088, 1024, 512, 256))
    nb = d // tn
    wg = lambda n: pl.BlockSpec((None, d, tn), lambda j, i: (l, 0, n * nb + j))
    bg = lambda n: pl.BlockSpec((None, 1, tn), lambda j, i: (l, 0, n * nb + j))
    yspec = pl.BlockSpec((tm, w), lambda j, i: (i, 0))
    return pl.pallas_call(
        _merge_kernel,
        grid=(nb, rows // tm),
        in_specs=[pl.BlockSpec((tm, d), lambda j, i: (i, 0)), yspec, yspec, yspec, yspec,
                  wg(0), wg(1), wg(2), wg(3), bg(0), bg(1), bg(2), bg(3),
                  pl.BlockSpec((None, 4, w, tn), lambda j, i: (l, 0, 0, j))],
        out_specs=pl.BlockSpec((tm, tn), lambda j, i: (i, j)),
        out_shape=jax.ShapeDtypeStruct((rows, d), BF16),
        scratch_shapes=[pltpu.VMEM((4, d, tn), BF16), pltpu.VMEM((4, w, tn), BF16)],
        compiler_params=_cp("arbitrary", "arbitrary"),
    )(xn, *ys, w_gate, w_gate, w_gate, w_gate, b_gate3, b_gate3, b_gate3, b_gate3, w_branch)


def _router_kernel(h_ref, wr_ref, br_ref, idx_o, gate_o, rank_o, cnt_o, carry):
    tm = h_ref.shape[0]

    @pl.when(pl.program_id(0) == 0)
    def _():
        carry[...] = jnp.zeros_like(carry)

    h_hi, h_lo = _split(h_ref[...], 2)
    w_hi, w_lo = _split(wr_ref[...], 2)
    logits = (jnp.dot(h_hi, w_hi, preferred_element_type=F32) + jnp.dot(h_lo, w_hi, preferred_element_type=F32)
              + jnp.dot(h_hi, w_lo, preferred_element_type=F32)) + br_ref[...]
    lane = lax.broadcasted_iota(I32, (tm, _LANES), 1)
    vals, idxs, hots = [], [], []
    cur = logits
    for _ in range(TOP_K):
        mx = jnp.max(cur, axis=-1, keepdims=True)
        ix = jnp.min(jnp.where(cur == mx, lane, _LANES), axis=-1, keepdims=True)
        hot = lane == ix
        vals.append(mx)
        idxs.append(ix)
        hots.append(hot)
        cur = jnp.where(hot, -jnp.inf, cur)
    ex = [jnp.exp(v - vals[0]) for v in vals]
    inv = 1.0 / sum(ex)
    sel = sum(jnp.where(h, 1.0, 0.0) for h in hots)
    ri = lax.broadcasted_iota(I32, (tm, tm), 0)
    ci = lax.broadcasted_iota(I32, (tm, tm), 1)
    earlier = jnp.where(ci < ri, 1.0, 0.0).astype(BF16)
    before = jnp.dot(earlier, sel.astype(BF16), preferred_element_type=F32) + carry[...]
    idx_out = jnp.zeros((tm, _LANES), I32)
    gate_out = jnp.zeros((tm, _LANES), F32)
    rank_out = jnp.zeros((tm, _LANES), I32)
    for k in range(TOP_K):
        rk = jnp.sum(jnp.where(hots[k], before, 0.0), axis=-1, keepdims=True)
        idx_out = jnp.where(lane == k, idxs[k], idx_out)
        gate_out = jnp.where(lane == k, ex[k] * inv, gate_out)
        rank_out = jnp.where(lane == k, rk.astype(I32), rank_out)
    idx_o[...] = idx_out
    gate_o[...] = gate_out
    rank_o[...] = rank_out
    carry[...] = carry[...] + jnp.sum(sel, axis=0, keepdims=True)
    cnt_o[...] = jnp.broadcast_to(carry[...], cnt_o.shape)


def _router(h, wr_pad, br_pad, l, rows):
    d = h.shape[1]
    tm = 256
    blk = pl.BlockSpec((tm, _LANES), lambda i: (i, 0))
    sd = jax.ShapeDtypeStruct
    return pl.pallas_call(
        _router_kernel,
        grid=(rows // tm,),
        in_specs=[pl.BlockSpec((tm, d), lambda i: (i, 0)),
                  pl.BlockSpec((None, d, _LANES), lambda i: (l, 0, 0)),
                  pl.BlockSpec((None, 1, _LANES), lambda i: (l, 0, 0))],
        out_specs=[blk, blk, blk, pl.BlockSpec((8, _LANES), lambda i: (0, 0))],
        out_shape=[sd((rows, _LANES), I32), sd((rows, _LANES), F32), sd((rows, _LANES), I32),
                   sd((8, _LANES), F32)],
        scratch_shapes=[pltpu.VMEM((1, _LANES), F32)],
        compiler_params=_cp("arbitrary"),
    )(h, wr_pad, br_pad)


def _row_copy(src_hbm, row, dst, slot, sem):
    return pltpu.make_async_copy(src_hbm.at[pl.ds(row, 1), :], dst.at[pl.ds(slot, 1), :], sem)


def _gather_kernel(tok_ref, h_hbm, o_ref, sem):
    n = o_ref.shape[0]

    def issue(r, carry):
        _row_copy(h_hbm, tok_ref[0, r], o_ref, r, sem).start()
        return carry

    lax.fori_loop(0, n, issue, 0)
    pltpu.make_async_copy(h_hbm.at[pl.ds(0, n), :], o_ref, sem).wait()


def _moe_gather(h, slot_tok3):
    n_blocks, _, blk = slot_tok3.shape
    d = h.shape[1]
    return pl.pallas_call(
        _gather_kernel,
        grid=(n_blocks,),
        in_specs=[pl.BlockSpec((None, 1, blk), lambda i: (i, 0, 0), memory_space=pltpu.SMEM),
                  pl.BlockSpec(memory_space=pl.ANY)],
        out_specs=pl.BlockSpec((blk, d), lambda i: (i, 0)),
        out_shape=jax.ShapeDtypeStruct((n_blocks * blk, d), F32),
        scratch_shapes=[pltpu.SemaphoreType.DMA(())],
        compiler_params=_cp("arbitrary"),
    )(slot_tok3, h)


def _expert_kernel(be_ref, first_ref, used_ref, x_ref, wgu_ref, bgu_ref, wdn_ref, bdn_ref, o_ref,
                   wgu_bf, wdn_bf):
    i = pl.program_id(0)

    @pl.when(first_ref[i] == 1)
    def _():
        wgu_bf[...] = wgu_ref[...].astype(BF16)
        wdn_bf[...] = wdn_ref[...].astype(BF16)

    @pl.when(i < used_ref[0])
    def _():
        gu = jnp.dot(x_ref[...].astype(BF16), wgu_bf[...], preferred_element_type=F32) + bgu_ref[...]
        g_lin = jnp.minimum(gu[:, :D_EXPERT], SWIGLU_LIMIT)
        u_lin = jnp.clip(gu[:, D_EXPERT:], -SWIGLU_LIMIT, SWIGLU_LIMIT)
        act = (u_lin + 1.0) * g_lin * _sigmoid(SWIGLU_ALPHA * g_lin)
        o_ref[...] = jnp.dot(act.astype(BF16), wdn_bf[...], preferred_element_type=F32) + bdn_ref[...]

    @pl.when(i >= used_ref[0])
    def _():
        o_ref[...] = jnp.zeros_like(o_ref)


def _moe_experts(xs, block_e, first, n_used, w_gu, b_gu4, w_dn, b_dn4, l):
    n_slots, d = xs.shape
    blk = MOE_BLOCK
    n_blocks = n_slots // blk
    de2 = w_gu.shape[3]
    grid_spec = pltpu.PrefetchScalarGridSpec(
        num_scalar_prefetch=3,
        grid=(n_blocks,),
        in_specs=[pl.BlockSpec((blk, d), lambda i, be, fi, us: (i, 0)),
                  pl.BlockSpec((None, None, d, de2), lambda i, be, fi, us: (l, be[i], 0, 0)),
                  pl.BlockSpec((None, None, 1, de2), lambda i, be, fi, us: (l, be[i], 0, 0)),
                  pl.BlockSpec((None, None, de2 // 2, d), lambda i, be, fi, us: (l, be[i], 0, 0)),
                  pl.BlockSpec((None, None, 1, d), lambda i, be, fi, us: (l, be[i], 0, 0))],
        out_specs=pl.BlockSpec((blk, d), lambda i, be, fi, us: (i, 0)),
        scratch_shapes=[pltpu.VMEM((d, de2), BF16), pltpu.VMEM((de2 // 2, d), BF16)])
    return pl.pallas_call(
        _expert_kernel,
        grid_spec=grid_spec,
        out_shape=jax.ShapeDtypeStruct((n_slots, d), F32),
        compiler_params=_cp("arbitrary"),
    )(block_e, first, n_used, xs, w_gu, b_gu4, w_dn, b_dn4)


def _combine_kernel(dest_ref, gate_ref, x_ref, g2_ref, ys_hbm, o_ref, buf, sem):
    tm = x_ref.shape[0]

    def issue(r, carry):
        for k in range(TOP_K):
            _row_copy(ys_hbm, dest_ref[0, r * TOP_K + k], buf.at[k], r, sem).start()
        return carry

    lax.fori_loop(0, tm, issue, 0)
    for k in range(TOP_K):
        pltpu.make_async_copy(ys_hbm.at[pl.ds(0, tm), :], buf.at[k], sem).wait()
    gate = gate_ref[...]
    f = None
    for k in range(TOP_K):
        term = gate[:, k:k + 1] * buf[k]
        f = term if f is None else f + term
    o_ref[...] = x_ref[...] + g2_ref[...] * f


def _moe_combine(x, ys, dest3, gate, mod, dims, rows):
    d = x.shape[1]
    tm = MOE_COMBINE_ROWS
    seg = dims.seg_fn(tm)
    return pl.pallas_call(
        _combine_kernel,
        grid=(rows // tm,),
        in_specs=[pl.BlockSpec((None, 1, tm * TOP_K), lambda i: (i, 0, 0), memory_space=pltpu.SMEM),
                  pl.BlockSpec((tm, _LANES), lambda i: (i, 0)),
                  pl.BlockSpec((tm, d), lambda i: (i, 0)),
                  pl.BlockSpec((None, 1, d), lambda i: (seg(i), 0, 5)),
                  pl.BlockSpec(memory_space=pl.ANY)],
        out_specs=pl.BlockSpec((tm, d), lambda i: (i, 0)),
        out_shape=jax.ShapeDtypeStruct((rows, d), F32),
        scratch_shapes=[pltpu.VMEM((TOP_K, tm, d), F32), pltpu.SemaphoreType.DMA(())],
        compiler_params=_cp("arbitrary"),
    )(dest3, gate, x, mod, ys)


def _moe(x, h, mod, pw, l, dims, rows):
    idx, gate, rank, cnt = _router(h, pw["moe_wr"], pw["moe_br"], l, rows)
    blk = MOE_BLOCK
    n_blocks = -(-(rows * TOP_K) // blk) + N_EXPERTS
    counts = cnt[0, :N_EXPERTS].astype(I32)
    padded = (counts + blk - 1) // blk * blk
    pad_end = jnp.cumsum(padded)
    pad_start = pad_end - padded
    top_i = idx[:, :TOP_K]
    dest = pad_start[top_i] + rank[:, :TOP_K]
    tok = jnp.broadcast_to(jnp.arange(rows, dtype=I32)[:, None], dest.shape)
    slot_tok = jnp.zeros((n_blocks * blk,), I32).at[dest.reshape(-1)].set(tok.reshape(-1))
    block_e = jnp.minimum(jnp.searchsorted(pad_end, jnp.arange(n_blocks, dtype=I32) * blk, side='right'),
                          N_EXPERTS - 1).astype(I32)
    first = jnp.concatenate([jnp.ones((1,), I32), (block_e[1:] != block_e[:-1]).astype(I32)])
    n_used = (pad_end[-1:] // blk).astype(I32)
    xs = _moe_gather(h, slot_tok.reshape(n_blocks, 1, blk))
    ys = _moe_experts(xs, block_e, first, n_used, pw["moe_w_gu"], pw["moe_b_gu"], pw["moe_w_dn"],
                      pw["moe_b_dn"], l)
    dest3 = dest.reshape(rows // MOE_COMBINE_ROWS, 1, MOE_COMBINE_ROWS * TOP_K)
    return _moe_combine(x, ys, dest3, gate, mod, dims, rows)


def _prep_params(p, dims):
    n_l = p["w_in"].shape[0]
    w = dims.W
    o_mla = 3 * w + (3 * w + 384)
    o_s5 = o_mla + MLA_Q_RANK + MLA_KV_RANK + MLA_ROPE
    pw = dict(p)
    row3 = lambda a: a.reshape(a.shape[0], 1, -1)
    pw["norm1_g"], pw["norm2_g"] = row3(p["norm1_g"]), row3(p["norm2_g"])
    pw["b_mod"], pw["b_gate"] = row3(p["b_mod"]), row3(p["b_gate"])
    pw["w_in_mla"] = p["w_in"][:, :, o_mla:o_s5]
    pw["w_in_s5"] = p["w_in"][:, :, o_s5:]
    zeros = jnp.zeros_like(p["rw_w2"][:, 0])
    bdiag = lambda x: jnp.concatenate([jnp.concatenate([x[:, 0], zeros], axis=-1),
                                       jnp.concatenate([zeros, x[:, 1]], axis=-1)], axis=1)
    pw["rw_w2"], pw["rw_a2"] = bdiag(p["rw_w2"]), bdiag(p["rw_a2"])
    pw["rw_w0"], pw["rw_a0"] = row3(p["rw_w0"]), row3(p["rw_a0"])
    pw["rw_kk"], pw["rw_ka"] = row3(p["rw_kk"]), row3(p["rw_ka"])
    pw["rw_rk"], pw["rw_gn_g"], pw["rw_gn_b"] = row3(p["rw_rk"]), row3(p["rw_gn_g"]), row3(p["rw_gn_b"])
    pad_h = lambda x, n: jnp.pad(x, ((0, 0), (0, 0), (0, 0), (0, _LANES - n)))
    wq = p["mla_wq_up"].reshape(n_l, MLA_Q_RANK, MLA_HEADS, MLA_QK)
    pw["mla_wq"] = pad_h(wq, MLA_QK).reshape(n_l, MLA_Q_RANK, -1).astype(BF16)
    wkv = p["mla_wkv_up"].reshape(n_l, MLA_KV_RANK, MLA_HEADS, MLA_NOPE + MLA_V)
    pw["mla_wk"] = pad_h(wkv[..., :MLA_NOPE], MLA_NOPE).reshape(n_l, MLA_KV_RANK, -1).astype(BF16)
    pw["mla_wv"] = wkv[..., MLA_NOPE:].reshape(n_l, MLA_KV_RANK, -1).astype(BF16)
    pw["mla_place"] = jnp.asarray(np.arange(MLA_ROPE)[:, None] + MLA_NOPE == np.arange(_LANES)[None, :], BF16)
    pw["mla_gcq"], pw["mla_gckv"] = row3(p["mla_qn_g"]), row3(p["mla_kvn_g"])
    pad_g = lambda g: jnp.pad(g, ((0, 0), (0, _LANES - MLA_QK))).reshape(n_l, 1, _LANES)
    pw["mla_gq"] = pad_g(p["mla_qkn_q"] * MLA_QK ** -0.5)
    pw["mla_gk"] = pad_g(p["mla_qkn_k"])
    pw["s5_b_glu"] = row3(p["s5_b_glu"])
    pw["moe_wr"] = jnp.pad(p["moe_wr"], ((0, 0), (0, 0), (0, _LANES - N_EXPERTS)))
    pw["moe_br"] = jnp.pad(p["moe_br"], ((0, 0), (0, _LANES - N_EXPERTS)), constant_values=-1e30)[:, None, :]
    pw["moe_b_gu"] = p["moe_b_gu"][:, :, None, :]
    pw["moe_b_dn"] = p["moe_b_dn"][:, :, None, :]
    return pw


def _residual_epilogue(acc, x, g):
    return x + g * acc


def _layer(xs, l, mod, pw, ropes, dims, need_ctx):
    m, d, w = dims.M, dims.D, dims.W
    rows = m if need_ctx else dims.BS
    tm_all = _pick(m, (1088, 1024, 512, 256))
    xn = _norm_mod(xs, pw["norm1_g"], l, mod, 0, 1, dims, m, BF16)
    f_na = _mm(xn, pw["w_in"], l, tm=tm_all, tn=512, col0=0, ncols=3 * w)
    f_rw = _mm(xn, pw["w_in"], l, tm=tm_all, tn=384, col0=3 * w, ncols=3 * w + 384)
    f_mla = _mm(xn, pw["w_in_mla"], l, tm=tm_all, tn=pw["w_in_mla"].shape[2])
    f_s5 = _mm(xn, pw["w_in_s5"], l, tm=tm_all, tn=w)

    qkv = _na_prep(f_na, pw["na_qn_g"][l], pw["na_kn_g"][l], dims)
    ya = _na_attention(qkv, _na_bias_tables(pw["na_rpb"][l], dims.S // GRID_W), dims, need_ctx)

    r, v, kk, g, lw, kd, bb = _rw_prep(f_rw, pw, l, dims)
    yf, yb_ = _rw_scan(r, v, kk, lw, kd, bb, dims)
    yb = _rw_readout(yf, yb_, r, v, g, kd, pw, l, dims, rows)

    q, k, vv = _mla_prep(f_mla, pw, l, ropes, dims)
    yc = _mla_attention(q, k, vv, dims, need_ctx)

    mats = _s5_matrices(pw["s5_a_re"][l], pw["s5_a_im"][l], pw["s5_log_dt"][l], pw["s5_b_re"][l],
                        pw["s5_b_im"][l], pw["s5_c_re"][l], pw["s5_c_im"][l], pw["s5_d"][l])
    yd = _s5_mixer(f_s5, mats, pw["s5_w_glu"], pw["s5_b_glu"], l, dims, rows)

    merged = _merge(xn, (ya, yb, yc, yd), pw["w_gate"], pw["b_gate"], pw["w_branch"], l, rows)
    tm_seg = _pick(dims.S, (512, 256))
    seg = dims.seg_fn(tm_seg)
    tn = 512
    xs = _mm(merged, pw["w_out"], l, tm=tm_seg, tn=tn, rows=rows, epilogue=_residual_epilogue,
             extras=((xs, pl.BlockSpec((tm_seg, tn), lambda j, i: (i, j))),
                     (mod, pl.BlockSpec((None, 1, tn), lambda j, i: (seg(i), 0, 2 * (d // tn) + j)))))
    h = _norm_mod(xs, pw["norm2_g"], l, mod, 3, 4, dims, rows, F32)
    return _moe(xs, h, mod, pw, l, dims, rows)


def kernel(x, c, ctx, c_ctx, w_mod, b_mod, norm1_g, norm2_g, w_in, w_gate, b_gate, w_branch, w_out, na_qn_g, na_kn_g, na_rpb, rw_mu, rw_w0, rw_w2, rw_a0, rw_a2, rw_g2, rw_kk, rw_ka, rw_rk, rw_gn_g, rw_gn_b, mla_qn_g, mla_wq_up, mla_kvn_g, mla_wkv_up, mla_qkn_q, mla_qkn_k, s5_a_re, s5_a_im, s5_log_dt, s5_b_re, s5_b_im, s5_c_re, s5_c_im, s5_d, s5_w_glu, s5_b_glu, moe_wr, moe_br, moe_w_gu, moe_b_gu, moe_w_dn, moe_b_dn):
    params = dict(w_mod=w_mod, b_mod=b_mod, norm1_g=norm1_g, norm2_g=norm2_g, w_in=w_in, w_gate=w_gate,
                  b_gate=b_gate, w_branch=w_branch, w_out=w_out, na_qn_g=na_qn_g, na_kn_g=na_kn_g,
                  na_rpb=na_rpb, rw_mu=rw_mu, rw_w0=rw_w0, rw_w2=rw_w2, rw_a0=rw_a0, rw_a2=rw_a2,
                  rw_g2=rw_g2, rw_kk=rw_kk, rw_ka=rw_ka, rw_rk=rw_rk, rw_gn_g=rw_gn_g, rw_gn_b=rw_gn_b,
                  mla_qn_g=mla_qn_g, mla_wq_up=mla_wq_up, mla_kvn_g=mla_kvn_g, mla_wkv_up=mla_wkv_up,
                  mla_qkn_q=mla_qkn_q, mla_qkn_k=mla_qkn_k, s5_a_re=s5_a_re, s5_a_im=s5_a_im,
                  s5_log_dt=s5_log_dt, s5_b_re=s5_b_re, s5_b_im=s5_b_im, s5_c_re=s5_c_re, s5_c_im=s5_c_im,
                  s5_d=s5_d, s5_w_glu=s5_w_glu, s5_b_glu=s5_b_glu, moe_wr=moe_wr, moe_br=moe_br,
                  moe_w_gu=moe_w_gu, moe_b_gu=moe_b_gu, moe_w_dn=moe_w_dn, moe_b_dn=moe_b_dn)
    b, s, d = x.shape
    c_len = ctx.shape[1]
    depth = w_mod.shape[0]
    dims = _Dims(b, s, c_len, d)
    assert b + 1 <= 8
    pw = _prep_params(params, dims)
    cond = jnp.concatenate([jax.nn.silu(c), jax.nn.silu(c_ctx)[None],
                            jnp.zeros((8 - b - 1, d), F32)], axis=0)
    ropes = _rope_tables(s, 256)
    xs = jnp.concatenate([x.reshape(b * s, d), ctx.reshape(b * c_len, d)], axis=0)
    for l in range(depth):
        mod = _mm(cond, pw["w_mod"], l, tm=8, tn=1024, bias=pw["b_mod"]).reshape(8, 1, 6 * d)
        xs = _layer(xs, l, mod, pw, ropes, dims, need_ctx=l < depth - 1)
    return xs[:b * s].reshape(b, s, d)
```

```python
import functools

import numpy as np
import jax
import jax.numpy as jnp
from jax import lax
from jax.experimental import pallas as pl
from jax.experimental.pallas import tpu as pltpu

F32 = jnp.float32
BF16 = jnp.bfloat16
I32 = jnp.int32

_VMEM_LIMIT_BYTES = 56 * 1024 * 1024
_LANES = 128
_LOG2E = 1.4426950408889634

NORM_EPS = 1e-6
GRID_W = 64
NA_HEADS = 8
NA_HEAD_DIM = 64
NA_WIN_R = 8
NA_WIN_C = 16
NA_QROWS = 4
NA_BAND_ROWS = NA_QROWS + NA_WIN_R - 1
NA_HEADS_TOGETHER = 2
RW_HEAD_DIM = 64
RW_GN_EPS = 64e-5
RW_CHUNK = 64
RW_PAIRS_PER_STEP = 4
MLA_HEADS = 8
MLA_NOPE = 64
MLA_ROPE = 32
MLA_QK = MLA_NOPE + MLA_ROPE
MLA_V = 64
MLA_Q_RANK = 512
MLA_KV_RANK = 256
ROPE_THETA = 10000.0
MLA_KEY_CHUNK = 256
MLA_Q_TILE = 512
S5_GROUP = 16
S5_STATE = 64
S5_CHUNK = 16
N_EXPERTS = 32
TOP_K = 4
D_EXPERT = 512
SWIGLU_LIMIT = 7.0
SWIGLU_ALPHA = 1.702
MOE_BLOCK = 256
MOE_COMBINE_ROWS = 128
_DMA_UNROLL = 16

_NT = (((1,), (1,)), ((), ()))
_TN = (((0,), (0,)), ((), ()))


def _cp(*sem):
    return pltpu.CompilerParams(dimension_semantics=sem, vmem_limit_bytes=_VMEM_LIMIT_BYTES)


def _dot(a, b):
    return jnp.dot(a.astype(BF16), b.astype(BF16), preferred_element_type=F32)


def _dot_nt(a, b):
    return lax.dot_general(a.astype(BF16), b.astype(BF16), _NT, preferred_element_type=F32)


def _dot_tn(a, b):
    return lax.dot_general(a.astype(BF16), b.astype(BF16), _TN, preferred_element_type=F32)


def _split(x, parts):
    out = []
    for _ in range(parts):
        h = x.astype(BF16)
        out.append(h)
        x = x - h.astype(F32)
    return out


def _dot01_right(x, m01, parts=2):
    return sum(jnp.dot(h, m01, preferred_element_type=F32) for h in _split(x, parts))


def _dot01_left(m01, x, parts=3):
    return sum(jnp.dot(m01, h, preferred_element_type=F32) for h in _split(x, parts))


def _sigmoid(x):
    return 1.0 / (1.0 + jnp.exp(-x))


def _softplus(x):
    return jnp.maximum(x, 0.0) + jnp.log(1.0 + jnp.exp(-jnp.abs(x)))


def _pick(n, cands):
    for c in cands:
        if n % c == 0:
            return c
    raise ValueError(f"no tile for {n} in {cands}")


def _head_blockdiag(width, head):
    i = np.arange(width)
    return jnp.asarray((i[:, None] // head) == (i[None, :] // head), BF16)


def _mm_kernel(*refs, has_bias, n_extra, epilogue):
    a_ref, w_ref = refs[0], refs[1]
    pos = 2
    bias_ref = None
    if has_bias:
        bias_ref = refs[pos]
        pos += 1
    extra = refs[pos:pos + n_extra]
    o_ref, wbf_ref = refs[pos + n_extra], refs[pos + n_extra + 1]

    @pl.when(pl.program_id(1) == 0)
    def _():
        wbf_ref[...] = w_ref[...].astype(BF16)

    acc = jnp.dot(a_ref[...].astype(BF16), wbf_ref[...], preferred_element_type=F32)
    if has_bias:
        acc = acc + bias_ref[...]
    if epilogue is not None:
        acc = epilogue(acc, *[e[...] for e in extra])
    o_ref[...] = acc.astype(o_ref.dtype)


def _mm(a, w, l, *, tm, tn, col0=0, ncols=None, rows=None, bias=None, extras=(), epilogue=None,
        out_dtype=F32):
    k = a.shape[1]
    rows = a.shape[0] if rows is None else rows
    ncols = w.shape[2] - col0 if ncols is None else ncols
    assert col0 % tn == 0 and ncols % tn == 0 and rows % tm == 0 and w.shape[1] == k
    c0 = col0 // tn
    in_specs = [pl.BlockSpec((tm, k), lambda j, i: (i, 0)),
                pl.BlockSpec((None, k, tn), lambda j, i: (l, 0, j + c0))]
    args = [a, w]
    if bias is not None:
        in_specs.append(pl.BlockSpec((None, 1, tn), lambda j, i: (l, 0, j + c0)))
        args.append(bias)
    for arr, spec in extras:
        in_specs.append(spec)
        args.append(arr)
    return pl.pallas_call(
        functools.partial(_mm_kernel, has_bias=bias is not None, n_extra=len(extras), epilogue=epilogue),
        grid=(ncols // tn, rows // tm),
        in_specs=in_specs,
        out_specs=pl.BlockSpec((tm, tn), lambda j, i: (i, j)),
        out_shape=jax.ShapeDtypeStruct((rows, ncols), out_dtype),
        scratch_shapes=[pltpu.VMEM((k, tn), BF16)],
        compiler_params=_cp("arbitrary", "arbitrary"),
    )(*args)


def _norm_mod_kernel(x_ref, g_ref, sc_ref, sh_ref, o_ref):
    x = x_ref[...]
    y = x * lax.rsqrt(jnp.mean(x * x, axis=-1, keepdims=True) + NORM_EPS) * g_ref[...]
    o_ref[...] = (y * (1.0 + sc_ref[...]) + sh_ref[...]).astype(o_ref.dtype)


def _norm_mod(x, gain3, l, mod, which_sh, which_sc, dims, rows, out_dtype):
    d = x.shape[1]
    tm = 256
    seg = dims.seg_fn(tm)
    return pl.pallas_call(
        _norm_mod_kernel,
        grid=(rows // tm,),
        in_specs=[pl.BlockSpec((tm, d), lambda i: (i, 0)),
                  pl.BlockSpec((None, 1, d), lambda i: (l, 0, 0)),
                  pl.BlockSpec((None, 1, d), lambda i: (seg(i), 0, which_sc)),
                  pl.BlockSpec((None, 1, d), lambda i: (seg(i), 0, which_sh))],
        out_specs=pl.BlockSpec((tm, d), lambda i: (i, 0)),
        out_shape=jax.ShapeDtypeStruct((rows, d), out_dtype),
        compiler_params=_cp("parallel"),
    )(x, gain3, mod, mod)


class _Dims:
    def __init__(self, b, s, c, d):
        self.B, self.S, self.C, self.D = b, s, c, d
        self.BS = b * s
        self.M = b * s + b * c
        self.W = d // 4

    def seg_fn(self, tm):
        assert self.S % tm == 0 and self.M % tm == 0
        s, b = self.S, self.B
        return lambda i: jnp.minimum((i * tm) // s, b)


def _attend(qs, segs_of):
    n = len(qs)
    m, l, acc = [None] * n, [None] * n, [None] * n
    for j in range(len(segs_of[0])):
        for i in range(n):
            k, v, bias = segs_of[i][j]
            s = lax.dot_general(qs[i], k, _NT, preferred_element_type=F32)
            if bias is not None:
                s = s + bias
            smax = jnp.max(s, axis=-1, keepdims=True)
            if j == 0:
                m[i] = smax
                p = jnp.exp2(s - smax)
                l[i] = jnp.sum(p, axis=-1, keepdims=True)
                acc[i] = jnp.dot(p.astype(BF16), v, preferred_element_type=F32)
            else:
                m_new = jnp.maximum(m[i], smax)
                alpha = jnp.exp2(m[i] - m_new)
                p = jnp.exp2(s - m_new)
                l[i] = alpha * l[i] + jnp.sum(p, axis=-1, keepdims=True)
                acc[i] = alpha * acc[i] + jnp.dot(p.astype(BF16), v, preferred_element_type=F32)
                m[i] = m_new
    return [acc[i] * (1.0 / l[i]) for i in range(n)]


def _half_masks(dtype):
    lane = lax.broadcasted_iota(I32, (1, _LANES), 1)
    lo = lane < 64
    return lo, jnp.where(lo, 1.0, 0.0).astype(dtype), jnp.where(lo, 0.0, 1.0).astype(dtype)


def _na_prep_kernel(f_ref, gq_ref, gk_ref, bd_ref, o_ref):
    w = gq_ref.shape[1]
    bd = bd_ref[...]

    def head_norm(x, g):
        ss = _dot01_right(x * x, bd)
        return x * lax.rsqrt(ss * (1.0 / NA_HEAD_DIM) + NORM_EPS) * g

    o_ref[:, 0:w] = head_norm(f_ref[:, 0:w], gq_ref[...]).astype(o_ref.dtype)
    o_ref[:, w:2 * w] = head_norm(f_ref[:, w:2 * w], gk_ref[...]).astype(o_ref.dtype)
    o_ref[:, 2 * w:3 * w] = f_ref[:, 2 * w:3 * w].astype(o_ref.dtype)


def _na_prep(f_na, qn_g, kn_g, dims):
    m, w3 = f_na.shape
    w = w3 // 3
    tm = 256
    gq = (jnp.tile(qn_g, NA_HEADS) * (NA_HEAD_DIM ** -0.5 * _LOG2E)).reshape(1, w)
    gk = jnp.tile(kn_g, NA_HEADS).reshape(1, w)
    return pl.pallas_call(
        _na_prep_kernel,
        grid=(m // tm,),
        in_specs=[pl.BlockSpec((tm, w3), lambda i: (i, 0)),
                  pl.BlockSpec((1, w), lambda i: (0, 0)),
                  pl.BlockSpec((1, w), lambda i: (0, 0)),
                  pl.BlockSpec((w, w), lambda i: (0, 0))],
        out_specs=pl.BlockSpec((tm, w3), lambda i: (i, 0)),
        out_shape=jax.ShapeDtypeStruct((m, w3), BF16),
        compiler_params=_cp("parallel"),
    )(f_na, gq, gk, _head_blockdiag(w, NA_HEAD_DIM))


def _na_bias_tables(rpb, rows):
    n_tiles = rows // NA_QROWS
    assert rows % NA_QROWS == 0 and rows >= NA_BAND_ROWS + NA_QROWS and NA_QROWS * 2 <= NA_WIN_R
    tabs = []
    for g in (0, 1, n_tiles - 1):
        r = g * NA_QROWS + np.arange(NA_QROWS)
        band0 = int(np.clip(g * NA_QROWS - NA_WIN_R // 2, 0, rows - NA_BAND_ROWS))
        key_row = band0 + np.arange(NA_BAND_ROWS)
        row_start = np.clip(r - NA_WIN_R // 2, 0, rows - NA_WIN_R)
        valid_r = (key_row[None, :] >= row_start[:, None]) & (key_row[None, :] < row_start[:, None] + NA_WIN_R)
        d_row = np.clip(key_row[None, :] - r[:, None] + NA_WIN_R - 1, 0, 2 * NA_WIN_R - 2)
        qc = np.arange(GRID_W)
        col_start = np.clip(qc - NA_WIN_C // 2, 0, GRID_W - NA_WIN_C)
        kc = np.arange(GRID_W)
        valid_c = (kc[None, :] >= col_start[:, None]) & (kc[None, :] < col_start[:, None] + NA_WIN_C)
        d_col = np.clip(kc[None, :] - qc[:, None] + NA_WIN_C - 1, 0, 2 * NA_WIN_C - 2)
        oh_r = jnp.asarray(d_row[:, :, None] == np.arange(2 * NA_WIN_R - 1), F32)
        oh_c = jnp.asarray(d_col[:, :, None] == np.arange(2 * NA_WIN_C - 1), F32)
        b = jnp.einsum('ajr,hrc->hajc', oh_r, rpb, precision=lax.Precision.HIGHEST)
        b = jnp.einsum('hajc,qkc->haqjk', b, oh_c, precision=lax.Precision.HIGHEST)
        valid = valid_r[:, None, :, None] & valid_c[None, :, None, :]
        b = jnp.where(valid[None], b, -jnp.inf)
        tabs.append(b.reshape(NA_HEADS, NA_QROWS * GRID_W, NA_BAND_ROWS * GRID_W))
    return (jnp.stack(tabs) * _LOG2E).astype(BF16)


def _na_attn_kernel(q_ref, k_ref, v_ref, kc_ref, vc_ref, bias_ref, o_ref, *, rows):
    g = pl.program_id(1)
    band0 = jnp.clip(g * NA_QROWS - NA_WIN_R // 2, 0, rows - NA_BAND_ROWS)
    off = pl.multiple_of(band0 * GRID_W, GRID_W)
    band = NA_BAND_ROWS * GRID_W
    lo, m_e, m_o = _half_masks(BF16)
    qs, segs = [], []
    for p in range(q_ref.shape[1] // _LANES):
        sl = slice(p * _LANES, (p + 1) * _LANES)
        qp = q_ref[:, sl]
        kb = k_ref[pl.ds(off, band), sl]
        vb = v_ref[pl.ds(off, band), sl]
        for half, msk in enumerate((m_e, m_o)):
            qs.append(qp * msk)
            segs.append([(kb, vb, bias_ref[2 * p + half].astype(F32)), (kc_ref[:, sl], vc_ref[:, sl], None)])
    outs = []
    for i in range(0, len(qs), NA_HEADS_TOGETHER):
        outs += _attend(qs[i:i + NA_HEADS_TOGETHER], segs[i:i + NA_HEADS_TOGETHER])
    for p in range(q_ref.shape[1] // _LANES):
        o_ref[:, p * _LANES:(p + 1) * _LANES] = jnp.where(lo, outs[2 * p], outs[2 * p + 1]).astype(o_ref.dtype)


def _na_ctx_kernel(q_ref, kc_ref, vc_ref, o_ref):
    lo, m_e, m_o = _half_masks(BF16)
    qs, segs = [], []
    for p in range(q_ref.shape[1] // _LANES):
        sl = slice(p * _LANES, (p + 1) * _LANES)
        for msk in (m_e, m_o):
            qs.append(q_ref[:, sl] * msk)
            segs.append([(kc_ref[:, sl], vc_ref[:, sl], None)])
    outs = _attend(qs, segs)
    for p in range(q_ref.shape[1] // _LANES):
        o_ref[:, p * _LANES:(p + 1) * _LANES] = jnp.where(lo, outs[2 * p], outs[2 * p + 1]).astype(o_ref.dtype)


def _na_attention(qkv, bias, dims, need_ctx):
    b, s, c, w = dims.B, dims.S, dims.C, dims.W
    rows = s // GRID_W
    n_tiles = rows // NA_QROWS
    tq = NA_QROWS * GRID_W
    band = NA_BAND_ROWS * GRID_W
    ctx_blk = dims.BS // c

    def cls(g):
        return (g > 0).astype(I32) + (g == n_tiles - 1).astype(I32)

    ya = pl.pallas_call(
        functools.partial(_na_attn_kernel, rows=rows),
        grid=(b, n_tiles),
        in_specs=[pl.BlockSpec((tq, w), lambda bi, g: (bi * n_tiles + g, 0)),
                  pl.BlockSpec((s, w), lambda bi, g: (bi, 1)),
                  pl.BlockSpec((s, w), lambda bi, g: (bi, 2)),
                  pl.BlockSpec((c, w), lambda bi, g: (ctx_blk + bi, 1)),
                  pl.BlockSpec((c, w), lambda bi, g: (ctx_blk + bi, 2)),
                  pl.BlockSpec((None, NA_HEADS, tq, band), lambda bi, g: (cls(g), 0, 0, 0))],
        out_specs=pl.BlockSpec((tq, w), lambda bi, g: (bi * n_tiles + g, 0)),
        out_shape=jax.ShapeDtypeStruct((dims.BS, w), BF16),
        compiler_params=_cp("parallel", "arbitrary"),
    )(qkv, qkv, qkv, qkv, qkv, bias)
    if not need_ctx:
        return ya
    yc = pl.pallas_call(
        _na_ctx_kernel,
        grid=(b,),
        in_specs=[pl.BlockSpec((c, w), lambda bi: (ctx_blk + bi, 0)),
                  pl.BlockSpec((c, w), lambda bi: (ctx_blk + bi, 1)),
                  pl.BlockSpec((c, w), lambda bi: (ctx_blk + bi, 2))],
        out_specs=pl.BlockSpec((c, w), lambda bi: (bi, 0)),
        out_shape=jax.ShapeDtypeStruct((b * c, w), BF16),
        compiler_params=_cp("parallel"),
    )(qkv, qkv, qkv)
    return jnp.concatenate([ya, yc], axis=0)


def _rope_tables(s, tm):
    n_freq = MLA_ROPE // 4
    inv_freq = ROPE_THETA ** (-jnp.arange(n_freq, dtype=F32) / n_freq)
    t = jnp.arange(s)
    row = (t // GRID_W).astype(F32)[:, None] * inv_freq
    col = (t % GRID_W).astype(F32)[:, None] * inv_freq
    zeros = jnp.zeros((s, n_freq), F32)

    def slab(row_x1, row_x2, col_x1, col_x2, fill):
        body = jnp.concatenate([row_x1, row_x2, col_x1, col_x2], axis=-1)
        return jnp.concatenate([jnp.full((s, MLA_NOPE), fill, F32), body,
                                jnp.zeros((s, _LANES - MLA_QK), F32)], axis=-1)

    cr, sr, cc, sc = jnp.cos(row), jnp.sin(row), jnp.cos(col), jnp.sin(col)
    c_tab = slab(cr, cr, cc, cc, 1.0)
    s1_tab = slab(-sr, zeros, -sc, zeros, 0.0)
    s2_tab = slab(zeros, sr, zeros, sc, 0.0)
    ident = jnp.concatenate([jnp.ones((tm, MLA_QK), F32), jnp.zeros((tm, _LANES - MLA_QK), F32)], axis=-1)
    zero = jnp.zeros((tm, _LANES), F32)
    return (jnp.concatenate([c_tab, ident]), jnp.concatenate([s1_tab, zero]),
            jnp.concatenate([s2_tab, zero]))


def _mla_prep_kernel(f_ref, wq_ref, wk_ref, wv_ref, place_ref, gcq_ref, gckv_ref, gq_ref, gk_ref,
                     ct_ref, s1_ref, s2_ref, q_o, k_o, v_o):
    cq = f_ref[:, 0:MLA_Q_RANK]
    ckv = f_ref[:, MLA_Q_RANK:MLA_Q_RANK + MLA_KV_RANK]
    kr = f_ref[:, MLA_Q_RANK + MLA_KV_RANK:MLA_Q_RANK + MLA_KV_RANK + MLA_ROPE]

    def rms(x, g):
        return x * lax.rsqrt(jnp.mean(x * x, axis=-1, keepdims=True) + NORM_EPS) * g

    cqn = rms(cq, gcq_ref[...]).astype(BF16)
    ckvn = rms(ckv, gckv_ref[...]).astype(BF16)
    q = jnp.dot(cqn, wq_ref[...], preferred_element_type=F32)
    kn = jnp.dot(ckvn, wk_ref[...], preferred_element_type=F32)
    v_o[...] = jnp.dot(ckvn, wv_ref[...], preferred_element_type=F32).astype(v_o.dtype)
    kr_slab = _dot01_right(kr, place_ref[...])
    ct, s1, s2 = ct_ref[...], s1_ref[...], s2_ref[...]

    def head(x, g):
        x = x * lax.rsqrt(jnp.sum(x * x, axis=-1, keepdims=True) * (1.0 / MLA_QK) + NORM_EPS) * g
        return x * ct + pltpu.roll(x, _LANES - MLA_ROPE // 4, 1) * s1 + pltpu.roll(x, MLA_ROPE // 4, 1) * s2

    for h in range(MLA_HEADS):
        sl = slice(h * _LANES, (h + 1) * _LANES)
        q_o[h] = head(q[:, sl], gq_ref[...]).astype(q_o.dtype)
        k_o[h] = head(kn[:, sl] + kr_slab, gk_ref[...]).astype(k_o.dtype)


def _mla_prep(f_mla, pw, l, ropes, dims):
    m = f_mla.shape[0]
    tm = 256
    n_lat = dims.S // tm
    bs = dims.BS

    def rope_blk(i):
        return jnp.where(i * tm < bs, i % n_lat, n_lat)

    hw = MLA_HEADS * _LANES
    full = lambda shape: pl.BlockSpec(shape, lambda i: tuple(0 for _ in shape))
    lyr = lambda shape: pl.BlockSpec((None,) + shape, lambda i: (l,) + tuple(0 for _ in shape))
    return pl.pallas_call(
        _mla_prep_kernel,
        grid=(m // tm,),
        in_specs=[pl.BlockSpec((tm, f_mla.shape[1]), lambda i: (i, 0)),
                  lyr((MLA_Q_RANK, hw)), lyr((MLA_KV_RANK, hw)), lyr((MLA_KV_RANK, MLA_HEADS * MLA_V)),
                  full((MLA_ROPE, _LANES)),
                  lyr((1, MLA_Q_RANK)), lyr((1, MLA_KV_RANK)), lyr((1, _LANES)), lyr((1, _LANES)),
                  pl.BlockSpec((tm, _LANES), lambda i: (rope_blk(i), 0)),
                  pl.BlockSpec((tm, _LANES), lambda i: (rope_blk(i), 0)),
                  pl.BlockSpec((tm, _LANES), lambda i: (rope_blk(i), 0))],
        out_specs=[pl.BlockSpec((MLA_HEADS, tm, _LANES), lambda i: (0, i, 0)),
                   pl.BlockSpec((MLA_HEADS, tm, _LANES), lambda i: (0, i, 0)),
                   pl.BlockSpec((tm, MLA_HEADS * MLA_V), lambda i: (i, 0))],
        out_shape=[jax.ShapeDtypeStruct((MLA_HEADS, m, _LANES), BF16),
                   jax.ShapeDtypeStruct((MLA_HEADS, m, _LANES), BF16),
                   jax.ShapeDtypeStruct((m, MLA_HEADS * MLA_V), BF16)],
        compiler_params=_cp("parallel"),
    )(f_mla, pw["mla_wq"], pw["mla_wk"], pw["mla_wv"], pw["mla_place"], pw["mla_gcq"], pw["mla_gckv"],
      pw["mla_gq"], pw["mla_gk"], *ropes)


def _mla_attn_kernel(qe_ref, qo_ref, kce_ref, kco_ref, kle_ref, klo_ref, vc_ref, vl_ref, o_ref):
    lo, _, _ = _half_masks(F32)
    n_chunks = kle_ref.shape[0] // MLA_KEY_CHUNK

    def segs(kc_ref, kl_ref):
        out = [(kc_ref[...], vc_ref[...], None)]
        for j in range(n_chunks):
            sl = slice(j * MLA_KEY_CHUNK, (j + 1) * MLA_KEY_CHUNK)
            out.append((kl_ref[sl, :], vl_ref[sl, :], None))
        return out

    o_e, o_o = _attend([qe_ref[...], qo_ref[...]], [segs(kce_ref, kle_ref), segs(kco_ref, klo_ref)])
    o_ref[...] = jnp.where(lo, o_e, o_o).astype(o_ref.dtype)


def _mla_ctx_kernel(qe_ref, qo_ref, kce_ref, kco_ref, vc_ref, o_ref):
    lo, _, _ = _half_masks(F32)
    o_e, o_o = _attend([qe_ref[...], qo_ref[...]],
                       [[(kce_ref[...], vc_ref[...], None)], [(kco_ref[...], vc_ref[...], None)]])
    o_ref[...] = jnp.where(lo, o_e, o_o).astype(o_ref.dtype)


def _mla_attention(q, k, v, dims, need_ctx):
    b, s, c, w = dims.B, dims.S, dims.C, dims.W
    tq = MLA_Q_TILE
    nq = s // tq
    ctx_blk = dims.BS // c
    pairs = MLA_HEADS // 2
    hq = lambda half: pl.BlockSpec((None, tq, _LANES), lambda bi, p, i: (2 * p + half, bi * nq + i, 0))
    hkc = lambda half: pl.BlockSpec((None, c, _LANES), lambda bi, p, i: (2 * p + half, ctx_blk + bi, 0))
    hkl = lambda half: pl.BlockSpec((None, s, _LANES), lambda bi, p, i: (2 * p + half, bi, 0))
    yc_x = pl.pallas_call(
        _mla_attn_kernel,
        grid=(b, pairs, nq),
        in_specs=[hq(0), hq(1), hkc(0), hkc(1), hkl(0), hkl(1),
                  pl.BlockSpec((c, _LANES), lambda bi, p, i: (ctx_blk + bi, p)),
                  pl.BlockSpec((s, _LANES), lambda bi, p, i: (bi, p))],
        out_specs=pl.BlockSpec((tq, _LANES), lambda bi, p, i: (bi * nq + i, p)),
        out_shape=jax.ShapeDtypeStruct((dims.BS, w), BF16),
        compiler_params=_cp("parallel", "parallel", "arbitrary"),
    )(q, q, k, k, k, k, v, v)
    if not need_ctx:
        return yc_x
    cq = lambda half: pl.BlockSpec((None, c, _LANES), lambda bi, p: (2 * p + half, ctx_blk + bi, 0))
    yc_s = pl.pallas_call(
        _mla_ctx_kernel,
        grid=(b, pairs),
        in_specs=[cq(0), cq(1), cq(0), cq(1),
                  pl.BlockSpec((c, _LANES), lambda bi, p: (ctx_blk + bi, p))],
        out_specs=pl.BlockSpec((c, _LANES), lambda bi, p: (bi, p)),
        out_shape=jax.ShapeDtypeStruct((b * c, w), BF16),
        compiler_params=_cp("parallel", "parallel"),
    )(q, q, k, k, v)
    return jnp.concatenate([yc_x, yc_s], axis=0)


def _rw_prep_kernel(f_ref, fp_ref, fn_ref, mu_ref, w0_ref, w2_ref, a0_ref, a2_ref, g2_ref, kkg_ref,
                    ka_ref, bd_ref, r_o, v_o, kk_o, g_o, lw_o, kd_o, bb_o, *, s, c, bs, tm):
    w = r_o.shape[1]
    row0 = pl.program_id(0) * tm
    is_lat = row0 < bs
    pos = jnp.where(is_lat, row0 % s, (row0 - bs) % c)
    seglen = jnp.where(is_lat, s, c)
    f = f_ref[...]
    prev_row = jnp.where(pos == 0, 0.0, fp_ref[7:8, :])
    next_row = jnp.where(pos + tm == seglen, 0.0, fn_ref[0:1, :])
    ridx = lax.broadcasted_iota(I32, (tm, 1), 0)
    prev = jnp.where(ridx == 0, prev_row, pltpu.roll(f, 1, 0))
    nxt = jnp.where(ridx == tm - 1, next_row, pltpu.roll(f, tm - 1, 0))
    fm = f + mu_ref[0:1, :] * (prev - f) + mu_ref[1:2, :] * (nxt - f)
    r, k, v = fm[:, 0:w], fm[:, w:2 * w], fm[:, 2 * w:3 * w]
    o = 3 * w
    w_low = jnp.tanh(fm[:, o:o + 128])
    a_low = fm[:, o + 128:o + 256]
    g_low = _sigmoid(fm[:, o + 256:o + 384])
    log_w = -_softplus(-(w0_ref[...] + _dot(w_low, w2_ref[...]))) - 0.5
    lw = -jnp.exp(log_w)
    a = _sigmoid(a0_ref[...] + _dot(a_low, a2_ref[...]))
    g_o[...] = _dot(g_low, g2_ref[...])
    kk = k * kkg_ref[...]
    kk = kk / jnp.maximum(jnp.sqrt(_dot01_right(kk * kk, bd_ref[...])), 1e-12)
    r_o[...] = r
    v_o[...] = v
    kk_o[...] = kk
    for d in range(2):
        ad = a[:, d * w:(d + 1) * w]
        lw_o[d] = lw[:, d * w:(d + 1) * w]
        kd_o[d] = k * (1.0 + (ad - 1.0) * ka_ref[...])
        bb_o[d] = ad * kk


def _rw_prep(f_rw, pw, l, dims):
    m, cols = f_rw.shape
    w = dims.W
    tm = 256
    assert dims.C % tm == 0 and dims.S % tm == 0
    nb8 = m // 8
    lyr = lambda shape: pl.BlockSpec((None,) + shape, lambda i: (l,) + tuple(0 for _ in shape))
    out_w = pl.BlockSpec((tm, w), lambda i: (i, 0))
    out_2w = pl.BlockSpec((2, tm, w), lambda i: (0, i, 0))
    sd = jax.ShapeDtypeStruct
    return pl.pallas_call(
        functools.partial(_rw_prep_kernel, s=dims.S, c=dims.C, bs=dims.BS, tm=tm),
        grid=(m // tm,),
        in_specs=[pl.BlockSpec((tm, cols), lambda i: (i, 0)),
                  pl.BlockSpec((8, cols), lambda i: (jnp.maximum(i * (tm // 8) - 1, 0), 0)),
                  pl.BlockSpec((8, cols), lambda i: (jnp.minimum((i + 1) * (tm // 8), nb8 - 1), 0)),
                  lyr((2, cols)), lyr((1, 2 * w)), lyr((128, 2 * w)), lyr((1, 2 * w)), lyr((128, 2 * w)),
                  lyr((128, w)), lyr((1, w)), lyr((1, w)),
                  pl.BlockSpec((w, w), lambda i: (0, 0))],
        out_specs=[out_w, out_w, out_w, out_w, out_2w, out_2w, out_2w],
        out_shape=[sd((m, w), F32)] * 4 + [sd((2, m, w), F32)] * 3,
        compiler_params=_cp("parallel"),
    )(f_rw, f_rw, f_rw, pw["rw_mu"], pw["rw_w0"], pw["rw_w2"], pw["rw_a0"], pw["rw_a2"], pw["rw_g2"],
      pw["rw_kk"], pw["rw_ka"], _head_blockdiag(w, RW_HEAD_DIM))


def _rw_chunk_kernel(rf, vf, kkf, lwf, kdf, bbf, rb, vb, kkb, lwb, kdb, bbb, yf_ref, yb_ref, h_ref):
    c = RW_CHUNK

    @pl.when(pl.program_id(2) == 0)
    def _():
        h_ref[...] = jnp.zeros_like(h_ref)

    ii = lax.broadcasted_iota(I32, (c, c), 0)
    jj = lax.broadcasted_iota(I32, (c, c), 1)
    tri_f = jnp.where(jj <= ii, 1.0, 0.0).astype(BF16)
    tri_b = jnp.where(jj >= ii, 1.0, 0.0).astype(BF16)
    _, m_e, m_o = _half_masks(F32)

    n4 = 4 * c
    i4 = lax.broadcasted_iota(I32, (n4, n4), 0)
    j4 = lax.broadcasted_iota(I32, (n4, n4), 1)
    lag = jnp.where(i4 < 2 * c, i4 - j4, j4 - i4)
    lag = jnp.where((i4 // c) == (j4 // c), lag, -1)
    strict = lag > 0
    incl = lag >= 0
    eye4 = jnp.where(i4 == j4, 1.0, 0.0)
    eye = lax.broadcasted_iota(I32, (_LANES, _LANES), 0) == lax.broadcasted_iota(I32, (_LANES, _LANES), 1)
    steps = int(np.log2(c)) - 1

    def feats(r_ref, kk_ref, lw_ref, kd_ref, bb_ref, sl, tri, last):
        lw = lw_ref[:, sl]
        cum = _dot01_left(tri, lw)
        tot = cum[last:last + 1, :]
        w_inv = jnp.exp(-cum)
        w_end = jnp.exp(tot - cum)
        bb, kd = bb_ref[:, sl], kd_ref[:, sl]
        return dict(rt=r_ref[:, sl] * jnp.exp(cum), at=-kk_ref[:, sl] * jnp.exp(cum - lw),
                    bbar=bb * w_inv, kbar=kd * w_inv, bhat=bb * w_end, khat=kd * w_end, wtot=jnp.exp(tot))

    def stack(xf, xb):
        return jnp.concatenate([xf * m_e, xf * m_o, xb * m_e, xb * m_o], axis=0).astype(BF16)

    pr = range(h_ref.shape[0])
    sls = [slice(p * _LANES, (p + 1) * _LANES) for p in pr]
    ff = [feats(rf, kkf, lwf, kdf, bbf, sl, tri_f, c - 1) for sl in sls]
    fb = [feats(rb, kkb, lwb, kdb, bbb, sl, tri_b, 0) for sl in sls]
    la = [stack(ff[p]["at"], fb[p]["at"]) for p in pr]
    lr = [stack(ff[p]["rt"], fb[p]["rt"]) for p in pr]
    rbm = [stack(ff[p]["bbar"], fb[p]["bbar"]) for p in pr]
    rkm = [stack(ff[p]["kbar"], fb[p]["kbar"]) for p in pr]
    vm = [stack(vf[:, sl], vb[:, sl]) for sl in sls]
    nmat = [jnp.where(strict, _dot_nt(la[p], rbm[p]), 0.0) for p in pr]
    aak = [jnp.where(strict, _dot_nt(la[p], rkm[p]), 0.0) for p in pr]
    arb = [jnp.where(incl, _dot_nt(lr[p], rbm[p]), 0.0) for p in pr]
    ark = [jnp.where(incl, _dot_nt(lr[p], rkm[p]), 0.0) for p in pr]
    t_inv = [eye4 + nmat[p] for p in pr]
    pw = nmat
    for _ in range(steps):
        pw = [_dot(pw[p], pw[p]) for p in pr]
        t_inv = [t_inv[p] + _dot(t_inv[p], pw[p]) for p in pr]
    x = [_dot(t_inv[p], la[p]) for p in pr]
    av = [_dot(aak[p], vm[p]) for p in pr]
    u = [_dot(t_inv[p], av[p]) for p in pr]
    q = [lr[p].astype(F32) + _dot(arb[p], x[p]) for p in pr]
    y0 = [_dot(arb[p], u[p]) + _dot(ark[p], vm[p]) for p in pr]
    for d, (fd, out_ref) in enumerate(((ff, yf_ref), (fb, yb_ref))):
        rows = slice(2 * c * d, 2 * c * (d + 1))
        h = [h_ref[p, d] for p in pr]
        yst = [_dot(q[p][rows], h[p]) + y0[p][rows] for p in pr]
        for p in pr:
            out_ref[:, sls[p]] = yst[p][0:c] + yst[p][c:2 * c]
        bh = [jnp.concatenate([fd[p]["bhat"] * m_e, fd[p]["bhat"] * m_o], axis=0) for p in pr]
        kh = [jnp.concatenate([fd[p]["khat"] * m_e, fd[p]["khat"] * m_o], axis=0) for p in pr]
        trans = [jnp.where(eye, fd[p]["wtot"], 0.0) + _dot_tn(bh[p], x[p][rows]) for p in pr]
        for p in pr:
            h_ref[p, d] = _dot(trans[p], h[p]) + _dot_tn(bh[p], u[p][rows]) + _dot_tn(kh[p], vm[p][rows])


def _rw_scan(r, v, kk, lw, kd, bb, dims):
    b, s, c, w = dims.B, dims.S, dims.C, dims.W
    ch = RW_CHUNK
    n_c, n_s = c // ch, s // ch
    ctx0 = dims.BS // ch
    pp = RW_PAIRS_PER_STEP
    bw = pp * _LANES
    assert w % bw == 0

    def cf(bi, t):
        return jnp.where(t < n_c, ctx0 + bi * n_c + t, bi * n_s + (t - n_c))

    def cb(bi, t):
        return jnp.where(t < n_c, ctx0 + bi * n_c + (n_c - 1 - t), bi * n_s + (n_s - 1 - (t - n_c)))

    sh = lambda fn: pl.BlockSpec((ch, bw), lambda bi, p, t: (fn(bi, t), p))
    dr = lambda fn, d: pl.BlockSpec((None, ch, bw), lambda bi, p, t: (d, fn(bi, t), p))
    m = dims.M
    return pl.pallas_call(
        _rw_chunk_kernel,
        grid=(b, w // bw, n_c + n_s),
        in_specs=[sh(cf), sh(cf), sh(cf), dr(cf, 0), dr(cf, 0), dr(cf, 0),
                  sh(cb), sh(cb), sh(cb), dr(cb, 1), dr(cb, 1), dr(cb, 1)],
        out_specs=[sh(cf), sh(cb)],
        out_shape=[jax.ShapeDtypeStruct((m, w), F32)] * 2,
        scratch_shapes=[pltpu.VMEM((pp, 2, _LANES, _LANES), F32)],
        compiler_params=_cp("parallel", "parallel", "arbitrary"),
    )(r, v, kk, lw, kd, bb, r, v, kk, lw, kd, bb)


def _rw_readout_kernel(yf_ref, yb_ref, r_ref, v_ref, g_ref, kd_ref, rk_ref, gg_ref, gb_ref, bd_ref, o_ref):
    bd = bd_ref[...]
    inv = 1.0 / RW_HEAD_DIM
    y = yf_ref[...] + yb_ref[...]
    yc = y - _dot01_right(y, bd) * inv
    var = _dot01_right(yc * yc, bd) * inv
    yn = yc * lax.rsqrt(var + RW_GN_EPS)
    bonus = _dot01_right(r_ref[...] * (kd_ref[0] + kd_ref[1]) * rk_ref[...], bd) * v_ref[...]
    o_ref[...] = ((yn * gg_ref[...] + gb_ref[...] + bonus) * g_ref[...]).astype(o_ref.dtype)


def _rw_readout(yf, yb, r, v, g, kd, pw, l, dims, rows):
    w = dims.W
    tm = 256
    blk = pl.BlockSpec((tm, w), lambda i: (i, 0))
    lyr = pl.BlockSpec((None, 1, w), lambda i: (l, 0, 0))
    return pl.pallas_call(
        _rw_readout_kernel,
        grid=(rows // tm,),
        in_specs=[blk, blk, blk, blk, blk, pl.BlockSpec((2, tm, w), lambda i: (0, i, 0)), lyr, lyr, lyr,
                  pl.BlockSpec((w, w), lambda i: (0, 0))],
        out_specs=blk,
        out_shape=jax.ShapeDtypeStruct((rows, w), BF16),
        compiler_params=_cp("parallel"),
    )(yf, yb, r, v, g, kd, pw["rw_rk"], pw["rw_gn_g"], pw["rw_gn_b"], _head_blockdiag(w, RW_HEAD_DIM))


def _s5_matrices(a_re, a_im, log_dt, b_re, b_im, c_re, c_im, d_skip):
    hi = lax.Precision.HIGHEST
    n_g, n_p, n_j = b_re.shape[1], b_re.shape[2], b_re.shape[3]
    ct = S5_CHUNK
    dt = jnp.exp(log_dt)[..., None]
    mag = jnp.exp(a_re * dt)
    lam_re, lam_im = mag * jnp.cos(a_im * dt), mag * jnp.sin(a_im * dt)
    den = a_re * a_re + a_im * a_im
    f_re = ((lam_re - 1.0) * a_re + lam_im * a_im) / den
    f_im = (lam_im * a_re - (lam_re - 1.0) * a_im) / den
    bb_re = f_re[..., None] * b_re - f_im[..., None] * b_im
    bb_im = f_re[..., None] * b_im + f_im[..., None] * b_re
    p_re, p_im = [jnp.ones_like(lam_re)], [jnp.zeros_like(lam_im)]
    for _ in range(ct):
        p_re.append(p_re[-1] * lam_re - p_im[-1] * lam_im)
        p_im.append(p_re[-2] * lam_im + p_im[-1] * lam_re)
    p_re, p_im = jnp.stack(p_re), jnp.stack(p_im)
    cl_re = c_re[None] * p_re[:, :, :, None, :] - c_im[None] * p_im[:, :, :, None, :]
    cl_im = c_re[None] * p_im[:, :, :, None, :] + c_im[None] * p_re[:, :, :, None, :]
    kern = (jnp.einsum('ndgip,dgpm->ndgim', cl_re, bb_re, precision=hi)
            - jnp.einsum('ndgip,dgpm->ndgim', cl_im, bb_im, precision=hi))
    j_idx = np.arange(ct)[:, None]
    t_idx = np.arange(ct)[None, :]
    lag_f, lag_b = t_idx - j_idx, j_idx - t_idx
    sel_f = jnp.asarray(lag_f[:, :, None] == np.arange(ct + 1), F32)
    sel_b = jnp.asarray(lag_b[:, :, None] == np.arange(ct + 1), F32)
    toep = (jnp.einsum('jtn,ngim->gjmti', sel_f, kern[:, 0], precision=hi)
            + jnp.einsum('jtn,ngim->gjmti', sel_b, kern[:, 1], precision=hi))
    skip = (jnp.eye(ct)[None, :, None, :, None] * jnp.eye(n_j)[None, None, :, None, :]
            * d_skip.reshape(n_g, 1, 1, 1, n_j))
    toep = (toep + skip).reshape(n_g, ct * n_j, ct * n_j)

    def powers(z, d, first, reverse):
        z = z[first:first + ct, d]
        return z[::-1] if reverse else z

    def state_in(first, reverse, d):
        pr, pi = powers(p_re, d, first, reverse), powers(p_im, d, first, reverse)
        re = pr[..., None] * bb_re[d][None] - pi[..., None] * bb_im[d][None]
        im = pr[..., None] * bb_im[d][None] + pi[..., None] * bb_re[d][None]
        fix = lambda z: z.transpose(1, 0, 3, 2).reshape(n_g, ct * n_j, n_p)
        return fix(re), fix(im)

    def state_out(first, reverse, d):
        fix = lambda z: z.transpose(1, 3, 0, 2).reshape(n_g, n_p, ct * n_j)
        return fix(powers(cl_re, d, first, reverse)), fix(-powers(cl_im, d, first, reverse))

    rf_re, rf_im = state_in(0, True, 0)
    rb_re, rb_im = state_in(0, False, 1)
    of_re, of_im = state_out(1, False, 0)
    ob_re, ob_im = state_out(1, True, 1)
    half = (np.arange(n_g) % 2)[:, None, None]

    def lane_half(z):
        zero = jnp.zeros_like(z)
        return jnp.concatenate([jnp.where(half == 0, z, zero), jnp.where(half == 1, z, zero)], axis=-1)

    def row_half(z):
        zero = jnp.zeros_like(z)
        return jnp.concatenate([jnp.where(half == 0, z, zero), jnp.where(half == 1, z, zero)], axis=1)

    r_re = jnp.stack([lane_half(rf_re), lane_half(rb_re)]).astype(BF16)
    r_im = jnp.stack([lane_half(rf_im), lane_half(rb_im)]).astype(BF16)
    o_re = jnp.stack([row_half(of_re), row_half(ob_re)]).astype(BF16)
    o_im = jnp.stack([row_half(of_im), row_half(ob_im)]).astype(BF16)
    lam_c = (p_re[ct].reshape(2, 1, n_g * n_p), p_im[ct].reshape(2, 1, n_g * n_p))
    return toep.astype(BF16), r_re, r_im, o_re, o_im, lam_c


def _s5_z_kernel(u_ref, rre_ref, rim_ref, zre_o, zim_o):
    u0, u1 = u_ref[0].astype(BF16), u_ref[1].astype(BF16)
    for d in range(2):
        zre_o[d] = (jnp.dot(u0, rre_ref[d, 0], preferred_element_type=F32)
                    + jnp.dot(u1, rre_ref[d, 1], preferred_element_type=F32))
        zim_o[d] = (jnp.dot(u0, rim_ref[d, 0], preferred_element_type=F32)
                    + jnp.dot(u1, rim_ref[d, 1], preferred_element_type=F32))


def _s5_scan_kernel(zre_ref, zim_ref, lre_ref, lim_ref, xre_o, xim_o, *, n_b, n_s, n_c):
    lanes = zre_ref.shape[2]
    chains = [(d, b) for d in range(2) for b in range(n_b)]
    lam = [(lre_ref[d], lim_ref[d]) for d in range(2)]

    def run(base_of, length, carry):
        def body(t, carry):
            new = []
            for (d, b), (xr, xi) in zip(chains, carry):
                row = base_of(b) + (t if d == 0 else length - 1 - t)
                xre_o[d, pl.ds(row, 1), :] = xr
                xim_o[d, pl.ds(row, 1), :] = xi
                lr, li = lam[d]
                new.append((lr * xr - li * xi + zre_ref[d, pl.ds(row, 1), :],
                            lr * xi + li * xr + zim_ref[d, pl.ds(row, 1), :]))
            return tuple(new)
        return lax.fori_loop(0, length, body, carry)

    zero = jnp.zeros((1, lanes), F32)
    carry = tuple((zero, zero) for _ in chains)
    carry = run(lambda b: n_b * n_s + b * n_c, n_c, carry)
    run(lambda b: b * n_s, n_s, carry)


def _s5_y_kernel(u_ref, toep_ref, xre_ref, xim_ref, ore_ref, oim_ref, y_o):
    xr = [xre_ref[d].astype(BF16) for d in range(2)]
    xi = [xim_ref[d].astype(BF16) for d in range(2)]
    for g in range(2):
        y = jnp.dot(u_ref[g].astype(BF16), toep_ref[g], preferred_element_type=F32)
        for d in range(2):
            y = y + jnp.dot(xr[d], ore_ref[d, g], preferred_element_type=F32)
            y = y + jnp.dot(xi[d], oim_ref[d, g], preferred_element_type=F32)
        y_o[g] = y


def _s5_glu_kernel(y_ref, w_ref, b_ref, o_ref, wbf_ref):
    @pl.when(pl.program_id(0) == 0)
    def _():
        wbf_ref[...] = w_ref[...].astype(BF16)

    y = y_ref[...]
    y = 0.5 * y * (1.0 + jnp.tanh(0.7978845608028654 * (y + 0.044715 * y * y * y)))
    z = jnp.dot(y.astype(BF16), wbf_ref[...], preferred_element_type=F32) + b_ref[...]
    o_ref[...] = (y * _sigmoid(z)).astype(o_ref.dtype)


def _s5_mixer(f_s5, mats, w_glu, b_glu3, l, dims, rows):
    toep, r_re, r_im, o_re, o_im, (lam_re, lam_im) = mats
    m, w = f_s5.shape
    ct = S5_CHUNK
    n_g = w // S5_GROUP
    n_ch = m // ct
    gw = ct * S5_GROUP
    u = f_s5.reshape(n_ch, ct, n_g, S5_GROUP).transpose(2, 0, 1, 3).reshape(n_g, n_ch, gw)
    sl = n_g * S5_STATE
    sd = jax.ShapeDtypeStruct
    zre, zim = pl.pallas_call(
        _s5_z_kernel,
        grid=(n_g // 2,),
        in_specs=[pl.BlockSpec((2, n_ch, gw), lambda g: (g, 0, 0)),
                  pl.BlockSpec((2, 2, gw, _LANES), lambda g: (0, g, 0, 0)),
                  pl.BlockSpec((2, 2, gw, _LANES), lambda g: (0, g, 0, 0))],
        out_specs=[pl.BlockSpec((2, n_ch, _LANES), lambda g: (0, 0, g))] * 2,
        out_shape=[sd((2, n_ch, sl), F32)] * 2,
        compiler_params=_cp("parallel"),
    )(u, r_re, r_im)
    lb = 512
    xre, xim = pl.pallas_call(
        functools.partial(_s5_scan_kernel, n_b=dims.B, n_s=dims.S // ct, n_c=dims.C // ct),
        grid=(sl // lb,),
        in_specs=[pl.BlockSpec((2, n_ch, lb), lambda j: (0, 0, j))] * 2
                 + [pl.BlockSpec((2, 1, lb), lambda j: (0, 0, j))] * 2,
        out_specs=[pl.BlockSpec((2, n_ch, lb), lambda j: (0, 0, j))] * 2,
        out_shape=[sd((2, n_ch, sl), F32)] * 2,
        compiler_params=_cp("parallel"),
    )(zre, zim, lam_re, lam_im)
    y = pl.pallas_call(
        _s5_y_kernel,
        grid=(n_g // 2,),
        in_specs=[pl.BlockSpec((2, n_ch, gw), lambda g: (g, 0, 0)),
                  pl.BlockSpec((2, gw, gw), lambda g: (g, 0, 0)),
                  pl.BlockSpec((2, n_ch, _LANES), lambda g: (0, 0, g)),
                  pl.BlockSpec((2, n_ch, _LANES), lambda g: (0, 0, g)),
                  pl.BlockSpec((2, 2, _LANES, gw), lambda g: (0, g, 0, 0)),
                  pl.BlockSpec((2, 2, _LANES, gw), lambda g: (0, g, 0, 0))],
        out_specs=pl.BlockSpec((2, n_ch, gw), lambda g: (g, 0, 0)),
        out_shape=sd((n_g, n_ch, gw), F32),
        compiler_params=_cp("parallel"),
    )(u, toep, xre, xim, o_re, o_im)
    y = y.reshape(n_g, n_ch, ct, S5_GROUP).transpose(1, 2, 0, 3).reshape(m, w)
    tm = 256
    return pl.pallas_call(
        _s5_glu_kernel,
        grid=(rows // tm,),
        in_specs=[pl.BlockSpec((tm, w), lambda i: (i, 0)),
                  pl.BlockSpec((None, w, w), lambda i: (l, 0, 0)),
                  pl.BlockSpec((None, 1, w), lambda i: (l, 0, 0))],
        out_specs=pl.BlockSpec((tm, w), lambda i: (i, 0)),
        out_shape=sd((rows, w), BF16),
        scratch_shapes=[pltpu.VMEM((w, w), BF16)],
        compiler_params=_cp("arbitrary"),
    )(y, w_glu, b_glu3)


def _merge_kernel(xn_ref, ya_ref, yb_ref, yc_ref, yd_ref, wg0, wg1, wg2, wg3, bg0, bg1, bg2, bg3, wb_ref,
                  o_ref, wg_bf, wb_bf):
    wgs = (wg0, wg1, wg2, wg3)

    @pl.when(pl.program_id(1) == 0)
    def _():
        for n in range(4):
            wg_bf[n] = wgs[n][...].astype(BF16)
            wb_bf[n] = wb_ref[n].astype(BF16)

    xn = xn_ref[...]
    acc = None
    for n, (y_ref, bg) in enumerate(zip((ya_ref, yb_ref, yc_ref, yd_ref), (bg0, bg1, bg2, bg3))):
        gate = _sigmoid(jnp.dot(xn, wg_bf[n], preferred_element_type=F32) + bg[...])
        up = jnp.dot(y_ref[...], wb_bf[n], preferred_element_type=F32)
        acc = gate * up if acc is None else acc + gate * up
    o_ref[...] = acc.astype(o_ref.dtype)


def _merge(xn, ys, w_gate, b_gate3, w_branch, l, rows):
    d = xn.shape[1]
    w = ys[0].shape[1]
    tn = 256
    tm = _pick(rows, (1088, 1024, 512, 256))
    nb = d // tn
    wg = lambda n: pl.BlockSpec((None, d, tn), lambda j, i: (l, 0, n * nb + j))
    bg = lambda n: pl.BlockSpec((None, 1, tn), lambda j, i: (l, 0, n * nb + j))
    yspec = pl.BlockSpec((tm, w), lambda j, i: (i, 0))
    return pl.pallas_call(
        _merge_kernel,
        grid=(nb, rows // tm),
        in_specs=[pl.BlockSpec((tm, d), lambda j, i: (i, 0)), yspec, yspec, yspec, yspec,
                  wg(0), wg(1), wg(2), wg(3), bg(0), bg(1), bg(2), bg(3),
                  pl.BlockSpec((None, 4, w, tn), lambda j, i: (l, 0, 0, j))],
        out_specs=pl.BlockSpec((tm, tn), lambda j, i: (i, j)),
        out_shape=jax.ShapeDtypeStruct((rows, d), BF16),
        scratch_shapes=[pltpu.VMEM((4, d, tn), BF16), pltpu.VMEM((4, w, tn), BF16)],
        compiler_params=_cp("arbitrary", "arbitrary"),
    )(xn, *ys, w_gate, w_gate, w_gate, w_gate, b_gate3, b_gate3, b_gate3, b_gate3, w_branch)


def _router_kernel(h_ref, wr_ref, br_ref, idx_o, gate_o, rank_o, cnt_o, carry):
    tm = h_ref.shape[0]

    @pl.when(pl.program_id(0) == 0)
    def _():
        carry[...] = jnp.zeros_like(carry)

    h_hi, h_lo = _split(h_ref[...], 2)
    w_hi, w_lo = _split(wr_ref[...], 2)
    logits = (jnp.dot(h_hi, w_hi, preferred_element_type=F32) + jnp.dot(h_lo, w_hi, preferred_element_type=F32)
              + jnp.dot(h_hi, w_lo, preferred_element_type=F32)) + br_ref[...]
    lane = lax.broadcasted_iota(I32, (tm, _LANES), 1)
    vals, idxs, hots = [], [], []
    cur = logits
    for _ in range(TOP_K):
        mx = jnp.max(cur, axis=-1, keepdims=True)
        ix = jnp.min(jnp.where(cur == mx, lane, _LANES), axis=-1, keepdims=True)
        hot = lane == ix
        vals.append(mx)
        idxs.append(ix)
        hots.append(hot)
        cur = jnp.where(hot, -jnp.inf, cur)
    ex = [jnp.exp(v - vals[0]) for v in vals]
    inv = 1.0 / sum(ex)
    sel = sum(jnp.where(h, 1.0, 0.0) for h in hots)
    ri = lax.broadcasted_iota(I32, (tm, tm), 0)
    ci = lax.broadcasted_iota(I32, (tm, tm), 1)
    earlier = jnp.where(ci < ri, 1.0, 0.0).astype(BF16)
    before = jnp.dot(earlier, sel.astype(BF16), preferred_element_type=F32) + carry[...]
    idx_out = jnp.zeros((tm, _LANES), I32)
    gate_out = jnp.zeros((tm, _LANES), F32)
    rank_out = jnp.zeros((tm, _LANES), I32)
    for k in range(TOP_K):
        rk = jnp.sum(jnp.where(hots[k], before, 0.0), axis=-1, keepdims=True)
        idx_out = jnp.where(lane == k, idxs[k], idx_out)
        gate_out = jnp.where(lane == k, ex[k] * inv, gate_out)
        rank_out = jnp.where(lane == k, rk.astype(I32), rank_out)
    idx_o[...] = idx_out
    gate_o[...] = gate_out
    rank_o[...] = rank_out
    carry[...] = carry[...] + jnp.sum(sel, axis=0, keepdims=True)
    cnt_o[...] = jnp.broadcast_to(carry[...], cnt_o.shape)


def _router(h, wr_pad, br_pad, l, rows):
    d = h.shape[1]
    tm = 256
    blk = pl.BlockSpec((tm, _LANES), lambda i: (i, 0))
    sd = jax.ShapeDtypeStruct
    return pl.pallas_call(
        _router_kernel,
        grid=(rows // tm,),
        in_specs=[pl.BlockSpec((tm, d), lambda i: (i, 0)),
                  pl.BlockSpec((None, d, _LANES), lambda i: (l, 0, 0)),
                  pl.BlockSpec((None, 1, _LANES), lambda i: (l, 0, 0))],
        out_specs=[blk, blk, blk, pl.BlockSpec((8, _LANES), lambda i: (0, 0))],
        out_shape=[sd((rows, _LANES), I32), sd((rows, _LANES), F32), sd((rows, _LANES), I32),
                   sd((8, _LANES), F32)],
        scratch_shapes=[pltpu.VMEM((1, _LANES), F32)],
        compiler_params=_cp("arbitrary"),
    )(h, wr_pad, br_pad)


def _row_copy(src_hbm, row, dst, slot, sem):
    return pltpu.make_async_copy(src_hbm.at[pl.ds(row, 1), :], dst.at[pl.ds(slot, 1), :], sem)


def _gather_kernel(tok_ref, h_hbm, o_ref, sem):
    n = o_ref.shape[0]

    def issue(i, carry):
        for u in range(_DMA_UNROLL):
            r = i * _DMA_UNROLL + u
            _row_copy(h_hbm, tok_ref[0, r], o_ref, r, sem).start()
        return carry

    lax.fori_loop(0, n // _DMA_UNROLL, issue, 0)
    pltpu.make_async_copy(h_hbm.at[pl.ds(0, n), :], o_ref, sem).wait()


def _moe_gather(h, slot_tok3):
    n_blocks, _, blk = slot_tok3.shape
    d = h.shape[1]
    return pl.pallas_call(
        _gather_kernel,
        grid=(n_blocks,),
        in_specs=[pl.BlockSpec((None, 1, blk), lambda i: (i, 0, 0), memory_space=pltpu.SMEM),
                  pl.BlockSpec(memory_space=pl.ANY)],
        out_specs=pl.BlockSpec((blk, d), lambda i: (i, 0)),
        out_shape=jax.ShapeDtypeStruct((n_blocks * blk, d), F32),
        scratch_shapes=[pltpu.SemaphoreType.DMA(())],
        compiler_params=_cp("arbitrary"),
    )(slot_tok3, h)


def _expert_kernel(be_ref, first_ref, used_ref, x_ref, wgu_ref, bgu_ref, wdn_ref, bdn_ref, o_ref,
                   wgu_bf, wdn_bf):
    i = pl.program_id(0)

    @pl.when(first_ref[i] == 1)
    def _():
        wgu_bf[...] = wgu_ref[...].astype(BF16)
        wdn_bf[...] = wdn_ref[...].astype(BF16)

    @pl.when(i < used_ref[0])
    def _():
        gu = jnp.dot(x_ref[...].astype(BF16), wgu_bf[...], preferred_element_type=F32) + bgu_ref[...]
        g_lin = jnp.minimum(gu[:, :D_EXPERT], SWIGLU_LIMIT)
        u_lin = jnp.clip(gu[:, D_EXPERT:], -SWIGLU_LIMIT, SWIGLU_LIMIT)
        act = (u_lin + 1.0) * g_lin * _sigmoid(SWIGLU_ALPHA * g_lin)
        o_ref[...] = jnp.dot(act.astype(BF16), wdn_bf[...], preferred_element_type=F32) + bdn_ref[...]

    @pl.when(i >= used_ref[0])
    def _():
        o_ref[...] = jnp.zeros_like(o_ref)


def _moe_experts(xs, block_e, first, n_used, w_gu, b_gu4, w_dn, b_dn4, l):
    n_slots, d = xs.shape
    blk = MOE_BLOCK
    n_blocks = n_slots // blk
    de2 = w_gu.shape[3]
    grid_spec = pltpu.PrefetchScalarGridSpec(
        num_scalar_prefetch=3,
        grid=(n_blocks,),
        in_specs=[pl.BlockSpec((blk, d), lambda i, be, fi, us: (i, 0)),
                  pl.BlockSpec((None, None, d, de2), lambda i, be, fi, us: (l, be[i], 0, 0)),
                  pl.BlockSpec((None, None, 1, de2), lambda i, be, fi, us: (l, be[i], 0, 0)),
                  pl.BlockSpec((None, None, de2 // 2, d), lambda i, be, fi, us: (l, be[i], 0, 0)),
                  pl.BlockSpec((None, None, 1, d), lambda i, be, fi, us: (l, be[i], 0, 0))],
        out_specs=pl.BlockSpec((blk, d), lambda i, be, fi, us: (i, 0)),
        scratch_shapes=[pltpu.VMEM((d, de2), BF16), pltpu.VMEM((de2 // 2, d), BF16)])
    return pl.pallas_call(
        _expert_kernel,
        grid_spec=grid_spec,
        out_shape=jax.ShapeDtypeStruct((n_slots, d), F32),
        compiler_params=_cp("arbitrary"),
    )(block_e, first, n_used, xs, w_gu, b_gu4, w_dn, b_dn4)


def _combine_kernel(dest_ref, gate_ref, x_ref, g2_ref, ys_hbm, o_ref, buf, sem):
    tm = x_ref.shape[0]

    rows_per_iter = _DMA_UNROLL // TOP_K

    def issue(i, carry):
        for u in range(rows_per_iter):
            r = i * rows_per_iter + u
            for k in range(TOP_K):
                _row_copy(ys_hbm, dest_ref[0, r * TOP_K + k], buf.at[k], r, sem).start()
        return carry

    lax.fori_loop(0, tm // rows_per_iter, issue, 0)
    for k in range(TOP_K):
        pltpu.make_async_copy(ys_hbm.at[pl.ds(0, tm), :], buf.at[k], sem).wait()
    gate = gate_ref[...]
    f = None
    for k in range(TOP_K):
        term = gate[:, k:k + 1] * buf[k]
        f = term if f is None else f + term
    o_ref[...] = x_ref[...] + g2_ref[...] * f


def _moe_combine(x, ys, dest3, gate, mod, dims, rows):
    d = x.shape[1]
    tm = MOE_COMBINE_ROWS
    seg = dims.seg_fn(tm)
    return pl.pallas_call(
        _combine_kernel,
        grid=(rows // tm,),
        in_specs=[pl.BlockSpec((None, 1, tm * TOP_K), lambda i: (i, 0, 0), memory_space=pltpu.SMEM),
                  pl.BlockSpec((tm, _LANES), lambda i: (i, 0)),
                  pl.BlockSpec((tm, d), lambda i: (i, 0)),
                  pl.BlockSpec((None, 1, d), lambda i: (seg(i), 0, 5)),
                  pl.BlockSpec(memory_space=pl.ANY)],
        out_specs=pl.BlockSpec((tm, d), lambda i: (i, 0)),
        out_shape=jax.ShapeDtypeStruct((rows, d), F32),
        scratch_shapes=[pltpu.VMEM((TOP_K, tm, d), F32), pltpu.SemaphoreType.DMA(())],
        compiler_params=_cp("arbitrary"),
    )(dest3, gate, x, mod, ys)


def _moe(x, h, mod, pw, l, dims, rows):
    idx, gate, rank, cnt = _router(h, pw["moe_wr"], pw["moe_br"], l, rows)
    blk = MOE_BLOCK
    n_blocks = -(-(rows * TOP_K) // blk) + N_EXPERTS
    counts = cnt[0, :N_EXPERTS].astype(I32)
    padded = (counts + blk - 1) // blk * blk
    pad_end = jnp.cumsum(padded)
    pad_start = pad_end - padded
    top_i = idx[:, :TOP_K]
    start_of = jnp.sum(jnp.where(top_i[..., None] == jnp.arange(N_EXPERTS, dtype=I32), pad_start, 0), axis=-1)
    dest = start_of + rank[:, :TOP_K]
    tok = jnp.broadcast_to(jnp.arange(rows, dtype=I32)[:, None], dest.shape)
    slot_tok = jnp.zeros((n_blocks * blk,), I32).at[dest.reshape(-1)].set(tok.reshape(-1))
    block_start = jnp.arange(n_blocks, dtype=I32) * blk
    block_e = jnp.minimum(jnp.sum((pad_end[None, :] <= block_start[:, None]).astype(I32), axis=1), N_EXPERTS - 1)
    first = jnp.concatenate([jnp.ones((1,), I32), (block_e[1:] != block_e[:-1]).astype(I32)])
    n_used = (pad_end[-1:] // blk).astype(I32)
    xs = _moe_gather(h, slot_tok.reshape(n_blocks, 1, blk))
    ys = _moe_experts(xs, block_e, first, n_used, pw["moe_w_gu"], pw["moe_b_gu"], pw["moe_w_dn"],
                      pw["moe_b_dn"], l)
    dest3 = dest.reshape(rows // MOE_COMBINE_ROWS, 1, MOE_COMBINE_ROWS * TOP_K)
    return _moe_combine(x, ys, dest3, gate, mod, dims, rows)


def _prep_params(p, dims):
    n_l = p["w_in"].shape[0]
    w = dims.W
    o_mla = 3 * w + (3 * w + 384)
    o_s5 = o_mla + MLA_Q_RANK + MLA_KV_RANK + MLA_ROPE
    pw = dict(p)
    row3 = lambda a: a.reshape(a.shape[0], 1, -1)
    pw["norm1_g"], pw["norm2_g"] = row3(p["norm1_g"]), row3(p["norm2_g"])
    pw["b_mod"], pw["b_gate"] = row3(p["b_mod"]), row3(p["b_gate"])
    pw["w_in_mla"] = p["w_in"][:, :, o_mla:o_s5]
    pw["w_in_s5"] = p["w_in"][:, :, o_s5:]
    zeros = jnp.zeros_like(p["rw_w2"][:, 0])
    bdiag = lambda x: jnp.concatenate([jnp.concatenate([x[:, 0], zeros], axis=-1),
                                       jnp.concatenate([zeros, x[:, 1]], axis=-1)], axis=1)
    pw["rw_w2"], pw["rw_a2"] = bdiag(p["rw_w2"]), bdiag(p["rw_a2"])
    pw["rw_w0"], pw["rw_a0"] = row3(p["rw_w0"]), row3(p["rw_a0"])
    pw["rw_kk"], pw["rw_ka"] = row3(p["rw_kk"]), row3(p["rw_ka"])
    pw["rw_rk"], pw["rw_gn_g"], pw["rw_gn_b"] = row3(p["rw_rk"]), row3(p["rw_gn_g"]), row3(p["rw_gn_b"])
    pad_h = lambda x, n: jnp.pad(x, ((0, 0), (0, 0), (0, 0), (0, _LANES - n)))
    wq = p["mla_wq_up"].reshape(n_l, MLA_Q_RANK, MLA_HEADS, MLA_QK)
    pw["mla_wq"] = pad_h(wq, MLA_QK).reshape(n_l, MLA_Q_RANK, -1).astype(BF16)
    wkv = p["mla_wkv_up"].reshape(n_l, MLA_KV_RANK, MLA_HEADS, MLA_NOPE + MLA_V)
    pw["mla_wk"] = pad_h(wkv[..., :MLA_NOPE], MLA_NOPE).reshape(n_l, MLA_KV_RANK, -1).astype(BF16)
    pw["mla_wv"] = wkv[..., MLA_NOPE:].reshape(n_l, MLA_KV_RANK, -1).astype(BF16)
    pw["mla_place"] = jnp.asarray(np.arange(MLA_ROPE)[:, None] + MLA_NOPE == np.arange(_LANES)[None, :], BF16)
    pw["mla_gcq"], pw["mla_gckv"] = row3(p["mla_qn_g"]), row3(p["mla_kvn_g"])
    pad_g = lambda g: jnp.pad(g, ((0, 0), (0, _LANES - MLA_QK))).reshape(n_l, 1, _LANES)
    pw["mla_gq"] = pad_g(p["mla_qkn_q"] * (MLA_QK ** -0.5 * _LOG2E))
    pw["mla_gk"] = pad_g(p["mla_qkn_k"])
    pw["s5_b_glu"] = row3(p["s5_b_glu"])
    pw["moe_wr"] = jnp.pad(p["moe_wr"], ((0, 0), (0, 0), (0, _LANES - N_EXPERTS)))
    pw["moe_br"] = jnp.pad(p["moe_br"], ((0, 0), (0, _LANES - N_EXPERTS)), constant_values=-1e30)[:, None, :]
    pw["moe_b_gu"] = p["moe_b_gu"][:, :, None, :]
    pw["moe_b_dn"] = p["moe_b_dn"][:, :, None, :]
    return pw


def _residual_epilogue(acc, x, g):
    return x + g * acc


def _layer(xs, l, mod, pw, ropes, dims, need_ctx):
    m, d, w = dims.M, dims.D, dims.W
    rows = m if need_ctx else dims.BS
    tm_all = _pick(m, (1088, 1024, 512, 256))
    xn = _norm_mod(xs, pw["norm1_g"], l, mod, 0, 1, dims, m, BF16)
    f_na = _mm(xn, pw["w_in"], l, tm=tm_all, tn=512, col0=0, ncols=3 * w)
    f_rw = _mm(xn, pw["w_in"], l, tm=tm_all, tn=384, col0=3 * w, ncols=3 * w + 384)
    f_mla = _mm(xn, pw["w_in_mla"], l, tm=tm_all, tn=pw["w_in_mla"].shape[2])
    f_s5 = _mm(xn, pw["w_in_s5"], l, tm=tm_all, tn=w)

    qkv = _na_prep(f_na, pw["na_qn_g"][l], pw["na_kn_g"][l], dims)
    ya = _na_attention(qkv, _na_bias_tables(pw["na_rpb"][l], dims.S // GRID_W), dims, need_ctx)

    r, v, kk, g, lw, kd, bb = _rw_prep(f_rw, pw, l, dims)
    yf, yb_ = _rw_scan(r, v, kk, lw, kd, bb, dims)
    yb = _rw_readout(yf, yb_, r, v, g, kd, pw, l, dims, rows)

    q, k, vv = _mla_prep(f_mla, pw, l, ropes, dims)
    yc = _mla_attention(q, k, vv, dims, need_ctx)

    mats = _s5_matrices(pw["s5_a_re"][l], pw["s5_a_im"][l], pw["s5_log_dt"][l], pw["s5_b_re"][l],
                        pw["s5_b_im"][l], pw["s5_c_re"][l], pw["s5_c_im"][l], pw["s5_d"][l])
    yd = _s5_mixer(f_s5, mats, pw["s5_w_glu"], pw["s5_b_glu"], l, dims, rows)

    merged = _merge(xn, (ya, yb, yc, yd), pw["w_gate"], pw["b_gate"], pw["w_branch"], l, rows)
    tm_seg = _pick(dims.S, (512, 256))
    seg = dims.seg_fn(tm_seg)
    tn = 512
    xs = _mm(merged, pw["w_out"], l, tm=tm_seg, tn=tn, rows=rows, epilogue=_residual_epilogue,
             extras=((xs, pl.BlockSpec((tm_seg, tn), lambda j, i: (i, j))),
                     (mod, pl.BlockSpec((None, 1, tn), lambda j, i: (seg(i), 0, 2 * (d // tn) + j)))))
    h = _norm_mod(xs, pw["norm2_g"], l, mod, 3, 4, dims, rows, F32)
    return _moe(xs, h, mod, pw, l, dims, rows)


def kernel(x, c, ctx, c_ctx, w_mod, b_mod, norm1_g, norm2_g, w_in, w_gate, b_gate, w_branch, w_out, na_qn_g, na_kn_g, na_rpb, rw_mu, rw_w0, rw_w2, rw_a0, rw_a2, rw_g2, rw_kk, rw_ka, rw_rk, rw_gn_g, rw_gn_b, mla_qn_g, mla_wq_up, mla_kvn_g, mla_wkv_up, mla_qkn_q, mla_qkn_k, s5_a_re, s5_a_im, s5_log_dt, s5_b_re, s5_b_im, s5_c_re, s5_c_im, s5_d, s5_w_glu, s5_b_glu, moe_wr, moe_br, moe_w_gu, moe_b_gu, moe_w_dn, moe_b_dn):
    params = dict(w_mod=w_mod, b_mod=b_mod, norm1_g=norm1_g, norm2_g=norm2_g, w_in=w_in, w_gate=w_gate,
                  b_gate=b_gate, w_branch=w_branch, w_out=w_out, na_qn_g=na_qn_g, na_kn_g=na_kn_g,
                  na_rpb=na_rpb, rw_mu=rw_mu, rw_w0=rw_w0, rw_w2=rw_w2, rw_a0=rw_a0, rw_a2=rw_a2,
                  rw_g2=rw_g2, rw_kk=rw_kk, rw_ka=rw_ka, rw_rk=rw_rk, rw_gn_g=rw_gn_g, rw_gn_b=rw_gn_b,
                  mla_qn_g=mla_qn_g, mla_wq_up=mla_wq_up, mla_kvn_g=mla_kvn_g, mla_wkv_up=mla_wkv_up,
                  mla_qkn_q=mla_qkn_q, mla_qkn_k=mla_qkn_k, s5_a_re=s5_a_re, s5_a_im=s5_a_im,
                  s5_log_dt=s5_log_dt, s5_b_re=s5_b_re, s5_b_im=s5_b_im, s5_c_re=s5_c_re, s5_c_im=s5_c_im,
                  s5_d=s5_d, s5_w_glu=s5_w_glu, s5_b_glu=s5_b_glu, moe_wr=moe_wr, moe_br=moe_br,
                  moe_w_gu=moe_w_gu, moe_b_gu=moe_b_gu, moe_w_dn=moe_w_dn, moe_b_dn=moe_b_dn)
    b, s, d = x.shape
    c_len = ctx.shape[1]
    depth = w_mod.shape[0]
    dims = _Dims(b, s, c_len, d)
    assert b + 1 <= 8
    pw = _prep_params(params, dims)
    cond = jnp.concatenate([jax.nn.silu(c), jax.nn.silu(c_ctx)[None],
                            jnp.zeros((8 - b - 1, d), F32)], axis=0)
    ropes = _rope_tables(s, 256)
    xs = jnp.concatenate([x.reshape(b * s, d), ctx.reshape(b * c_len, d)], axis=0)
    for l in range(depth):
        mod = _mm(cond, pw["w_mod"], l, tm=8, tn=1024, bias=pw["b_mod"]).reshape(8, 1, 6 * d)
        xs = _layer(xs, l, mod, pw, ropes, dims, need_ctx=l < depth - 1)
    return xs[:b * s].reshape(b, s, d)
```

```python
import functools

import numpy as np
import jax
import jax.numpy as jnp
from jax import lax
from jax.experimental import pallas as pl
from jax.experimental.pallas import tpu as pltpu

F32 = jnp.float32
BF16 = jnp.bfloat16
I32 = jnp.int32

_VMEM_LIMIT_BYTES = 56 * 1024 * 1024
_LANES = 128
_LOG2E = 1.4426950408889634

NORM_EPS = 1e-6
GRID_W = 64
NA_HEADS = 8
NA_HEAD_DIM = 64
NA_WIN_R = 8
NA_WIN_C = 16
NA_QROWS = 4
NA_BAND_ROWS = NA_QROWS + NA_WIN_R - 1
NA_HEADS_TOGETHER = 2
RW_HEAD_DIM = 64
RW_GN_EPS = 64e-5
RW_CHUNK = 64
RW_PAIRS_PER_STEP = 4
MLA_HEADS = 8
MLA_NOPE = 64
MLA_ROPE = 32
MLA_QK = MLA_NOPE + MLA_ROPE
MLA_V = 64
MLA_Q_RANK = 512
MLA_KV_RANK = 256
ROPE_THETA = 10000.0
MLA_KEY_CHUNK = 256
MLA_Q_TILE = 512
S5_GROUP = 16
S5_STATE = 64
S5_CHUNK = 16
N_EXPERTS = 32
TOP_K = 4
D_EXPERT = 512
SWIGLU_LIMIT = 7.0
SWIGLU_ALPHA = 1.702
MOE_BLOCK = 256
MOE_COMBINE_ROWS = 128

_NT = (((1,), (1,)), ((), ()))
_TN = (((0,), (0,)), ((), ()))


def _cp(*sem):
    return pltpu.CompilerParams(dimension_semantics=sem, vmem_limit_bytes=_VMEM_LIMIT_BYTES)


def _dot(a, b):
    return jnp.dot(a.astype(BF16), b.astype(BF16), preferred_element_type=F32)


def _dot_nt(a, b):
    return lax.dot_general(a.astype(BF16), b.astype(BF16), _NT, preferred_element_type=F32)


def _dot_tn(a, b):
    return lax.dot_general(a.astype(BF16), b.astype(BF16), _TN, preferred_element_type=F32)


def _split(x, parts):
    out = []
    for _ in range(parts):
        h = x.astype(BF16)
        out.append(h)
        x = x - h.astype(F32)
    return out


def _dot01_right(x, m01, parts=2):
    return sum(jnp.dot(h, m01, preferred_element_type=F32) for h in _split(x, parts))


def _dot01_left(m01, x, parts=3):
    return sum(jnp.dot(m01, h, preferred_element_type=F32) for h in _split(x, parts))


def _sigmoid(x):
    return 1.0 / (1.0 + jnp.exp(-x))


def _softplus(x):
    return jnp.maximum(x, 0.0) + jnp.log(1.0 + jnp.exp(-jnp.abs(x)))


def _pick(n, cands):
    for c in cands:
        if n % c == 0:
            return c
    raise ValueError(f"no tile for {n} in {cands}")


def _head_blockdiag(width, head):
    i = np.arange(width)
    return jnp.asarray((i[:, None] // head) == (i[None, :] // head), BF16)


def _mm_kernel(*refs, has_bias, n_extra, epilogue):
    a_ref, w_ref = refs[0], refs[1]
    pos = 2
    bias_ref = None
    if has_bias:
        bias_ref = refs[pos]
        pos += 1
    extra = refs[pos:pos + n_extra]
    o_ref, wbf_ref = refs[pos + n_extra], refs[pos + n_extra + 1]

    @pl.when(pl.program_id(1) == 0)
    def _():
        wbf_ref[...] = w_ref[...].astype(BF16)

    acc = jnp.dot(a_ref[...].astype(BF16), wbf_ref[...], preferred_element_type=F32)
    if has_bias:
        acc = acc + bias_ref[...]
    if epilogue is not None:
        acc = epilogue(acc, *[e[...] for e in extra])
    o_ref[...] = acc.astype(o_ref.dtype)


def _mm(a, w, l, *, tm, tn, col0=0, ncols=None, rows=None, bias=None, extras=(), epilogue=None,
        out_dtype=F32):
    k = a.shape[1]
    rows = a.shape[0] if rows is None else rows
    ncols = w.shape[2] - col0 if ncols is None else ncols
    assert col0 % tn == 0 and ncols % tn == 0 and rows % tm == 0 and w.shape[1] == k
    c0 = col0 // tn
    in_specs = [pl.BlockSpec((tm, k), lambda j, i: (i, 0)),
                pl.BlockSpec((None, k, tn), lambda j, i: (l, 0, j + c0))]
    args = [a, w]
    if bias is not None:
        in_specs.append(pl.BlockSpec((None, 1, tn), lambda j, i: (l, 0, j + c0)))
        args.append(bias)
    for arr, spec in extras:
        in_specs.append(spec)
        args.append(arr)
    return pl.pallas_call(
        functools.partial(_mm_kernel, has_bias=bias is not None, n_extra=len(extras), epilogue=epilogue),
        grid=(ncols // tn, rows // tm),
        in_specs=in_specs,
        out_specs=pl.BlockSpec((tm, tn), lambda j, i: (i, j)),
        out_shape=jax.ShapeDtypeStruct((rows, ncols), out_dtype),
        scratch_shapes=[pltpu.VMEM((k, tn), BF16)],
        compiler_params=_cp("arbitrary", "arbitrary"),
    )(*args)


def _norm_mod_kernel(x_ref, g_ref, sc_ref, sh_ref, o_ref):
    x = x_ref[...]
    y = x * lax.rsqrt(jnp.mean(x * x, axis=-1, keepdims=True) + NORM_EPS) * g_ref[...]
    o_ref[...] = (y * (1.0 + sc_ref[...]) + sh_ref[...]).astype(o_ref.dtype)


def _norm_mod(x, gain3, l, mod, which_sh, which_sc, dims, rows, out_dtype):
    d = x.shape[1]
    tm = 256
    seg = dims.seg_fn(tm)
    return pl.pallas_call(
        _norm_mod_kernel,
        grid=(rows // tm,),
        in_specs=[pl.BlockSpec((tm, d), lambda i: (i, 0)),
                  pl.BlockSpec((None, 1, d), lambda i: (l, 0, 0)),
                  pl.BlockSpec((None, 1, d), lambda i: (seg(i), 0, which_sc)),
                  pl.BlockSpec((None, 1, d), lambda i: (seg(i), 0, which_sh))],
        out_specs=pl.BlockSpec((tm, d), lambda i: (i, 0)),
        out_shape=jax.ShapeDtypeStruct((rows, d), out_dtype),
        compiler_params=_cp("parallel"),
    )(x, gain3, mod, mod)


class _Dims:
    def __init__(self, b, s, c, d):
        self.B, self.S, self.C, self.D = b, s, c, d
        self.BS = b * s
        self.M = b * s + b * c
        self.W = d // 4

    def seg_fn(self, tm):
        assert self.S % tm == 0 and self.M % tm == 0
        s, b = self.S, self.B
        return lambda i: jnp.minimum((i * tm) // s, b)


def _attend(qs, segs_of):
    n = len(qs)
    m, l, acc = [None] * n, [None] * n, [None] * n
    for j in range(len(segs_of[0])):
        for i in range(n):
            k, v, bias = segs_of[i][j]
            s = lax.dot_general(qs[i], k, _NT, preferred_element_type=F32)
            if bias is not None:
                s = s + bias
            smax = jnp.max(s, axis=-1, keepdims=True)
            if j == 0:
                m[i] = smax
                p = jnp.exp2(s - smax)
                l[i] = jnp.sum(p, axis=-1, keepdims=True)
                acc[i] = jnp.dot(p.astype(BF16), v, preferred_element_type=F32)
            else:
                m_new = jnp.maximum(m[i], smax)
                alpha = jnp.exp2(m[i] - m_new)
                p = jnp.exp2(s - m_new)
                l[i] = alpha * l[i] + jnp.sum(p, axis=-1, keepdims=True)
                acc[i] = alpha * acc[i] + jnp.dot(p.astype(BF16), v, preferred_element_type=F32)
                m[i] = m_new
    return [acc[i] * (1.0 / l[i]) for i in range(n)]


def _half_masks(dtype):
    lane = lax.broadcasted_iota(I32, (1, _LANES), 1)
    lo = lane < 64
    return lo, jnp.where(lo, 1.0, 0.0).astype(dtype), jnp.where(lo, 0.0, 1.0).astype(dtype)


def _na_prep_kernel(f_ref, gq_ref, gk_ref, bd_ref, o_ref):
    w = gq_ref.shape[1]
    bd = bd_ref[...]

    def head_norm(x, g):
        ss = _dot01_right(x * x, bd)
        return x * lax.rsqrt(ss * (1.0 / NA_HEAD_DIM) + NORM_EPS) * g

    o_ref[:, 0:w] = head_norm(f_ref[:, 0:w], gq_ref[...]).astype(o_ref.dtype)
    o_ref[:, w:2 * w] = head_norm(f_ref[:, w:2 * w], gk_ref[...]).astype(o_ref.dtype)
    o_ref[:, 2 * w:3 * w] = f_ref[:, 2 * w:3 * w].astype(o_ref.dtype)


def _na_prep(f_na, qn_g, kn_g, dims):
    m, w3 = f_na.shape
    w = w3 // 3
    tm = 256
    gq = (jnp.tile(qn_g, NA_HEADS) * (NA_HEAD_DIM ** -0.5 * _LOG2E)).reshape(1, w)
    gk = jnp.tile(kn_g, NA_HEADS).reshape(1, w)
    return pl.pallas_call(
        _na_prep_kernel,
        grid=(m // tm,),
        in_specs=[pl.BlockSpec((tm, w3), lambda i: (i, 0)),
                  pl.BlockSpec((1, w), lambda i: (0, 0)),
                  pl.BlockSpec((1, w), lambda i: (0, 0)),
                  pl.BlockSpec((w, w), lambda i: (0, 0))],
        out_specs=pl.BlockSpec((tm, w3), lambda i: (i, 0)),
        out_shape=jax.ShapeDtypeStruct((m, w3), BF16),
        compiler_params=_cp("parallel"),
    )(f_na, gq, gk, _head_blockdiag(w, NA_HEAD_DIM))


def _na_bias_tables(rpb, rows):
    n_tiles = rows // NA_QROWS
    assert rows % NA_QROWS == 0 and rows >= NA_BAND_ROWS + NA_QROWS and NA_QROWS * 2 <= NA_WIN_R
    tabs = []
    for g in (0, 1, n_tiles - 1):
        r = g * NA_QROWS + np.arange(NA_QROWS)
        band0 = int(np.clip(g * NA_QROWS - NA_WIN_R // 2, 0, rows - NA_BAND_ROWS))
        key_row = band0 + np.arange(NA_BAND_ROWS)
        row_start = np.clip(r - NA_WIN_R // 2, 0, rows - NA_WIN_R)
        valid_r = (key_row[None, :] >= row_start[:, None]) & (key_row[None, :] < row_start[:, None] + NA_WIN_R)
        d_row = np.clip(key_row[None, :] - r[:, None] + NA_WIN_R - 1, 0, 2 * NA_WIN_R - 2)
        qc = np.arange(GRID_W)
        col_start = np.clip(qc - NA_WIN_C // 2, 0, GRID_W - NA_WIN_C)
        kc = np.arange(GRID_W)
        valid_c = (kc[None, :] >= col_start[:, None]) & (kc[None, :] < col_start[:, None] + NA_WIN_C)
        d_col = np.clip(kc[None, :] - qc[:, None] + NA_WIN_C - 1, 0, 2 * NA_WIN_C - 2)
        oh_r = jnp.asarray(d_row[:, :, None] == np.arange(2 * NA_WIN_R - 1), F32)
        oh_c = jnp.asarray(d_col[:, :, None] == np.arange(2 * NA_WIN_C - 1), F32)
        b = jnp.einsum('ajr,hrc->hajc', oh_r, rpb, precision=lax.Precision.HIGHEST)
        b = jnp.einsum('hajc,qkc->haqjk', b, oh_c, precision=lax.Precision.HIGHEST)
        valid = valid_r[:, None, :, None] & valid_c[None, :, None, :]
        b = jnp.where(valid[None], b, -jnp.inf)
        tabs.append(b.reshape(NA_HEADS, NA_QROWS * GRID_W, NA_BAND_ROWS * GRID_W))
    return (jnp.stack(tabs) * _LOG2E).astype(BF16)


def _na_attn_kernel(q_ref, k_ref, v_ref, kc_ref, vc_ref, bias_ref, o_ref, *, rows):
    g = pl.program_id(1)
    band0 = jnp.clip(g * NA_QROWS - NA_WIN_R // 2, 0, rows - NA_BAND_ROWS)
    off = pl.multiple_of(band0 * GRID_W, GRID_W)
    band = NA_BAND_ROWS * GRID_W
    lo, m_e, m_o = _half_masks(BF16)
    qs, segs = [], []
    for p in range(q_ref.shape[1] // _LANES):
        sl = slice(p * _LANES, (p + 1) * _LANES)
        qp = q_ref[:, sl]
        kb = k_ref[pl.ds(off, band), sl]
        vb = v_ref[pl.ds(off, band), sl]
        for half, msk in enumerate((m_e, m_o)):
            qs.append(qp * msk)
            segs.append([(kb, vb, bias_ref[2 * p + half].astype(F32)), (kc_ref[:, sl], vc_ref[:, sl], None)])
    outs = []
    for i in range(0, len(qs), NA_HEADS_TOGETHER):
        outs += _attend(qs[i:i + NA_HEADS_TOGETHER], segs[i:i + NA_HEADS_TOGETHER])
    for p in range(q_ref.shape[1] // _LANES):
        o_ref[:, p * _LANES:(p + 1) * _LANES] = jnp.where(lo, outs[2 * p], outs[2 * p + 1]).astype(o_ref.dtype)


def _na_ctx_kernel(q_ref, kc_ref, vc_ref, o_ref):
    lo, m_e, m_o = _half_masks(BF16)
    qs, segs = [], []
    for p in range(q_ref.shape[1] // _LANES):
        sl = slice(p * _LANES, (p + 1) * _LANES)
        for msk in (m_e, m_o):
            qs.append(q_ref[:, sl] * msk)
            segs.append([(kc_ref[:, sl], vc_ref[:, sl], None)])
    outs = _attend(qs, segs)
    for p in range(q_ref.shape[1] // _LANES):
        o_ref[:, p * _LANES:(p + 1) * _LANES] = jnp.where(lo, outs[2 * p], outs[2 * p + 1]).astype(o_ref.dtype)


def _na_attention(qkv, bias, dims, need_ctx):
    b, s, c, w = dims.B, dims.S, dims.C, dims.W
    rows = s // GRID_W
    n_tiles = rows // NA_QROWS
    tq = NA_QROWS * GRID_W
    band = NA_BAND_ROWS * GRID_W
    ctx_blk = dims.BS // c

    def cls(g):
        return (g > 0).astype(I32) + (g == n_tiles - 1).astype(I32)

    ya = pl.pallas_call(
        functools.partial(_na_attn_kernel, rows=rows),
        grid=(b, n_tiles),
        in_specs=[pl.BlockSpec((tq, w), lambda bi, g: (bi * n_tiles + g, 0)),
                  pl.BlockSpec((s, w), lambda bi, g: (bi, 1)),
                  pl.BlockSpec((s, w), lambda bi, g: (bi, 2)),
                  pl.BlockSpec((c, w), lambda bi, g: (ctx_blk + bi, 1)),
                  pl.BlockSpec((c, w), lambda bi, g: (ctx_blk + bi, 2)),
                  pl.BlockSpec((None, NA_HEADS, tq, band), lambda bi, g: (cls(g), 0, 0, 0))],
        out_specs=pl.BlockSpec((tq, w), lambda bi, g: (bi * n_tiles + g, 0)),
        out_shape=jax.ShapeDtypeStruct((dims.BS, w), BF16),
        compiler_params=_cp("parallel", "arbitrary"),
    )(qkv, qkv, qkv, qkv, qkv, bias)
    if not need_ctx:
        return ya
    yc = pl.pallas_call(
        _na_ctx_kernel,
        grid=(b,),
        in_specs=[pl.BlockSpec((c, w), lambda bi: (ctx_blk + bi, 0)),
                  pl.BlockSpec((c, w), lambda bi: (ctx_blk + bi, 1)),
                  pl.BlockSpec((c, w), lambda bi: (ctx_blk + bi, 2))],
        out_specs=pl.BlockSpec((c, w), lambda bi: (bi, 0)),
        out_shape=jax.ShapeDtypeStruct((b * c, w), BF16),
        compiler_params=_cp("parallel"),
    )(qkv, qkv, qkv)
    return jnp.concatenate([ya, yc], axis=0)


def _rope_tables(s, tm):
    n_freq = MLA_ROPE // 4
    inv_freq = ROPE_THETA ** (-jnp.arange(n_freq, dtype=F32) / n_freq)
    t = jnp.arange(s)
    row = (t // GRID_W).astype(F32)[:, None] * inv_freq
    col = (t % GRID_W).astype(F32)[:, None] * inv_freq
    zeros = jnp.zeros((s, n_freq), F32)

    def slab(row_x1, row_x2, col_x1, col_x2, fill):
        body = jnp.concatenate([row_x1, row_x2, col_x1, col_x2], axis=-1)
        return jnp.concatenate([jnp.full((s, MLA_NOPE), fill, F32), body,
                                jnp.zeros((s, _LANES - MLA_QK), F32)], axis=-1)

    cr, sr, cc, sc = jnp.cos(row), jnp.sin(row), jnp.cos(col), jnp.sin(col)
    c_tab = slab(cr, cr, cc, cc, 1.0)
    s1_tab = slab(-sr, zeros, -sc, zeros, 0.0)
    s2_tab = slab(zeros, sr, zeros, sc, 0.0)
    ident = jnp.concatenate([jnp.ones((tm, MLA_QK), F32), jnp.zeros((tm, _LANES - MLA_QK), F32)], axis=-1)
    zero = jnp.zeros((tm, _LANES), F32)
    return (jnp.concatenate([c_tab, ident]), jnp.concatenate([s1_tab, zero]),
            jnp.concatenate([s2_tab, zero]))


def _mla_prep_kernel(f_ref, wq_ref, wk_ref, wv_ref, place_ref, gcq_ref, gckv_ref, gq_ref, gk_ref,
                     ct_ref, s1_ref, s2_ref, q_o, k_o, v_o):
    cq = f_ref[:, 0:MLA_Q_RANK]
    ckv = f_ref[:, MLA_Q_RANK:MLA_Q_RANK + MLA_KV_RANK]
    kr = f_ref[:, MLA_Q_RANK + MLA_KV_RANK:MLA_Q_RANK + MLA_KV_RANK + MLA_ROPE]

    def rms(x, g):
        return x * lax.rsqrt(jnp.mean(x * x, axis=-1, keepdims=True) + NORM_EPS) * g

    cqn = rms(cq, gcq_ref[...]).astype(BF16)
    ckvn = rms(ckv, gckv_ref[...]).astype(BF16)
    q = jnp.dot(cqn, wq_ref[...], preferred_element_type=F32)
    kn = jnp.dot(ckvn, wk_ref[...], preferred_element_type=F32)
    v_o[...] = jnp.dot(ckvn, wv_ref[...], preferred_element_type=F32).astype(v_o.dtype)
    kr_slab = _dot01_right(kr, place_ref[...])
    ct, s1, s2 = ct_ref[...], s1_ref[...], s2_ref[...]

    def head(x, g):
        x = x * lax.rsqrt(jnp.sum(x * x, axis=-1, keepdims=True) * (1.0 / MLA_QK) + NORM_EPS) * g
        return x * ct + pltpu.roll(x, _LANES - MLA_ROPE // 4, 1) * s1 + pltpu.roll(x, MLA_ROPE // 4, 1) * s2

    for h in range(MLA_HEADS):
        sl = slice(h * _LANES, (h + 1) * _LANES)
        q_o[h] = head(q[:, sl], gq_ref[...]).astype(q_o.dtype)
        k_o[h] = head(kn[:, sl] + kr_slab, gk_ref[...]).astype(k_o.dtype)


def _mla_prep(f_mla, pw, l, ropes, dims):
    m = f_mla.shape[0]
    tm = 256
    n_lat = dims.S // tm
    bs = dims.BS

    def rope_blk(i):
        return jnp.where(i * tm < bs, i % n_lat, n_lat)

    hw = MLA_HEADS * _LANES
    full = lambda shape: pl.BlockSpec(shape, lambda i: tuple(0 for _ in shape))
    lyr = lambda shape: pl.BlockSpec((None,) + shape, lambda i: (l,) + tuple(0 for _ in shape))
    return pl.pallas_call(
        _mla_prep_kernel,
        grid=(m // tm,),
        in_specs=[pl.BlockSpec((tm, f_mla.shape[1]), lambda i: (i, 0)),
                  lyr((MLA_Q_RANK, hw)), lyr((MLA_KV_RANK, hw)), lyr((MLA_KV_RANK, MLA_HEADS * MLA_V)),
                  full((MLA_ROPE, _LANES)),
                  lyr((1, MLA_Q_RANK)), lyr((1, MLA_KV_RANK)), lyr((1, _LANES)), lyr((1, _LANES)),
                  pl.BlockSpec((tm, _LANES), lambda i: (rope_blk(i), 0)),
                  pl.BlockSpec((tm, _LANES), lambda i: (rope_blk(i), 0)),
                  pl.BlockSpec((tm, _LANES), lambda i: (rope_blk(i), 0))],
        out_specs=[pl.BlockSpec((MLA_HEADS, tm, _LANES), lambda i: (0, i, 0)),
                   pl.BlockSpec((MLA_HEADS, tm, _LANES), lambda i: (0, i, 0)),
                   pl.BlockSpec((tm, MLA_HEADS * MLA_V), lambda i: (i, 0))],
        out_shape=[jax.ShapeDtypeStruct((MLA_HEADS, m, _LANES), BF16),
                   jax.ShapeDtypeStruct((MLA_HEADS, m, _LANES), BF16),
                   jax.ShapeDtypeStruct((m, MLA_HEADS * MLA_V), BF16)],
        compiler_params=_cp("parallel"),
    )(f_mla, pw["mla_wq"], pw["mla_wk"], pw["mla_wv"], pw["mla_place"], pw["mla_gcq"], pw["mla_gckv"],
      pw["mla_gq"], pw["mla_gk"], *ropes)


def _mla_attn_kernel(qe_ref, qo_ref, kce_ref, kco_ref, kle_ref, klo_ref, vc_ref, vl_ref, o_ref):
    lo, _, _ = _half_masks(F32)
    n_chunks = kle_ref.shape[0] // MLA_KEY_CHUNK

    def segs(kc_ref, kl_ref):
        out = [(kc_ref[...], vc_ref[...], None)]
        for j in range(n_chunks):
            sl = slice(j * MLA_KEY_CHUNK, (j + 1) * MLA_KEY_CHUNK)
            out.append((kl_ref[sl, :], vl_ref[sl, :], None))
        return out

    o_e, o_o = _attend([qe_ref[...], qo_ref[...]], [segs(kce_ref, kle_ref), segs(kco_ref, klo_ref)])
    o_ref[...] = jnp.where(lo, o_e, o_o).astype(o_ref.dtype)


def _mla_ctx_kernel(qe_ref, qo_ref, kce_ref, kco_ref, vc_ref, o_ref):
    lo, _, _ = _half_masks(F32)
    o_e, o_o = _attend([qe_ref[...], qo_ref[...]],
                       [[(kce_ref[...], vc_ref[...], None)], [(kco_ref[...], vc_ref[...], None)]])
    o_ref[...] = jnp.where(lo, o_e, o_o).astype(o_ref.dtype)


def _mla_attention(q, k, v, dims, need_ctx):
    b, s, c, w = dims.B, dims.S, dims.C, dims.W
    tq = MLA_Q_TILE
    nq = s // tq
    ctx_blk = dims.BS // c
    pairs = MLA_HEADS // 2
    hq = lambda half: pl.BlockSpec((None, tq, _LANES), lambda bi, p, i: (2 * p + half, bi * nq + i, 0))
    hkc = lambda half: pl.BlockSpec((None, c, _LANES), lambda bi, p, i: (2 * p + half, ctx_blk + bi, 0))
    hkl = lambda half: pl.BlockSpec((None, s, _LANES), lambda bi, p, i: (2 * p + half, bi, 0))
    yc_x = pl.pallas_call(
        _mla_attn_kernel,
        grid=(b, pairs, nq),
        in_specs=[hq(0), hq(1), hkc(0), hkc(1), hkl(0), hkl(1),
                  pl.BlockSpec((c, _LANES), lambda bi, p, i: (ctx_blk + bi, p)),
                  pl.BlockSpec((s, _LANES), lambda bi, p, i: (bi, p))],
        out_specs=pl.BlockSpec((tq, _LANES), lambda bi, p, i: (bi * nq + i, p)),
        out_shape=jax.ShapeDtypeStruct((dims.BS, w), BF16),
        compiler_params=_cp("parallel", "parallel", "arbitrary"),
    )(q, q, k, k, k, k, v, v)
    if not need_ctx:
        return yc_x
    cq = lambda half: pl.BlockSpec((None, c, _LANES), lambda bi, p: (2 * p + half, ctx_blk + bi, 0))
    yc_s = pl.pallas_call(
        _mla_ctx_kernel,
        grid=(b, pairs),
        in_specs=[cq(0), cq(1), cq(0), cq(1),
                  pl.BlockSpec((c, _LANES), lambda bi, p: (ctx_blk + bi, p))],
        out_specs=pl.BlockSpec((c, _LANES), lambda bi, p: (bi, p)),
        out_shape=jax.ShapeDtypeStruct((b * c, w), BF16),
        compiler_params=_cp("parallel", "parallel"),
    )(q, q, k, k, v)
    return jnp.concatenate([yc_x, yc_s], axis=0)


def _rw_prep_kernel(f_ref, fp_ref, fn_ref, mu_ref, w0_ref, w2_ref, a0_ref, a2_ref, g2_ref, kkg_ref,
                    ka_ref, bd_ref, r_o, v_o, kk_o, g_o, lw_o, kd_o, bb_o, *, s, c, bs, tm):
    w = r_o.shape[1]
    row0 = pl.program_id(0) * tm
    is_lat = row0 < bs
    pos = jnp.where(is_lat, row0 % s, (row0 - bs) % c)
    seglen = jnp.where(is_lat, s, c)
    f = f_ref[...]
    prev_row = jnp.where(pos == 0, 0.0, fp_ref[7:8, :])
    next_row = jnp.where(pos + tm == seglen, 0.0, fn_ref[0:1, :])
    ridx = lax.broadcasted_iota(I32, (tm, 1), 0)
    prev = jnp.where(ridx == 0, prev_row, pltpu.roll(f, 1, 0))
    nxt = jnp.where(ridx == tm - 1, next_row, pltpu.roll(f, tm - 1, 0))
    fm = f + mu_ref[0:1, :] * (prev - f) + mu_ref[1:2, :] * (nxt - f)
    r, k, v = fm[:, 0:w], fm[:, w:2 * w], fm[:, 2 * w:3 * w]
    o = 3 * w
    w_low = jnp.tanh(fm[:, o:o + 128])
    a_low = fm[:, o + 128:o + 256]
    g_low = _sigmoid(fm[:, o + 256:o + 384])
    log_w = -_softplus(-(w0_ref[...] + _dot(w_low, w2_ref[...]))) - 0.5
    lw = -jnp.exp(log_w)
    a = _sigmoid(a0_ref[...] + _dot(a_low, a2_ref[...]))
    g_o[...] = _dot(g_low, g2_ref[...])
    kk = k * kkg_ref[...]
    kk = kk / jnp.maximum(jnp.sqrt(_dot01_right(kk * kk, bd_ref[...])), 1e-12)
    r_o[...] = r
    v_o[...] = v
    kk_o[...] = kk
    for d in range(2):
        ad = a[:, d * w:(d + 1) * w]
        lw_o[d] = lw[:, d * w:(d + 1) * w]
        kd_o[d] = k * (1.0 + (ad - 1.0) * ka_ref[...])
        bb_o[d] = ad * kk


def _rw_prep(f_rw, pw, l, dims):
    m, cols = f_rw.shape
    w = dims.W
    tm = 256
    assert dims.C % tm == 0 and dims.S % tm == 0
    nb8 = m // 8
    lyr = lambda shape: pl.BlockSpec((None,) + shape, lambda i: (l,) + tuple(0 for _ in shape))
    out_w = pl.BlockSpec((tm, w), lambda i: (i, 0))
    out_2w = pl.BlockSpec((2, tm, w), lambda i: (0, i, 0))
    sd = jax.ShapeDtypeStruct
    return pl.pallas_call(
        functools.partial(_rw_prep_kernel, s=dims.S, c=dims.C, bs=dims.BS, tm=tm),
        grid=(m // tm,),
        in_specs=[pl.BlockSpec((tm, cols), lambda i: (i, 0)),
                  pl.BlockSpec((8, cols), lambda i: (jnp.maximum(i * (tm // 8) - 1, 0), 0)),
                  pl.BlockSpec((8, cols), lambda i: (jnp.minimum((i + 1) * (tm // 8), nb8 - 1), 0)),
                  lyr((2, cols)), lyr((1, 2 * w)), lyr((128, 2 * w)), lyr((1, 2 * w)), lyr((128, 2 * w)),
                  lyr((128, w)), lyr((1, w)), lyr((1, w)),
                  pl.BlockSpec((w, w), lambda i: (0, 0))],
        out_specs=[out_w, out_w, out_w, out_w, out_2w, out_2w, out_2w],
        out_shape=[sd((m, w), F32)] * 4 + [sd((2, m, w), F32)] * 3,
        compiler_params=_cp("parallel"),
    )(f_rw, f_rw, f_rw, pw["rw_mu"], pw["rw_w0"], pw["rw_w2"], pw["rw_a0"], pw["rw_a2"], pw["rw_g2"],
      pw["rw_kk"], pw["rw_ka"], _head_blockdiag(w, RW_HEAD_DIM))


def _rw_chunk_kernel(rf, vf, kkf, lwf, kdf, bbf, rb, vb, kkb, lwb, kdb, bbb, yf_ref, yb_ref, h_ref):
    c = RW_CHUNK

    @pl.when(pl.program_id(2) == 0)
    def _():
        h_ref[...] = jnp.zeros_like(h_ref)

    ii = lax.broadcasted_iota(I32, (c, c), 0)
    jj = lax.broadcasted_iota(I32, (c, c), 1)
    tri_f = jnp.where(jj <= ii, 1.0, 0.0).astype(BF16)
    tri_b = jnp.where(jj >= ii, 1.0, 0.0).astype(BF16)
    _, m_e, m_o = _half_masks(F32)

    n2 = 2 * c
    i2 = lax.broadcasted_iota(I32, (n2, n2), 0)
    j2 = lax.broadcasted_iota(I32, (n2, n2), 1)
    same = (i2 // c) == (j2 // c)
    eye2 = jnp.where(i2 == j2, 1.0, 0.0)
    eye = lax.broadcasted_iota(I32, (_LANES, _LANES), 0) == lax.broadcasted_iota(I32, (_LANES, _LANES), 1)
    steps = int(np.log2(c)) - 1

    def stack(x):
        return jnp.concatenate([x * m_e, x * m_o], axis=0).astype(BF16)

    def feats(r_ref, kk_ref, lw_ref, kd_ref, bb_ref, v_ref, sl, tri, last, lag):
        lw = lw_ref[:, sl]
        cum = _dot01_left(tri, lw)
        tot = cum[last:last + 1, :]
        w_inv = jnp.exp(-cum)
        w_end = jnp.exp(tot - cum)
        bb, kd = bb_ref[:, sl], kd_ref[:, sl]
        lag = jnp.where(same, lag, -1)
        return dict(la=stack(-kk_ref[:, sl] * jnp.exp(cum - lw)), lr=stack(r_ref[:, sl] * jnp.exp(cum)),
                    rb=stack(bb * w_inv), rk=stack(kd * w_inv), vm=stack(v_ref[:, sl]),
                    bh=stack(bb * w_end), kh=stack(kd * w_end), wtot=jnp.exp(tot),
                    strict=lag > 0, incl=lag >= 0)

    sls = [slice(p * _LANES, (p + 1) * _LANES) for p in range(h_ref.shape[0])]
    f = []
    for sl in sls:
        f.append(feats(rf, kkf, lwf, kdf, bbf, vf, sl, tri_f, c - 1, i2 - j2))
        f.append(feats(rb, kkb, lwb, kdb, bbb, vb, sl, tri_b, 0, j2 - i2))
    un = range(len(f))
    nmat = [jnp.where(f[k]["strict"], _dot_nt(f[k]["la"], f[k]["rb"]), 0.0) for k in un]
    aak = [jnp.where(f[k]["strict"], _dot_nt(f[k]["la"], f[k]["rk"]), 0.0) for k in un]
    arb = [jnp.where(f[k]["incl"], _dot_nt(f[k]["lr"], f[k]["rb"]), 0.0) for k in un]
    ark = [jnp.where(f[k]["incl"], _dot_nt(f[k]["lr"], f[k]["rk"]), 0.0) for k in un]
    t_inv = [eye2 + nmat[k] for k in un]
    pw = nmat
    for _ in range(steps):
        pw = [_dot(pw[k], pw[k]) for k in un]
        t_inv = [t_inv[k] + _dot(t_inv[k], pw[k]) for k in un]
    x = [_dot(t_inv[k], f[k]["la"]) for k in un]
    av = [_dot(aak[k], f[k]["vm"]) for k in un]
    u = [_dot(t_inv[k], av[k]) for k in un]
    q = [f[k]["lr"].astype(F32) + _dot(arb[k], x[k]) for k in un]
    y0 = [_dot(arb[k], u[k]) + _dot(ark[k], f[k]["vm"]) for k in un]
    h = [h_ref[k // 2, k % 2] for k in un]
    yst = [_dot(q[k], h[k]) + y0[k] for k in un]
    for k in un:
        out_ref = yb_ref if k % 2 else yf_ref
        out_ref[:, sls[k // 2]] = yst[k][0:c] + yst[k][c:n2]
    trans = [jnp.where(eye, f[k]["wtot"], 0.0) + _dot_tn(f[k]["bh"], x[k]) for k in un]
    for k in un:
        h_ref[k // 2, k % 2] = (_dot(trans[k], h[k]) + _dot_tn(f[k]["bh"], u[k])
                                + _dot_tn(f[k]["kh"], f[k]["vm"]))


def _rw_scan(r, v, kk, lw, kd, bb, dims):
    b, s, c, w = dims.B, dims.S, dims.C, dims.W
    ch = RW_CHUNK
    n_c, n_s = c // ch, s // ch
    ctx0 = dims.BS // ch
    pp = RW_PAIRS_PER_STEP
    bw = pp * _LANES
    assert w % bw == 0

    def cf(bi, t):
        return jnp.where(t < n_c, ctx0 + bi * n_c + t, bi * n_s + (t - n_c))

    def cb(bi, t):
        return jnp.where(t < n_c, ctx0 + bi * n_c + (n_c - 1 - t), bi * n_s + (n_s - 1 - (t - n_c)))

    sh = lambda fn: pl.BlockSpec((ch, bw), lambda bi, p, t: (fn(bi, t), p))
    dr = lambda fn, d: pl.BlockSpec((None, ch, bw), lambda bi, p, t: (d, fn(bi, t), p))
    m = dims.M
    return pl.pallas_call(
        _rw_chunk_kernel,
        grid=(b, w // bw, n_c + n_s),
        in_specs=[sh(cf), sh(cf), sh(cf), dr(cf, 0), dr(cf, 0), dr(cf, 0),
                  sh(cb), sh(cb), sh(cb), dr(cb, 1), dr(cb, 1), dr(cb, 1)],
        out_specs=[sh(cf), sh(cb)],
        out_shape=[jax.ShapeDtypeStruct((m, w), F32)] * 2,
        scratch_shapes=[pltpu.VMEM((pp, 2, _LANES, _LANES), F32)],
        compiler_params=_cp("parallel", "parallel", "arbitrary"),
    )(r, v, kk, lw, kd, bb, r, v, kk, lw, kd, bb)


def _rw_readout_kernel(yf_ref, yb_ref, r_ref, v_ref, g_ref, kd_ref, rk_ref, gg_ref, gb_ref, bd_ref, o_ref):
    bd = bd_ref[...]
    inv = 1.0 / RW_HEAD_DIM
    y = yf_ref[...] + yb_ref[...]
    yc = y - _dot01_right(y, bd) * inv
    var = _dot01_right(yc * yc, bd) * inv
    yn = yc * lax.rsqrt(var + RW_GN_EPS)
    bonus = _dot01_right(r_ref[...] * (kd_ref[0] + kd_ref[1]) * rk_ref[...], bd) * v_ref[...]
    o_ref[...] = ((yn * gg_ref[...] + gb_ref[...] + bonus) * g_ref[...]).astype(o_ref.dtype)


def _rw_readout(yf, yb, r, v, g, kd, pw, l, dims, rows):
    w = dims.W
    tm = 256
    blk = pl.BlockSpec((tm, w), lambda i: (i, 0))
    lyr = pl.BlockSpec((None, 1, w), lambda i: (l, 0, 0))
    return pl.pallas_call(
        _rw_readout_kernel,
        grid=(rows // tm,),
        in_specs=[blk, blk, blk, blk, blk, pl.BlockSpec((2, tm, w), lambda i: (0, i, 0)), lyr, lyr, lyr,
                  pl.BlockSpec((w, w), lambda i: (0, 0))],
        out_specs=blk,
        out_shape=jax.ShapeDtypeStruct((rows, w), BF16),
        compiler_params=_cp("parallel"),
    )(yf, yb, r, v, g, kd, pw["rw_rk"], pw["rw_gn_g"], pw["rw_gn_b"], _head_blockdiag(w, RW_HEAD_DIM))


def _s5_matrices(a_re, a_im, log_dt, b_re, b_im, c_re, c_im, d_skip):
    hi = lax.Precision.HIGHEST
    n_g, n_p, n_j = b_re.shape[1], b_re.shape[2], b_re.shape[3]
    ct = S5_CHUNK
    dt = jnp.exp(log_dt)[..., None]
    mag = jnp.exp(a_re * dt)
    lam_re, lam_im = mag * jnp.cos(a_im * dt), mag * jnp.sin(a_im * dt)
    den = a_re * a_re + a_im * a_im
    f_re = ((lam_re - 1.0) * a_re + lam_im * a_im) / den
    f_im = (lam_im * a_re - (lam_re - 1.0) * a_im) / den
    bb_re = f_re[..., None] * b_re - f_im[..., None] * b_im
    bb_im = f_re[..., None] * b_im + f_im[..., None] * b_re
    p_re, p_im = [jnp.ones_like(lam_re)], [jnp.zeros_like(lam_im)]
    for _ in range(ct):
        p_re.append(p_re[-1] * lam_re - p_im[-1] * lam_im)
        p_im.append(p_re[-2] * lam_im + p_im[-1] * lam_re)
    p_re, p_im = jnp.stack(p_re), jnp.stack(p_im)
    cl_re = c_re[None] * p_re[:, :, :, None, :] - c_im[None] * p_im[:, :, :, None, :]
    cl_im = c_re[None] * p_im[:, :, :, None, :] + c_im[None] * p_re[:, :, :, None, :]
    kern = jnp.sum(cl_re[..., None] * bb_re[None, :, :, None] - cl_im[..., None] * bb_im[None, :, :, None],
                   axis=4)
    j_idx = np.arange(ct)[:, None]
    t_idx = np.arange(ct)[None, :]
    lag_f, lag_b = t_idx - j_idx, j_idx - t_idx
    sel_f = jnp.asarray(lag_f[:, :, None] == np.arange(ct + 1), F32)
    sel_b = jnp.asarray(lag_b[:, :, None] == np.arange(ct + 1), F32)
    toep = (jnp.einsum('jtn,ngim->gjmti', sel_f, kern[:, 0], precision=hi)
            + jnp.einsum('jtn,ngim->gjmti', sel_b, kern[:, 1], precision=hi))
    skip = (jnp.eye(ct)[None, :, None, :, None] * jnp.eye(n_j)[None, None, :, None, :]
            * d_skip.reshape(n_g, 1, 1, 1, n_j))
    toep = (toep + skip).reshape(n_g, ct * n_j, ct * n_j)

    def powers(z, d, first, reverse):
        z = z[first:first + ct, d]
        return z[::-1] if reverse else z

    def state_in(first, reverse, d):
        pr, pi = powers(p_re, d, first, reverse), powers(p_im, d, first, reverse)
        re = pr[..., None] * bb_re[d][None] - pi[..., None] * bb_im[d][None]
        im = pr[..., None] * bb_im[d][None] + pi[..., None] * bb_re[d][None]
        fix = lambda z: z.transpose(1, 0, 3, 2).reshape(n_g, ct * n_j, n_p)
        return fix(re), fix(im)

    def state_out(first, reverse, d):
        fix = lambda z: z.transpose(1, 3, 0, 2).reshape(n_g, n_p, ct * n_j)
        return fix(powers(cl_re, d, first, reverse)), fix(-powers(cl_im, d, first, reverse))

    rf_re, rf_im = state_in(0, True, 0)
    rb_re, rb_im = state_in(0, False, 1)
    of_re, of_im = state_out(1, False, 0)
    ob_re, ob_im = state_out(1, True, 1)
    half = (np.arange(n_g) % 2)[:, None, None]

    def lane_half(z):
        zero = jnp.zeros_like(z)
        return jnp.concatenate([jnp.where(half == 0, z, zero), jnp.where(half == 1, z, zero)], axis=-1)

    def row_half(z):
        zero = jnp.zeros_like(z)
        return jnp.concatenate([jnp.where(half == 0, z, zero), jnp.where(half == 1, z, zero)], axis=1)

    r_re = jnp.stack([lane_half(rf_re), lane_half(rb_re)]).astype(BF16)
    r_im = jnp.stack([lane_half(rf_im), lane_half(rb_im)]).astype(BF16)
    o_re = jnp.stack([row_half(of_re), row_half(ob_re)]).astype(BF16)
    o_im = jnp.stack([row_half(of_im), row_half(ob_im)]).astype(BF16)
    lam_c = (p_re[ct].reshape(2, 1, n_g * n_p), p_im[ct].reshape(2, 1, n_g * n_p))
    return toep.astype(BF16), r_re, r_im, o_re, o_im, lam_c


def _s5_z_kernel(u_ref, rre_ref, rim_ref, zre_o, zim_o):
    u0, u1 = u_ref[0].astype(BF16), u_ref[1].astype(BF16)
    for d in range(2):
        zre_o[d] = (jnp.dot(u0, rre_ref[d, 0], preferred_element_type=F32)
                    + jnp.dot(u1, rre_ref[d, 1], preferred_element_type=F32))
        zim_o[d] = (jnp.dot(u0, rim_ref[d, 0], preferred_element_type=F32)
                    + jnp.dot(u1, rim_ref[d, 1], preferred_element_type=F32))


def _s5_scan_kernel(zre_ref, zim_ref, lre_ref, lim_ref, xre_o, xim_o, *, n_b, n_s, n_c):
    lanes = zre_ref.shape[2]
    chains = [(d, b) for d in range(2) for b in range(n_b)]
    lam = [(lre_ref[d], lim_ref[d]) for d in range(2)]

    def run(base_of, length, carry):
        def body(t, carry):
            new = []
            for (d, b), (xr, xi) in zip(chains, carry):
                row = base_of(b) + (t if d == 0 else length - 1 - t)
                xre_o[d, pl.ds(row, 1), :] = xr
                xim_o[d, pl.ds(row, 1), :] = xi
                lr, li = lam[d]
                new.append((lr * xr - li * xi + zre_ref[d, pl.ds(row, 1), :],
                            lr * xi + li * xr + zim_ref[d, pl.ds(row, 1), :]))
            return tuple(new)
        return lax.fori_loop(0, length, body, carry)

    zero = jnp.zeros((1, lanes), F32)
    carry = tuple((zero, zero) for _ in chains)
    carry = run(lambda b: n_b * n_s + b * n_c, n_c, carry)
    run(lambda b: b * n_s, n_s, carry)


def _s5_y_kernel(u_ref, toep_ref, xre_ref, xim_ref, ore_ref, oim_ref, y_o):
    xr = [xre_ref[d].astype(BF16) for d in range(2)]
    xi = [xim_ref[d].astype(BF16) for d in range(2)]
    for g in range(2):
        y = jnp.dot(u_ref[g].astype(BF16), toep_ref[g], preferred_element_type=F32)
        for d in range(2):
            y = y + jnp.dot(xr[d], ore_ref[d, g], preferred_element_type=F32)
            y = y + jnp.dot(xi[d], oim_ref[d, g], preferred_element_type=F32)
        y_o[g] = y


def _s5_glu_kernel(y_ref, w_ref, b_ref, o_ref, wbf_ref):
    @pl.when(pl.program_id(0) == 0)
    def _():
        wbf_ref[...] = w_ref[...].astype(BF16)

    y = y_ref[...]
    y = 0.5 * y * (1.0 + jnp.tanh(0.7978845608028654 * (y + 0.044715 * y * y * y)))
    z = jnp.dot(y.astype(BF16), wbf_ref[...], preferred_element_type=F32) + b_ref[...]
    o_ref[...] = (y * _sigmoid(z)).astype(o_ref.dtype)


def _s5_mixer(f_s5, mats, w_glu, b_glu3, l, dims, rows):
    toep, r_re, r_im, o_re, o_im, (lam_re, lam_im) = mats
    m, w = f_s5.shape
    ct = S5_CHUNK
    n_g = w // S5_GROUP
    n_ch = m // ct
    gw = ct * S5_GROUP
    u = f_s5.reshape(n_ch, ct, n_g, S5_GROUP).transpose(2, 0, 1, 3).reshape(n_g, n_ch, gw)
    sl = n_g * S5_STATE
    sd = jax.ShapeDtypeStruct
    zre, zim = pl.pallas_call(
        _s5_z_kernel,
        grid=(n_g // 2,),
        in_specs=[pl.BlockSpec((2, n_ch, gw), lambda g: (g, 0, 0)),
                  pl.BlockSpec((2, 2, gw, _LANES), lambda g: (0, g, 0, 0)),
                  pl.BlockSpec((2, 2, gw, _LANES), lambda g: (0, g, 0, 0))],
        out_specs=[pl.BlockSpec((2, n_ch, _LANES), lambda g: (0, 0, g))] * 2,
        out_shape=[sd((2, n_ch, sl), F32)] * 2,
        compiler_params=_cp("parallel"),
    )(u, r_re, r_im)
    lb = 512
    xre, xim = pl.pallas_call(
        functools.partial(_s5_scan_kernel, n_b=dims.B, n_s=dims.S // ct, n_c=dims.C // ct),
        grid=(sl // lb,),
        in_specs=[pl.BlockSpec((2, n_ch, lb), lambda j: (0, 0, j))] * 2
                 + [pl.BlockSpec((2, 1, lb), lambda j: (0, 0, j))] * 2,
        out_specs=[pl.BlockSpec((2, n_ch, lb), lambda j: (0, 0, j))] * 2,
        out_shape=[sd((2, n_ch, sl), F32)] * 2,
        compiler_params=_cp("parallel"),
    )(zre, zim, lam_re, lam_im)
    y = pl.pallas_call(
        _s5_y_kernel,
        grid=(n_g // 2,),
        in_specs=[pl.BlockSpec((2, n_ch, gw), lambda g: (g, 0, 0)),
                  pl.BlockSpec((2, gw, gw), lambda g: (g, 0, 0)),
                  pl.BlockSpec((2, n_ch, _LANES), lambda g: (0, 0, g)),
                  pl.BlockSpec((2, n_ch, _LANES), lambda g: (0, 0, g)),
                  pl.BlockSpec((2, 2, _LANES, gw), lambda g: (0, g, 0, 0)),
                  pl.BlockSpec((2, 2, _LANES, gw), lambda g: (0, g, 0, 0))],
        out_specs=pl.BlockSpec((2, n_ch, gw), lambda g: (g, 0, 0)),
        out_shape=sd((n_g, n_ch, gw), F32),
        compiler_params=_cp("parallel"),
    )(u, toep, xre, xim, o_re, o_im)
    y = y.reshape(n_g, n_ch, ct, S5_GROUP).transpose(1, 2, 0, 3).reshape(m, w)
    tm = 256
    return pl.pallas_call(
        _s5_glu_kernel,
        grid=(rows // tm,),
        in_specs=[pl.BlockSpec((tm, w), lambda i: (i, 0)),
                  pl.BlockSpec((None, w, w), lambda i: (l, 0, 0)),
                  pl.BlockSpec((None, 1, w), lambda i: (l, 0, 0))],
        out_specs=pl.BlockSpec((tm, w), lambda i: (i, 0)),
        out_shape=sd((rows, w), BF16),
        scratch_shapes=[pltpu.VMEM((w, w), BF16)],
        compiler_params=_cp("arbitrary"),
    )(y, w_glu, b_glu3)


def _merge_kernel(xn_ref, ya_ref, yb_ref, yc_ref, yd_ref, wg0, wg1, wg2, wg3, bg0, bg1, bg2, bg3, wb_ref,
                  o_ref, wg_bf, wb_bf):
    wgs = (wg0, wg1, wg2, wg3)

    @pl.when(pl.program_id(1) == 0)
    def _():
        for n in range(4):
            wg_bf[n] = wgs[n][...].astype(BF16)
            wb_bf[n] = wb_ref[n].astype(BF16)

    xn = xn_ref[...]
    acc = None
    for n, (y_ref, bg) in enumerate(zip((ya_ref, yb_ref, yc_ref, yd_ref), (bg0, bg1, bg2, bg3))):
        gate = _sigmoid(jnp.dot(xn, wg_bf[n], preferred_element_type=F32) + bg[...])
        up = jnp.dot(y_ref[...], wb_bf[n], preferred_element_type=F32)
        acc = gate * up if acc is None else acc + gate * up
    o_ref[...] = acc.astype(o_ref.dtype)


def _merge(xn, ys, w_gate, b_gate3, w_branch, l, rows):
    d = xn.shape[1]
    w = ys[0].shape[1]
    tn = 256
    tm = _pick(rows, (1088, 1024, 512, 256))
    nb = d // tn
    wg = lambda n: pl.BlockSpec((None, d, tn), lambda j, i: (l, 0, n * nb + j))
    bg = lambda n: pl.BlockSpec((None, 1, tn), lambda j, i: (l, 0, n * nb + j))
    yspec = pl.BlockSpec((tm, w), lambda j, i: (i, 0))
    return pl.pallas_call(
        _merge_kernel,
        grid=(nb, rows // tm),
        in_specs=[pl.BlockSpec((tm, d), lambda j, i: (i, 0)), yspec, yspec, yspec, yspec,
                  wg(0), wg(1), wg(2), wg(3), bg(0), bg(1), bg(2), bg(3),
                  pl.BlockSpec((None, 4, w, tn), lambda j, i: (l, 0, 0, j))],
        out_specs=pl.BlockSpec((tm, tn), lambda j, i: (i, j)),
        out_shape=jax.ShapeDtypeStruct((rows, d), BF16),
        scratch_shapes=[pltpu.VMEM((4, d, tn), BF16), pltpu.VMEM((4, w, tn), BF16)],
        compiler_params=_cp("arbitrary", "arbitrary"),
    )(xn, *ys, w_gate, w_gate, w_gate, w_gate, b_gate3, b_gate3, b_gate3, b_gate3, w_branch)


def _router_kernel(h_ref, wr_ref, br_ref, idx_o, gate_o, rank_o, cnt_o, carry):
    tm = h_ref.shape[0]

    @pl.when(pl.program_id(0) == 0)
    def _():
        carry[...] = jnp.zeros_like(carry)

    h_hi, h_lo = _split(h_ref[...], 2)
    w_hi, w_lo = _split(wr_ref[...], 2)
    logits = (jnp.dot(h_hi, w_hi, preferred_element_type=F32) + jnp.dot(h_lo, w_hi, preferred_element_type=F32)
              + jnp.dot(h_hi, w_lo, preferred_element_type=F32)) + br_ref[...]
    lane = lax.broadcasted_iota(I32, (tm, _LANES), 1)
    vals, idxs, hots = [], [], []
    cur = logits
    for _ in range(TOP_K):
        mx = jnp.max(cur, axis=-1, keepdims=True)
        ix = jnp.min(jnp.where(cur == mx, lane, _LANES), axis=-1, keepdims=True)
        hot = lane == ix
        vals.append(mx)
        idxs.append(ix)
        hots.append(hot)
        cur = jnp.where(hot, -jnp.inf, cur)
    ex = [jnp.exp(v - vals[0]) for v in vals]
    inv = 1.0 / sum(ex)
    sel = sum(jnp.where(h, 1.0, 0.0) for h in hots)
    ri = lax.broadcasted_iota(I32, (tm, tm), 0)
    ci = lax.broadcasted_iota(I32, (tm, tm), 1)
    earlier = jnp.where(ci < ri, 1.0, 0.0).astype(BF16)
    before = jnp.dot(earlier, sel.astype(BF16), preferred_element_type=F32) + carry[...]
    idx_out = jnp.zeros((tm, _LANES), I32)
    gate_out = jnp.zeros((tm, _LANES), F32)
    rank_out = jnp.zeros((tm, _LANES), I32)
    for k in range(TOP_K):
        rk = jnp.sum(jnp.where(hots[k], before, 0.0), axis=-1, keepdims=True)
        idx_out = jnp.where(lane == k, idxs[k], idx_out)
        gate_out = jnp.where(lane == k, ex[k] * inv, gate_out)
        rank_out = jnp.where(lane == k, rk.astype(I32), rank_out)
    idx_o[...] = idx_out
    gate_o[...] = gate_out
    rank_o[...] = rank_out
    carry[...] = carry[...] + jnp.sum(sel, axis=0, keepdims=True)
    cnt_o[...] = jnp.broadcast_to(carry[...], cnt_o.shape)


def _router(h, wr_pad, br_pad, l, rows):
    d = h.shape[1]
    tm = 256
    blk = pl.BlockSpec((tm, _LANES), lambda i: (i, 0))
    sd = jax.ShapeDtypeStruct
    return pl.pallas_call(
        _router_kernel,
        grid=(rows // tm,),
        in_specs=[pl.BlockSpec((tm, d), lambda i: (i, 0)),
                  pl.BlockSpec((None, d, _LANES), lambda i: (l, 0, 0)),
                  pl.BlockSpec((None, 1, _LANES), lambda i: (l, 0, 0))],
        out_specs=[blk, blk, blk, pl.BlockSpec((8, _LANES), lambda i: (0, 0))],
        out_shape=[sd((rows, _LANES), I32), sd((rows, _LANES), F32), sd((rows, _LANES), I32),
                   sd((8, _LANES), F32)],
        scratch_shapes=[pltpu.VMEM((1, _LANES), F32)],
        compiler_params=_cp("arbitrary"),
    )(h, wr_pad, br_pad)


def _row_copy(src_hbm, row, dst, slot, sem):
    return pltpu.make_async_copy(src_hbm.at[pl.ds(row, 1), :], dst.at[pl.ds(slot, 1), :], sem)


def _issue_rows(idx_ref, n, src_hbm, dst, sem):
    for r in range(n):
        _row_copy(src_hbm, idx_ref[0, r], dst, r, sem).start()


def _expert_kernel(be_ref, first_ref, used_ref, tok_ref, tok_next_ref, h_hbm, wgu_ref, bgu_ref, wdn_ref,
                   bdn_ref, o_ref, xbuf, sems, wgu_bf, wdn_bf):
    i = pl.program_id(0)
    n_used = used_ref[0]
    slot = i % 2
    blk = xbuf.shape[1]

    @pl.when(i == 0)
    def _():
        _issue_rows(tok_ref, blk, h_hbm, xbuf.at[0], sems.at[0])

    @pl.when(first_ref[i] == 1)
    def _():
        wgu_bf[...] = wgu_ref[...].astype(BF16)
        wdn_bf[...] = wdn_ref[...].astype(BF16)

    @pl.when(i <= n_used)
    def _():
        pltpu.make_async_copy(h_hbm.at[pl.ds(0, blk), :], xbuf.at[slot], sems.at[slot]).wait()

    @pl.when(i < n_used)
    def _():
        _issue_rows(tok_next_ref, blk, h_hbm, xbuf.at[1 - slot], sems.at[1 - slot])
        gu = jnp.dot(xbuf[slot].astype(BF16), wgu_bf[...], preferred_element_type=F32) + bgu_ref[...]
        g_lin = jnp.minimum(gu[:, :D_EXPERT], SWIGLU_LIMIT)
        u_lin = jnp.clip(gu[:, D_EXPERT:], -SWIGLU_LIMIT, SWIGLU_LIMIT)
        act = (u_lin + 1.0) * g_lin * _sigmoid(SWIGLU_ALPHA * g_lin)
        o_ref[...] = jnp.dot(act.astype(BF16), wdn_bf[...], preferred_element_type=F32) + bdn_ref[...]

    @pl.when(i >= n_used)
    def _():
        o_ref[...] = jnp.zeros_like(o_ref)


def _moe_experts(h, slot_tok3, block_e, first, n_used, w_gu, b_gu4, w_dn, b_dn4, l):
    n_blocks = slot_tok3.shape[0] - 1
    blk = slot_tok3.shape[2]
    d = h.shape[1]
    de2 = w_gu.shape[3]
    grid_spec = pltpu.PrefetchScalarGridSpec(
        num_scalar_prefetch=3,
        grid=(n_blocks,),
        in_specs=[pl.BlockSpec((None, 1, blk), lambda i, be, fi, us: (i, 0, 0), memory_space=pltpu.SMEM),
                  pl.BlockSpec((None, 1, blk), lambda i, be, fi, us: (i + 1, 0, 0), memory_space=pltpu.SMEM),
                  pl.BlockSpec(memory_space=pl.ANY),
                  pl.BlockSpec((None, None, d, de2), lambda i, be, fi, us: (l, be[i], 0, 0)),
                  pl.BlockSpec((None, None, 1, de2), lambda i, be, fi, us: (l, be[i], 0, 0)),
                  pl.BlockSpec((None, None, de2 // 2, d), lambda i, be, fi, us: (l, be[i], 0, 0)),
                  pl.BlockSpec((None, None, 1, d), lambda i, be, fi, us: (l, be[i], 0, 0))],
        out_specs=pl.BlockSpec((blk, d), lambda i, be, fi, us: (i, 0)),
        scratch_shapes=[pltpu.VMEM((2, blk, d), F32), pltpu.SemaphoreType.DMA((2,)),
                        pltpu.VMEM((d, de2), BF16), pltpu.VMEM((de2 // 2, d), BF16)])
    return pl.pallas_call(
        _expert_kernel,
        grid_spec=grid_spec,
        out_shape=jax.ShapeDtypeStruct((n_blocks * blk, d), F32),
        compiler_params=_cp("arbitrary"),
    )(block_e, first, n_used, slot_tok3, slot_tok3, h, w_gu, b_gu4, w_dn, b_dn4)


def _combine_kernel(dest_ref, dest_next_ref, gate_ref, x_ref, g2_ref, ys_hbm, o_ref, buf, sems):
    tm = x_ref.shape[0]
    i = pl.program_id(0)
    slot = i % 2

    def issue(idx_ref, s):
        for r in range(tm):
            for k in range(TOP_K):
                _row_copy(ys_hbm, idx_ref[0, r * TOP_K + k], buf.at[s, k], r, sems.at[s]).start()

    @pl.when(i == 0)
    def _():
        issue(dest_ref, 0)

    @pl.when(i + 1 < pl.num_programs(0))
    def _():
        issue(dest_next_ref, 1 - slot)

    for k in range(TOP_K):
        pltpu.make_async_copy(ys_hbm.at[pl.ds(0, tm), :], buf.at[slot, k], sems.at[slot]).wait()
    gate = gate_ref[...]
    f = None
    for k in range(TOP_K):
        term = gate[:, k:k + 1] * buf[slot, k]
        f = term if f is None else f + term
    o_ref[...] = x_ref[...] + g2_ref[...] * f


def _moe_combine(x, ys, dest3, gate, mod, dims, rows):
    d = x.shape[1]
    tm = MOE_COMBINE_ROWS
    seg = dims.seg_fn(tm)
    return pl.pallas_call(
        _combine_kernel,
        grid=(rows // tm,),
        in_specs=[pl.BlockSpec((None, 1, tm * TOP_K), lambda i: (i, 0, 0), memory_space=pltpu.SMEM),
                  pl.BlockSpec((None, 1, tm * TOP_K), lambda i: (i + 1, 0, 0), memory_space=pltpu.SMEM),
                  pl.BlockSpec((tm, _LANES), lambda i: (i, 0)),
                  pl.BlockSpec((tm, d), lambda i: (i, 0)),
                  pl.BlockSpec((None, 1, d), lambda i: (seg(i), 0, 5)),
                  pl.BlockSpec(memory_space=pl.ANY)],
        out_specs=pl.BlockSpec((tm, d), lambda i: (i, 0)),
        out_shape=jax.ShapeDtypeStruct((rows, d), F32),
        scratch_shapes=[pltpu.VMEM((2, TOP_K, tm, d), F32), pltpu.SemaphoreType.DMA((2,))],
        compiler_params=_cp("arbitrary"),
    )(dest3, dest3, gate, x, mod, ys)


def _moe(x, h, mod, pw, l, dims, rows):
    idx, gate, rank, cnt = _router(h, pw["moe_wr"], pw["moe_br"], l, rows)
    blk = MOE_BLOCK
    n_blocks = -(-(rows * TOP_K) // blk) + N_EXPERTS + 1
    counts = cnt[0, :N_EXPERTS].astype(I32)
    padded = (counts + blk - 1) // blk * blk
    pad_end = jnp.cumsum(padded)
    pad_start = pad_end - padded
    top_i = idx[:, :TOP_K]
    start_of = jnp.sum(jnp.where(top_i[..., None] == jnp.arange(N_EXPERTS, dtype=I32), pad_start, 0), axis=-1)
    dest = start_of + rank[:, :TOP_K]
    tok = jnp.broadcast_to(jnp.arange(rows, dtype=I32)[:, None], dest.shape)
    slot_tok = jnp.zeros(((n_blocks + 1) * blk,), I32).at[dest.reshape(-1)].set(tok.reshape(-1))
    block_start = jnp.arange(n_blocks, dtype=I32) * blk
    block_e = jnp.minimum(jnp.sum((pad_end[None, :] <= block_start[:, None]).astype(I32), axis=1), N_EXPERTS - 1)
    first = jnp.concatenate([jnp.ones((1,), I32), (block_e[1:] != block_e[:-1]).astype(I32)])
    n_used = (pad_end[-1:] // blk).astype(I32)
    ys = _moe_experts(h, slot_tok.reshape(n_blocks + 1, 1, blk), block_e, first, n_used, pw["moe_w_gu"],
                      pw["moe_b_gu"], pw["moe_w_dn"], pw["moe_b_dn"], l)
    dest3 = dest.reshape(rows // MOE_COMBINE_ROWS, 1, MOE_COMBINE_ROWS * TOP_K)
    dest3 = jnp.concatenate([dest3, jnp.zeros_like(dest3[:1])], axis=0)
    return _moe_combine(x, ys, dest3, gate, mod, dims, rows)


def _prep_params(p, dims):
    n_l = p["w_in"].shape[0]
    w = dims.W
    o_mla = 3 * w + (3 * w + 384)
    o_s5 = o_mla + MLA_Q_RANK + MLA_KV_RANK + MLA_ROPE
    pw = dict(p)
    row3 = lambda a: a.reshape(a.shape[0], 1, -1)
    pw["norm1_g"], pw["norm2_g"] = row3(p["norm1_g"]), row3(p["norm2_g"])
    pw["b_mod"], pw["b_gate"] = row3(p["b_mod"]), row3(p["b_gate"])
    pw["w_in_mla"] = p["w_in"][:, :, o_mla:o_s5]
    pw["w_in_s5"] = p["w_in"][:, :, o_s5:]
    zeros = jnp.zeros_like(p["rw_w2"][:, 0])
    bdiag = lambda x: jnp.concatenate([jnp.concatenate([x[:, 0], zeros], axis=-1),
                                       jnp.concatenate([zeros, x[:, 1]], axis=-1)], axis=1)
    pw["rw_w2"], pw["rw_a2"] = bdiag(p["rw_w2"]), bdiag(p["rw_a2"])
    pw["rw_w0"], pw["rw_a0"] = row3(p["rw_w0"]), row3(p["rw_a0"])
    pw["rw_kk"], pw["rw_ka"] = row3(p["rw_kk"]), row3(p["rw_ka"])
    pw["rw_rk"], pw["rw_gn_g"], pw["rw_gn_b"] = row3(p["rw_rk"]), row3(p["rw_gn_g"]), row3(p["rw_gn_b"])
    pad_h = lambda x, n: jnp.pad(x, ((0, 0), (0, 0), (0, 0), (0, _LANES - n)))
    wq = p["mla_wq_up"].reshape(n_l, MLA_Q_RANK, MLA_HEADS, MLA_QK)
    pw["mla_wq"] = pad_h(wq, MLA_QK).reshape(n_l, MLA_Q_RANK, -1).astype(BF16)
    wkv = p["mla_wkv_up"].reshape(n_l, MLA_KV_RANK, MLA_HEADS, MLA_NOPE + MLA_V)
    pw["mla_wk"] = pad_h(wkv[..., :MLA_NOPE], MLA_NOPE).reshape(n_l, MLA_KV_RANK, -1).astype(BF16)
    pw["mla_wv"] = wkv[..., MLA_NOPE:].reshape(n_l, MLA_KV_RANK, -1).astype(BF16)
    pw["mla_place"] = jnp.asarray(np.arange(MLA_ROPE)[:, None] + MLA_NOPE == np.arange(_LANES)[None, :], BF16)
    pw["mla_gcq"], pw["mla_gckv"] = row3(p["mla_qn_g"]), row3(p["mla_kvn_g"])
    pad_g = lambda g: jnp.pad(g, ((0, 0), (0, _LANES - MLA_QK))).reshape(n_l, 1, _LANES)
    pw["mla_gq"] = pad_g(p["mla_qkn_q"] * (MLA_QK ** -0.5 * _LOG2E))
    pw["mla_gk"] = pad_g(p["mla_qkn_k"])
    pw["s5_b_glu"] = row3(p["s5_b_glu"])
    pw["moe_wr"] = jnp.pad(p["moe_wr"], ((0, 0), (0, 0), (0, _LANES - N_EXPERTS)))
    pw["moe_br"] = jnp.pad(p["moe_br"], ((0, 0), (0, _LANES - N_EXPERTS)), constant_values=-1e30)[:, None, :]
    pw["moe_b_gu"] = p["moe_b_gu"][:, :, None, :]
    pw["moe_b_dn"] = p["moe_b_dn"][:, :, None, :]
    return pw


def _residual_epilogue(acc, x, g):
    return x + g * acc


def _layer(xs, l, mod, pw, ropes, dims, need_ctx):
    m, d, w = dims.M, dims.D, dims.W
    rows = m if need_ctx else dims.BS
    tm_all = _pick(m, (1088, 1024, 512, 256))
    xn = _norm_mod(xs, pw["norm1_g"], l, mod, 0, 1, dims, m, BF16)
    f_na = _mm(xn, pw["w_in"], l, tm=tm_all, tn=512, col0=0, ncols=3 * w)
    f_rw = _mm(xn, pw["w_in"], l, tm=tm_all, tn=384, col0=3 * w, ncols=3 * w + 384)
    f_mla = _mm(xn, pw["w_in_mla"], l, tm=tm_all, tn=pw["w_in_mla"].shape[2])
    f_s5 = _mm(xn, pw["w_in_s5"], l, tm=tm_all, tn=w, out_dtype=BF16)

    qkv = _na_prep(f_na, pw["na_qn_g"][l], pw["na_kn_g"][l], dims)
    ya = _na_attention(qkv, _na_bias_tables(pw["na_rpb"][l], dims.S // GRID_W), dims, need_ctx)

    r, v, kk, g, lw, kd, bb = _rw_prep(f_rw, pw, l, dims)
    yf, yb_ = _rw_scan(r, v, kk, lw, kd, bb, dims)
    yb = _rw_readout(yf, yb_, r, v, g, kd, pw, l, dims, rows)

    q, k, vv = _mla_prep(f_mla, pw, l, ropes, dims)
    yc = _mla_attention(q, k, vv, dims, need_ctx)

    mats = _s5_matrices(pw["s5_a_re"][l], pw["s5_a_im"][l], pw["s5_log_dt"][l], pw["s5_b_re"][l],
                        pw["s5_b_im"][l], pw["s5_c_re"][l], pw["s5_c_im"][l], pw["s5_d"][l])
    yd = _s5_mixer(f_s5, mats, pw["s5_w_glu"], pw["s5_b_glu"], l, dims, rows)

    merged = _merge(xn, (ya, yb, yc, yd), pw["w_gate"], pw["b_gate"], pw["w_branch"], l, rows)
    tm_seg = _pick(dims.S, (512, 256))
    seg = dims.seg_fn(tm_seg)
    tn = 512
    xs = _mm(merged, pw["w_out"], l, tm=tm_seg, tn=tn, rows=rows, epilogue=_residual_epilogue,
             extras=((xs, pl.BlockSpec((tm_seg, tn), lambda j, i: (i, j))),
                     (mod, pl.BlockSpec((None, 1, tn), lambda j, i: (seg(i), 0, 2 * (d // tn) + j)))))
    h = _norm_mod(xs, pw["norm2_g"], l, mod, 3, 4, dims, rows, F32)
    return _moe(xs, h, mod, pw, l, dims, rows)


def kernel(x, c, ctx, c_ctx, w_mod, b_mod, norm1_g, norm2_g, w_in, w_gate, b_gate, w_branch, w_out, na_qn_g, na_kn_g, na_rpb, rw_mu, rw_w0, rw_w2, rw_a0, rw_a2, rw_g2, rw_kk, rw_ka, rw_rk, rw_gn_g, rw_gn_b, mla_qn_g, mla_wq_up, mla_kvn_g, mla_wkv_up, mla_qkn_q, mla_qkn_k, s5_a_re, s5_a_im, s5_log_dt, s5_b_re, s5_b_im, s5_c_re, s5_c_im, s5_d, s5_w_glu, s5_b_glu, moe_wr, moe_br, moe_w_gu, moe_b_gu, moe_w_dn, moe_b_dn):
    params = dict(w_mod=w_mod, b_mod=b_mod, norm1_g=norm1_g, norm2_g=norm2_g, w_in=w_in, w_gate=w_gate,
                  b_gate=b_gate, w_branch=w_branch, w_out=w_out, na_qn_g=na_qn_g, na_kn_g=na_kn_g,
                  na_rpb=na_rpb, rw_mu=rw_mu, rw_w0=rw_w0, rw_w2=rw_w2, rw_a0=rw_a0, rw_a2=rw_a2,
                  rw_g2=rw_g2, rw_kk=rw_kk, rw_ka=rw_ka, rw_rk=rw_rk, rw_gn_g=rw_gn_g, rw_gn_b=rw_gn_b,
                  mla_qn_g=mla_qn_g, mla_wq_up=mla_wq_up, mla_kvn_g=mla_kvn_g, mla_wkv_up=mla_wkv_up,
                  mla_qkn_q=mla_qkn_q, mla_qkn_k=mla_qkn_k, s5_a_re=s5_a_re, s5_a_im=s5_a_im,
                  s5_log_dt=s5_log_dt, s5_b_re=s5_b_re, s5_b_im=s5_b_im, s5_c_re=s5_c_re, s5_c_im=s5_c_im,
                  s5_d=s5_d, s5_w_glu=s5_w_glu, s5_b_glu=s5_b_glu, moe_wr=moe_wr, moe_br=moe_br,
                  moe_w_gu=moe_w_gu, moe_b_gu=moe_b_gu, moe_w_dn=moe_w_dn, moe_b_dn=moe_b_dn)
    b, s, d = x.shape
    c_len = ctx.shape[1]
    depth = w_mod.shape[0]
    dims = _Dims(b, s, c_len, d)
    assert b + 1 <= 8
    pw = _prep_params(params, dims)
    cond = jnp.concatenate([jax.nn.silu(c), jax.nn.silu(c_ctx)[None],
                            jnp.zeros((8 - b - 1, d), F32)], axis=0)
    ropes = _rope_tables(s, 256)
    xs = jnp.concatenate([x.reshape(b * s, d), ctx.reshape(b * c_len, d)], axis=0)
    for l in range(depth):
        mod = _mm(cond, pw["w_mod"], l, tm=8, tn=1024, bias=pw["b_mod"]).reshape(8, 1, 6 * d)
        xs = _layer(xs, l, mod, pw, ropes, dims, need_ctx=l < depth - 1)
    return xs[:b * s].reshape(b, s, d)
```

```python
import functools

import numpy as np
import jax
import jax.numpy as jnp
from jax import lax
from jax.experimental import pallas as pl
from jax.experimental.pallas import tpu as pltpu

F32 = jnp.float32
BF16 = jnp.bfloat16
I32 = jnp.int32

_VMEM_LIMIT_BYTES = 56 * 1024 * 1024
_LANES = 128
_LOG2E = 1.4426950408889634

NORM_EPS = 1e-6
GRID_W = 64
NA_HEADS = 8
NA_HEAD_DIM = 64
NA_WIN_R = 8
NA_WIN_C = 16
NA_QROWS = 4
NA_BAND_ROWS = NA_QROWS + NA_WIN_R - 1
NA_HEADS_TOGETHER = 2
RW_HEAD_DIM = 64
RW_GN_EPS = 64e-5
RW_CHUNK = 64
RW_PAIRS_PER_STEP = 4
MLA_HEADS = 8
MLA_NOPE = 64
MLA_ROPE = 32
MLA_QK = MLA_NOPE + MLA_ROPE
MLA_V = 64
MLA_Q_RANK = 512
MLA_KV_RANK = 256
ROPE_THETA = 10000.0
MLA_KEY_CHUNK = 256
MLA_Q_TILE = 512
S5_GROUP = 16
S5_STATE = 64
S5_CHUNK = 16
N_EXPERTS = 32
TOP_K = 4
D_EXPERT = 512
SWIGLU_LIMIT = 7.0
SWIGLU_ALPHA = 1.702
MOE_BLOCK = 256
MOE_COMBINE_ROWS = 128

_NT = (((1,), (1,)), ((), ()))
_TN = (((0,), (0,)), ((), ()))


def _cp(*sem):
    return pltpu.CompilerParams(dimension_semantics=sem, vmem_limit_bytes=_VMEM_LIMIT_BYTES)


def _dot(a, b):
    return jnp.dot(a.astype(BF16), b.astype(BF16), preferred_element_type=F32)


def _dot_nt(a, b):
    return lax.dot_general(a.astype(BF16), b.astype(BF16), _NT, preferred_element_type=F32)


def _dot_tn(a, b):
    return lax.dot_general(a.astype(BF16), b.astype(BF16), _TN, preferred_element_type=F32)


def _split(x, parts):
    out = []
    for _ in range(parts):
        h = x.astype(BF16)
        out.append(h)
        x = x - h.astype(F32)
    return out


def _dot01_right(x, m01, parts=2):
    return sum(jnp.dot(h, m01, preferred_element_type=F32) for h in _split(x, parts))


def _dot01_left(m01, x, parts=3):
    return sum(jnp.dot(m01, h, preferred_element_type=F32) for h in _split(x, parts))


def _sigmoid(x):
    return 1.0 / (1.0 + jnp.exp(-x))


def _softplus(x):
    return jnp.maximum(x, 0.0) + jnp.log(1.0 + jnp.exp(-jnp.abs(x)))


def _pick(n, cands):
    for c in cands:
        if n % c == 0:
            return c
    raise ValueError(f"no tile for {n} in {cands}")


def _head_blockdiag(width, head):
    i = np.arange(width)
    return jnp.asarray((i[:, None] // head) == (i[None, :] // head), BF16)


def _mm_kernel(*refs, has_bias, n_extra, epilogue):
    a_ref, w_ref = refs[0], refs[1]
    pos = 2
    bias_ref = None
    if has_bias:
        bias_ref = refs[pos]
        pos += 1
    extra = refs[pos:pos + n_extra]
    o_ref, wbf_ref = refs[pos + n_extra], refs[pos + n_extra + 1]

    @pl.when(pl.program_id(1) == 0)
    def _():
        wbf_ref[...] = w_ref[...].astype(BF16)

    acc = jnp.dot(a_ref[...].astype(BF16), wbf_ref[...], preferred_element_type=F32)
    if has_bias:
        acc = acc + bias_ref[...]
    if epilogue is not None:
        acc = epilogue(acc, *[e[...] for e in extra])
    o_ref[...] = acc.astype(o_ref.dtype)


def _mm(a, w, l, *, tm, tn, col0=0, ncols=None, rows=None, bias=None, extras=(), epilogue=None,
        out_dtype=F32):
    k = a.shape[1]
    rows = a.shape[0] if rows is None else rows
    ncols = w.shape[2] - col0 if ncols is None else ncols
    assert col0 % tn == 0 and ncols % tn == 0 and rows % tm == 0 and w.shape[1] == k
    c0 = col0 // tn
    in_specs = [pl.BlockSpec((tm, k), lambda j, i: (i, 0)),
                pl.BlockSpec((None, k, tn), lambda j, i: (l, 0, j + c0))]
    args = [a, w]
    if bias is not None:
        in_specs.append(pl.BlockSpec((None, 1, tn), lambda j, i: (l, 0, j + c0)))
        args.append(bias)
    for arr, spec in extras:
        in_specs.append(spec)
        args.append(arr)
    return pl.pallas_call(
        functools.partial(_mm_kernel, has_bias=bias is not None, n_extra=len(extras), epilogue=epilogue),
        grid=(ncols // tn, rows // tm),
        in_specs=in_specs,
        out_specs=pl.BlockSpec((tm, tn), lambda j, i: (i, j)),
        out_shape=jax.ShapeDtypeStruct((rows, ncols), out_dtype),
        scratch_shapes=[pltpu.VMEM((k, tn), BF16)],
        compiler_params=_cp("arbitrary", "arbitrary"),
    )(*args)


def _norm_mod_kernel(x_ref, g_ref, sc_ref, sh_ref, o_ref):
    x = x_ref[...]
    y = x * lax.rsqrt(jnp.mean(x * x, axis=-1, keepdims=True) + NORM_EPS) * g_ref[...]
    o_ref[...] = (y * (1.0 + sc_ref[...]) + sh_ref[...]).astype(o_ref.dtype)


def _norm_mod(x, gain3, l, mod, which_sh, which_sc, dims, rows, out_dtype):
    d = x.shape[1]
    tm = 256
    seg = dims.seg_fn(tm)
    return pl.pallas_call(
        _norm_mod_kernel,
        grid=(rows // tm,),
        in_specs=[pl.BlockSpec((tm, d), lambda i: (i, 0)),
                  pl.BlockSpec((None, 1, d), lambda i: (l, 0, 0)),
                  pl.BlockSpec((None, 1, d), lambda i: (seg(i), 0, which_sc)),
                  pl.BlockSpec((None, 1, d), lambda i: (seg(i), 0, which_sh))],
        out_specs=pl.BlockSpec((tm, d), lambda i: (i, 0)),
        out_shape=jax.ShapeDtypeStruct((rows, d), out_dtype),
        compiler_params=_cp("parallel"),
    )(x, gain3, mod, mod)


class _Dims:
    def __init__(self, b, s, c, d):
        self.B, self.S, self.C, self.D = b, s, c, d
        self.BS = b * s
        self.M = b * s + b * c
        self.W = d // 4

    def seg_fn(self, tm):
        assert self.S % tm == 0 and self.M % tm == 0
        s, b = self.S, self.B
        return lambda i: jnp.minimum((i * tm) // s, b)


def _attend(qs, segs_of):
    n = len(qs)
    m, l, acc = [None] * n, [None] * n, [None] * n
    for j in range(len(segs_of[0])):
        for i in range(n):
            k, v, bias = segs_of[i][j]
            s = lax.dot_general(qs[i], k, _NT, preferred_element_type=F32)
            if bias is not None:
                s = s + bias
            smax = jnp.max(s, axis=-1, keepdims=True)
            if j == 0:
                m[i] = smax
                p = jnp.exp2(s - smax)
                l[i] = jnp.sum(p, axis=-1, keepdims=True)
                acc[i] = jnp.dot(p.astype(BF16), v, preferred_element_type=F32)
            else:
                m_new = jnp.maximum(m[i], smax)
                alpha = jnp.exp2(m[i] - m_new)
                p = jnp.exp2(s - m_new)
                l[i] = alpha * l[i] + jnp.sum(p, axis=-1, keepdims=True)
                acc[i] = alpha * acc[i] + jnp.dot(p.astype(BF16), v, preferred_element_type=F32)
                m[i] = m_new
    return [acc[i] * (1.0 / l[i]) for i in range(n)]


def _half_masks(dtype):
    lane = lax.broadcasted_iota(I32, (1, _LANES), 1)
    lo = lane < 64
    return lo, jnp.where(lo, 1.0, 0.0).astype(dtype), jnp.where(lo, 0.0, 1.0).astype(dtype)


def _na_prep_kernel(f_ref, gq_ref, gk_ref, bd_ref, o_ref):
    w = gq_ref.shape[1]
    bd = bd_ref[...]

    def head_norm(x, g):
        ss = _dot01_right(x * x, bd)
        return x * lax.rsqrt(ss * (1.0 / NA_HEAD_DIM) + NORM_EPS) * g

    o_ref[:, 0:w] = head_norm(f_ref[:, 0:w], gq_ref[...]).astype(o_ref.dtype)
    o_ref[:, w:2 * w] = head_norm(f_ref[:, w:2 * w], gk_ref[...]).astype(o_ref.dtype)
    o_ref[:, 2 * w:3 * w] = f_ref[:, 2 * w:3 * w].astype(o_ref.dtype)


def _na_prep(f_na, qn_g, kn_g, dims):
    m, w3 = f_na.shape
    w = w3 // 3
    tm = 256
    gq = (jnp.tile(qn_g, NA_HEADS) * (NA_HEAD_DIM ** -0.5 * _LOG2E)).reshape(1, w)
    gk = jnp.tile(kn_g, NA_HEADS).reshape(1, w)
    return pl.pallas_call(
        _na_prep_kernel,
        grid=(m // tm,),
        in_specs=[pl.BlockSpec((tm, w3), lambda i: (i, 0)),
                  pl.BlockSpec((1, w), lambda i: (0, 0)),
                  pl.BlockSpec((1, w), lambda i: (0, 0)),
                  pl.BlockSpec((w, w), lambda i: (0, 0))],
        out_specs=pl.BlockSpec((tm, w3), lambda i: (i, 0)),
        out_shape=jax.ShapeDtypeStruct((m, w3), BF16),
        compiler_params=_cp("parallel"),
    )(f_na, gq, gk, _head_blockdiag(w, NA_HEAD_DIM))


def _na_bias_tables(rpb, rows):
    n_tiles = rows // NA_QROWS
    assert rows % NA_QROWS == 0 and rows >= NA_BAND_ROWS + NA_QROWS and NA_QROWS * 2 <= NA_WIN_R
    tabs = []
    for g in (0, 1, n_tiles - 1):
        r = g * NA_QROWS + np.arange(NA_QROWS)
        band0 = int(np.clip(g * NA_QROWS - NA_WIN_R // 2, 0, rows - NA_BAND_ROWS))
        key_row = band0 + np.arange(NA_BAND_ROWS)
        row_start = np.clip(r - NA_WIN_R // 2, 0, rows - NA_WIN_R)
        valid_r = (key_row[None, :] >= row_start[:, None]) & (key_row[None, :] < row_start[:, None] + NA_WIN_R)
        d_row = np.clip(key_row[None, :] - r[:, None] + NA_WIN_R - 1, 0, 2 * NA_WIN_R - 2)
        qc = np.arange(GRID_W)
        col_start = np.clip(qc - NA_WIN_C // 2, 0, GRID_W - NA_WIN_C)
        kc = np.arange(GRID_W)
        valid_c = (kc[None, :] >= col_start[:, None]) & (kc[None, :] < col_start[:, None] + NA_WIN_C)
        d_col = np.clip(kc[None, :] - qc[:, None] + NA_WIN_C - 1, 0, 2 * NA_WIN_C - 2)
        oh_r = jnp.asarray(d_row[:, :, None] == np.arange(2 * NA_WIN_R - 1), F32)
        oh_c = jnp.asarray(d_col[:, :, None] == np.arange(2 * NA_WIN_C - 1), F32)
        b = jnp.einsum('ajr,hrc->hajc', oh_r, rpb, precision=lax.Precision.HIGHEST)
        b = jnp.einsum('hajc,qkc->haqjk', b, oh_c, precision=lax.Precision.HIGHEST)
        valid = valid_r[:, None, :, None] & valid_c[None, :, None, :]
        b = jnp.where(valid[None], b, -jnp.inf)
        tabs.append(b.reshape(NA_HEADS, NA_QROWS * GRID_W, NA_BAND_ROWS * GRID_W))
    return (jnp.stack(tabs) * _LOG2E).astype(BF16)


def _na_attn_kernel(q_ref, k_ref, v_ref, kc_ref, vc_ref, bias_ref, o_ref, *, rows):
    g = pl.program_id(1)
    band0 = jnp.clip(g * NA_QROWS - NA_WIN_R // 2, 0, rows - NA_BAND_ROWS)
    off = pl.multiple_of(band0 * GRID_W, GRID_W)
    band = NA_BAND_ROWS * GRID_W
    lo, m_e, m_o = _half_masks(BF16)
    qs, segs = [], []
    for p in range(q_ref.shape[1] // _LANES):
        sl = slice(p * _LANES, (p + 1) * _LANES)
        qp = q_ref[:, sl]
        kb = k_ref[pl.ds(off, band), sl]
        vb = v_ref[pl.ds(off, band), sl]
        for half, msk in enumerate((m_e, m_o)):
            qs.append(qp * msk)
            segs.append([(kb, vb, bias_ref[2 * p + half].astype(F32)), (kc_ref[:, sl], vc_ref[:, sl], None)])
    outs = []
    for i in range(0, len(qs), NA_HEADS_TOGETHER):
        outs += _attend(qs[i:i + NA_HEADS_TOGETHER], segs[i:i + NA_HEADS_TOGETHER])
    for p in range(q_ref.shape[1] // _LANES):
        o_ref[:, p * _LANES:(p + 1) * _LANES] = jnp.where(lo, outs[2 * p], outs[2 * p + 1]).astype(o_ref.dtype)


def _na_ctx_kernel(q_ref, kc_ref, vc_ref, o_ref):
    lo, m_e, m_o = _half_masks(BF16)
    qs, segs = [], []
    for p in range(q_ref.shape[1] // _LANES):
        sl = slice(p * _LANES, (p + 1) * _LANES)
        for msk in (m_e, m_o):
            qs.append(q_ref[:, sl] * msk)
            segs.append([(kc_ref[:, sl], vc_ref[:, sl], None)])
    outs = _attend(qs, segs)
    for p in range(q_ref.shape[1] // _LANES):
        o_ref[:, p * _LANES:(p + 1) * _LANES] = jnp.where(lo, outs[2 * p], outs[2 * p + 1]).astype(o_ref.dtype)


def _na_attention(qkv, bias, l, dims, need_ctx):
    b, s, c, w = dims.B, dims.S, dims.C, dims.W
    rows = s // GRID_W
    n_tiles = rows // NA_QROWS
    tq = NA_QROWS * GRID_W
    band = NA_BAND_ROWS * GRID_W
    ctx_blk = dims.BS // c

    def cls(g):
        return (g > 0).astype(I32) + (g == n_tiles - 1).astype(I32)

    ya = pl.pallas_call(
        functools.partial(_na_attn_kernel, rows=rows),
        grid=(b, n_tiles),
        in_specs=[pl.BlockSpec((tq, w), lambda bi, g: (bi * n_tiles + g, 0)),
                  pl.BlockSpec((s, w), lambda bi, g: (bi, 1)),
                  pl.BlockSpec((s, w), lambda bi, g: (bi, 2)),
                  pl.BlockSpec((c, w), lambda bi, g: (ctx_blk + bi, 1)),
                  pl.BlockSpec((c, w), lambda bi, g: (ctx_blk + bi, 2)),
                  pl.BlockSpec((None, None, NA_HEADS, tq, band), lambda bi, g: (l, cls(g), 0, 0, 0))],
        out_specs=pl.BlockSpec((tq, w), lambda bi, g: (bi * n_tiles + g, 0)),
        out_shape=jax.ShapeDtypeStruct((dims.BS, w), BF16),
        compiler_params=_cp("parallel", "arbitrary"),
    )(qkv, qkv, qkv, qkv, qkv, bias)
    if not need_ctx:
        return ya
    yc = pl.pallas_call(
        _na_ctx_kernel,
        grid=(b,),
        in_specs=[pl.BlockSpec((c, w), lambda bi: (ctx_blk + bi, 0)),
                  pl.BlockSpec((c, w), lambda bi: (ctx_blk + bi, 1)),
                  pl.BlockSpec((c, w), lambda bi: (ctx_blk + bi, 2))],
        out_specs=pl.BlockSpec((c, w), lambda bi: (bi, 0)),
        out_shape=jax.ShapeDtypeStruct((b * c, w), BF16),
        compiler_params=_cp("parallel"),
    )(qkv, qkv, qkv)
    return jnp.concatenate([ya, yc], axis=0)


def _rope_tables(s, tm):
    n_freq = MLA_ROPE // 4
    inv_freq = ROPE_THETA ** (-jnp.arange(n_freq, dtype=F32) / n_freq)
    t = jnp.arange(s)
    row = (t // GRID_W).astype(F32)[:, None] * inv_freq
    col = (t % GRID_W).astype(F32)[:, None] * inv_freq
    zeros = jnp.zeros((s, n_freq), F32)

    def slab(row_x1, row_x2, col_x1, col_x2, fill):
        body = jnp.concatenate([row_x1, row_x2, col_x1, col_x2], axis=-1)
        return jnp.concatenate([jnp.full((s, MLA_NOPE), fill, F32), body,
                                jnp.zeros((s, _LANES - MLA_QK), F32)], axis=-1)

    cr, sr, cc, sc = jnp.cos(row), jnp.sin(row), jnp.cos(col), jnp.sin(col)
    c_tab = slab(cr, cr, cc, cc, 1.0)
    s1_tab = slab(-sr, zeros, -sc, zeros, 0.0)
    s2_tab = slab(zeros, sr, zeros, sc, 0.0)
    ident = jnp.concatenate([jnp.ones((tm, MLA_QK), F32), jnp.zeros((tm, _LANES - MLA_QK), F32)], axis=-1)
    zero = jnp.zeros((tm, _LANES), F32)
    return (jnp.concatenate([c_tab, ident]), jnp.concatenate([s1_tab, zero]),
            jnp.concatenate([s2_tab, zero]))


def _mla_prep_kernel(f_ref, wq_ref, wk_ref, wv_ref, place_ref, gcq_ref, gckv_ref, gq_ref, gk_ref,
                     ct_ref, s1_ref, s2_ref, q_o, k_o, v_o):
    cq = f_ref[:, 0:MLA_Q_RANK]
    ckv = f_ref[:, MLA_Q_RANK:MLA_Q_RANK + MLA_KV_RANK]
    kr = f_ref[:, MLA_Q_RANK + MLA_KV_RANK:MLA_Q_RANK + MLA_KV_RANK + MLA_ROPE]

    def rms(x, g):
        return x * lax.rsqrt(jnp.mean(x * x, axis=-1, keepdims=True) + NORM_EPS) * g

    cqn = rms(cq, gcq_ref[...]).astype(BF16)
    ckvn = rms(ckv, gckv_ref[...]).astype(BF16)
    q = jnp.dot(cqn, wq_ref[...], preferred_element_type=F32)
    kn = jnp.dot(ckvn, wk_ref[...], preferred_element_type=F32)
    v_o[...] = jnp.dot(ckvn, wv_ref[...], preferred_element_type=F32).astype(v_o.dtype)
    kr_slab = _dot01_right(kr, place_ref[...])
    ct, s1, s2 = ct_ref[...], s1_ref[...], s2_ref[...]

    def head(x, g):
        x = x * lax.rsqrt(jnp.sum(x * x, axis=-1, keepdims=True) * (1.0 / MLA_QK) + NORM_EPS) * g
        return x * ct + pltpu.roll(x, _LANES - MLA_ROPE // 4, 1) * s1 + pltpu.roll(x, MLA_ROPE // 4, 1) * s2

    for h in range(MLA_HEADS):
        sl = slice(h * _LANES, (h + 1) * _LANES)
        q_o[h] = head(q[:, sl], gq_ref[...]).astype(q_o.dtype)
        k_o[h] = head(kn[:, sl] + kr_slab, gk_ref[...]).astype(k_o.dtype)


def _mla_prep(f_mla, pw, l, ropes, dims):
    m = f_mla.shape[0]
    tm = 256
    n_lat = dims.S // tm
    bs = dims.BS

    def rope_blk(i):
        return jnp.where(i * tm < bs, i % n_lat, n_lat)

    hw = MLA_HEADS * _LANES
    full = lambda shape: pl.BlockSpec(shape, lambda i: tuple(0 for _ in shape))
    lyr = lambda shape: pl.BlockSpec((None,) + shape, lambda i: (l,) + tuple(0 for _ in shape))
    return pl.pallas_call(
        _mla_prep_kernel,
        grid=(m // tm,),
        in_specs=[pl.BlockSpec((tm, f_mla.shape[1]), lambda i: (i, 0)),
                  lyr((MLA_Q_RANK, hw)), lyr((MLA_KV_RANK, hw)), lyr((MLA_KV_RANK, MLA_HEADS * MLA_V)),
                  full((MLA_ROPE, _LANES)),
                  lyr((1, MLA_Q_RANK)), lyr((1, MLA_KV_RANK)), lyr((1, _LANES)), lyr((1, _LANES)),
                  pl.BlockSpec((tm, _LANES), lambda i: (rope_blk(i), 0)),
                  pl.BlockSpec((tm, _LANES), lambda i: (rope_blk(i), 0)),
                  pl.BlockSpec((tm, _LANES), lambda i: (rope_blk(i), 0))],
        out_specs=[pl.BlockSpec((MLA_HEADS, tm, _LANES), lambda i: (0, i, 0)),
                   pl.BlockSpec((MLA_HEADS, tm, _LANES), lambda i: (0, i, 0)),
                   pl.BlockSpec((tm, MLA_HEADS * MLA_V), lambda i: (i, 0))],
        out_shape=[jax.ShapeDtypeStruct((MLA_HEADS, m, _LANES), BF16),
                   jax.ShapeDtypeStruct((MLA_HEADS, m, _LANES), BF16),
                   jax.ShapeDtypeStruct((m, MLA_HEADS * MLA_V), BF16)],
        compiler_params=_cp("parallel"),
    )(f_mla, pw["mla_wq"], pw["mla_wk"], pw["mla_wv"], pw["mla_place"], pw["mla_gcq"], pw["mla_gckv"],
      pw["mla_gq"], pw["mla_gk"], *ropes)


def _mla_attn_kernel(qe_ref, qo_ref, kce_ref, kco_ref, kle_ref, klo_ref, vc_ref, vl_ref, o_ref):
    lo, _, _ = _half_masks(F32)
    n_chunks = kle_ref.shape[0] // MLA_KEY_CHUNK

    def segs(kc_ref, kl_ref):
        out = [(kc_ref[...], vc_ref[...], None)]
        for j in range(n_chunks):
            sl = slice(j * MLA_KEY_CHUNK, (j + 1) * MLA_KEY_CHUNK)
            out.append((kl_ref[sl, :], vl_ref[sl, :], None))
        return out

    o_e, o_o = _attend([qe_ref[...], qo_ref[...]], [segs(kce_ref, kle_ref), segs(kco_ref, klo_ref)])
    o_ref[...] = jnp.where(lo, o_e, o_o).astype(o_ref.dtype)


def _mla_ctx_kernel(qe_ref, qo_ref, kce_ref, kco_ref, vc_ref, o_ref):
    lo, _, _ = _half_masks(F32)
    o_e, o_o = _attend([qe_ref[...], qo_ref[...]],
                       [[(kce_ref[...], vc_ref[...], None)], [(kco_ref[...], vc_ref[...], None)]])
    o_ref[...] = jnp.where(lo, o_e, o_o).astype(o_ref.dtype)


def _mla_attention(q, k, v, dims, need_ctx):
    b, s, c, w = dims.B, dims.S, dims.C, dims.W
    tq = MLA_Q_TILE
    nq = s // tq
    ctx_blk = dims.BS // c
    pairs = MLA_HEADS // 2
    hq = lambda half: pl.BlockSpec((None, tq, _LANES), lambda bi, p, i: (2 * p + half, bi * nq + i, 0))
    hkc = lambda half: pl.BlockSpec((None, c, _LANES), lambda bi, p, i: (2 * p + half, ctx_blk + bi, 0))
    hkl = lambda half: pl.BlockSpec((None, s, _LANES), lambda bi, p, i: (2 * p + half, bi, 0))
    yc_x = pl.pallas_call(
        _mla_attn_kernel,
        grid=(b, pairs, nq),
        in_specs=[hq(0), hq(1), hkc(0), hkc(1), hkl(0), hkl(1),
                  pl.BlockSpec((c, _LANES), lambda bi, p, i: (ctx_blk + bi, p)),
                  pl.BlockSpec((s, _LANES), lambda bi, p, i: (bi, p))],
        out_specs=pl.BlockSpec((tq, _LANES), lambda bi, p, i: (bi * nq + i, p)),
        out_shape=jax.ShapeDtypeStruct((dims.BS, w), BF16),
        compiler_params=_cp("parallel", "parallel", "arbitrary"),
    )(q, q, k, k, k, k, v, v)
    if not need_ctx:
        return yc_x
    cq = lambda half: pl.BlockSpec((None, c, _LANES), lambda bi, p: (2 * p + half, ctx_blk + bi, 0))
    yc_s = pl.pallas_call(
        _mla_ctx_kernel,
        grid=(b, pairs),
        in_specs=[cq(0), cq(1), cq(0), cq(1),
                  pl.BlockSpec((c, _LANES), lambda bi, p: (ctx_blk + bi, p))],
        out_specs=pl.BlockSpec((c, _LANES), lambda bi, p: (bi, p)),
        out_shape=jax.ShapeDtypeStruct((b * c, w), BF16),
        compiler_params=_cp("parallel", "parallel"),
    )(q, q, k, k, v)
    return jnp.concatenate([yc_x, yc_s], axis=0)


def _rw_prep_kernel(f_ref, fp_ref, fn_ref, mu_ref, w0_ref, w2_ref, a0_ref, a2_ref, g2_ref, kkg_ref,
                    ka_ref, bd_ref, r_o, v_o, kk_o, g_o, lw_o, kd_o, bb_o, *, s, c, bs, tm):
    w = r_o.shape[1]
    row0 = pl.program_id(0) * tm
    is_lat = row0 < bs
    pos = jnp.where(is_lat, row0 % s, (row0 - bs) % c)
    seglen = jnp.where(is_lat, s, c)
    f = f_ref[...]
    prev_row = jnp.where(pos == 0, 0.0, fp_ref[7:8, :])
    next_row = jnp.where(pos + tm == seglen, 0.0, fn_ref[0:1, :])
    ridx = lax.broadcasted_iota(I32, (tm, 1), 0)
    prev = jnp.where(ridx == 0, prev_row, pltpu.roll(f, 1, 0))
    nxt = jnp.where(ridx == tm - 1, next_row, pltpu.roll(f, tm - 1, 0))
    fm = f + mu_ref[0:1, :] * (prev - f) + mu_ref[1:2, :] * (nxt - f)
    r, k, v = fm[:, 0:w], fm[:, w:2 * w], fm[:, 2 * w:3 * w]
    o = 3 * w
    w_low = jnp.tanh(fm[:, o:o + 128])
    a_low = fm[:, o + 128:o + 256]
    g_low = _sigmoid(fm[:, o + 256:o + 384])
    log_w = -_softplus(-(w0_ref[...] + _dot(w_low, w2_ref[...]))) - 0.5
    lw = -jnp.exp(log_w)
    a = _sigmoid(a0_ref[...] + _dot(a_low, a2_ref[...]))
    g_o[...] = _dot(g_low, g2_ref[...])
    kk = k * kkg_ref[...]
    kk = kk / jnp.maximum(jnp.sqrt(_dot01_right(kk * kk, bd_ref[...])), 1e-12)
    r_o[...] = r
    v_o[...] = v
    kk_o[...] = kk
    for d in range(2):
        ad = a[:, d * w:(d + 1) * w]
        lw_o[d] = lw[:, d * w:(d + 1) * w]
        kd_o[d] = k * (1.0 + (ad - 1.0) * ka_ref[...])
        bb_o[d] = ad * kk


def _rw_prep(f_rw, pw, l, dims):
    m, cols = f_rw.shape
    w = dims.W
    tm = 256
    assert dims.C % tm == 0 and dims.S % tm == 0
    nb8 = m // 8
    lyr = lambda shape: pl.BlockSpec((None,) + shape, lambda i: (l,) + tuple(0 for _ in shape))
    out_w = pl.BlockSpec((tm, w), lambda i: (i, 0))
    out_2w = pl.BlockSpec((2, tm, w), lambda i: (0, i, 0))
    sd = jax.ShapeDtypeStruct
    return pl.pallas_call(
        functools.partial(_rw_prep_kernel, s=dims.S, c=dims.C, bs=dims.BS, tm=tm),
        grid=(m // tm,),
        in_specs=[pl.BlockSpec((tm, cols), lambda i: (i, 0)),
                  pl.BlockSpec((8, cols), lambda i: (jnp.maximum(i * (tm // 8) - 1, 0), 0)),
                  pl.BlockSpec((8, cols), lambda i: (jnp.minimum((i + 1) * (tm // 8), nb8 - 1), 0)),
                  lyr((2, cols)), lyr((1, 2 * w)), lyr((128, 2 * w)), lyr((1, 2 * w)), lyr((128, 2 * w)),
                  lyr((128, w)), lyr((1, w)), lyr((1, w)),
                  pl.BlockSpec((w, w), lambda i: (0, 0))],
        out_specs=[out_w, out_w, out_w, out_w, out_2w, out_2w, out_2w],
        out_shape=[sd((m, w), F32)] * 4 + [sd((2, m, w), F32)] * 3,
        compiler_params=_cp("parallel"),
    )(f_rw, f_rw, f_rw, pw["rw_mu"], pw["rw_w0"], pw["rw_w2"], pw["rw_a0"], pw["rw_a2"], pw["rw_g2"],
      pw["rw_kk"], pw["rw_ka"], _head_blockdiag(w, RW_HEAD_DIM))


def _rw_chunk_kernel(rf, vf, kkf, lwf, kdf, bbf, rb, vb, kkb, lwb, kdb, bbb, yf_ref, yb_ref, h_ref):
    c = RW_CHUNK

    @pl.when(pl.program_id(2) == 0)
    def _():
        h_ref[...] = jnp.zeros_like(h_ref)

    ii = lax.broadcasted_iota(I32, (c, c), 0)
    jj = lax.broadcasted_iota(I32, (c, c), 1)
    tri_f = jnp.where(jj <= ii, 1.0, 0.0).astype(BF16)
    tri_b = jnp.where(jj >= ii, 1.0, 0.0).astype(BF16)
    _, m_e, m_o = _half_masks(F32)

    n2 = 2 * c
    i2 = lax.broadcasted_iota(I32, (n2, n2), 0)
    j2 = lax.broadcasted_iota(I32, (n2, n2), 1)
    same = (i2 // c) == (j2 // c)
    eye2 = jnp.where(i2 == j2, 1.0, 0.0)
    eye = lax.broadcasted_iota(I32, (_LANES, _LANES), 0) == lax.broadcasted_iota(I32, (_LANES, _LANES), 1)
    steps = int(np.log2(c)) - 1

    def stack(x):
        return jnp.concatenate([x * m_e, x * m_o], axis=0).astype(BF16)

    def feats(r_ref, kk_ref, lw_ref, kd_ref, bb_ref, v_ref, sl, tri, last, lag):
        lw = lw_ref[:, sl]
        cum = _dot01_left(tri, lw)
        tot = cum[last:last + 1, :]
        w_inv = jnp.exp(-cum)
        w_end = jnp.exp(tot - cum)
        bb, kd = bb_ref[:, sl], kd_ref[:, sl]
        lag = jnp.where(same, lag, -1)
        return dict(la=stack(-kk_ref[:, sl] * jnp.exp(cum - lw)), lr=stack(r_ref[:, sl] * jnp.exp(cum)),
                    rb=stack(bb * w_inv), rk=stack(kd * w_inv), vm=stack(v_ref[:, sl]),
                    bh=stack(bb * w_end), kh=stack(kd * w_end), wtot=jnp.exp(tot),
                    strict=lag > 0, incl=lag >= 0)

    sls = [slice(p * _LANES, (p + 1) * _LANES) for p in range(h_ref.shape[0])]
    f = []
    for sl in sls:
        f.append(feats(rf, kkf, lwf, kdf, bbf, vf, sl, tri_f, c - 1, i2 - j2))
        f.append(feats(rb, kkb, lwb, kdb, bbb, vb, sl, tri_b, 0, j2 - i2))
    un = range(len(f))
    nmat = [jnp.where(f[k]["strict"], _dot_nt(f[k]["la"], f[k]["rb"]), 0.0) for k in un]
    aak = [jnp.where(f[k]["strict"], _dot_nt(f[k]["la"], f[k]["rk"]), 0.0) for k in un]
    arb = [jnp.where(f[k]["incl"], _dot_nt(f[k]["lr"], f[k]["rb"]), 0.0) for k in un]
    ark = [jnp.where(f[k]["incl"], _dot_nt(f[k]["lr"], f[k]["rk"]), 0.0) for k in un]
    t_inv = [eye2 + nmat[k] for k in un]
    pw = nmat
    for _ in range(steps):
        pw = [_dot(pw[k], pw[k]) for k in un]
        t_inv = [t_inv[k] + _dot(t_inv[k], pw[k]) for k in un]
    x = [_dot(t_inv[k], f[k]["la"]) for k in un]
    av = [_dot(aak[k], f[k]["vm"]) for k in un]
    u = [_dot(t_inv[k], av[k]) for k in un]
    q = [f[k]["lr"].astype(F32) + _dot(arb[k], x[k]) for k in un]
    y0 = [_dot(arb[k], u[k]) + _dot(ark[k], f[k]["vm"]) for k in un]
    h = [h_ref[k // 2, k % 2] for k in un]
    yst = [_dot(q[k], h[k]) + y0[k] for k in un]
    for k in un:
        out_ref = yb_ref if k % 2 else yf_ref
        out_ref[:, sls[k // 2]] = yst[k][0:c] + yst[k][c:n2]
    trans = [jnp.where(eye, f[k]["wtot"], 0.0) + _dot_tn(f[k]["bh"], x[k]) for k in un]
    for k in un:
        h_ref[k // 2, k % 2] = (_dot(trans[k], h[k]) + _dot_tn(f[k]["bh"], u[k])
                                + _dot_tn(f[k]["kh"], f[k]["vm"]))


def _rw_scan(r, v, kk, lw, kd, bb, dims):
    b, s, c, w = dims.B, dims.S, dims.C, dims.W
    ch = RW_CHUNK
    n_c, n_s = c // ch, s // ch
    ctx0 = dims.BS // ch
    pp = RW_PAIRS_PER_STEP
    bw = pp * _LANES
    assert w % bw == 0

    def cf(bi, t):
        return jnp.where(t < n_c, ctx0 + bi * n_c + t, bi * n_s + (t - n_c))

    def cb(bi, t):
        return jnp.where(t < n_c, ctx0 + bi * n_c + (n_c - 1 - t), bi * n_s + (n_s - 1 - (t - n_c)))

    sh = lambda fn: pl.BlockSpec((ch, bw), lambda bi, p, t: (fn(bi, t), p))
    dr = lambda fn, d: pl.BlockSpec((None, ch, bw), lambda bi, p, t: (d, fn(bi, t), p))
    m = dims.M
    return pl.pallas_call(
        _rw_chunk_kernel,
        grid=(b, w // bw, n_c + n_s),
        in_specs=[sh(cf), sh(cf), sh(cf), dr(cf, 0), dr(cf, 0), dr(cf, 0),
                  sh(cb), sh(cb), sh(cb), dr(cb, 1), dr(cb, 1), dr(cb, 1)],
        out_specs=[sh(cf), sh(cb)],
        out_shape=[jax.ShapeDtypeStruct((m, w), F32)] * 2,
        scratch_shapes=[pltpu.VMEM((pp, 2, _LANES, _LANES), F32)],
        compiler_params=_cp("parallel", "parallel", "arbitrary"),
    )(r, v, kk, lw, kd, bb, r, v, kk, lw, kd, bb)


def _rw_readout_kernel(yf_ref, yb_ref, r_ref, v_ref, g_ref, kd_ref, rk_ref, gg_ref, gb_ref, bd_ref, o_ref):
    bd = bd_ref[...]
    inv = 1.0 / RW_HEAD_DIM
    y = yf_ref[...] + yb_ref[...]
    yc = y - _dot01_right(y, bd) * inv
    var = _dot01_right(yc * yc, bd) * inv
    yn = yc * lax.rsqrt(var + RW_GN_EPS)
    bonus = _dot01_right(r_ref[...] * (kd_ref[0] + kd_ref[1]) * rk_ref[...], bd) * v_ref[...]
    o_ref[...] = ((yn * gg_ref[...] + gb_ref[...] + bonus) * g_ref[...]).astype(o_ref.dtype)


def _rw_readout(yf, yb, r, v, g, kd, pw, l, dims, rows):
    w = dims.W
    tm = 256
    blk = pl.BlockSpec((tm, w), lambda i: (i, 0))
    lyr = pl.BlockSpec((None, 1, w), lambda i: (l, 0, 0))
    return pl.pallas_call(
        _rw_readout_kernel,
        grid=(rows // tm,),
        in_specs=[blk, blk, blk, blk, blk, pl.BlockSpec((2, tm, w), lambda i: (0, i, 0)), lyr, lyr, lyr,
                  pl.BlockSpec((w, w), lambda i: (0, 0))],
        out_specs=blk,
        out_shape=jax.ShapeDtypeStruct((rows, w), BF16),
        compiler_params=_cp("parallel"),
    )(yf, yb, r, v, g, kd, pw["rw_rk"], pw["rw_gn_g"], pw["rw_gn_b"], _head_blockdiag(w, RW_HEAD_DIM))


def _s5_matrices(a_re, a_im, log_dt, b_re, b_im, c_re, c_im, d_skip):
    hi = lax.Precision.HIGHEST
    n_g, n_p, n_j = b_re.shape[1], b_re.shape[2], b_re.shape[3]
    ct = S5_CHUNK
    dt = jnp.exp(log_dt)[..., None]
    mag = jnp.exp(a_re * dt)
    lam_re, lam_im = mag * jnp.cos(a_im * dt), mag * jnp.sin(a_im * dt)
    den = a_re * a_re + a_im * a_im
    f_re = ((lam_re - 1.0) * a_re + lam_im * a_im) / den
    f_im = (lam_im * a_re - (lam_re - 1.0) * a_im) / den
    bb_re = f_re[..., None] * b_re - f_im[..., None] * b_im
    bb_im = f_re[..., None] * b_im + f_im[..., None] * b_re
    p_re, p_im = [jnp.ones_like(lam_re)], [jnp.zeros_like(lam_im)]
    for _ in range(ct):
        p_re.append(p_re[-1] * lam_re - p_im[-1] * lam_im)
        p_im.append(p_re[-2] * lam_im + p_im[-1] * lam_re)
    p_re, p_im = jnp.stack(p_re), jnp.stack(p_im)
    cl_re = c_re[None] * p_re[:, :, :, None, :] - c_im[None] * p_im[:, :, :, None, :]
    cl_im = c_re[None] * p_im[:, :, :, None, :] + c_im[None] * p_re[:, :, :, None, :]
    kern = jnp.sum(cl_re[..., None] * bb_re[None, :, :, None] - cl_im[..., None] * bb_im[None, :, :, None],
                   axis=4)
    j_idx = np.arange(ct)[:, None]
    t_idx = np.arange(ct)[None, :]
    lag_f, lag_b = t_idx - j_idx, j_idx - t_idx
    sel_f = jnp.asarray(lag_f[:, :, None] == np.arange(ct + 1), F32)
    sel_b = jnp.asarray(lag_b[:, :, None] == np.arange(ct + 1), F32)
    toep = (jnp.einsum('jtn,ngim->gjmti', sel_f, kern[:, 0], precision=hi)
            + jnp.einsum('jtn,ngim->gjmti', sel_b, kern[:, 1], precision=hi))
    skip = (jnp.eye(ct)[None, :, None, :, None] * jnp.eye(n_j)[None, None, :, None, :]
            * d_skip.reshape(n_g, 1, 1, 1, n_j))
    toep = (toep + skip).reshape(n_g, ct * n_j, ct * n_j)

    def powers(z, d, first, reverse):
        z = z[first:first + ct, d]
        return z[::-1] if reverse else z

    def state_in(first, reverse, d):
        pr, pi = powers(p_re, d, first, reverse), powers(p_im, d, first, reverse)
        re = pr[..., None] * bb_re[d][None] - pi[..., None] * bb_im[d][None]
        im = pr[..., None] * bb_im[d][None] + pi[..., None] * bb_re[d][None]
        fix = lambda z: z.transpose(1, 0, 3, 2).reshape(n_g, ct * n_j, n_p)
        return fix(re), fix(im)

    def state_out(first, reverse, d):
        fix = lambda z: z.transpose(1, 3, 0, 2).reshape(n_g, n_p, ct * n_j)
        return fix(powers(cl_re, d, first, reverse)), fix(-powers(cl_im, d, first, reverse))

    rf_re, rf_im = state_in(0, True, 0)
    rb_re, rb_im = state_in(0, False, 1)
    of_re, of_im = state_out(1, False, 0)
    ob_re, ob_im = state_out(1, True, 1)
    half = (np.arange(n_g) % 2)[:, None, None]

    def lane_half(z):
        zero = jnp.zeros_like(z)
        return jnp.concatenate([jnp.where(half == 0, z, zero), jnp.where(half == 1, z, zero)], axis=-1)

    def row_half(z):
        zero = jnp.zeros_like(z)
        return jnp.concatenate([jnp.where(half == 0, z, zero), jnp.where(half == 1, z, zero)], axis=1)

    r_re = jnp.stack([lane_half(rf_re), lane_half(rb_re)]).astype(BF16)
    r_im = jnp.stack([lane_half(rf_im), lane_half(rb_im)]).astype(BF16)
    o_re = jnp.stack([row_half(of_re), row_half(ob_re)]).astype(BF16)
    o_im = jnp.stack([row_half(of_im), row_half(ob_im)]).astype(BF16)
    lam_c = (p_re[ct].reshape(2, 1, n_g * n_p), p_im[ct].reshape(2, 1, n_g * n_p))
    return toep.astype(BF16), r_re, r_im, o_re, o_im, lam_c


def _s5_z_kernel(u_ref, rre_ref, rim_ref, zre_o, zim_o):
    u0, u1 = u_ref[0].astype(BF16), u_ref[1].astype(BF16)
    for d in range(2):
        zre_o[d] = (jnp.dot(u0, rre_ref[d, 0], preferred_element_type=F32)
                    + jnp.dot(u1, rre_ref[d, 1], preferred_element_type=F32))
        zim_o[d] = (jnp.dot(u0, rim_ref[d, 0], preferred_element_type=F32)
                    + jnp.dot(u1, rim_ref[d, 1], preferred_element_type=F32))


def _s5_scan_kernel(zre_ref, zim_ref, lre_ref, lim_ref, xre_o, xim_o, *, n_b, n_s, n_c):
    lanes = zre_ref.shape[2]
    chains = [(d, b) for d in range(2) for b in range(n_b)]
    lam = [(lre_ref[d], lim_ref[d]) for d in range(2)]

    def run(base_of, length, carry):
        def body(t, carry):
            new = []
            for (d, b), (xr, xi) in zip(chains, carry):
                row = base_of(b) + (t if d == 0 else length - 1 - t)
                xre_o[d, pl.ds(row, 1), :] = xr
                xim_o[d, pl.ds(row, 1), :] = xi
                lr, li = lam[d]
                new.append((lr * xr - li * xi + zre_ref[d, pl.ds(row, 1), :],
                            lr * xi + li * xr + zim_ref[d, pl.ds(row, 1), :]))
            return tuple(new)
        return lax.fori_loop(0, length, body, carry)

    zero = jnp.zeros((1, lanes), F32)
    carry = tuple((zero, zero) for _ in chains)
    carry = run(lambda b: n_b * n_s + b * n_c, n_c, carry)
    run(lambda b: b * n_s, n_s, carry)


def _s5_y_kernel(u_ref, toep_ref, xre_ref, xim_ref, ore_ref, oim_ref, y_o):
    xr = [xre_ref[d].astype(BF16) for d in range(2)]
    xi = [xim_ref[d].astype(BF16) for d in range(2)]
    for g in range(2):
        y = jnp.dot(u_ref[g].astype(BF16), toep_ref[g], preferred_element_type=F32)
        for d in range(2):
            y = y + jnp.dot(xr[d], ore_ref[d, g], preferred_element_type=F32)
            y = y + jnp.dot(xi[d], oim_ref[d, g], preferred_element_type=F32)
        y_o[g] = y.astype(y_o.dtype)


def _s5_glu_kernel(y_ref, w_ref, b_ref, o_ref, wbf_ref):
    @pl.when(pl.program_id(0) == 0)
    def _():
        wbf_ref[...] = w_ref[...].astype(BF16)

    y = y_ref[...].astype(F32)
    y = 0.5 * y * (1.0 + jnp.tanh(0.7978845608028654 * (y + 0.044715 * y * y * y)))
    z = jnp.dot(y.astype(BF16), wbf_ref[...], preferred_element_type=F32) + b_ref[...]
    o_ref[...] = (y * _sigmoid(z)).astype(o_ref.dtype)


def _s5_mixer(f_s5, mats, w_glu, b_glu3, l, dims, rows):
    toep, r_re, r_im, o_re, o_im, (lam_re, lam_im) = mats
    m, w = f_s5.shape
    ct = S5_CHUNK
    n_g = w // S5_GROUP
    n_ch = m // ct
    gw = ct * S5_GROUP
    u = f_s5.reshape(n_ch, ct, n_g, S5_GROUP).transpose(2, 0, 1, 3).reshape(n_g, n_ch, gw)
    sl = n_g * S5_STATE
    sd = jax.ShapeDtypeStruct
    zre, zim = pl.pallas_call(
        _s5_z_kernel,
        grid=(n_g // 2,),
        in_specs=[pl.BlockSpec((2, n_ch, gw), lambda g: (g, 0, 0)),
                  pl.BlockSpec((None, 2, 2, gw, _LANES), lambda g: (l, 0, g, 0, 0)),
                  pl.BlockSpec((None, 2, 2, gw, _LANES), lambda g: (l, 0, g, 0, 0))],
        out_specs=[pl.BlockSpec((2, n_ch, _LANES), lambda g: (0, 0, g))] * 2,
        out_shape=[sd((2, n_ch, sl), F32)] * 2,
        compiler_params=_cp("parallel"),
    )(u, r_re, r_im)
    lb = 512
    xre, xim = pl.pallas_call(
        functools.partial(_s5_scan_kernel, n_b=dims.B, n_s=dims.S // ct, n_c=dims.C // ct),
        grid=(sl // lb,),
        in_specs=[pl.BlockSpec((2, n_ch, lb), lambda j: (0, 0, j))] * 2
                 + [pl.BlockSpec((None, 2, 1, lb), lambda j: (l, 0, 0, j))] * 2,
        out_specs=[pl.BlockSpec((2, n_ch, lb), lambda j: (0, 0, j))] * 2,
        out_shape=[sd((2, n_ch, sl), F32)] * 2,
        compiler_params=_cp("parallel"),
    )(zre, zim, lam_re, lam_im)
    y = pl.pallas_call(
        _s5_y_kernel,
        grid=(n_g // 2,),
        in_specs=[pl.BlockSpec((2, n_ch, gw), lambda g: (g, 0, 0)),
                  pl.BlockSpec((None, 2, gw, gw), lambda g: (l, g, 0, 0)),
                  pl.BlockSpec((2, n_ch, _LANES), lambda g: (0, 0, g)),
                  pl.BlockSpec((2, n_ch, _LANES), lambda g: (0, 0, g)),
                  pl.BlockSpec((None, 2, 2, _LANES, gw), lambda g: (l, 0, g, 0, 0)),
                  pl.BlockSpec((None, 2, 2, _LANES, gw), lambda g: (l, 0, g, 0, 0))],
        out_specs=pl.BlockSpec((2, n_ch, gw), lambda g: (g, 0, 0)),
        out_shape=sd((n_g, n_ch, gw), BF16),
        compiler_params=_cp("parallel"),
    )(u, toep, xre, xim, o_re, o_im)
    y = y.reshape(n_g, n_ch, ct, S5_GROUP).transpose(1, 2, 0, 3).reshape(m, w)
    tm = 256
    return pl.pallas_call(
        _s5_glu_kernel,
        grid=(rows // tm,),
        in_specs=[pl.BlockSpec((tm, w), lambda i: (i, 0)),
                  pl.BlockSpec((None, w, w), lambda i: (l, 0, 0)),
                  pl.BlockSpec((None, 1, w), lambda i: (l, 0, 0))],
        out_specs=pl.BlockSpec((tm, w), lambda i: (i, 0)),
        out_shape=sd((rows, w), BF16),
        scratch_shapes=[pltpu.VMEM((w, w), BF16)],
        compiler_params=_cp("arbitrary"),
    )(y, w_glu, b_glu3)


def _merge_kernel(xn_ref, ya_ref, yb_ref, yc_ref, yd_ref, wg0, wg1, wg2, wg3, bg0, bg1, bg2, bg3, wb_ref,
                  o_ref, wg_bf, wb_bf):
    wgs = (wg0, wg1, wg2, wg3)

    @pl.when(pl.program_id(1) == 0)
    def _():
        for n in range(4):
            wg_bf[n] = wgs[n][...].astype(BF16)
            wb_bf[n] = wb_ref[n].astype(BF16)

    xn = xn_ref[...]
    acc = None
    for n, (y_ref, bg) in enumerate(zip((ya_ref, yb_ref, yc_ref, yd_ref), (bg0, bg1, bg2, bg3))):
        gate = _sigmoid(jnp.dot(xn, wg_bf[n], preferred_element_type=F32) + bg[...])
        up = jnp.dot(y_ref[...], wb_bf[n], preferred_element_type=F32)
        acc = gate * up if acc is None else acc + gate * up
    o_ref[...] = acc.astype(o_ref.dtype)


def _merge(xn, ys, w_gate, b_gate3, w_branch, l, rows):
    d = xn.shape[1]
    w = ys[0].shape[1]
    tn = 256
    tm = _pick(rows, (1088, 1024, 512, 256))
    nb = d // tn
    wg = lambda n: pl.BlockSpec((None, d, tn), lambda j, i: (l, 0, n * nb + j))
    bg = lambda n: pl.BlockSpec((None, 1, tn), lambda j, i: (l, 0, n * nb + j))
    yspec = pl.BlockSpec((tm, w), lambda j, i: (i, 0))
    return pl.pallas_call(
        _merge_kernel,
        grid=(nb, rows // tm),
        in_specs=[pl.BlockSpec((tm, d), lambda j, i: (i, 0)), yspec, yspec, yspec, yspec,
                  wg(0), wg(1), wg(2), wg(3), bg(0), bg(1), bg(2), bg(3),
                  pl.BlockSpec((None, 4, w, tn), lambda j, i: (l, 0, 0, j))],
        out_specs=pl.BlockSpec((tm, tn), lambda j, i: (i, j)),
        out_shape=jax.ShapeDtypeStruct((rows, d), BF16),
        scratch_shapes=[pltpu.VMEM((4, d, tn), BF16), pltpu.VMEM((4, w, tn), BF16)],
        compiler_params=_cp("arbitrary", "arbitrary"),
    )(xn, *ys, w_gate, w_gate, w_gate, w_gate, b_gate3, b_gate3, b_gate3, b_gate3, w_branch)


def _router_kernel(h_ref, wr_ref, br_ref, idx_o, gate_o, rank_o, cnt_o, carry):
    tm = h_ref.shape[0]

    @pl.when(pl.program_id(0) == 0)
    def _():
        carry[...] = jnp.zeros_like(carry)

    h_hi, h_lo = _split(h_ref[...], 2)
    w_hi, w_lo = _split(wr_ref[...], 2)
    logits = (jnp.dot(h_hi, w_hi, preferred_element_type=F32) + jnp.dot(h_lo, w_hi, preferred_element_type=F32)
              + jnp.dot(h_hi, w_lo, preferred_element_type=F32)) + br_ref[...]
    lane = lax.broadcasted_iota(I32, (tm, _LANES), 1)
    vals, idxs, hots = [], [], []
    cur = logits
    for _ in range(TOP_K):
        mx = jnp.max(cur, axis=-1, keepdims=True)
        ix = jnp.min(jnp.where(cur == mx, lane, _LANES), axis=-1, keepdims=True)
        hot = lane == ix
        vals.append(mx)
        idxs.append(ix)
        hots.append(hot)
        cur = jnp.where(hot, -jnp.inf, cur)
    ex = [jnp.exp(v - vals[0]) for v in vals]
    inv = 1.0 / sum(ex)
    sel = sum(jnp.where(h, 1.0, 0.0) for h in hots)
    ri = lax.broadcasted_iota(I32, (tm, tm), 0)
    ci = lax.broadcasted_iota(I32, (tm, tm), 1)
    earlier = jnp.where(ci < ri, 1.0, 0.0).astype(BF16)
    before = jnp.dot(earlier, sel.astype(BF16), preferred_element_type=F32) + carry[...]
    idx_out = jnp.zeros((tm, _LANES), I32)
    gate_out = jnp.zeros((tm, _LANES), F32)
    rank_out = jnp.zeros((tm, _LANES), I32)
    for k in range(TOP_K):
        rk = jnp.sum(jnp.where(hots[k], before, 0.0), axis=-1, keepdims=True)
        idx_out = jnp.where(lane == k, idxs[k], idx_out)
        gate_out = jnp.where(lane == k, ex[k] * inv, gate_out)
        rank_out = jnp.where(lane == k, rk.astype(I32), rank_out)
    idx_o[...] = idx_out
    gate_o[...] = gate_out
    rank_o[...] = rank_out
    carry[...] = carry[...] + jnp.sum(sel, axis=0, keepdims=True)
    cnt_o[...] = jnp.broadcast_to(carry[...], cnt_o.shape)


def _router(h, wr_pad, br_pad, l, rows):
    d = h.shape[1]
    tm = 256
    blk = pl.BlockSpec((tm, _LANES), lambda i: (i, 0))
    sd = jax.ShapeDtypeStruct
    return pl.pallas_call(
        _router_kernel,
        grid=(rows // tm,),
        in_specs=[pl.BlockSpec((tm, d), lambda i: (i, 0)),
                  pl.BlockSpec((None, d, _LANES), lambda i: (l, 0, 0)),
                  pl.BlockSpec((None, 1, _LANES), lambda i: (l, 0, 0))],
        out_specs=[blk, blk, blk, pl.BlockSpec((8, _LANES), lambda i: (0, 0))],
        out_shape=[sd((rows, _LANES), I32), sd((rows, _LANES), F32), sd((rows, _LANES), I32),
                   sd((8, _LANES), F32)],
        scratch_shapes=[pltpu.VMEM((1, _LANES), F32)],
        compiler_params=_cp("arbitrary"),
    )(h, wr_pad, br_pad)


def _row_copy(src_hbm, row, dst, slot, sem):
    return pltpu.make_async_copy(src_hbm.at[pl.ds(row, 1), :], dst.at[pl.ds(slot, 1), :], sem)


def _issue_rows(idx_ref, n, src_hbm, dst, sem):
    for r in range(n):
        _row_copy(src_hbm, idx_ref[0, r], dst, r, sem).start()


def _expert_kernel(be_ref, first_ref, used_ref, tok_ref, tok_next_ref, h_hbm, wgu_ref, bgu_ref, wdn_ref,
                   bdn_ref, o_ref, xbuf, sems, wgu_bf, wdn_bf):
    i = pl.program_id(0)
    n_used = used_ref[0]
    slot = i % 2
    blk = xbuf.shape[1]

    @pl.when(i == 0)
    def _():
        _issue_rows(tok_ref, blk, h_hbm, xbuf.at[0], sems.at[0])

    @pl.when(first_ref[i] == 1)
    def _():
        wgu_bf[...] = wgu_ref[...].astype(BF16)
        wdn_bf[...] = wdn_ref[...].astype(BF16)

    @pl.when(i <= n_used)
    def _():
        pltpu.make_async_copy(h_hbm.at[pl.ds(0, blk), :], xbuf.at[slot], sems.at[slot]).wait()

    @pl.when(i < n_used)
    def _():
        _issue_rows(tok_next_ref, blk, h_hbm, xbuf.at[1 - slot], sems.at[1 - slot])
        gu = jnp.dot(xbuf[slot].astype(BF16), wgu_bf[...], preferred_element_type=F32) + bgu_ref[...]
        g_lin = jnp.minimum(gu[:, :D_EXPERT], SWIGLU_LIMIT)
        u_lin = jnp.clip(gu[:, D_EXPERT:], -SWIGLU_LIMIT, SWIGLU_LIMIT)
        act = (u_lin + 1.0) * g_lin * _sigmoid(SWIGLU_ALPHA * g_lin)
        o_ref[...] = jnp.dot(act.astype(BF16), wdn_bf[...], preferred_element_type=F32) + bdn_ref[...]

    @pl.when(i >= n_used)
    def _():
        o_ref[...] = jnp.zeros_like(o_ref)


def _moe_experts(h, slot_tok3, block_e, first, n_used, w_gu, b_gu4, w_dn, b_dn4, l):
    n_blocks = slot_tok3.shape[0] - 1
    blk = slot_tok3.shape[2]
    d = h.shape[1]
    de2 = w_gu.shape[3]
    grid_spec = pltpu.PrefetchScalarGridSpec(
        num_scalar_prefetch=3,
        grid=(n_blocks,),
        in_specs=[pl.BlockSpec((None, 1, blk), lambda i, be, fi, us: (i, 0, 0), memory_space=pltpu.SMEM),
                  pl.BlockSpec((None, 1, blk), lambda i, be, fi, us: (i + 1, 0, 0), memory_space=pltpu.SMEM),
                  pl.BlockSpec(memory_space=pl.ANY),
                  pl.BlockSpec((None, None, d, de2), lambda i, be, fi, us: (l, be[i], 0, 0)),
                  pl.BlockSpec((None, None, 1, de2), lambda i, be, fi, us: (l, be[i], 0, 0)),
                  pl.BlockSpec((None, None, de2 // 2, d), lambda i, be, fi, us: (l, be[i], 0, 0)),
                  pl.BlockSpec((None, None, 1, d), lambda i, be, fi, us: (l, be[i], 0, 0))],
        out_specs=pl.BlockSpec((blk, d), lambda i, be, fi, us: (i, 0)),
        scratch_shapes=[pltpu.VMEM((2, blk, d), F32), pltpu.SemaphoreType.DMA((2,)),
                        pltpu.VMEM((d, de2), BF16), pltpu.VMEM((de2 // 2, d), BF16)])
    return pl.pallas_call(
        _expert_kernel,
        grid_spec=grid_spec,
        out_shape=jax.ShapeDtypeStruct((n_blocks * blk, d), F32),
        compiler_params=_cp("arbitrary"),
    )(block_e, first, n_used, slot_tok3, slot_tok3, h, w_gu, b_gu4, w_dn, b_dn4)


def _combine_kernel(dest_ref, dest_next_ref, gate_ref, x_ref, g2_ref, ys_hbm, o_ref, buf, sems):
    tm = x_ref.shape[0]
    i = pl.program_id(0)
    slot = i % 2

    def issue(idx_ref, s):
        for r in range(tm):
            for k in range(TOP_K):
                _row_copy(ys_hbm, idx_ref[0, r * TOP_K + k], buf.at[s, k], r, sems.at[s]).start()

    @pl.when(i == 0)
    def _():
        issue(dest_ref, 0)

    @pl.when(i + 1 < pl.num_programs(0))
    def _():
        issue(dest_next_ref, 1 - slot)

    for k in range(TOP_K):
        pltpu.make_async_copy(ys_hbm.at[pl.ds(0, tm), :], buf.at[slot, k], sems.at[slot]).wait()
    gate = gate_ref[...]
    f = None
    for k in range(TOP_K):
        term = gate[:, k:k + 1] * buf[slot, k]
        f = term if f is None else f + term
    o_ref[...] = x_ref[...] + g2_ref[...] * f


def _moe_combine(x, ys, dest3, gate, mod, dims, rows):
    d = x.shape[1]
    tm = MOE_COMBINE_ROWS
    seg = dims.seg_fn(tm)
    return pl.pallas_call(
        _combine_kernel,
        grid=(rows // tm,),
        in_specs=[pl.BlockSpec((None, 1, tm * TOP_K), lambda i: (i, 0, 0), memory_space=pltpu.SMEM),
                  pl.BlockSpec((None, 1, tm * TOP_K), lambda i: (i + 1, 0, 0), memory_space=pltpu.SMEM),
                  pl.BlockSpec((tm, _LANES), lambda i: (i, 0)),
                  pl.BlockSpec((tm, d), lambda i: (i, 0)),
                  pl.BlockSpec((None, 1, d), lambda i: (seg(i), 0, 5)),
                  pl.BlockSpec(memory_space=pl.ANY)],
        out_specs=pl.BlockSpec((tm, d), lambda i: (i, 0)),
        out_shape=jax.ShapeDtypeStruct((rows, d), F32),
        scratch_shapes=[pltpu.VMEM((2, TOP_K, tm, d), F32), pltpu.SemaphoreType.DMA((2,))],
        compiler_params=_cp("arbitrary"),
    )(dest3, dest3, gate, x, mod, ys)


def _moe(x, h, mod, pw, l, dims, rows):
    idx, gate, rank, cnt = _router(h, pw["moe_wr"], pw["moe_br"], l, rows)
    blk = MOE_BLOCK
    n_blocks = -(-(rows * TOP_K) // blk) + N_EXPERTS + 1
    counts = cnt[0, :N_EXPERTS].astype(I32)
    padded = (counts + blk - 1) // blk * blk
    pad_end = jnp.cumsum(padded)
    pad_start = pad_end - padded
    top_i = idx[:, :TOP_K]
    start_of = jnp.sum(jnp.where(top_i[..., None] == jnp.arange(N_EXPERTS, dtype=I32), pad_start, 0), axis=-1)
    dest = start_of + rank[:, :TOP_K]
    tok = jnp.broadcast_to(jnp.arange(rows, dtype=I32)[:, None], dest.shape)
    fill = jnp.arange((n_blocks + 1) * blk, dtype=I32) % rows
    slot_tok = fill.at[dest.reshape(-1)].set(tok.reshape(-1))
    block_start = jnp.arange(n_blocks, dtype=I32) * blk
    block_e = jnp.minimum(jnp.sum((pad_end[None, :] <= block_start[:, None]).astype(I32), axis=1), N_EXPERTS - 1)
    first = jnp.concatenate([jnp.ones((1,), I32), (block_e[1:] != block_e[:-1]).astype(I32)])
    n_used = (pad_end[-1:] // blk).astype(I32)
    ys = _moe_experts(h, slot_tok.reshape(n_blocks + 1, 1, blk), block_e, first, n_used, pw["moe_w_gu"],
                      pw["moe_b_gu"], pw["moe_w_dn"], pw["moe_b_dn"], l)
    dest3 = dest.reshape(rows // MOE_COMBINE_ROWS, 1, MOE_COMBINE_ROWS * TOP_K)
    dest3 = jnp.concatenate([dest3, jnp.zeros_like(dest3[:1])], axis=0)
    return _moe_combine(x, ys, dest3, gate, mod, dims, rows)


def _prep_params(p, dims):
    n_l = p["w_in"].shape[0]
    w = dims.W
    o_mla = 3 * w + (3 * w + 384)
    o_s5 = o_mla + MLA_Q_RANK + MLA_KV_RANK + MLA_ROPE
    pw = dict(p)
    row3 = lambda a: a.reshape(a.shape[0], 1, -1)
    pw["norm1_g"], pw["norm2_g"] = row3(p["norm1_g"]), row3(p["norm2_g"])
    pw["b_mod"], pw["b_gate"] = row3(p["b_mod"]), row3(p["b_gate"])
    pw["w_in_mla"] = p["w_in"][:, :, o_mla:o_s5]
    pw["w_in_s5"] = p["w_in"][:, :, o_s5:]
    zeros = jnp.zeros_like(p["rw_w2"][:, 0])
    bdiag = lambda x: jnp.concatenate([jnp.concatenate([x[:, 0], zeros], axis=-1),
                                       jnp.concatenate([zeros, x[:, 1]], axis=-1)], axis=1)
    pw["rw_w2"], pw["rw_a2"] = bdiag(p["rw_w2"]), bdiag(p["rw_a2"])
    pw["rw_w0"], pw["rw_a0"] = row3(p["rw_w0"]), row3(p["rw_a0"])
    pw["rw_kk"], pw["rw_ka"] = row3(p["rw_kk"]), row3(p["rw_ka"])
    pw["rw_rk"], pw["rw_gn_g"], pw["rw_gn_b"] = row3(p["rw_rk"]), row3(p["rw_gn_g"]), row3(p["rw_gn_b"])
    pad_h = lambda x, n: jnp.pad(x, ((0, 0), (0, 0), (0, 0), (0, _LANES - n)))
    wq = p["mla_wq_up"].reshape(n_l, MLA_Q_RANK, MLA_HEADS, MLA_QK)
    pw["mla_wq"] = pad_h(wq, MLA_QK).reshape(n_l, MLA_Q_RANK, -1).astype(BF16)
    wkv = p["mla_wkv_up"].reshape(n_l, MLA_KV_RANK, MLA_HEADS, MLA_NOPE + MLA_V)
    pw["mla_wk"] = pad_h(wkv[..., :MLA_NOPE], MLA_NOPE).reshape(n_l, MLA_KV_RANK, -1).astype(BF16)
    pw["mla_wv"] = wkv[..., MLA_NOPE:].reshape(n_l, MLA_KV_RANK, -1).astype(BF16)
    pw["mla_place"] = jnp.asarray(np.arange(MLA_ROPE)[:, None] + MLA_NOPE == np.arange(_LANES)[None, :], BF16)
    pw["mla_gcq"], pw["mla_gckv"] = row3(p["mla_qn_g"]), row3(p["mla_kvn_g"])
    pad_g = lambda g: jnp.pad(g, ((0, 0), (0, _LANES - MLA_QK))).reshape(n_l, 1, _LANES)
    pw["mla_gq"] = pad_g(p["mla_qkn_q"] * (MLA_QK ** -0.5 * _LOG2E))
    pw["mla_gk"] = pad_g(p["mla_qkn_k"])
    pw["s5_b_glu"] = row3(p["s5_b_glu"])
    pw["moe_wr"] = jnp.pad(p["moe_wr"], ((0, 0), (0, 0), (0, _LANES - N_EXPERTS)))
    pw["moe_br"] = jnp.pad(p["moe_br"], ((0, 0), (0, _LANES - N_EXPERTS)), constant_values=-1e30)[:, None, :]
    pw["moe_b_gu"] = p["moe_b_gu"][:, :, None, :]
    pw["moe_b_dn"] = p["moe_b_dn"][:, :, None, :]
    return pw


def _residual_epilogue(acc, x, g):
    return x + g * acc


def _layer(xs, l, mod, pw, ropes, dims, need_ctx):
    m, d, w = dims.M, dims.D, dims.W
    rows = m if need_ctx else dims.BS
    tm_all = _pick(m, (1088, 1024, 512, 256))
    xn = _norm_mod(xs, pw["norm1_g"], l, mod, 0, 1, dims, m, BF16)
    f_na = _mm(xn, pw["w_in"], l, tm=tm_all, tn=768, col0=0, ncols=3 * w)
    f_rw = _mm(xn, pw["w_in"], l, tm=tm_all, tn=384, col0=3 * w, ncols=3 * w + 384)
    f_mla = _mm(xn, pw["w_in_mla"], l, tm=tm_all, tn=pw["w_in_mla"].shape[2])
    f_s5 = _mm(xn, pw["w_in_s5"], l, tm=tm_all, tn=w, out_dtype=BF16)

    qkv = _na_prep(f_na, pw["na_qn_g"][l], pw["na_kn_g"][l], dims)
    ya = _na_attention(qkv, pw["na_bias"], l, dims, need_ctx)

    r, v, kk, g, lw, kd, bb = _rw_prep(f_rw, pw, l, dims)
    yf, yb_ = _rw_scan(r, v, kk, lw, kd, bb, dims)
    yb = _rw_readout(yf, yb_, r, v, g, kd, pw, l, dims, rows)

    q, k, vv = _mla_prep(f_mla, pw, l, ropes, dims)
    yc = _mla_attention(q, k, vv, dims, need_ctx)

    yd = _s5_mixer(f_s5, pw["s5_mats"], pw["s5_w_glu"], pw["s5_b_glu"], l, dims, rows)

    merged = _merge(xn, (ya, yb, yc, yd), pw["w_gate"], pw["b_gate"], pw["w_branch"], l, rows)
    tm_seg = _pick(dims.S, (512, 256))
    seg = dims.seg_fn(tm_seg)
    tn = 512
    xs = _mm(merged, pw["w_out"], l, tm=tm_seg, tn=tn, rows=rows, epilogue=_residual_epilogue,
             extras=((xs, pl.BlockSpec((tm_seg, tn), lambda j, i: (i, j))),
                     (mod, pl.BlockSpec((None, 1, tn), lambda j, i: (seg(i), 0, 2 * (d // tn) + j)))))
    h = _norm_mod(xs, pw["norm2_g"], l, mod, 3, 4, dims, rows, F32)
    return _moe(xs, h, mod, pw, l, dims, rows)


def kernel(x, c, ctx, c_ctx, w_mod, b_mod, norm1_g, norm2_g, w_in, w_gate, b_gate, w_branch, w_out, na_qn_g, na_kn_g, na_rpb, rw_mu, rw_w0, rw_w2, rw_a0, rw_a2, rw_g2, rw_kk, rw_ka, rw_rk, rw_gn_g, rw_gn_b, mla_qn_g, mla_wq_up, mla_kvn_g, mla_wkv_up, mla_qkn_q, mla_qkn_k, s5_a_re, s5_a_im, s5_log_dt, s5_b_re, s5_b_im, s5_c_re, s5_c_im, s5_d, s5_w_glu, s5_b_glu, moe_wr, moe_br, moe_w_gu, moe_b_gu, moe_w_dn, moe_b_dn):
    params = dict(w_mod=w_mod, b_mod=b_mod, norm1_g=norm1_g, norm2_g=norm2_g, w_in=w_in, w_gate=w_gate,
                  b_gate=b_gate, w_branch=w_branch, w_out=w_out, na_qn_g=na_qn_g, na_kn_g=na_kn_g,
                  na_rpb=na_rpb, rw_mu=rw_mu, rw_w0=rw_w0, rw_w2=rw_w2, rw_a0=rw_a0, rw_a2=rw_a2,
                  rw_g2=rw_g2, rw_kk=rw_kk, rw_ka=rw_ka, rw_rk=rw_rk, rw_gn_g=rw_gn_g, rw_gn_b=rw_gn_b,
                  mla_qn_g=mla_qn_g, mla_wq_up=mla_wq_up, mla_kvn_g=mla_kvn_g, mla_wkv_up=mla_wkv_up,
                  mla_qkn_q=mla_qkn_q, mla_qkn_k=mla_qkn_k, s5_a_re=s5_a_re, s5_a_im=s5_a_im,
                  s5_log_dt=s5_log_dt, s5_b_re=s5_b_re, s5_b_im=s5_b_im, s5_c_re=s5_c_re, s5_c_im=s5_c_im,
                  s5_d=s5_d, s5_w_glu=s5_w_glu, s5_b_glu=s5_b_glu, moe_wr=moe_wr, moe_br=moe_br,
                  moe_w_gu=moe_w_gu, moe_b_gu=moe_b_gu, moe_w_dn=moe_w_dn, moe_b_dn=moe_b_dn)
    b, s, d = x.shape
    c_len = ctx.shape[1]
    depth = w_mod.shape[0]
    dims = _Dims(b, s, c_len, d)
    assert b + 1 <= 8
    pw = _prep_params(params, dims)
    pw["na_bias"] = jax.vmap(lambda rpb: _na_bias_tables(rpb, s // GRID_W))(na_rpb)
    pw["s5_mats"] = jax.vmap(_s5_matrices)(s5_a_re, s5_a_im, s5_log_dt, s5_b_re, s5_b_im, s5_c_re, s5_c_im, s5_d)
    cond = jnp.concatenate([jax.nn.silu(c), jax.nn.silu(c_ctx)[None],
                            jnp.zeros((8 - b - 1, d), F32)], axis=0)
    ropes = _rope_tables(s, 256)
    xs = jnp.concatenate([x.reshape(b * s, d), ctx.reshape(b * c_len, d)], axis=0)
    for l in range(depth):
        mod = _mm(cond, pw["w_mod"], l, tm=8, tn=1024, bias=pw["b_mod"]).reshape(8, 1, 6 * d)
        xs = _layer(xs, l, mod, pw, ropes, dims, need_ctx=l < depth - 1)
    return xs[:b * s].reshape(b, s, d)
```

```python
import functools

import numpy as np
import jax
import jax.numpy as jnp
from jax import lax
from jax.experimental import pallas as pl
from jax.experimental.pallas import tpu as pltpu

F32 = jnp.float32
BF16 = jnp.bfloat16
I32 = jnp.int32

_VMEM_LIMIT_BYTES = 56 * 1024 * 1024
_LANES = 128
_LOG2E = 1.4426950408889634

NORM_EPS = 1e-6
GRID_W = 64
NA_HEADS = 8
NA_HEAD_DIM = 64
NA_WIN_R = 8
NA_WIN_C = 16
NA_QROWS = 4
NA_BAND_ROWS = NA_QROWS + NA_WIN_R - 1
NA_HEADS_TOGETHER = 2
RW_HEAD_DIM = 64
RW_GN_EPS = 64e-5
RW_CHUNK = 64
RW_PAIRS_PER_STEP = 4
MLA_HEADS = 8
MLA_NOPE = 64
MLA_ROPE = 32
MLA_QK = MLA_NOPE + MLA_ROPE
MLA_V = 64
MLA_Q_RANK = 512
MLA_KV_RANK = 256
ROPE_THETA = 10000.0
MLA_KEY_CHUNK = 256
MLA_Q_TILE = 512
S5_GROUP = 16
S5_STATE = 64
S5_CHUNK = 16
N_EXPERTS = 32
TOP_K = 4
D_EXPERT = 512
SWIGLU_LIMIT = 7.0
SWIGLU_ALPHA = 1.702
MOE_BLOCK = 256
MOE_COMBINE_ROWS = 128

_NT = (((1,), (1,)), ((), ()))
_TN = (((0,), (0,)), ((), ()))


def _cp(*sem):
    return pltpu.CompilerParams(dimension_semantics=sem, vmem_limit_bytes=_VMEM_LIMIT_BYTES)


def _dot(a, b):
    return jnp.dot(a.astype(BF16), b.astype(BF16), preferred_element_type=F32)


def _dot_nt(a, b):
    return lax.dot_general(a.astype(BF16), b.astype(BF16), _NT, preferred_element_type=F32)


def _dot_tn(a, b):
    return lax.dot_general(a.astype(BF16), b.astype(BF16), _TN, preferred_element_type=F32)


def _split(x, parts):
    out = []
    for _ in range(parts):
        h = x.astype(BF16)
        out.append(h)
        x = x - h.astype(F32)
    return out


def _dot01_right(x, m01, parts=2):
    return sum(jnp.dot(h, m01, preferred_element_type=F32) for h in _split(x, parts))


def _dot01_left(m01, x, parts=3):
    return sum(jnp.dot(m01, h, preferred_element_type=F32) for h in _split(x, parts))


def _pack_pairs(x):
    n = x.shape[1] // 2
    bits = pltpu.bitcast(x.astype(BF16).astype(F32), jnp.uint32)
    return (bits[:, :n] >> 16) | bits[:, n:]


def _unpack_pairs(w):
    return pltpu.bitcast(w << 16, F32), pltpu.bitcast(w & jnp.uint32(0xFFFF0000), F32)


def _sigmoid(x):
    return 1.0 / (1.0 + jnp.exp(-x))


def _softplus(x):
    return jnp.maximum(x, 0.0) + jnp.log(1.0 + jnp.exp(-jnp.abs(x)))


def _pick(n, cands):
    for c in cands:
        if n % c == 0:
            return c
    raise ValueError(f"no tile for {n} in {cands}")


def _head_blockdiag(width, head):
    i = np.arange(width)
    return jnp.asarray((i[:, None] // head) == (i[None, :] // head), BF16)


def _mm_kernel(*refs, has_bias, n_extra, epilogue):
    a_ref, w_ref = refs[0], refs[1]
    pos = 2
    bias_ref = None
    if has_bias:
        bias_ref = refs[pos]
        pos += 1
    extra = refs[pos:pos + n_extra]
    o_ref, wbf_ref = refs[pos + n_extra], refs[pos + n_extra + 1]

    @pl.when(pl.program_id(1) == 0)
    def _():
        wbf_ref[...] = w_ref[...].astype(BF16)

    acc = jnp.dot(a_ref[...].astype(BF16), wbf_ref[...], preferred_element_type=F32)
    if has_bias:
        acc = acc + bias_ref[...]
    if epilogue is not None:
        acc = epilogue(acc, *[e[...] for e in extra])
    o_ref[...] = acc.astype(o_ref.dtype)


def _mm(a, w, l, *, tm, tn, col0=0, ncols=None, rows=None, bias=None, extras=(), epilogue=None,
        out_dtype=F32):
    k = a.shape[1]
    rows = a.shape[0] if rows is None else rows
    ncols = w.shape[2] - col0 if ncols is None else ncols
    assert col0 % tn == 0 and ncols % tn == 0 and rows % tm == 0 and w.shape[1] == k
    c0 = col0 // tn
    in_specs = [pl.BlockSpec((tm, k), lambda j, i: (i, 0)),
                pl.BlockSpec((None, k, tn), lambda j, i: (l, 0, j + c0))]
    args = [a, w]
    if bias is not None:
        in_specs.append(pl.BlockSpec((None, 1, tn), lambda j, i: (l, 0, j + c0)))
        args.append(bias)
    for arr, spec in extras:
        in_specs.append(spec)
        args.append(arr)
    return pl.pallas_call(
        functools.partial(_mm_kernel, has_bias=bias is not None, n_extra=len(extras), epilogue=epilogue),
        grid=(ncols // tn, rows // tm),
        in_specs=in_specs,
        out_specs=pl.BlockSpec((tm, tn), lambda j, i: (i, j)),
        out_shape=jax.ShapeDtypeStruct((rows, ncols), out_dtype),
        scratch_shapes=[pltpu.VMEM((k, tn), BF16)],
        compiler_params=_cp("arbitrary", "arbitrary"),
    )(*args)


def _norm_mod_kernel(x_ref, g_ref, sc_ref, sh_ref, o_ref):
    x = x_ref[...]
    y = x * lax.rsqrt(jnp.mean(x * x, axis=-1, keepdims=True) + NORM_EPS) * g_ref[...]
    o_ref[...] = (y * (1.0 + sc_ref[...]) + sh_ref[...]).astype(o_ref.dtype)


def _norm_mod(x, gain3, l, mod, which_sh, which_sc, dims, rows, out_dtype):
    d = x.shape[1]
    tm = 256
    seg = dims.seg_fn(tm)
    return pl.pallas_call(
        _norm_mod_kernel,
        grid=(rows // tm,),
        in_specs=[pl.BlockSpec((tm, d), lambda i: (i, 0)),
                  pl.BlockSpec((None, 1, d), lambda i: (l, 0, 0)),
                  pl.BlockSpec((None, 1, d), lambda i: (seg(i), 0, which_sc)),
                  pl.BlockSpec((None, 1, d), lambda i: (seg(i), 0, which_sh))],
        out_specs=pl.BlockSpec((tm, d), lambda i: (i, 0)),
        out_shape=jax.ShapeDtypeStruct((rows, d), out_dtype),
        compiler_params=_cp("parallel"),
    )(x, gain3, mod, mod)


class _Dims:
    def __init__(self, b, s, c, d):
        self.B, self.S, self.C, self.D = b, s, c, d
        self.BS = b * s
        self.M = b * s + b * c
        self.W = d // 4

    def seg_fn(self, tm):
        assert self.S % tm == 0 and self.M % tm == 0
        s, b = self.S, self.B
        return lambda i: jnp.minimum((i * tm) // s, b)


def _attend(qs, segs_of):
    n = len(qs)
    m, l, acc = [None] * n, [None] * n, [None] * n
    for j in range(len(segs_of[0])):
        for i in range(n):
            k, v, bias = segs_of[i][j]
            s = lax.dot_general(qs[i], k, _NT, preferred_element_type=F32)
            if bias is not None:
                s = s + bias
            smax = jnp.max(s, axis=-1, keepdims=True)
            if j == 0:
                m[i] = smax
                p = jnp.exp2(s - smax)
                l[i] = jnp.sum(p, axis=-1, keepdims=True)
                acc[i] = jnp.dot(p.astype(BF16), v, preferred_element_type=F32)
            else:
                m_new = jnp.maximum(m[i], smax)
                alpha = jnp.exp2(m[i] - m_new)
                p = jnp.exp2(s - m_new)
                l[i] = alpha * l[i] + jnp.sum(p, axis=-1, keepdims=True)
                acc[i] = alpha * acc[i] + jnp.dot(p.astype(BF16), v, preferred_element_type=F32)
                m[i] = m_new
    return [acc[i] * (1.0 / l[i]) for i in range(n)]


def _half_masks(dtype):
    lane = lax.broadcasted_iota(I32, (1, _LANES), 1)
    lo = lane < 64
    return lo, jnp.where(lo, 1.0, 0.0).astype(dtype), jnp.where(lo, 0.0, 1.0).astype(dtype)


def _na_prep_kernel(f_ref, gq_ref, gk_ref, bd_ref, o_ref):
    w = gq_ref.shape[1]
    bd = bd_ref[...]

    def head_norm(x, g):
        ss = _dot01_right(x * x, bd)
        return x * lax.rsqrt(ss * (1.0 / NA_HEAD_DIM) + NORM_EPS) * g

    o_ref[:, 0:w] = head_norm(f_ref[:, 0:w], gq_ref[...]).astype(o_ref.dtype)
    o_ref[:, w:2 * w] = head_norm(f_ref[:, w:2 * w], gk_ref[...]).astype(o_ref.dtype)
    o_ref[:, 2 * w:3 * w] = f_ref[:, 2 * w:3 * w].astype(o_ref.dtype)


def _na_prep(f_na, qn_g, kn_g, dims):
    m, w3 = f_na.shape
    w = w3 // 3
    tm = 256
    gq = (jnp.tile(qn_g, NA_HEADS) * (NA_HEAD_DIM ** -0.5 * _LOG2E)).reshape(1, w)
    gk = jnp.tile(kn_g, NA_HEADS).reshape(1, w)
    return pl.pallas_call(
        _na_prep_kernel,
        grid=(m // tm,),
        in_specs=[pl.BlockSpec((tm, w3), lambda i: (i, 0)),
                  pl.BlockSpec((1, w), lambda i: (0, 0)),
                  pl.BlockSpec((1, w), lambda i: (0, 0)),
                  pl.BlockSpec((w, w), lambda i: (0, 0))],
        out_specs=pl.BlockSpec((tm, w3), lambda i: (i, 0)),
        out_shape=jax.ShapeDtypeStruct((m, w3), BF16),
        compiler_params=_cp("parallel"),
    )(f_na, gq, gk, _head_blockdiag(w, NA_HEAD_DIM))


def _na_bias_tables(rpb, rows):
    n_tiles = rows // NA_QROWS
    assert rows % NA_QROWS == 0 and rows >= NA_BAND_ROWS + NA_QROWS and NA_QROWS * 2 <= NA_WIN_R
    tabs = []
    for g in (0, 1, n_tiles - 1):
        r = g * NA_QROWS + np.arange(NA_QROWS)
        band0 = int(np.clip(g * NA_QROWS - NA_WIN_R // 2, 0, rows - NA_BAND_ROWS))
        key_row = band0 + np.arange(NA_BAND_ROWS)
        row_start = np.clip(r - NA_WIN_R // 2, 0, rows - NA_WIN_R)
        valid_r = (key_row[None, :] >= row_start[:, None]) & (key_row[None, :] < row_start[:, None] + NA_WIN_R)
        d_row = np.clip(key_row[None, :] - r[:, None] + NA_WIN_R - 1, 0, 2 * NA_WIN_R - 2)
        qc = np.arange(GRID_W)
        col_start = np.clip(qc - NA_WIN_C // 2, 0, GRID_W - NA_WIN_C)
        kc = np.arange(GRID_W)
        valid_c = (kc[None, :] >= col_start[:, None]) & (kc[None, :] < col_start[:, None] + NA_WIN_C)
        d_col = np.clip(kc[None, :] - qc[:, None] + NA_WIN_C - 1, 0, 2 * NA_WIN_C - 2)
        oh_r = jnp.asarray(d_row[:, :, None] == np.arange(2 * NA_WIN_R - 1), F32)
        oh_c = jnp.asarray(d_col[:, :, None] == np.arange(2 * NA_WIN_C - 1), F32)
        b = jnp.einsum('ajr,hrc->hajc', oh_r, rpb, precision=lax.Precision.HIGHEST)
        b = jnp.einsum('hajc,qkc->haqjk', b, oh_c, precision=lax.Precision.HIGHEST)
        valid = valid_r[:, None, :, None] & valid_c[None, :, None, :]
        b = jnp.where(valid[None], b, -jnp.inf)
        tabs.append(b.reshape(NA_HEADS, NA_QROWS * GRID_W, NA_BAND_ROWS * GRID_W))
    return (jnp.stack(tabs) * _LOG2E).astype(BF16)


def _na_attn_kernel(q_ref, k_ref, v_ref, kc_ref, vc_ref, bias_ref, o_ref, *, rows):
    g = pl.program_id(1)
    band0 = jnp.clip(g * NA_QROWS - NA_WIN_R // 2, 0, rows - NA_BAND_ROWS)
    off = pl.multiple_of(band0 * GRID_W, GRID_W)
    band = NA_BAND_ROWS * GRID_W
    lo, m_e, m_o = _half_masks(BF16)
    qs, segs = [], []
    for p in range(q_ref.shape[1] // _LANES):
        sl = slice(p * _LANES, (p + 1) * _LANES)
        qp = q_ref[:, sl]
        kb = k_ref[pl.ds(off, band), sl]
        vb = v_ref[pl.ds(off, band), sl]
        for half, msk in enumerate((m_e, m_o)):
            qs.append(qp * msk)
            segs.append([(kb, vb, bias_ref[2 * p + half].astype(F32)), (kc_ref[:, sl], vc_ref[:, sl], None)])
    outs = []
    for i in range(0, len(qs), NA_HEADS_TOGETHER):
        outs += _attend(qs[i:i + NA_HEADS_TOGETHER], segs[i:i + NA_HEADS_TOGETHER])
    for p in range(q_ref.shape[1] // _LANES):
        o_ref[:, p * _LANES:(p + 1) * _LANES] = jnp.where(lo, outs[2 * p], outs[2 * p + 1]).astype(o_ref.dtype)


def _na_ctx_kernel(q_ref, kc_ref, vc_ref, o_ref):
    lo, m_e, m_o = _half_masks(BF16)
    qs, segs = [], []
    for p in range(q_ref.shape[1] // _LANES):
        sl = slice(p * _LANES, (p + 1) * _LANES)
        for msk in (m_e, m_o):
            qs.append(q_ref[:, sl] * msk)
            segs.append([(kc_ref[:, sl], vc_ref[:, sl], None)])
    outs = _attend(qs, segs)
    for p in range(q_ref.shape[1] // _LANES):
        o_ref[:, p * _LANES:(p + 1) * _LANES] = jnp.where(lo, outs[2 * p], outs[2 * p + 1]).astype(o_ref.dtype)


def _na_attention(qkv, bias, l, dims, need_ctx):
    b, s, c, w = dims.B, dims.S, dims.C, dims.W
    rows = s // GRID_W
    n_tiles = rows // NA_QROWS
    tq = NA_QROWS * GRID_W
    band = NA_BAND_ROWS * GRID_W
    ctx_blk = dims.BS // c

    def cls(g):
        return (g > 0).astype(I32) + (g == n_tiles - 1).astype(I32)

    ya = pl.pallas_call(
        functools.partial(_na_attn_kernel, rows=rows),
        grid=(b, n_tiles),
        in_specs=[pl.BlockSpec((tq, w), lambda bi, g: (bi * n_tiles + g, 0)),
                  pl.BlockSpec((s, w), lambda bi, g: (bi, 1)),
                  pl.BlockSpec((s, w), lambda bi, g: (bi, 2)),
                  pl.BlockSpec((c, w), lambda bi, g: (ctx_blk + bi, 1)),
                  pl.BlockSpec((c, w), lambda bi, g: (ctx_blk + bi, 2)),
                  pl.BlockSpec((None, None, NA_HEADS, tq, band), lambda bi, g: (l, cls(g), 0, 0, 0))],
        out_specs=pl.BlockSpec((tq, w), lambda bi, g: (bi * n_tiles + g, 0)),
        out_shape=jax.ShapeDtypeStruct((dims.BS, w), BF16),
        compiler_params=_cp("parallel", "arbitrary"),
    )(qkv, qkv, qkv, qkv, qkv, bias)
    if not need_ctx:
        return ya
    yc = pl.pallas_call(
        _na_ctx_kernel,
        grid=(b,),
        in_specs=[pl.BlockSpec((c, w), lambda bi: (ctx_blk + bi, 0)),
                  pl.BlockSpec((c, w), lambda bi: (ctx_blk + bi, 1)),
                  pl.BlockSpec((c, w), lambda bi: (ctx_blk + bi, 2))],
        out_specs=pl.BlockSpec((c, w), lambda bi: (bi, 0)),
        out_shape=jax.ShapeDtypeStruct((b * c, w), BF16),
        compiler_params=_cp("parallel"),
    )(qkv, qkv, qkv)
    return jnp.concatenate([ya, yc], axis=0)


def _rope_tables(s, tm):
    n_freq = MLA_ROPE // 4
    inv_freq = ROPE_THETA ** (-jnp.arange(n_freq, dtype=F32) / n_freq)
    t = jnp.arange(s)
    row = (t // GRID_W).astype(F32)[:, None] * inv_freq
    col = (t % GRID_W).astype(F32)[:, None] * inv_freq
    zeros = jnp.zeros((s, n_freq), F32)

    def slab(row_x1, row_x2, col_x1, col_x2, fill):
        body = jnp.concatenate([row_x1, row_x2, col_x1, col_x2], axis=-1)
        return jnp.concatenate([jnp.full((s, MLA_NOPE), fill, F32), body,
                                jnp.zeros((s, _LANES - MLA_QK), F32)], axis=-1)

    cr, sr, cc, sc = jnp.cos(row), jnp.sin(row), jnp.cos(col), jnp.sin(col)
    c_tab = slab(cr, cr, cc, cc, 1.0)
    s1_tab = slab(-sr, zeros, -sc, zeros, 0.0)
    s2_tab = slab(zeros, sr, zeros, sc, 0.0)
    ident = jnp.concatenate([jnp.ones((tm, MLA_QK), F32), jnp.zeros((tm, _LANES - MLA_QK), F32)], axis=-1)
    zero = jnp.zeros((tm, _LANES), F32)
    return (jnp.concatenate([c_tab, ident]), jnp.concatenate([s1_tab, zero]),
            jnp.concatenate([s2_tab, zero]))


def _mla_prep_kernel(f_ref, wq_ref, wk_ref, wv_ref, place_ref, gcq_ref, gckv_ref, gq_ref, gk_ref,
                     ct_ref, s1_ref, s2_ref, q_o, k_o, v_o):
    cq = f_ref[:, 0:MLA_Q_RANK]
    ckv = f_ref[:, MLA_Q_RANK:MLA_Q_RANK + MLA_KV_RANK]
    kr = f_ref[:, MLA_Q_RANK + MLA_KV_RANK:MLA_Q_RANK + MLA_KV_RANK + MLA_ROPE]

    def rms(x, g):
        return x * lax.rsqrt(jnp.mean(x * x, axis=-1, keepdims=True) + NORM_EPS) * g

    cqn = rms(cq, gcq_ref[...]).astype(BF16)
    ckvn = rms(ckv, gckv_ref[...]).astype(BF16)
    q = jnp.dot(cqn, wq_ref[...], preferred_element_type=F32)
    kn = jnp.dot(ckvn, wk_ref[...], preferred_element_type=F32)
    v_o[...] = jnp.dot(ckvn, wv_ref[...], preferred_element_type=F32).astype(v_o.dtype)
    kr_slab = _dot01_right(kr, place_ref[...])
    ct, s1, s2 = ct_ref[...], s1_ref[...], s2_ref[...]

    def head(x, g):
        x = x * lax.rsqrt(jnp.sum(x * x, axis=-1, keepdims=True) * (1.0 / MLA_QK) + NORM_EPS) * g
        return x * ct + pltpu.roll(x, _LANES - MLA_ROPE // 4, 1) * s1 + pltpu.roll(x, MLA_ROPE // 4, 1) * s2

    for h in range(MLA_HEADS):
        sl = slice(h * _LANES, (h + 1) * _LANES)
        q_o[h] = head(q[:, sl], gq_ref[...]).astype(q_o.dtype)
        k_o[h] = head(kn[:, sl] + kr_slab, gk_ref[...]).astype(k_o.dtype)


def _mla_prep(f_mla, pw, l, ropes, dims):
    m = f_mla.shape[0]
    tm = 256
    n_lat = dims.S // tm
    bs = dims.BS

    def rope_blk(i):
        return jnp.where(i * tm < bs, i % n_lat, n_lat)

    hw = MLA_HEADS * _LANES
    full = lambda shape: pl.BlockSpec(shape, lambda i: tuple(0 for _ in shape))
    lyr = lambda shape: pl.BlockSpec((None,) + shape, lambda i: (l,) + tuple(0 for _ in shape))
    return pl.pallas_call(
        _mla_prep_kernel,
        grid=(m // tm,),
        in_specs=[pl.BlockSpec((tm, f_mla.shape[1]), lambda i: (i, 0)),
                  lyr((MLA_Q_RANK, hw)), lyr((MLA_KV_RANK, hw)), lyr((MLA_KV_RANK, MLA_HEADS * MLA_V)),
                  full((MLA_ROPE, _LANES)),
                  lyr((1, MLA_Q_RANK)), lyr((1, MLA_KV_RANK)), lyr((1, _LANES)), lyr((1, _LANES)),
                  pl.BlockSpec((tm, _LANES), lambda i: (rope_blk(i), 0)),
                  pl.BlockSpec((tm, _LANES), lambda i: (rope_blk(i), 0)),
                  pl.BlockSpec((tm, _LANES), lambda i: (rope_blk(i), 0))],
        out_specs=[pl.BlockSpec((MLA_HEADS, tm, _LANES), lambda i: (0, i, 0)),
                   pl.BlockSpec((MLA_HEADS, tm, _LANES), lambda i: (0, i, 0)),
                   pl.BlockSpec((tm, MLA_HEADS * MLA_V), lambda i: (i, 0))],
        out_shape=[jax.ShapeDtypeStruct((MLA_HEADS, m, _LANES), BF16),
                   jax.ShapeDtypeStruct((MLA_HEADS, m, _LANES), BF16),
                   jax.ShapeDtypeStruct((m, MLA_HEADS * MLA_V), BF16)],
        compiler_params=_cp("parallel"),
    )(f_mla, pw["mla_wq"], pw["mla_wk"], pw["mla_wv"], pw["mla_place"], pw["mla_gcq"], pw["mla_gckv"],
      pw["mla_gq"], pw["mla_gk"], *ropes)


def _mla_attn_kernel(qe_ref, qo_ref, kce_ref, kco_ref, kle_ref, klo_ref, vc_ref, vl_ref, o_ref):
    lo, _, _ = _half_masks(F32)
    n_chunks = kle_ref.shape[0] // MLA_KEY_CHUNK

    def segs(kc_ref, kl_ref):
        out = [(kc_ref[...], vc_ref[...], None)]
        for j in range(n_chunks):
            sl = slice(j * MLA_KEY_CHUNK, (j + 1) * MLA_KEY_CHUNK)
            out.append((kl_ref[sl, :], vl_ref[sl, :], None))
        return out

    o_e, o_o = _attend([qe_ref[...], qo_ref[...]], [segs(kce_ref, kle_ref), segs(kco_ref, klo_ref)])
    o_ref[...] = jnp.where(lo, o_e, o_o).astype(o_ref.dtype)


def _mla_ctx_kernel(qe_ref, qo_ref, kce_ref, kco_ref, vc_ref, o_ref):
    lo, _, _ = _half_masks(F32)
    o_e, o_o = _attend([qe_ref[...], qo_ref[...]],
                       [[(kce_ref[...], vc_ref[...], None)], [(kco_ref[...], vc_ref[...], None)]])
    o_ref[...] = jnp.where(lo, o_e, o_o).astype(o_ref.dtype)


def _mla_attention(q, k, v, dims, need_ctx):
    b, s, c, w = dims.B, dims.S, dims.C, dims.W
    tq = MLA_Q_TILE
    nq = s // tq
    ctx_blk = dims.BS // c
    pairs = MLA_HEADS // 2
    hq = lambda half: pl.BlockSpec((None, tq, _LANES), lambda bi, p, i: (2 * p + half, bi * nq + i, 0))
    hkc = lambda half: pl.BlockSpec((None, c, _LANES), lambda bi, p, i: (2 * p + half, ctx_blk + bi, 0))
    hkl = lambda half: pl.BlockSpec((None, s, _LANES), lambda bi, p, i: (2 * p + half, bi, 0))
    yc_x = pl.pallas_call(
        _mla_attn_kernel,
        grid=(b, pairs, nq),
        in_specs=[hq(0), hq(1), hkc(0), hkc(1), hkl(0), hkl(1),
                  pl.BlockSpec((c, _LANES), lambda bi, p, i: (ctx_blk + bi, p)),
                  pl.BlockSpec((s, _LANES), lambda bi, p, i: (bi, p))],
        out_specs=pl.BlockSpec((tq, _LANES), lambda bi, p, i: (bi * nq + i, p)),
        out_shape=jax.ShapeDtypeStruct((dims.BS, w), BF16),
        compiler_params=_cp("parallel", "parallel", "arbitrary"),
    )(q, q, k, k, k, k, v, v)
    if not need_ctx:
        return yc_x
    cq = lambda half: pl.BlockSpec((None, c, _LANES), lambda bi, p: (2 * p + half, ctx_blk + bi, 0))
    yc_s = pl.pallas_call(
        _mla_ctx_kernel,
        grid=(b, pairs),
        in_specs=[cq(0), cq(1), cq(0), cq(1),
                  pl.BlockSpec((c, _LANES), lambda bi, p: (ctx_blk + bi, p))],
        out_specs=pl.BlockSpec((c, _LANES), lambda bi, p: (bi, p)),
        out_shape=jax.ShapeDtypeStruct((b * c, w), BF16),
        compiler_params=_cp("parallel", "parallel"),
    )(q, q, k, k, v)
    return jnp.concatenate([yc_x, yc_s], axis=0)


def _rw_prep_kernel(f_ref, fp_ref, fn_ref, mu_ref, w0_ref, w2_ref, a0_ref, a2_ref, g2_ref, kkg_ref,
                    ka_ref, bd_ref, r_o, v_o, kk_o, g_o, lw_o, kd_o, bb_o, *, s, c, bs, tm):
    w = r_o.shape[1]
    row0 = pl.program_id(0) * tm
    is_lat = row0 < bs
    pos = jnp.where(is_lat, row0 % s, (row0 - bs) % c)
    seglen = jnp.where(is_lat, s, c)
    f = f_ref[...]
    prev_row = jnp.where(pos == 0, 0.0, fp_ref[7:8, :])
    next_row = jnp.where(pos + tm == seglen, 0.0, fn_ref[0:1, :])
    ridx = lax.broadcasted_iota(I32, (tm, 1), 0)
    prev = jnp.where(ridx == 0, prev_row, pltpu.roll(f, 1, 0))
    nxt = jnp.where(ridx == tm - 1, next_row, pltpu.roll(f, tm - 1, 0))
    fm = f + mu_ref[0:1, :] * (prev - f) + mu_ref[1:2, :] * (nxt - f)
    r, k, v = fm[:, 0:w], fm[:, w:2 * w], fm[:, 2 * w:3 * w]
    o = 3 * w
    w_low = jnp.tanh(fm[:, o:o + 128])
    a_low = fm[:, o + 128:o + 256]
    g_low = _sigmoid(fm[:, o + 256:o + 384])
    log_w = -_softplus(-(w0_ref[...] + _dot(w_low, w2_ref[...]))) - 0.5
    lw = -jnp.exp(log_w)
    a = _sigmoid(a0_ref[...] + _dot(a_low, a2_ref[...]))
    g_o[...] = _dot(g_low, g2_ref[...])
    kk = k * kkg_ref[...]
    kk = kk / jnp.maximum(jnp.sqrt(_dot01_right(kk * kk, bd_ref[...])), 1e-12)
    r_o[...] = r
    v_o[...] = v
    kk_o[...] = kk
    for d in range(2):
        ad = a[:, d * w:(d + 1) * w]
        lw_o[d] = lw[:, d * w:(d + 1) * w]
        kd_o[d] = k * (1.0 + (ad - 1.0) * ka_ref[...])
        bb_o[d] = ad * kk


def _rw_prep(f_rw, pw, l, dims):
    m, cols = f_rw.shape
    w = dims.W
    tm = 256
    assert dims.C % tm == 0 and dims.S % tm == 0
    nb8 = m // 8
    lyr = lambda shape: pl.BlockSpec((None,) + shape, lambda i: (l,) + tuple(0 for _ in shape))
    out_w = pl.BlockSpec((tm, w), lambda i: (i, 0))
    out_2w = pl.BlockSpec((2, tm, w), lambda i: (0, i, 0))
    sd = jax.ShapeDtypeStruct
    return pl.pallas_call(
        functools.partial(_rw_prep_kernel, s=dims.S, c=dims.C, bs=dims.BS, tm=tm),
        grid=(m // tm,),
        in_specs=[pl.BlockSpec((tm, cols), lambda i: (i, 0)),
                  pl.BlockSpec((8, cols), lambda i: (jnp.maximum(i * (tm // 8) - 1, 0), 0)),
                  pl.BlockSpec((8, cols), lambda i: (jnp.minimum((i + 1) * (tm // 8), nb8 - 1), 0)),
                  lyr((2, cols)), lyr((1, 2 * w)), lyr((128, 2 * w)), lyr((1, 2 * w)), lyr((128, 2 * w)),
                  lyr((128, w)), lyr((1, w)), lyr((1, w)),
                  pl.BlockSpec((w, w), lambda i: (0, 0))],
        out_specs=[out_w, out_w, out_w, out_w, out_2w, out_2w, out_2w],
        out_shape=[sd((m, w), F32)] * 4 + [sd((2, m, w), F32)] * 3,
        compiler_params=_cp("parallel"),
    )(f_rw, f_rw, f_rw, pw["rw_mu"], pw["rw_w0"], pw["rw_w2"], pw["rw_a0"], pw["rw_a2"], pw["rw_g2"],
      pw["rw_kk"], pw["rw_ka"], _head_blockdiag(w, RW_HEAD_DIM))


def _rw_chunk_kernel(rf, vf, kkf, lwf, kdf, bbf, rb, vb, kkb, lwb, kdb, bbb, yf_ref, yb_ref, h_ref):
    c = RW_CHUNK

    @pl.when(pl.program_id(2) == 0)
    def _():
        h_ref[...] = jnp.zeros_like(h_ref)

    ii = lax.broadcasted_iota(I32, (c, c), 0)
    jj = lax.broadcasted_iota(I32, (c, c), 1)
    tri_f = jnp.where(jj <= ii, 1.0, 0.0).astype(BF16)
    tri_b = jnp.where(jj >= ii, 1.0, 0.0).astype(BF16)
    _, m_e, m_o = _half_masks(F32)

    n2 = 2 * c
    i2 = lax.broadcasted_iota(I32, (n2, n2), 0)
    j2 = lax.broadcasted_iota(I32, (n2, n2), 1)
    same = (i2 // c) == (j2 // c)
    eye2 = jnp.where(i2 == j2, 1.0, 0.0)
    eye = lax.broadcasted_iota(I32, (_LANES, _LANES), 0) == lax.broadcasted_iota(I32, (_LANES, _LANES), 1)
    steps = int(np.log2(c)) - 1

    def stack(x):
        return jnp.concatenate([x * m_e, x * m_o], axis=0).astype(BF16)

    def feats(r_ref, kk_ref, lw_ref, kd_ref, bb_ref, v_ref, sl, tri, last, lag):
        lw = lw_ref[:, sl]
        cum = _dot01_left(tri, lw)
        tot = cum[last:last + 1, :]
        w_inv = jnp.exp(-cum)
        w_end = jnp.exp(tot - cum)
        bb, kd = bb_ref[:, sl], kd_ref[:, sl]
        lag = jnp.where(same, lag, -1)
        return dict(la=stack(-kk_ref[:, sl] * jnp.exp(cum - lw)), lr=stack(r_ref[:, sl] * jnp.exp(cum)),
                    rb=stack(bb * w_inv), rk=stack(kd * w_inv), vm=stack(v_ref[:, sl]),
                    bh=stack(bb * w_end), kh=stack(kd * w_end), wtot=jnp.exp(tot),
                    strict=lag > 0, incl=lag >= 0)

    sls = [slice(p * _LANES, (p + 1) * _LANES) for p in range(h_ref.shape[0])]
    f = []
    for sl in sls:
        f.append(feats(rf, kkf, lwf, kdf, bbf, vf, sl, tri_f, c - 1, i2 - j2))
        f.append(feats(rb, kkb, lwb, kdb, bbb, vb, sl, tri_b, 0, j2 - i2))
    un = range(len(f))
    nmat = [jnp.where(f[k]["strict"], _dot_nt(f[k]["la"], f[k]["rb"]), 0.0) for k in un]
    aak = [jnp.where(f[k]["strict"], _dot_nt(f[k]["la"], f[k]["rk"]), 0.0) for k in un]
    arb = [jnp.where(f[k]["incl"], _dot_nt(f[k]["lr"], f[k]["rb"]), 0.0) for k in un]
    ark = [jnp.where(f[k]["incl"], _dot_nt(f[k]["lr"], f[k]["rk"]), 0.0) for k in un]
    t_inv = [eye2 + nmat[k] for k in un]
    pw = nmat
    for _ in range(steps):
        pw = [_dot(pw[k], pw[k]) for k in un]
        t_inv = [t_inv[k] + _dot(t_inv[k], pw[k]) for k in un]
    x = [_dot(t_inv[k], f[k]["la"]) for k in un]
    av = [_dot(aak[k], f[k]["vm"]) for k in un]
    u = [_dot(t_inv[k], av[k]) for k in un]
    q = [f[k]["lr"].astype(F32) + _dot(arb[k], x[k]) for k in un]
    y0 = [_dot(arb[k], u[k]) + _dot(ark[k], f[k]["vm"]) for k in un]
    h = [h_ref[k // 2, k % 2] for k in un]
    yst = [_dot(q[k], h[k]) + y0[k] for k in un]
    for k in un:
        out_ref = yb_ref if k % 2 else yf_ref
        out_ref[:, sls[k // 2]] = yst[k][0:c] + yst[k][c:n2]
    trans = [jnp.where(eye, f[k]["wtot"], 0.0) + _dot_tn(f[k]["bh"], x[k]) for k in un]
    for k in un:
        h_ref[k // 2, k % 2] = (_dot(trans[k], h[k]) + _dot_tn(f[k]["bh"], u[k])
                                + _dot_tn(f[k]["kh"], f[k]["vm"]))


def _rw_scan(r, v, kk, lw, kd, bb, dims):
    b, s, c, w = dims.B, dims.S, dims.C, dims.W
    ch = RW_CHUNK
    n_c, n_s = c // ch, s // ch
    ctx0 = dims.BS // ch
    pp = RW_PAIRS_PER_STEP
    bw = pp * _LANES
    assert w % bw == 0

    def cf(bi, t):
        return jnp.where(t < n_c, ctx0 + bi * n_c + t, bi * n_s + (t - n_c))

    def cb(bi, t):
        return jnp.where(t < n_c, ctx0 + bi * n_c + (n_c - 1 - t), bi * n_s + (n_s - 1 - (t - n_c)))

    sh = lambda fn: pl.BlockSpec((ch, bw), lambda bi, p, t: (fn(bi, t), p))
    dr = lambda fn, d: pl.BlockSpec((None, ch, bw), lambda bi, p, t: (d, fn(bi, t), p))
    m = dims.M
    return pl.pallas_call(
        _rw_chunk_kernel,
        grid=(b, w // bw, n_c + n_s),
        in_specs=[sh(cf), sh(cf), sh(cf), dr(cf, 0), dr(cf, 0), dr(cf, 0),
                  sh(cb), sh(cb), sh(cb), dr(cb, 1), dr(cb, 1), dr(cb, 1)],
        out_specs=[sh(cf), sh(cb)],
        out_shape=[jax.ShapeDtypeStruct((m, w), F32)] * 2,
        scratch_shapes=[pltpu.VMEM((pp, 2, _LANES, _LANES), F32)],
        compiler_params=_cp("parallel", "parallel", "arbitrary"),
    )(r, v, kk, lw, kd, bb, r, v, kk, lw, kd, bb)


def _rw_readout_kernel(yf_ref, yb_ref, r_ref, v_ref, g_ref, kd_ref, rk_ref, gg_ref, gb_ref, bd_ref, o_ref):
    bd = bd_ref[...]
    inv = 1.0 / RW_HEAD_DIM
    y = yf_ref[...] + yb_ref[...]
    yc = y - _dot01_right(y, bd) * inv
    var = _dot01_right(yc * yc, bd) * inv
    yn = yc * lax.rsqrt(var + RW_GN_EPS)
    bonus = _dot01_right(r_ref[...] * (kd_ref[0] + kd_ref[1]) * rk_ref[...], bd) * v_ref[...]
    o_ref[...] = ((yn * gg_ref[...] + gb_ref[...] + bonus) * g_ref[...]).astype(o_ref.dtype)


def _rw_readout(yf, yb, r, v, g, kd, pw, l, dims, rows):
    w = dims.W
    tm = 256
    blk = pl.BlockSpec((tm, w), lambda i: (i, 0))
    lyr = pl.BlockSpec((None, 1, w), lambda i: (l, 0, 0))
    return pl.pallas_call(
        _rw_readout_kernel,
        grid=(rows // tm,),
        in_specs=[blk, blk, blk, blk, blk, pl.BlockSpec((2, tm, w), lambda i: (0, i, 0)), lyr, lyr, lyr,
                  pl.BlockSpec((w, w), lambda i: (0, 0))],
        out_specs=blk,
        out_shape=jax.ShapeDtypeStruct((rows, w), BF16),
        compiler_params=_cp("parallel"),
    )(yf, yb, r, v, g, kd, pw["rw_rk"], pw["rw_gn_g"], pw["rw_gn_b"], _head_blockdiag(w, RW_HEAD_DIM))


def _s5_matrices(a_re, a_im, log_dt, b_re, b_im, c_re, c_im, d_skip):
    hi = lax.Precision.HIGHEST
    n_g, n_p, n_j = b_re.shape[1], b_re.shape[2], b_re.shape[3]
    ct = S5_CHUNK
    dt = jnp.exp(log_dt)[..., None]
    mag = jnp.exp(a_re * dt)
    lam_re, lam_im = mag * jnp.cos(a_im * dt), mag * jnp.sin(a_im * dt)
    den = a_re * a_re + a_im * a_im
    f_re = ((lam_re - 1.0) * a_re + lam_im * a_im) / den
    f_im = (lam_im * a_re - (lam_re - 1.0) * a_im) / den
    bb_re = f_re[..., None] * b_re - f_im[..., None] * b_im
    bb_im = f_re[..., None] * b_im + f_im[..., None] * b_re
    p_re, p_im = [jnp.ones_like(lam_re)], [jnp.zeros_like(lam_im)]
    for _ in range(ct):
        p_re.append(p_re[-1] * lam_re - p_im[-1] * lam_im)
        p_im.append(p_re[-2] * lam_im + p_im[-1] * lam_re)
    p_re, p_im = jnp.stack(p_re), jnp.stack(p_im)
    cl_re = c_re[None] * p_re[:, :, :, None, :] - c_im[None] * p_im[:, :, :, None, :]
    cl_im = c_re[None] * p_im[:, :, :, None, :] + c_im[None] * p_re[:, :, :, None, :]
    kern = jnp.sum(cl_re[..., None] * bb_re[None, :, :, None] - cl_im[..., None] * bb_im[None, :, :, None],
                   axis=4)
    j_idx = np.arange(ct)[:, None]
    t_idx = np.arange(ct)[None, :]
    lag_f, lag_b = t_idx - j_idx, j_idx - t_idx
    sel_f = jnp.asarray(lag_f[:, :, None] == np.arange(ct + 1), F32)
    sel_b = jnp.asarray(lag_b[:, :, None] == np.arange(ct + 1), F32)
    toep = (jnp.einsum('jtn,ngim->gjmti', sel_f, kern[:, 0], precision=hi)
            + jnp.einsum('jtn,ngim->gjmti', sel_b, kern[:, 1], precision=hi))
    skip = (jnp.eye(ct)[None, :, None, :, None] * jnp.eye(n_j)[None, None, :, None, :]
            * d_skip.reshape(n_g, 1, 1, 1, n_j))
    toep = (toep + skip).reshape(n_g, ct * n_j, ct * n_j)

    def powers(z, d, first, reverse):
        z = z[first:first + ct, d]
        return z[::-1] if reverse else z

    def state_in(first, reverse, d):
        pr, pi = powers(p_re, d, first, reverse), powers(p_im, d, first, reverse)
        re = pr[..., None] * bb_re[d][None] - pi[..., None] * bb_im[d][None]
        im = pr[..., None] * bb_im[d][None] + pi[..., None] * bb_re[d][None]
        fix = lambda z: z.transpose(1, 0, 3, 2).reshape(n_g, ct * n_j, n_p)
        return fix(re), fix(im)

    def state_out(first, reverse, d):
        fix = lambda z: z.transpose(1, 3, 0, 2).reshape(n_g, n_p, ct * n_j)
        return fix(powers(cl_re, d, first, reverse)), fix(-powers(cl_im, d, first, reverse))

    rf_re, rf_im = state_in(0, True, 0)
    rb_re, rb_im = state_in(0, False, 1)
    of_re, of_im = state_out(1, False, 0)
    ob_re, ob_im = state_out(1, True, 1)
    half = (np.arange(n_g) % 2)[:, None, None]

    def lane_half(z):
        zero = jnp.zeros_like(z)
        return jnp.concatenate([jnp.where(half == 0, z, zero), jnp.where(half == 1, z, zero)], axis=-1)

    def row_half(z):
        zero = jnp.zeros_like(z)
        return jnp.concatenate([jnp.where(half == 0, z, zero), jnp.where(half == 1, z, zero)], axis=1)

    r_re = jnp.stack([lane_half(rf_re), lane_half(rb_re)]).astype(BF16)
    r_im = jnp.stack([lane_half(rf_im), lane_half(rb_im)]).astype(BF16)
    o_re = jnp.stack([row_half(of_re), row_half(ob_re)]).astype(BF16)
    o_im = jnp.stack([row_half(of_im), row_half(ob_im)]).astype(BF16)
    lam_c = (p_re[ct].reshape(2, 1, n_g * n_p), p_im[ct].reshape(2, 1, n_g * n_p))
    return toep.astype(BF16), r_re, r_im, o_re, o_im, lam_c


def _s5_z_kernel(u_ref, rre_ref, rim_ref, zre_o, zim_o):
    u0, u1 = u_ref[0].astype(BF16), u_ref[1].astype(BF16)
    for d in range(2):
        zre_o[d] = (jnp.dot(u0, rre_ref[d, 0], preferred_element_type=F32)
                    + jnp.dot(u1, rre_ref[d, 1], preferred_element_type=F32))
        zim_o[d] = (jnp.dot(u0, rim_ref[d, 0], preferred_element_type=F32)
                    + jnp.dot(u1, rim_ref[d, 1], preferred_element_type=F32))


def _s5_scan_kernel(zre_ref, zim_ref, lre_ref, lim_ref, xre_o, xim_o, *, n_b, n_s, n_c):
    lanes = zre_ref.shape[2]
    chains = [(d, b) for d in range(2) for b in range(n_b)]
    lam = [(lre_ref[d], lim_ref[d]) for d in range(2)]

    def run(base_of, length, carry):
        def body(t, carry):
            new = []
            for (d, b), (xr, xi) in zip(chains, carry):
                row = base_of(b) + (t if d == 0 else length - 1 - t)
                xre_o[d, pl.ds(row, 1), :] = xr
                xim_o[d, pl.ds(row, 1), :] = xi
                lr, li = lam[d]
                new.append((lr * xr - li * xi + zre_ref[d, pl.ds(row, 1), :],
                            lr * xi + li * xr + zim_ref[d, pl.ds(row, 1), :]))
            return tuple(new)
        return lax.fori_loop(0, length, body, carry)

    zero = jnp.zeros((1, lanes), F32)
    carry = tuple((zero, zero) for _ in chains)
    carry = run(lambda b: n_b * n_s + b * n_c, n_c, carry)
    run(lambda b: b * n_s, n_s, carry)


def _s5_y_kernel(u_ref, toep_ref, xre_ref, xim_ref, ore_ref, oim_ref, y_o):
    xr = [xre_ref[d].astype(BF16) for d in range(2)]
    xi = [xim_ref[d].astype(BF16) for d in range(2)]
    for g in range(2):
        y = jnp.dot(u_ref[g].astype(BF16), toep_ref[g], preferred_element_type=F32)
        for d in range(2):
            y = y + jnp.dot(xr[d], ore_ref[d, g], preferred_element_type=F32)
            y = y + jnp.dot(xi[d], oim_ref[d, g], preferred_element_type=F32)
        y_o[g] = y.astype(y_o.dtype)


def _s5_glu_kernel(y_ref, w_ref, b_ref, o_ref, wbf_ref):
    @pl.when(pl.program_id(0) == 0)
    def _():
        wbf_ref[...] = w_ref[...].astype(BF16)

    y = y_ref[...].astype(F32)
    y = 0.5 * y * (1.0 + jnp.tanh(0.7978845608028654 * (y + 0.044715 * y * y * y)))
    z = jnp.dot(y.astype(BF16), wbf_ref[...], preferred_element_type=F32) + b_ref[...]
    o_ref[...] = (y * _sigmoid(z)).astype(o_ref.dtype)


def _s5_mixer(f_s5, mats, w_glu, b_glu3, l, dims, rows):
    toep, r_re, r_im, o_re, o_im, (lam_re, lam_im) = mats
    m, w = f_s5.shape
    ct = S5_CHUNK
    n_g = w // S5_GROUP
    n_ch = m // ct
    gw = ct * S5_GROUP
    u = f_s5.reshape(n_ch, ct, n_g, S5_GROUP).transpose(2, 0, 1, 3).reshape(n_g, n_ch, gw)
    sl = n_g * S5_STATE
    sd = jax.ShapeDtypeStruct
    zre, zim = pl.pallas_call(
        _s5_z_kernel,
        grid=(n_g // 2,),
        in_specs=[pl.BlockSpec((2, n_ch, gw), lambda g: (g, 0, 0)),
                  pl.BlockSpec((None, 2, 2, gw, _LANES), lambda g: (l, 0, g, 0, 0)),
                  pl.BlockSpec((None, 2, 2, gw, _LANES), lambda g: (l, 0, g, 0, 0))],
        out_specs=[pl.BlockSpec((2, n_ch, _LANES), lambda g: (0, 0, g))] * 2,
        out_shape=[sd((2, n_ch, sl), F32)] * 2,
        compiler_params=_cp("parallel"),
    )(u, r_re, r_im)
    lb = 512
    xre, xim = pl.pallas_call(
        functools.partial(_s5_scan_kernel, n_b=dims.B, n_s=dims.S // ct, n_c=dims.C // ct),
        grid=(sl // lb,),
        in_specs=[pl.BlockSpec((2, n_ch, lb), lambda j: (0, 0, j))] * 2
                 + [pl.BlockSpec((None, 2, 1, lb), lambda j: (l, 0, 0, j))] * 2,
        out_specs=[pl.BlockSpec((2, n_ch, lb), lambda j: (0, 0, j))] * 2,
        out_shape=[sd((2, n_ch, sl), F32)] * 2,
        compiler_params=_cp("parallel"),
    )(zre, zim, lam_re, lam_im)
    y = pl.pallas_call(
        _s5_y_kernel,
        grid=(n_g // 2,),
        in_specs=[pl.BlockSpec((2, n_ch, gw), lambda g: (g, 0, 0)),
                  pl.BlockSpec((None, 2, gw, gw), lambda g: (l, g, 0, 0)),
                  pl.BlockSpec((2, n_ch, _LANES), lambda g: (0, 0, g)),
                  pl.BlockSpec((2, n_ch, _LANES), lambda g: (0, 0, g)),
                  pl.BlockSpec((None, 2, 2, _LANES, gw), lambda g: (l, 0, g, 0, 0)),
                  pl.BlockSpec((None, 2, 2, _LANES, gw), lambda g: (l, 0, g, 0, 0))],
        out_specs=pl.BlockSpec((2, n_ch, gw), lambda g: (g, 0, 0)),
        out_shape=sd((n_g, n_ch, gw), BF16),
        compiler_params=_cp("parallel"),
    )(u, toep, xre, xim, o_re, o_im)
    y = y.reshape(n_g, n_ch, ct, S5_GROUP).transpose(1, 2, 0, 3).reshape(m, w)
    tm = 256
    return pl.pallas_call(
        _s5_glu_kernel,
        grid=(rows // tm,),
        in_specs=[pl.BlockSpec((tm, w), lambda i: (i, 0)),
                  pl.BlockSpec((None, w, w), lambda i: (l, 0, 0)),
                  pl.BlockSpec((None, 1, w), lambda i: (l, 0, 0))],
        out_specs=pl.BlockSpec((tm, w), lambda i: (i, 0)),
        out_shape=sd((rows, w), BF16),
        scratch_shapes=[pltpu.VMEM((w, w), BF16)],
        compiler_params=_cp("arbitrary"),
    )(y, w_glu, b_glu3)


def _merge_kernel(xn_ref, ya_ref, yb_ref, yc_ref, yd_ref, wg0, wg1, wg2, wg3, bg0, bg1, bg2, bg3, wb_ref,
                  o_ref, wg_bf, wb_bf):
    wgs = (wg0, wg1, wg2, wg3)

    @pl.when(pl.program_id(1) == 0)
    def _():
        for n in range(4):
            wg_bf[n] = wgs[n][...].astype(BF16)
            wb_bf[n] = wb_ref[n].astype(BF16)

    xn = xn_ref[...]
    acc = None
    for n, (y_ref, bg) in enumerate(zip((ya_ref, yb_ref, yc_ref, yd_ref), (bg0, bg1, bg2, bg3))):
        gate = _sigmoid(jnp.dot(xn, wg_bf[n], preferred_element_type=F32) + bg[...])
        up = jnp.dot(y_ref[...], wb_bf[n], preferred_element_type=F32)
        acc = gate * up if acc is None else acc + gate * up
    o_ref[...] = acc.astype(o_ref.dtype)


def _merge(xn, ys, w_gate, b_gate3, w_branch, l, rows):
    d = xn.shape[1]
    w = ys[0].shape[1]
    tn = 256
    tm = _pick(rows, (1088, 1024, 512, 256))
    nb = d // tn
    wg = lambda n: pl.BlockSpec((None, d, tn), lambda j, i: (l, 0, n * nb + j))
    bg = lambda n: pl.BlockSpec((None, 1, tn), lambda j, i: (l, 0, n * nb + j))
    yspec = pl.BlockSpec((tm, w), lambda j, i: (i, 0))
    return pl.pallas_call(
        _merge_kernel,
        grid=(nb, rows // tm),
        in_specs=[pl.BlockSpec((tm, d), lambda j, i: (i, 0)), yspec, yspec, yspec, yspec,
                  wg(0), wg(1), wg(2), wg(3), bg(0), bg(1), bg(2), bg(3),
                  pl.BlockSpec((None, 4, w, tn), lambda j, i: (l, 0, 0, j))],
        out_specs=pl.BlockSpec((tm, tn), lambda j, i: (i, j)),
        out_shape=jax.ShapeDtypeStruct((rows, d), BF16),
        scratch_shapes=[pltpu.VMEM((4, d, tn), BF16), pltpu.VMEM((4, w, tn), BF16)],
        compiler_params=_cp("arbitrary", "arbitrary"),
    )(xn, *ys, w_gate, w_gate, w_gate, w_gate, b_gate3, b_gate3, b_gate3, b_gate3, w_branch)


def _router_kernel(h_ref, wr_ref, br_ref, idx_o, gate_o, rank_o, cnt_o, hp_o, carry):
    tm = h_ref.shape[0]
    hp_o[...] = _pack_pairs(h_ref[...])

    @pl.when(pl.program_id(0) == 0)
    def _():
        carry[...] = jnp.zeros_like(carry)

    h_hi, h_lo = _split(h_ref[...], 2)
    w_hi, w_lo = _split(wr_ref[...], 2)
    logits = (jnp.dot(h_hi, w_hi, preferred_element_type=F32) + jnp.dot(h_lo, w_hi, preferred_element_type=F32)
              + jnp.dot(h_hi, w_lo, preferred_element_type=F32)) + br_ref[...]
    lane = lax.broadcasted_iota(I32, (tm, _LANES), 1)
    vals, idxs, hots = [], [], []
    cur = logits
    for _ in range(TOP_K):
        mx = jnp.max(cur, axis=-1, keepdims=True)
        ix = jnp.min(jnp.where(cur == mx, lane, _LANES), axis=-1, keepdims=True)
        hot = lane == ix
        vals.append(mx)
        idxs.append(ix)
        hots.append(hot)
        cur = jnp.where(hot, -jnp.inf, cur)
    ex = [jnp.exp(v - vals[0]) for v in vals]
    inv = 1.0 / sum(ex)
    sel = sum(jnp.where(h, 1.0, 0.0) for h in hots)
    ri = lax.broadcasted_iota(I32, (tm, tm), 0)
    ci = lax.broadcasted_iota(I32, (tm, tm), 1)
    earlier = jnp.where(ci < ri, 1.0, 0.0).astype(BF16)
    before = jnp.dot(earlier, sel.astype(BF16), preferred_element_type=F32) + carry[...]
    idx_out = jnp.zeros((tm, _LANES), I32)
    gate_out = jnp.zeros((tm, _LANES), F32)
    rank_out = jnp.zeros((tm, _LANES), I32)
    for k in range(TOP_K):
        rk = jnp.sum(jnp.where(hots[k], before, 0.0), axis=-1, keepdims=True)
        idx_out = jnp.where(lane == k, idxs[k], idx_out)
        gate_out = jnp.where(lane == k, ex[k] * inv, gate_out)
        rank_out = jnp.where(lane == k, rk.astype(I32), rank_out)
    idx_o[...] = idx_out
    gate_o[...] = gate_out
    rank_o[...] = rank_out
    carry[...] = carry[...] + jnp.sum(sel, axis=0, keepdims=True)
    cnt_o[...] = jnp.broadcast_to(carry[...], cnt_o.shape)


def _router(h, wr_pad, br_pad, l, rows):
    d = h.shape[1]
    tm = 256
    blk = pl.BlockSpec((tm, _LANES), lambda i: (i, 0))
    sd = jax.ShapeDtypeStruct
    return pl.pallas_call(
        _router_kernel,
        grid=(rows // tm,),
        in_specs=[pl.BlockSpec((tm, d), lambda i: (i, 0)),
                  pl.BlockSpec((None, d, _LANES), lambda i: (l, 0, 0)),
                  pl.BlockSpec((None, 1, _LANES), lambda i: (l, 0, 0))],
        out_specs=[blk, blk, blk, pl.BlockSpec((8, _LANES), lambda i: (0, 0)),
                   pl.BlockSpec((tm, d // 2), lambda i: (i, 0))],
        out_shape=[sd((rows, _LANES), I32), sd((rows, _LANES), F32), sd((rows, _LANES), I32),
                   sd((8, _LANES), F32), sd((rows, d // 2), jnp.uint32)],
        scratch_shapes=[pltpu.VMEM((1, _LANES), F32)],
        compiler_params=_cp("arbitrary"),
    )(h, wr_pad, br_pad)


def _row_copy(src_hbm, row, dst, slot, sem):
    return pltpu.make_async_copy(src_hbm.at[pl.ds(row, 1), :], dst.at[pl.ds(slot, 1), :], sem)


def _issue_rows(idx_ref, n, src_hbm, dst, sem):
    for r in range(n):
        _row_copy(src_hbm, idx_ref[0, r], dst, r, sem).start()


def _expert_kernel(be_ref, first_ref, used_ref, tok_ref, tok_next_ref, h_hbm, wgu_ref, bgu_ref, wdn_ref,
                   bdn_ref, o_ref, xbuf, sems, wgu_bf, wdn_bf):
    i = pl.program_id(0)
    n_used = used_ref[0]
    slot = i % 2
    blk = xbuf.shape[1]

    @pl.when(i == 0)
    def _():
        _issue_rows(tok_ref, blk, h_hbm, xbuf.at[0], sems.at[0])

    @pl.when(first_ref[i] == 1)
    def _():
        wgu_bf[...] = wgu_ref[...].astype(BF16)
        wdn_bf[...] = wdn_ref[...].astype(BF16)

    @pl.when(i <= n_used)
    def _():
        pltpu.make_async_copy(h_hbm.at[pl.ds(0, blk), :], xbuf.at[slot], sems.at[slot]).wait()

    @pl.when(i < n_used)
    def _():
        _issue_rows(tok_next_ref, blk, h_hbm, xbuf.at[1 - slot], sems.at[1 - slot])
        x = jnp.concatenate(_unpack_pairs(xbuf[slot]), axis=1).astype(BF16)
        gu = jnp.dot(x, wgu_bf[...], preferred_element_type=F32) + bgu_ref[...]
        g_lin = jnp.minimum(gu[:, :D_EXPERT], SWIGLU_LIMIT)
        u_lin = jnp.clip(gu[:, D_EXPERT:], -SWIGLU_LIMIT, SWIGLU_LIMIT)
        act = (u_lin + 1.0) * g_lin * _sigmoid(SWIGLU_ALPHA * g_lin)
        y = jnp.dot(act.astype(BF16), wdn_bf[...], preferred_element_type=F32) + bdn_ref[...]
        o_ref[...] = _pack_pairs(y)

    @pl.when(i >= n_used)
    def _():
        o_ref[...] = jnp.zeros_like(o_ref)


def _moe_experts(h, slot_tok3, block_e, first, n_used, w_gu, b_gu4, w_dn, b_dn4, l):
    n_blocks = slot_tok3.shape[0] - 1
    blk = slot_tok3.shape[2]
    dp = h.shape[1]
    d = 2 * dp
    de2 = w_gu.shape[3]
    grid_spec = pltpu.PrefetchScalarGridSpec(
        num_scalar_prefetch=3,
        grid=(n_blocks,),
        in_specs=[pl.BlockSpec((None, 1, blk), lambda i, be, fi, us: (i, 0, 0), memory_space=pltpu.SMEM),
                  pl.BlockSpec((None, 1, blk), lambda i, be, fi, us: (i + 1, 0, 0), memory_space=pltpu.SMEM),
                  pl.BlockSpec(memory_space=pl.ANY),
                  pl.BlockSpec((None, None, d, de2), lambda i, be, fi, us: (l, be[i], 0, 0)),
                  pl.BlockSpec((None, None, 1, de2), lambda i, be, fi, us: (l, be[i], 0, 0)),
                  pl.BlockSpec((None, None, de2 // 2, d), lambda i, be, fi, us: (l, be[i], 0, 0)),
                  pl.BlockSpec((None, None, 1, d), lambda i, be, fi, us: (l, be[i], 0, 0))],
        out_specs=pl.BlockSpec((blk, dp), lambda i, be, fi, us: (i, 0)),
        scratch_shapes=[pltpu.VMEM((2, blk, dp), jnp.uint32), pltpu.SemaphoreType.DMA((2,)),
                        pltpu.VMEM((d, de2), BF16), pltpu.VMEM((de2 // 2, d), BF16)])
    return pl.pallas_call(
        _expert_kernel,
        grid_spec=grid_spec,
        out_shape=jax.ShapeDtypeStruct((n_blocks * blk, dp), jnp.uint32),
        compiler_params=_cp("arbitrary"),
    )(block_e, first, n_used, slot_tok3, slot_tok3, h, w_gu, b_gu4, w_dn, b_dn4)


def _combine_kernel(dest_ref, dest_next_ref, gate_ref, x_ref, g2_ref, ys_hbm, o_ref, buf, sems):
    tm = x_ref.shape[0]
    i = pl.program_id(0)
    slot = i % 2

    def issue(idx_ref, s):
        for r in range(tm):
            for k in range(TOP_K):
                _row_copy(ys_hbm, idx_ref[0, r * TOP_K + k], buf.at[s, k], r, sems.at[s]).start()

    @pl.when(i == 0)
    def _():
        issue(dest_ref, 0)

    @pl.when(i + 1 < pl.num_programs(0))
    def _():
        issue(dest_next_ref, 1 - slot)

    for k in range(TOP_K):
        pltpu.make_async_copy(ys_hbm.at[pl.ds(0, tm), :], buf.at[slot, k], sems.at[slot]).wait()
    gate = gate_ref[...]
    n = buf.shape[3]
    f_lo = f_hi = None
    for k in range(TOP_K):
        lo, hi = _unpack_pairs(buf[slot, k])
        gk = gate[:, k:k + 1]
        f_lo = gk * lo if f_lo is None else f_lo + gk * lo
        f_hi = gk * hi if f_hi is None else f_hi + gk * hi
    o_ref[:, :n] = x_ref[:, :n] + g2_ref[:, :n] * f_lo
    o_ref[:, n:] = x_ref[:, n:] + g2_ref[:, n:] * f_hi


def _moe_combine(x, ys, dest3, gate, mod, dims, rows):
    d = x.shape[1]
    tm = MOE_COMBINE_ROWS
    seg = dims.seg_fn(tm)
    return pl.pallas_call(
        _combine_kernel,
        grid=(rows // tm,),
        in_specs=[pl.BlockSpec((None, 1, tm * TOP_K), lambda i: (i, 0, 0), memory_space=pltpu.SMEM),
                  pl.BlockSpec((None, 1, tm * TOP_K), lambda i: (i + 1, 0, 0), memory_space=pltpu.SMEM),
                  pl.BlockSpec((tm, _LANES), lambda i: (i, 0)),
                  pl.BlockSpec((tm, d), lambda i: (i, 0)),
                  pl.BlockSpec((None, 1, d), lambda i: (seg(i), 0, 5)),
                  pl.BlockSpec(memory_space=pl.ANY)],
        out_specs=pl.BlockSpec((tm, d), lambda i: (i, 0)),
        out_shape=jax.ShapeDtypeStruct((rows, d), F32),
        scratch_shapes=[pltpu.VMEM((2, TOP_K, tm, d // 2), jnp.uint32), pltpu.SemaphoreType.DMA((2,))],
        compiler_params=_cp("arbitrary"),
    )(dest3, dest3, gate, x, mod, ys)


def _moe(x, h, mod, pw, l, dims, rows):
    idx, gate, rank, cnt, hp = _router(h, pw["moe_wr"], pw["moe_br"], l, rows)
    blk = MOE_BLOCK
    n_blocks = -(-(rows * TOP_K) // blk) + N_EXPERTS + 1
    counts = cnt[0, :N_EXPERTS].astype(I32)
    padded = (counts + blk - 1) // blk * blk
    pad_end = jnp.cumsum(padded)
    pad_start = pad_end - padded
    top_i = idx[:, :TOP_K]
    start_of = jnp.sum(jnp.where(top_i[..., None] == jnp.arange(N_EXPERTS, dtype=I32), pad_start, 0), axis=-1)
    dest = start_of + rank[:, :TOP_K]
    tok = jnp.broadcast_to(jnp.arange(rows, dtype=I32)[:, None], dest.shape)
    fill = jnp.arange((n_blocks + 1) * blk, dtype=I32) % rows
    slot_tok = fill.at[dest.reshape(-1)].set(tok.reshape(-1))
    block_start = jnp.arange(n_blocks, dtype=I32) * blk
    block_e = jnp.minimum(jnp.sum((pad_end[None, :] <= block_start[:, None]).astype(I32), axis=1), N_EXPERTS - 1)
    first = jnp.concatenate([jnp.ones((1,), I32), (block_e[1:] != block_e[:-1]).astype(I32)])
    n_used = (pad_end[-1:] // blk).astype(I32)
    ys = _moe_experts(hp, slot_tok.reshape(n_blocks + 1, 1, blk), block_e, first, n_used, pw["moe_w_gu"],
                      pw["moe_b_gu"], pw["moe_w_dn"], pw["moe_b_dn"], l)
    dest3 = dest.reshape(rows // MOE_COMBINE_ROWS, 1, MOE_COMBINE_ROWS * TOP_K)
    dest3 = jnp.concatenate([dest3, jnp.zeros_like(dest3[:1])], axis=0)
    return _moe_combine(x, ys, dest3, gate, mod, dims, rows)


def _prep_params(p, dims):
    n_l = p["w_in"].shape[0]
    w = dims.W
    o_mla = 3 * w + (3 * w + 384)
    o_s5 = o_mla + MLA_Q_RANK + MLA_KV_RANK + MLA_ROPE
    pw = dict(p)
    row3 = lambda a: a.reshape(a.shape[0], 1, -1)
    pw["norm1_g"], pw["norm2_g"] = row3(p["norm1_g"]), row3(p["norm2_g"])
    pw["b_mod"], pw["b_gate"] = row3(p["b_mod"]), row3(p["b_gate"])
    pw["w_in_mla"] = p["w_in"][:, :, o_mla:o_s5]
    pw["w_in_s5"] = p["w_in"][:, :, o_s5:]
    zeros = jnp.zeros_like(p["rw_w2"][:, 0])
    bdiag = lambda x: jnp.concatenate([jnp.concatenate([x[:, 0], zeros], axis=-1),
                                       jnp.concatenate([zeros, x[:, 1]], axis=-1)], axis=1)
    pw["rw_w2"], pw["rw_a2"] = bdiag(p["rw_w2"]), bdiag(p["rw_a2"])
    pw["rw_w0"], pw["rw_a0"] = row3(p["rw_w0"]), row3(p["rw_a0"])
    pw["rw_kk"], pw["rw_ka"] = row3(p["rw_kk"]), row3(p["rw_ka"])
    pw["rw_rk"], pw["rw_gn_g"], pw["rw_gn_b"] = row3(p["rw_rk"]), row3(p["rw_gn_g"]), row3(p["rw_gn_b"])
    pad_h = lambda x, n: jnp.pad(x, ((0, 0), (0, 0), (0, 0), (0, _LANES - n)))
    wq = p["mla_wq_up"].reshape(n_l, MLA_Q_RANK, MLA_HEADS, MLA_QK)
    pw["mla_wq"] = pad_h(wq, MLA_QK).reshape(n_l, MLA_Q_RANK, -1).astype(BF16)
    wkv = p["mla_wkv_up"].reshape(n_l, MLA_KV_RANK, MLA_HEADS, MLA_NOPE + MLA_V)
    pw["mla_wk"] = pad_h(wkv[..., :MLA_NOPE], MLA_NOPE).reshape(n_l, MLA_KV_RANK, -1).astype(BF16)
    pw["mla_wv"] = wkv[..., MLA_NOPE:].reshape(n_l, MLA_KV_RANK, -1).astype(BF16)
    pw["mla_place"] = jnp.asarray(np.arange(MLA_ROPE)[:, None] + MLA_NOPE == np.arange(_LANES)[None, :], BF16)
    pw["mla_gcq"], pw["mla_gckv"] = row3(p["mla_qn_g"]), row3(p["mla_kvn_g"])
    pad_g = lambda g: jnp.pad(g, ((0, 0), (0, _LANES - MLA_QK))).reshape(n_l, 1, _LANES)
    pw["mla_gq"] = pad_g(p["mla_qkn_q"] * (MLA_QK ** -0.5 * _LOG2E))
    pw["mla_gk"] = pad_g(p["mla_qkn_k"])
    pw["s5_b_glu"] = row3(p["s5_b_glu"])
    pw["moe_wr"] = jnp.pad(p["moe_wr"], ((0, 0), (0, 0), (0, _LANES - N_EXPERTS)))
    pw["moe_br"] = jnp.pad(p["moe_br"], ((0, 0), (0, _LANES - N_EXPERTS)), constant_values=-1e30)[:, None, :]
    pw["moe_b_gu"] = p["moe_b_gu"][:, :, None, :]
    pw["moe_b_dn"] = p["moe_b_dn"][:, :, None, :]
    return pw


def _residual_epilogue(acc, x, g):
    return x + g * acc


def _layer(xs, l, mod, pw, ropes, dims, need_ctx):
    m, d, w = dims.M, dims.D, dims.W
    rows = m if need_ctx else dims.BS
    tm_all = _pick(m, (1088, 1024, 512, 256))
    xn = _norm_mod(xs, pw["norm1_g"], l, mod, 0, 1, dims, m, BF16)
    f_na = _mm(xn, pw["w_in"], l, tm=tm_all, tn=768, col0=0, ncols=3 * w)
    f_rw = _mm(xn, pw["w_in"], l, tm=tm_all, tn=384, col0=3 * w, ncols=3 * w + 384)
    f_mla = _mm(xn, pw["w_in_mla"], l, tm=tm_all, tn=pw["w_in_mla"].shape[2])
    f_s5 = _mm(xn, pw["w_in_s5"], l, tm=tm_all, tn=w, out_dtype=BF16)

    qkv = _na_prep(f_na, pw["na_qn_g"][l], pw["na_kn_g"][l], dims)
    ya = _na_attention(qkv, pw["na_bias"], l, dims, need_ctx)

    r, v, kk, g, lw, kd, bb = _rw_prep(f_rw, pw, l, dims)
    yf, yb_ = _rw_scan(r, v, kk, lw, kd, bb, dims)
    yb = _rw_readout(yf, yb_, r, v, g, kd, pw, l, dims, rows)

    q, k, vv = _mla_prep(f_mla, pw, l, ropes, dims)
    yc = _mla_attention(q, k, vv, dims, need_ctx)

    yd = _s5_mixer(f_s5, pw["s5_mats"], pw["s5_w_glu"], pw["s5_b_glu"], l, dims, rows)

    merged = _merge(xn, (ya, yb, yc, yd), pw["w_gate"], pw["b_gate"], pw["w_branch"], l, rows)
    tm_seg = _pick(dims.S, (512, 256))
    seg = dims.seg_fn(tm_seg)
    tn = 1024
    xs = _mm(merged, pw["w_out"], l, tm=tm_seg, tn=tn, rows=rows, epilogue=_residual_epilogue,
             extras=((xs, pl.BlockSpec((tm_seg, tn), lambda j, i: (i, j))),
                     (mod, pl.BlockSpec((None, 1, tn), lambda j, i: (seg(i), 0, 2 * (d // tn) + j)))))
    h = _norm_mod(xs, pw["norm2_g"], l, mod, 3, 4, dims, rows, F32)
    return _moe(xs, h, mod, pw, l, dims, rows)


def kernel(x, c, ctx, c_ctx, w_mod, b_mod, norm1_g, norm2_g, w_in, w_gate, b_gate, w_branch, w_out, na_qn_g, na_kn_g, na_rpb, rw_mu, rw_w0, rw_w2, rw_a0, rw_a2, rw_g2, rw_kk, rw_ka, rw_rk, rw_gn_g, rw_gn_b, mla_qn_g, mla_wq_up, mla_kvn_g, mla_wkv_up, mla_qkn_q, mla_qkn_k, s5_a_re, s5_a_im, s5_log_dt, s5_b_re, s5_b_im, s5_c_re, s5_c_im, s5_d, s5_w_glu, s5_b_glu, moe_wr, moe_br, moe_w_gu, moe_b_gu, moe_w_dn, moe_b_dn):
    params = dict(w_mod=w_mod, b_mod=b_mod, norm1_g=norm1_g, norm2_g=norm2_g, w_in=w_in, w_gate=w_gate,
                  b_gate=b_gate, w_branch=w_branch, w_out=w_out, na_qn_g=na_qn_g, na_kn_g=na_kn_g,
                  na_rpb=na_rpb, rw_mu=rw_mu, rw_w0=rw_w0, rw_w2=rw_w2, rw_a0=rw_a0, rw_a2=rw_a2,
                  rw_g2=rw_g2, rw_kk=rw_kk, rw_ka=rw_ka, rw_rk=rw_rk, rw_gn_g=rw_gn_g, rw_gn_b=rw_gn_b,
                  mla_qn_g=mla_qn_g, mla_wq_up=mla_wq_up, mla_kvn_g=mla_kvn_g, mla_wkv_up=mla_wkv_up,
                  mla_qkn_q=mla_qkn_q, mla_qkn_k=mla_qkn_k, s5_a_re=s5_a_re, s5_a_im=s5_a_im,
                  s5_log_dt=s5_log_dt, s5_b_re=s5_b_re, s5_b_im=s5_b_im, s5_c_re=s5_c_re, s5_c_im=s5_c_im,
                  s5_d=s5_d, s5_w_glu=s5_w_glu, s5_b_glu=s5_b_glu, moe_wr=moe_wr, moe_br=moe_br,
                  moe_w_gu=moe_w_gu, moe_b_gu=moe_b_gu, moe_w_dn=moe_w_dn, moe_b_dn=moe_b_dn)
    b, s, d = x.shape
    c_len = ctx.shape[1]
    depth = w_mod.shape[0]
    dims = _Dims(b, s, c_len, d)
    assert b + 1 <= 8
    pw = _prep_params(params, dims)
    pw["na_bias"] = jax.vmap(lambda rpb: _na_bias_tables(rpb, s // GRID_W))(na_rpb)
    pw["s5_mats"] = jax.vmap(_s5_matrices)(s5_a_re, s5_a_im, s5_log_dt, s5_b_re, s5_b_im, s5_c_re, s5_c_im, s5_d)
    cond = jnp.concatenate([jax.nn.silu(c), jax.nn.silu(c_ctx)[None],
                            jnp.zeros((8 - b - 1, d), F32)], axis=0)
    ropes = _rope_tables(s, 256)
    xs = jnp.concatenate([x.reshape(b * s, d), ctx.reshape(b * c_len, d)], axis=0)
    for l in range(depth):
        mod = _mm(cond, pw["w_mod"], l, tm=8, tn=1024, bias=pw["b_mod"]).reshape(8, 1, 6 * d)
        xs = _layer(xs, l, mod, pw, ropes, dims, need_ctx=l < depth - 1)
    return xs[:b * s].reshape(b, s, d)
```

```python
import functools

import numpy as np
import jax
import jax.numpy as jnp
from jax import lax
from jax.experimental import pallas as pl
from jax.experimental.pallas import tpu as pltpu

F32 = jnp.float32
BF16 = jnp.bfloat16
I32 = jnp.int32

_VMEM_LIMIT_BYTES = 56 * 1024 * 1024
_LANES = 128
_LOG2E = 1.4426950408889634

NORM_EPS = 1e-6
GRID_W = 64
NA_HEADS = 8
NA_HEAD_DIM = 64
NA_WIN_R = 8
NA_WIN_C = 16
NA_QROWS = 4
NA_BAND_ROWS = NA_QROWS + NA_WIN_R - 1
NA_HEADS_TOGETHER = 2
RW_HEAD_DIM = 64
RW_GN_EPS = 64e-5
RW_CHUNK = 64
RW_PAIRS_PER_STEP = 4
MLA_HEADS = 8
MLA_NOPE = 64
MLA_ROPE = 32
MLA_QK = MLA_NOPE + MLA_ROPE
MLA_V = 64
MLA_Q_RANK = 512
MLA_KV_RANK = 256
ROPE_THETA = 10000.0
MLA_KEY_CHUNK = 256
MLA_Q_TILE = 1024
S5_GROUP = 16
S5_STATE = 64
S5_CHUNK = 16
N_EXPERTS = 32
TOP_K = 4
D_EXPERT = 512
SWIGLU_LIMIT = 7.0
SWIGLU_ALPHA = 1.702
MOE_BLOCK = 256
MOE_COMBINE_ROWS = 128

_NT = (((1,), (1,)), ((), ()))
_TN = (((0,), (0,)), ((), ()))


def _cp(*sem):
    return pltpu.CompilerParams(dimension_semantics=sem, vmem_limit_bytes=_VMEM_LIMIT_BYTES)


def _dot(a, b):
    return jnp.dot(a.astype(BF16), b.astype(BF16), preferred_element_type=F32)


def _dot_nt(a, b):
    return lax.dot_general(a.astype(BF16), b.astype(BF16), _NT, preferred_element_type=F32)


def _dot_tn(a, b):
    return lax.dot_general(a.astype(BF16), b.astype(BF16), _TN, preferred_element_type=F32)


def _split(x, parts):
    out = []
    for _ in range(parts):
        h = x.astype(BF16)
        out.append(h)
        x = x - h.astype(F32)
    return out


def _dot01_right(x, m01, parts=2):
    return sum(jnp.dot(h, m01, preferred_element_type=F32) for h in _split(x, parts))


def _dot01_left(m01, x, parts=3):
    return sum(jnp.dot(m01, h, preferred_element_type=F32) for h in _split(x, parts))


def _pack_pairs(x):
    n = x.shape[1] // 2
    bits = pltpu.bitcast(x.astype(BF16).astype(F32), jnp.uint32)
    return (bits[:, :n] >> 16) | bits[:, n:]


def _unpack_pairs(w):
    return pltpu.bitcast(w << 16, F32), pltpu.bitcast(w & jnp.uint32(0xFFFF0000), F32)


def _sigmoid(x):
    return 1.0 / (1.0 + jnp.exp(-x))


def _softplus(x):
    return jnp.maximum(x, 0.0) + jnp.log(1.0 + jnp.exp(-jnp.abs(x)))


def _pick(n, cands):
    for c in cands:
        if n % c == 0:
            return c
    raise ValueError(f"no tile for {n} in {cands}")


def _head_blockdiag(width, head):
    i = np.arange(width)
    return jnp.asarray((i[:, None] // head) == (i[None, :] // head), BF16)


def _mm_kernel(*refs, has_bias, n_extra, epilogue):
    a_ref, w_ref = refs[0], refs[1]
    pos = 2
    bias_ref = None
    if has_bias:
        bias_ref = refs[pos]
        pos += 1
    extra = refs[pos:pos + n_extra]
    o_ref, wbf_ref = refs[pos + n_extra], refs[pos + n_extra + 1]

    @pl.when(pl.program_id(1) == 0)
    def _():
        wbf_ref[...] = w_ref[...].astype(BF16)

    acc = jnp.dot(a_ref[...].astype(BF16), wbf_ref[...], preferred_element_type=F32)
    if has_bias:
        acc = acc + bias_ref[...]
    if epilogue is not None:
        acc = epilogue(acc, *[e[...] for e in extra])
    o_ref[...] = acc.astype(o_ref.dtype)


def _mm(a, w, l, *, tm, tn, col0=0, ncols=None, rows=None, bias=None, extras=(), epilogue=None,
        out_dtype=F32):
    k = a.shape[1]
    rows = a.shape[0] if rows is None else rows
    ncols = w.shape[2] - col0 if ncols is None else ncols
    assert col0 % tn == 0 and ncols % tn == 0 and rows % tm == 0 and w.shape[1] == k
    c0 = col0 // tn
    in_specs = [pl.BlockSpec((tm, k), lambda j, i: (i, 0)),
                pl.BlockSpec((None, k, tn), lambda j, i: (l, 0, j + c0))]
    args = [a, w]
    if bias is not None:
        in_specs.append(pl.BlockSpec((None, 1, tn), lambda j, i: (l, 0, j + c0)))
        args.append(bias)
    for arr, spec in extras:
        in_specs.append(spec)
        args.append(arr)
    return pl.pallas_call(
        functools.partial(_mm_kernel, has_bias=bias is not None, n_extra=len(extras), epilogue=epilogue),
        grid=(ncols // tn, rows // tm),
        in_specs=in_specs,
        out_specs=pl.BlockSpec((tm, tn), lambda j, i: (i, j)),
        out_shape=jax.ShapeDtypeStruct((rows, ncols), out_dtype),
        scratch_shapes=[pltpu.VMEM((k, tn), BF16)],
        compiler_params=_cp("arbitrary", "arbitrary"),
    )(*args)


def _norm_mod_kernel(x_ref, g_ref, sc_ref, sh_ref, o_ref):
    x = x_ref[...]
    y = x * lax.rsqrt(jnp.mean(x * x, axis=-1, keepdims=True) + NORM_EPS) * g_ref[...]
    o_ref[...] = (y * (1.0 + sc_ref[...]) + sh_ref[...]).astype(o_ref.dtype)


def _norm_mod(x, gain3, l, mod, which_sh, which_sc, dims, rows, out_dtype):
    d = x.shape[1]
    tm = 256
    seg = dims.seg_fn(tm)
    return pl.pallas_call(
        _norm_mod_kernel,
        grid=(rows // tm,),
        in_specs=[pl.BlockSpec((tm, d), lambda i: (i, 0)),
                  pl.BlockSpec((None, 1, d), lambda i: (l, 0, 0)),
                  pl.BlockSpec((None, 1, d), lambda i: (seg(i), 0, which_sc)),
                  pl.BlockSpec((None, 1, d), lambda i: (seg(i), 0, which_sh))],
        out_specs=pl.BlockSpec((tm, d), lambda i: (i, 0)),
        out_shape=jax.ShapeDtypeStruct((rows, d), out_dtype),
        compiler_params=_cp("parallel"),
    )(x, gain3, mod, mod)


class _Dims:
    def __init__(self, b, s, c, d):
        self.B, self.S, self.C, self.D = b, s, c, d
        self.BS = b * s
        self.M = b * s + b * c
        self.W = d // 4

    def seg_fn(self, tm):
        assert self.S % tm == 0 and self.M % tm == 0
        s, b = self.S, self.B
        return lambda i: jnp.minimum((i * tm) // s, b)


def _attend(qs, segs_of):
    n = len(qs)
    m, l, acc = [None] * n, [None] * n, [None] * n
    for j in range(len(segs_of[0])):
        for i in range(n):
            k, v, bias = segs_of[i][j]
            s = lax.dot_general(qs[i], k, _NT, preferred_element_type=F32)
            if bias is not None:
                s = s + bias
            smax = jnp.max(s, axis=-1, keepdims=True)
            if j == 0:
                m[i] = smax
                p = jnp.exp2(s - smax)
                l[i] = jnp.sum(p, axis=-1, keepdims=True)
                acc[i] = jnp.dot(p.astype(BF16), v, preferred_element_type=F32)
            else:
                m_new = jnp.maximum(m[i], smax)
                alpha = jnp.exp2(m[i] - m_new)
                p = jnp.exp2(s - m_new)
                l[i] = alpha * l[i] + jnp.sum(p, axis=-1, keepdims=True)
                acc[i] = alpha * acc[i] + jnp.dot(p.astype(BF16), v, preferred_element_type=F32)
                m[i] = m_new
    return [acc[i] * (1.0 / l[i]) for i in range(n)]


def _half_masks(dtype):
    lane = lax.broadcasted_iota(I32, (1, _LANES), 1)
    lo = lane < 64
    return lo, jnp.where(lo, 1.0, 0.0).astype(dtype), jnp.where(lo, 0.0, 1.0).astype(dtype)


def _na_prep_kernel(f_ref, gq_ref, gk_ref, bd_ref, o_ref):
    w = gq_ref.shape[1]
    bd = bd_ref[...]

    def head_norm(x, g):
        ss = _dot01_right(x * x, bd)
        return x * lax.rsqrt(ss * (1.0 / NA_HEAD_DIM) + NORM_EPS) * g

    o_ref[:, 0:w] = head_norm(f_ref[:, 0:w], gq_ref[...]).astype(o_ref.dtype)
    o_ref[:, w:2 * w] = head_norm(f_ref[:, w:2 * w], gk_ref[...]).astype(o_ref.dtype)
    o_ref[:, 2 * w:3 * w] = f_ref[:, 2 * w:3 * w].astype(o_ref.dtype)


def _na_prep(f_na, qn_g, kn_g, dims):
    m, w3 = f_na.shape
    w = w3 // 3
    tm = 256
    gq = (jnp.tile(qn_g, NA_HEADS) * (NA_HEAD_DIM ** -0.5 * _LOG2E)).reshape(1, w)
    gk = jnp.tile(kn_g, NA_HEADS).reshape(1, w)
    return pl.pallas_call(
        _na_prep_kernel,
        grid=(m // tm,),
        in_specs=[pl.BlockSpec((tm, w3), lambda i: (i, 0)),
                  pl.BlockSpec((1, w), lambda i: (0, 0)),
                  pl.BlockSpec((1, w), lambda i: (0, 0)),
                  pl.BlockSpec((w, w), lambda i: (0, 0))],
        out_specs=pl.BlockSpec((tm, w3), lambda i: (i, 0)),
        out_shape=jax.ShapeDtypeStruct((m, w3), BF16),
        compiler_params=_cp("parallel"),
    )(f_na, gq, gk, _head_blockdiag(w, NA_HEAD_DIM))


def _na_bias_tables(rpb, rows):
    n_tiles = rows // NA_QROWS
    assert rows % NA_QROWS == 0 and rows >= NA_BAND_ROWS + NA_QROWS and NA_QROWS * 2 <= NA_WIN_R
    tabs = []
    for g in (0, 1, n_tiles - 1):
        r = g * NA_QROWS + np.arange(NA_QROWS)
        band0 = int(np.clip(g * NA_QROWS - NA_WIN_R // 2, 0, rows - NA_BAND_ROWS))
        key_row = band0 + np.arange(NA_BAND_ROWS)
        row_start = np.clip(r - NA_WIN_R // 2, 0, rows - NA_WIN_R)
        valid_r = (key_row[None, :] >= row_start[:, None]) & (key_row[None, :] < row_start[:, None] + NA_WIN_R)
        d_row = np.clip(key_row[None, :] - r[:, None] + NA_WIN_R - 1, 0, 2 * NA_WIN_R - 2)
        qc = np.arange(GRID_W)
        col_start = np.clip(qc - NA_WIN_C // 2, 0, GRID_W - NA_WIN_C)
        kc = np.arange(GRID_W)
        valid_c = (kc[None, :] >= col_start[:, None]) & (kc[None, :] < col_start[:, None] + NA_WIN_C)
        d_col = np.clip(kc[None, :] - qc[:, None] + NA_WIN_C - 1, 0, 2 * NA_WIN_C - 2)
        oh_r = jnp.asarray(d_row[:, :, None] == np.arange(2 * NA_WIN_R - 1), F32)
        oh_c = jnp.asarray(d_col[:, :, None] == np.arange(2 * NA_WIN_C - 1), F32)
        b = jnp.einsum('ajr,hrc->hajc', oh_r, rpb, precision=lax.Precision.HIGHEST)
        b = jnp.einsum('hajc,qkc->haqjk', b, oh_c, precision=lax.Precision.HIGHEST)
        valid = valid_r[:, None, :, None] & valid_c[None, :, None, :]
        b = jnp.where(valid[None], b, -jnp.inf)
        tabs.append(b.reshape(NA_HEADS, NA_QROWS * GRID_W, NA_BAND_ROWS * GRID_W))
    return (jnp.stack(tabs) * _LOG2E).astype(BF16)


def _na_attn_kernel(q_ref, k_ref, v_ref, kc_ref, vc_ref, bias_ref, o_ref, *, rows):
    g = pl.program_id(1)
    band0 = jnp.clip(g * NA_QROWS - NA_WIN_R // 2, 0, rows - NA_BAND_ROWS)
    off = pl.multiple_of(band0 * GRID_W, GRID_W)
    band = NA_BAND_ROWS * GRID_W
    lo, m_e, m_o = _half_masks(BF16)
    qs, segs = [], []
    for p in range(q_ref.shape[1] // _LANES):
        sl = slice(p * _LANES, (p + 1) * _LANES)
        qp = q_ref[:, sl]
        kb = k_ref[pl.ds(off, band), sl]
        vb = v_ref[pl.ds(off, band), sl]
        for half, msk in enumerate((m_e, m_o)):
            qs.append(qp * msk)
            segs.append([(kb, vb, bias_ref[2 * p + half].astype(F32)), (kc_ref[:, sl], vc_ref[:, sl], None)])
    outs = []
    for i in range(0, len(qs), NA_HEADS_TOGETHER):
        outs += _attend(qs[i:i + NA_HEADS_TOGETHER], segs[i:i + NA_HEADS_TOGETHER])
    for p in range(q_ref.shape[1] // _LANES):
        o_ref[:, p * _LANES:(p + 1) * _LANES] = jnp.where(lo, outs[2 * p], outs[2 * p + 1]).astype(o_ref.dtype)


def _na_ctx_kernel(q_ref, kc_ref, vc_ref, o_ref):
    lo, m_e, m_o = _half_masks(BF16)
    qs, segs = [], []
    for p in range(q_ref.shape[1] // _LANES):
        sl = slice(p * _LANES, (p + 1) * _LANES)
        for msk in (m_e, m_o):
            qs.append(q_ref[:, sl] * msk)
            segs.append([(kc_ref[:, sl], vc_ref[:, sl], None)])
    outs = _attend(qs, segs)
    for p in range(q_ref.shape[1] // _LANES):
        o_ref[:, p * _LANES:(p + 1) * _LANES] = jnp.where(lo, outs[2 * p], outs[2 * p + 1]).astype(o_ref.dtype)


def _na_attention(qkv, bias, l, dims, need_ctx):
    b, s, c, w = dims.B, dims.S, dims.C, dims.W
    rows = s // GRID_W
    n_tiles = rows // NA_QROWS
    tq = NA_QROWS * GRID_W
    band = NA_BAND_ROWS * GRID_W
    ctx_blk = dims.BS // c

    def cls(g):
        return (g > 0).astype(I32) + (g == n_tiles - 1).astype(I32)

    ya = pl.pallas_call(
        functools.partial(_na_attn_kernel, rows=rows),
        grid=(b, n_tiles),
        in_specs=[pl.BlockSpec((tq, w), lambda bi, g: (bi * n_tiles + g, 0)),
                  pl.BlockSpec((s, w), lambda bi, g: (bi, 1)),
                  pl.BlockSpec((s, w), lambda bi, g: (bi, 2)),
                  pl.BlockSpec((c, w), lambda bi, g: (ctx_blk + bi, 1)),
                  pl.BlockSpec((c, w), lambda bi, g: (ctx_blk + bi, 2)),
                  pl.BlockSpec((None, None, NA_HEADS, tq, band), lambda bi, g: (l, cls(g), 0, 0, 0))],
        out_specs=pl.BlockSpec((tq, w), lambda bi, g: (bi * n_tiles + g, 0)),
        out_shape=jax.ShapeDtypeStruct((dims.BS, w), BF16),
        compiler_params=_cp("parallel", "arbitrary"),
    )(qkv, qkv, qkv, qkv, qkv, bias)
    if not need_ctx:
        return ya
    yc = pl.pallas_call(
        _na_ctx_kernel,
        grid=(b,),
        in_specs=[pl.BlockSpec((c, w), lambda bi: (ctx_blk + bi, 0)),
                  pl.BlockSpec((c, w), lambda bi: (ctx_blk + bi, 1)),
                  pl.BlockSpec((c, w), lambda bi: (ctx_blk + bi, 2))],
        out_specs=pl.BlockSpec((c, w), lambda bi: (bi, 0)),
        out_shape=jax.ShapeDtypeStruct((b * c, w), BF16),
        compiler_params=_cp("parallel"),
    )(qkv, qkv, qkv)
    return jnp.concatenate([ya, yc], axis=0)


def _rope_tables(s, tm):
    n_freq = MLA_ROPE // 4
    inv_freq = ROPE_THETA ** (-jnp.arange(n_freq, dtype=F32) / n_freq)
    t = jnp.arange(s)
    row = (t // GRID_W).astype(F32)[:, None] * inv_freq
    col = (t % GRID_W).astype(F32)[:, None] * inv_freq
    zeros = jnp.zeros((s, n_freq), F32)

    def slab(row_x1, row_x2, col_x1, col_x2, fill):
        body = jnp.concatenate([row_x1, row_x2, col_x1, col_x2], axis=-1)
        return jnp.concatenate([jnp.full((s, MLA_NOPE), fill, F32), body,
                                jnp.zeros((s, _LANES - MLA_QK), F32)], axis=-1)

    cr, sr, cc, sc = jnp.cos(row), jnp.sin(row), jnp.cos(col), jnp.sin(col)
    c_tab = slab(cr, cr, cc, cc, 1.0)
    s1_tab = slab(-sr, zeros, -sc, zeros, 0.0)
    s2_tab = slab(zeros, sr, zeros, sc, 0.0)
    ident = jnp.concatenate([jnp.ones((tm, MLA_QK), F32), jnp.zeros((tm, _LANES - MLA_QK), F32)], axis=-1)
    zero = jnp.zeros((tm, _LANES), F32)
    return (jnp.concatenate([c_tab, ident]), jnp.concatenate([s1_tab, zero]),
            jnp.concatenate([s2_tab, zero]))


def _mla_prep_kernel(f_ref, wq_ref, wk_ref, wv_ref, place_ref, gcq_ref, gckv_ref, gq_ref, gk_ref,
                     ct_ref, s1_ref, s2_ref, q_o, k_o, v_o):
    cq = f_ref[:, 0:MLA_Q_RANK]
    ckv = f_ref[:, MLA_Q_RANK:MLA_Q_RANK + MLA_KV_RANK]
    kr = f_ref[:, MLA_Q_RANK + MLA_KV_RANK:MLA_Q_RANK + MLA_KV_RANK + MLA_ROPE]

    def rms(x, g):
        return x * lax.rsqrt(jnp.mean(x * x, axis=-1, keepdims=True) + NORM_EPS) * g

    cqn = rms(cq, gcq_ref[...]).astype(BF16)
    ckvn = rms(ckv, gckv_ref[...]).astype(BF16)
    q = jnp.dot(cqn, wq_ref[...], preferred_element_type=F32)
    kn = jnp.dot(ckvn, wk_ref[...], preferred_element_type=F32)
    v_o[...] = jnp.dot(ckvn, wv_ref[...], preferred_element_type=F32).astype(v_o.dtype)
    kr_slab = _dot01_right(kr, place_ref[...])
    ct, s1, s2 = ct_ref[...], s1_ref[...], s2_ref[...]

    def head(x, g):
        x = x * lax.rsqrt(jnp.sum(x * x, axis=-1, keepdims=True) * (1.0 / MLA_QK) + NORM_EPS) * g
        return x * ct + pltpu.roll(x, _LANES - MLA_ROPE // 4, 1) * s1 + pltpu.roll(x, MLA_ROPE // 4, 1) * s2

    for h in range(MLA_HEADS):
        sl = slice(h * _LANES, (h + 1) * _LANES)
        q_o[h] = head(q[:, sl], gq_ref[...]).astype(q_o.dtype)
        k_o[h] = head(kn[:, sl] + kr_slab, gk_ref[...]).astype(k_o.dtype)


def _mla_prep(f_mla, pw, l, ropes, dims):
    m = f_mla.shape[0]
    tm = 256
    n_lat = dims.S // tm
    bs = dims.BS

    def rope_blk(i):
        return jnp.where(i * tm < bs, i % n_lat, n_lat)

    hw = MLA_HEADS * _LANES
    full = lambda shape: pl.BlockSpec(shape, lambda i: tuple(0 for _ in shape))
    lyr = lambda shape: pl.BlockSpec((None,) + shape, lambda i: (l,) + tuple(0 for _ in shape))
    return pl.pallas_call(
        _mla_prep_kernel,
        grid=(m // tm,),
        in_specs=[pl.BlockSpec((tm, f_mla.shape[1]), lambda i: (i, 0)),
                  lyr((MLA_Q_RANK, hw)), lyr((MLA_KV_RANK, hw)), lyr((MLA_KV_RANK, MLA_HEADS * MLA_V)),
                  full((MLA_ROPE, _LANES)),
                  lyr((1, MLA_Q_RANK)), lyr((1, MLA_KV_RANK)), lyr((1, _LANES)), lyr((1, _LANES)),
                  pl.BlockSpec((tm, _LANES), lambda i: (rope_blk(i), 0)),
                  pl.BlockSpec((tm, _LANES), lambda i: (rope_blk(i), 0)),
                  pl.BlockSpec((tm, _LANES), lambda i: (rope_blk(i), 0))],
        out_specs=[pl.BlockSpec((MLA_HEADS, tm, _LANES), lambda i: (0, i, 0)),
                   pl.BlockSpec((MLA_HEADS, tm, _LANES), lambda i: (0, i, 0)),
                   pl.BlockSpec((tm, MLA_HEADS * MLA_V), lambda i: (i, 0))],
        out_shape=[jax.ShapeDtypeStruct((MLA_HEADS, m, _LANES), BF16),
                   jax.ShapeDtypeStruct((MLA_HEADS, m, _LANES), BF16),
                   jax.ShapeDtypeStruct((m, MLA_HEADS * MLA_V), BF16)],
        compiler_params=_cp("parallel"),
    )(f_mla, pw["mla_wq"], pw["mla_wk"], pw["mla_wv"], pw["mla_place"], pw["mla_gcq"], pw["mla_gckv"],
      pw["mla_gq"], pw["mla_gk"], *ropes)


def _mla_attn_kernel(qe_ref, qo_ref, kce_ref, kco_ref, kle_ref, klo_ref, vc_ref, vl_ref, o_ref):
    lo, _, _ = _half_masks(F32)
    n_chunks = kle_ref.shape[0] // MLA_KEY_CHUNK

    def segs(kc_ref, kl_ref):
        out = [(kc_ref[...], vc_ref[...], None)]
        for j in range(n_chunks):
            sl = slice(j * MLA_KEY_CHUNK, (j + 1) * MLA_KEY_CHUNK)
            out.append((kl_ref[sl, :], vl_ref[sl, :], None))
        return out

    o_e, o_o = _attend([qe_ref[...], qo_ref[...]], [segs(kce_ref, kle_ref), segs(kco_ref, klo_ref)])
    o_ref[...] = jnp.where(lo, o_e, o_o).astype(o_ref.dtype)


def _mla_ctx_kernel(qe_ref, qo_ref, kce_ref, kco_ref, vc_ref, o_ref):
    lo, _, _ = _half_masks(F32)
    o_e, o_o = _attend([qe_ref[...], qo_ref[...]],
                       [[(kce_ref[...], vc_ref[...], None)], [(kco_ref[...], vc_ref[...], None)]])
    o_ref[...] = jnp.where(lo, o_e, o_o).astype(o_ref.dtype)


def _mla_attention(q, k, v, dims, need_ctx):
    b, s, c, w = dims.B, dims.S, dims.C, dims.W
    tq = MLA_Q_TILE
    nq = s // tq
    ctx_blk = dims.BS // c
    pairs = MLA_HEADS // 2
    hq = lambda half: pl.BlockSpec((None, tq, _LANES), lambda bi, p, i: (2 * p + half, bi * nq + i, 0))
    hkc = lambda half: pl.BlockSpec((None, c, _LANES), lambda bi, p, i: (2 * p + half, ctx_blk + bi, 0))
    hkl = lambda half: pl.BlockSpec((None, s, _LANES), lambda bi, p, i: (2 * p + half, bi, 0))
    yc_x = pl.pallas_call(
        _mla_attn_kernel,
        grid=(b, pairs, nq),
        in_specs=[hq(0), hq(1), hkc(0), hkc(1), hkl(0), hkl(1),
                  pl.BlockSpec((c, _LANES), lambda bi, p, i: (ctx_blk + bi, p)),
                  pl.BlockSpec((s, _LANES), lambda bi, p, i: (bi, p))],
        out_specs=pl.BlockSpec((tq, _LANES), lambda bi, p, i: (bi * nq + i, p)),
        out_shape=jax.ShapeDtypeStruct((dims.BS, w), BF16),
        compiler_params=_cp("parallel", "parallel", "arbitrary"),
    )(q, q, k, k, k, k, v, v)
    if not need_ctx:
        return yc_x
    cq = lambda half: pl.BlockSpec((None, c, _LANES), lambda bi, p: (2 * p + half, ctx_blk + bi, 0))
    yc_s = pl.pallas_call(
        _mla_ctx_kernel,
        grid=(b, pairs),
        in_specs=[cq(0), cq(1), cq(0), cq(1),
                  pl.BlockSpec((c, _LANES), lambda bi, p: (ctx_blk + bi, p))],
        out_specs=pl.BlockSpec((c, _LANES), lambda bi, p: (bi, p)),
        out_shape=jax.ShapeDtypeStruct((b * c, w), BF16),
        compiler_params=_cp("parallel", "parallel"),
    )(q, q, k, k, v)
    return jnp.concatenate([yc_x, yc_s], axis=0)


def _rw_prep_kernel(f_ref, fp_ref, fn_ref, mu_ref, w0_ref, w2_ref, a0_ref, a2_ref, g2_ref, kkg_ref,
                    ka_ref, bd_ref, r_o, v_o, kk_o, g_o, lw_o, kd_o, bb_o, *, s, c, bs, tm):
    w = r_o.shape[1]
    row0 = pl.program_id(0) * tm
    is_lat = row0 < bs
    pos = jnp.where(is_lat, row0 % s, (row0 - bs) % c)
    seglen = jnp.where(is_lat, s, c)
    f = f_ref[...]
    prev_row = jnp.where(pos == 0, 0.0, fp_ref[7:8, :])
    next_row = jnp.where(pos + tm == seglen, 0.0, fn_ref[0:1, :])
    ridx = lax.broadcasted_iota(I32, (tm, 1), 0)
    prev = jnp.where(ridx == 0, prev_row, pltpu.roll(f, 1, 0))
    nxt = jnp.where(ridx == tm - 1, next_row, pltpu.roll(f, tm - 1, 0))
    fm = f + mu_ref[0:1, :] * (prev - f) + mu_ref[1:2, :] * (nxt - f)
    r, k, v = fm[:, 0:w], fm[:, w:2 * w], fm[:, 2 * w:3 * w]
    o = 3 * w
    w_low = jnp.tanh(fm[:, o:o + 128])
    a_low = fm[:, o + 128:o + 256]
    g_low = _sigmoid(fm[:, o + 256:o + 384])
    log_w = -_softplus(-(w0_ref[...] + _dot(w_low, w2_ref[...]))) - 0.5
    lw = -jnp.exp(log_w)
    a = _sigmoid(a0_ref[...] + _dot(a_low, a2_ref[...]))
    g_o[...] = _dot(g_low, g2_ref[...])
    kk = k * kkg_ref[...]
    kk = kk / jnp.maximum(jnp.sqrt(_dot01_right(kk * kk, bd_ref[...])), 1e-12)
    r_o[...] = r
    v_o[...] = v
    kk_o[...] = kk
    for d in range(2):
        ad = a[:, d * w:(d + 1) * w]
        lw_o[d] = lw[:, d * w:(d + 1) * w]
        kd_o[d] = k * (1.0 + (ad - 1.0) * ka_ref[...])
        bb_o[d] = ad * kk


def _rw_prep(f_rw, pw, l, dims):
    m, cols = f_rw.shape
    w = dims.W
    tm = 256
    assert dims.C % tm == 0 and dims.S % tm == 0
    nb8 = m // 8
    lyr = lambda shape: pl.BlockSpec((None,) + shape, lambda i: (l,) + tuple(0 for _ in shape))
    out_w = pl.BlockSpec((tm, w), lambda i: (i, 0))
    out_2w = pl.BlockSpec((2, tm, w), lambda i: (0, i, 0))
    sd = jax.ShapeDtypeStruct
    return pl.pallas_call(
        functools.partial(_rw_prep_kernel, s=dims.S, c=dims.C, bs=dims.BS, tm=tm),
        grid=(m // tm,),
        in_specs=[pl.BlockSpec((tm, cols), lambda i: (i, 0)),
                  pl.BlockSpec((8, cols), lambda i: (jnp.maximum(i * (tm // 8) - 1, 0), 0)),
                  pl.BlockSpec((8, cols), lambda i: (jnp.minimum((i + 1) * (tm // 8), nb8 - 1), 0)),
                  lyr((2, cols)), lyr((1, 2 * w)), lyr((128, 2 * w)), lyr((1, 2 * w)), lyr((128, 2 * w)),
                  lyr((128, w)), lyr((1, w)), lyr((1, w)),
                  pl.BlockSpec((w, w), lambda i: (0, 0))],
        out_specs=[out_w, out_w, out_w, out_w, out_2w, out_2w, out_2w],
        out_shape=[sd((m, w), F32)] * 4 + [sd((2, m, w), F32)] * 3,
        compiler_params=_cp("parallel"),
    )(f_rw, f_rw, f_rw, pw["rw_mu"], pw["rw_w0"], pw["rw_w2"], pw["rw_a0"], pw["rw_a2"], pw["rw_g2"],
      pw["rw_kk"], pw["rw_ka"], _head_blockdiag(w, RW_HEAD_DIM))


def _rw_chunk_kernel(rf, vf, kkf, lwf, kdf, bbf, rb, vb, kkb, lwb, kdb, bbb, yf_ref, yb_ref, h_ref):
    c = RW_CHUNK

    @pl.when(pl.program_id(2) == 0)
    def _():
        h_ref[...] = jnp.zeros_like(h_ref)

    ii = lax.broadcasted_iota(I32, (c, c), 0)
    jj = lax.broadcasted_iota(I32, (c, c), 1)
    tri_f = jnp.where(jj <= ii, 1.0, 0.0).astype(BF16)
    tri_b = jnp.where(jj >= ii, 1.0, 0.0).astype(BF16)
    _, m_e, m_o = _half_masks(F32)

    n2 = 2 * c
    i2 = lax.broadcasted_iota(I32, (n2, n2), 0)
    j2 = lax.broadcasted_iota(I32, (n2, n2), 1)
    same = (i2 // c) == (j2 // c)
    eye2 = jnp.where(i2 == j2, 1.0, 0.0)
    eye = lax.broadcasted_iota(I32, (_LANES, _LANES), 0) == lax.broadcasted_iota(I32, (_LANES, _LANES), 1)
    steps = int(np.log2(c)) - 1

    def stack(x):
        return jnp.concatenate([x * m_e, x * m_o], axis=0).astype(BF16)

    def feats(r_ref, kk_ref, lw_ref, kd_ref, bb_ref, v_ref, sl, tri, last, lag):
        lw = lw_ref[:, sl]
        cum = _dot01_left(tri, lw)
        tot = cum[last:last + 1, :]
        w_inv = jnp.exp(-cum)
        w_end = jnp.exp(tot - cum)
        bb, kd = bb_ref[:, sl], kd_ref[:, sl]
        lag = jnp.where(same, lag, -1)
        return dict(la=stack(-kk_ref[:, sl] * jnp.exp(cum - lw)), lr=stack(r_ref[:, sl] * jnp.exp(cum)),
                    rb=stack(bb * w_inv), rk=stack(kd * w_inv), vm=stack(v_ref[:, sl]),
                    bh=stack(bb * w_end), kh=stack(kd * w_end), wtot=jnp.exp(tot),
                    strict=lag > 0, incl=lag >= 0)

    sls = [slice(p * _LANES, (p + 1) * _LANES) for p in range(h_ref.shape[0])]
    f = []
    for sl in sls:
        f.append(feats(rf, kkf, lwf, kdf, bbf, vf, sl, tri_f, c - 1, i2 - j2))
        f.append(feats(rb, kkb, lwb, kdb, bbb, vb, sl, tri_b, 0, j2 - i2))
    un = range(len(f))
    nmat = [jnp.where(f[k]["strict"], _dot_nt(f[k]["la"], f[k]["rb"]), 0.0) for k in un]
    aak = [jnp.where(f[k]["strict"], _dot_nt(f[k]["la"], f[k]["rk"]), 0.0) for k in un]
    arb = [jnp.where(f[k]["incl"], _dot_nt(f[k]["lr"], f[k]["rb"]), 0.0) for k in un]
    ark = [jnp.where(f[k]["incl"], _dot_nt(f[k]["lr"], f[k]["rk"]), 0.0) for k in un]
    t_inv = [eye2 + nmat[k] for k in un]
    pw = nmat
    for _ in range(steps):
        pw = [_dot(pw[k], pw[k]) for k in un]
        t_inv = [t_inv[k] + _dot(t_inv[k], pw[k]) for k in un]
    x = [_dot(t_inv[k], f[k]["la"]) for k in un]
    av = [_dot(aak[k], f[k]["vm"]) for k in un]
    u = [_dot(t_inv[k], av[k]) for k in un]
    q = [f[k]["lr"].astype(F32) + _dot(arb[k], x[k]) for k in un]
    y0 = [_dot(arb[k], u[k]) + _dot(ark[k], f[k]["vm"]) for k in un]
    h = [h_ref[k // 2, k % 2] for k in un]
    yst = [_dot(q[k], h[k]) + y0[k] for k in un]
    for k in un:
        out_ref = yb_ref if k % 2 else yf_ref
        out_ref[:, sls[k // 2]] = yst[k][0:c] + yst[k][c:n2]
    trans = [jnp.where(eye, f[k]["wtot"], 0.0) + _dot_tn(f[k]["bh"], x[k]) for k in un]
    for k in un:
        h_ref[k // 2, k % 2] = (_dot(trans[k], h[k]) + _dot_tn(f[k]["bh"], u[k])
                                + _dot_tn(f[k]["kh"], f[k]["vm"]))


def _rw_scan(r, v, kk, lw, kd, bb, dims):
    b, s, c, w = dims.B, dims.S, dims.C, dims.W
    ch = RW_CHUNK
    n_c, n_s = c // ch, s // ch
    ctx0 = dims.BS // ch
    pp = RW_PAIRS_PER_STEP
    bw = pp * _LANES
    assert w % bw == 0

    def cf(bi, t):
        return jnp.where(t < n_c, ctx0 + bi * n_c + t, bi * n_s + (t - n_c))

    def cb(bi, t):
        return jnp.where(t < n_c, ctx0 + bi * n_c + (n_c - 1 - t), bi * n_s + (n_s - 1 - (t - n_c)))

    sh = lambda fn: pl.BlockSpec((ch, bw), lambda bi, p, t: (fn(bi, t), p))
    dr = lambda fn, d: pl.BlockSpec((None, ch, bw), lambda bi, p, t: (d, fn(bi, t), p))
    m = dims.M
    return pl.pallas_call(
        _rw_chunk_kernel,
        grid=(b, w // bw, n_c + n_s),
        in_specs=[sh(cf), sh(cf), sh(cf), dr(cf, 0), dr(cf, 0), dr(cf, 0),
                  sh(cb), sh(cb), sh(cb), dr(cb, 1), dr(cb, 1), dr(cb, 1)],
        out_specs=[sh(cf), sh(cb)],
        out_shape=[jax.ShapeDtypeStruct((m, w), F32)] * 2,
        scratch_shapes=[pltpu.VMEM((pp, 2, _LANES, _LANES), F32)],
        compiler_params=_cp("parallel", "parallel", "arbitrary"),
    )(r, v, kk, lw, kd, bb, r, v, kk, lw, kd, bb)


def _rw_readout_kernel(yf_ref, yb_ref, r_ref, v_ref, g_ref, kd_ref, rk_ref, gg_ref, gb_ref, bd_ref, o_ref):
    bd = bd_ref[...]
    inv = 1.0 / RW_HEAD_DIM
    y = yf_ref[...] + yb_ref[...]
    yc = y - _dot01_right(y, bd) * inv
    var = _dot01_right(yc * yc, bd) * inv
    yn = yc * lax.rsqrt(var + RW_GN_EPS)
    bonus = _dot01_right(r_ref[...] * (kd_ref[0] + kd_ref[1]) * rk_ref[...], bd) * v_ref[...]
    o_ref[...] = ((yn * gg_ref[...] + gb_ref[...] + bonus) * g_ref[...]).astype(o_ref.dtype)


def _rw_readout(yf, yb, r, v, g, kd, pw, l, dims, rows):
    w = dims.W
    tm = 256
    blk = pl.BlockSpec((tm, w), lambda i: (i, 0))
    lyr = pl.BlockSpec((None, 1, w), lambda i: (l, 0, 0))
    return pl.pallas_call(
        _rw_readout_kernel,
        grid=(rows // tm,),
        in_specs=[blk, blk, blk, blk, blk, pl.BlockSpec((2, tm, w), lambda i: (0, i, 0)), lyr, lyr, lyr,
                  pl.BlockSpec((w, w), lambda i: (0, 0))],
        out_specs=blk,
        out_shape=jax.ShapeDtypeStruct((rows, w), BF16),
        compiler_params=_cp("parallel"),
    )(yf, yb, r, v, g, kd, pw["rw_rk"], pw["rw_gn_g"], pw["rw_gn_b"], _head_blockdiag(w, RW_HEAD_DIM))


def _s5_matrices(a_re, a_im, log_dt, b_re, b_im, c_re, c_im, d_skip):
    hi = lax.Precision.HIGHEST
    n_g, n_p, n_j = b_re.shape[1], b_re.shape[2], b_re.shape[3]
    ct = S5_CHUNK
    dt = jnp.exp(log_dt)[..., None]
    mag = jnp.exp(a_re * dt)
    lam_re, lam_im = mag * jnp.cos(a_im * dt), mag * jnp.sin(a_im * dt)
    den = a_re * a_re + a_im * a_im
    f_re = ((lam_re - 1.0) * a_re + lam_im * a_im) / den
    f_im = (lam_im * a_re - (lam_re - 1.0) * a_im) / den
    bb_re = f_re[..., None] * b_re - f_im[..., None] * b_im
    bb_im = f_re[..., None] * b_im + f_im[..., None] * b_re
    p_re, p_im = [jnp.ones_like(lam_re)], [jnp.zeros_like(lam_im)]
    for _ in range(ct):
        p_re.append(p_re[-1] * lam_re - p_im[-1] * lam_im)
        p_im.append(p_re[-2] * lam_im + p_im[-1] * lam_re)
    p_re, p_im = jnp.stack(p_re), jnp.stack(p_im)
    cl_re = c_re[None] * p_re[:, :, :, None, :] - c_im[None] * p_im[:, :, :, None, :]
    cl_im = c_re[None] * p_im[:, :, :, None, :] + c_im[None] * p_re[:, :, :, None, :]
    kern = jnp.sum(cl_re[..., None] * bb_re[None, :, :, None] - cl_im[..., None] * bb_im[None, :, :, None],
                   axis=4)
    j_idx = np.arange(ct)[:, None]
    t_idx = np.arange(ct)[None, :]
    lag_f, lag_b = t_idx - j_idx, j_idx - t_idx
    sel_f = jnp.asarray(lag_f[:, :, None] == np.arange(ct + 1), F32)
    sel_b = jnp.asarray(lag_b[:, :, None] == np.arange(ct + 1), F32)
    toep = (jnp.einsum('jtn,ngim->gjmti', sel_f, kern[:, 0], precision=hi)
            + jnp.einsum('jtn,ngim->gjmti', sel_b, kern[:, 1], precision=hi))
    skip = (jnp.eye(ct)[None, :, None, :, None] * jnp.eye(n_j)[None, None, :, None, :]
            * d_skip.reshape(n_g, 1, 1, 1, n_j))
    toep = (toep + skip).reshape(n_g, ct * n_j, ct * n_j)

    def powers(z, d, first, reverse):
        z = z[first:first + ct, d]
        return z[::-1] if reverse else z

    def state_in(first, reverse, d):
        pr, pi = powers(p_re, d, first, reverse), powers(p_im, d, first, reverse)
        re = pr[..., None] * bb_re[d][None] - pi[..., None] * bb_im[d][None]
        im = pr[..., None] * bb_im[d][None] + pi[..., None] * bb_re[d][None]
        fix = lambda z: z.transpose(1, 0, 3, 2).reshape(n_g, ct * n_j, n_p)
        return fix(re), fix(im)

    def state_out(first, reverse, d):
        fix = lambda z: z.transpose(1, 3, 0, 2).reshape(n_g, n_p, ct * n_j)
        return fix(powers(cl_re, d, first, reverse)), fix(-powers(cl_im, d, first, reverse))

    rf_re, rf_im = state_in(0, True, 0)
    rb_re, rb_im = state_in(0, False, 1)
    of_re, of_im = state_out(1, False, 0)
    ob_re, ob_im = state_out(1, True, 1)
    half = (np.arange(n_g) % 2)[:, None, None]

    def lane_half(z):
        zero = jnp.zeros_like(z)
        return jnp.concatenate([jnp.where(half == 0, z, zero), jnp.where(half == 1, z, zero)], axis=-1)

    def row_half(z):
        zero = jnp.zeros_like(z)
        return jnp.concatenate([jnp.where(half == 0, z, zero), jnp.where(half == 1, z, zero)], axis=1)

    r_re = jnp.stack([lane_half(rf_re), lane_half(rb_re)]).astype(BF16)
    r_im = jnp.stack([lane_half(rf_im), lane_half(rb_im)]).astype(BF16)
    o_re = jnp.stack([row_half(of_re), row_half(ob_re)]).astype(BF16)
    o_im = jnp.stack([row_half(of_im), row_half(ob_im)]).astype(BF16)
    lam_c = (p_re[ct].reshape(2, 1, n_g * n_p), p_im[ct].reshape(2, 1, n_g * n_p))
    return toep.astype(BF16), r_re, r_im, o_re, o_im, lam_c


def _s5_z_kernel(u_ref, rre_ref, rim_ref, zre_o, zim_o):
    u0, u1 = u_ref[0].astype(BF16), u_ref[1].astype(BF16)
    for d in range(2):
        zre_o[d] = (jnp.dot(u0, rre_ref[d, 0], preferred_element_type=F32)
                    + jnp.dot(u1, rre_ref[d, 1], preferred_element_type=F32))
        zim_o[d] = (jnp.dot(u0, rim_ref[d, 0], preferred_element_type=F32)
                    + jnp.dot(u1, rim_ref[d, 1], preferred_element_type=F32))


def _s5_scan_kernel(zre_ref, zim_ref, lre_ref, lim_ref, xre_o, xim_o, *, n_b, n_s, n_c):
    lanes = zre_ref.shape[2]
    chains = [(d, b) for d in range(2) for b in range(n_b)]
    lam = [(lre_ref[d], lim_ref[d]) for d in range(2)]

    def run(base_of, length, carry):
        def body(t, carry):
            new = []
            for (d, b), (xr, xi) in zip(chains, carry):
                row = base_of(b) + (t if d == 0 else length - 1 - t)
                xre_o[d, pl.ds(row, 1), :] = xr
                xim_o[d, pl.ds(row, 1), :] = xi
                lr, li = lam[d]
                new.append((lr * xr - li * xi + zre_ref[d, pl.ds(row, 1), :],
                            lr * xi + li * xr + zim_ref[d, pl.ds(row, 1), :]))
            return tuple(new)
        return lax.fori_loop(0, length, body, carry)

    zero = jnp.zeros((1, lanes), F32)
    carry = tuple((zero, zero) for _ in chains)
    carry = run(lambda b: n_b * n_s + b * n_c, n_c, carry)
    run(lambda b: b * n_s, n_s, carry)


def _s5_y_kernel(u_ref, toep_ref, xre_ref, xim_ref, ore_ref, oim_ref, y_o):
    xr = [xre_ref[d].astype(BF16) for d in range(2)]
    xi = [xim_ref[d].astype(BF16) for d in range(2)]
    for g in range(2):
        y = jnp.dot(u_ref[g].astype(BF16), toep_ref[g], preferred_element_type=F32)
        for d in range(2):
            y = y + jnp.dot(xr[d], ore_ref[d, g], preferred_element_type=F32)
            y = y + jnp.dot(xi[d], oim_ref[d, g], preferred_element_type=F32)
        y_o[g] = y.astype(y_o.dtype)


def _s5_glu_kernel(y_ref, w_ref, b_ref, o_ref, wbf_ref):
    @pl.when(pl.program_id(0) == 0)
    def _():
        wbf_ref[...] = w_ref[...].astype(BF16)

    y = y_ref[...].astype(F32)
    y = 0.5 * y * (1.0 + jnp.tanh(0.7978845608028654 * (y + 0.044715 * y * y * y)))
    z = jnp.dot(y.astype(BF16), wbf_ref[...], preferred_element_type=F32) + b_ref[...]
    o_ref[...] = (y * _sigmoid(z)).astype(o_ref.dtype)


def _s5_mixer(f_s5, mats, w_glu, b_glu3, l, dims, rows):
    toep, r_re, r_im, o_re, o_im, (lam_re, lam_im) = mats
    m, w = f_s5.shape
    ct = S5_CHUNK
    n_g = w // S5_GROUP
    n_ch = m // ct
    gw = ct * S5_GROUP
    u = f_s5.reshape(n_ch, ct, n_g, S5_GROUP).transpose(2, 0, 1, 3).reshape(n_g, n_ch, gw)
    sl = n_g * S5_STATE
    sd = jax.ShapeDtypeStruct
    zre, zim = pl.pallas_call(
        _s5_z_kernel,
        grid=(n_g // 2,),
        in_specs=[pl.BlockSpec((2, n_ch, gw), lambda g: (g, 0, 0)),
                  pl.BlockSpec((None, 2, 2, gw, _LANES), lambda g: (l, 0, g, 0, 0)),
                  pl.BlockSpec((None, 2, 2, gw, _LANES), lambda g: (l, 0, g, 0, 0))],
        out_specs=[pl.BlockSpec((2, n_ch, _LANES), lambda g: (0, 0, g))] * 2,
        out_shape=[sd((2, n_ch, sl), F32)] * 2,
        compiler_params=_cp("parallel"),
    )(u, r_re, r_im)
    lb = 512
    xre, xim = pl.pallas_call(
        functools.partial(_s5_scan_kernel, n_b=dims.B, n_s=dims.S // ct, n_c=dims.C // ct),
        grid=(sl // lb,),
        in_specs=[pl.BlockSpec((2, n_ch, lb), lambda j: (0, 0, j))] * 2
                 + [pl.BlockSpec((None, 2, 1, lb), lambda j: (l, 0, 0, j))] * 2,
        out_specs=[pl.BlockSpec((2, n_ch, lb), lambda j: (0, 0, j))] * 2,
        out_shape=[sd((2, n_ch, sl), F32)] * 2,
        compiler_params=_cp("parallel"),
    )(zre, zim, lam_re, lam_im)
    y = pl.pallas_call(
        _s5_y_kernel,
        grid=(n_g // 2,),
        in_specs=[pl.BlockSpec((2, n_ch, gw), lambda g: (g, 0, 0)),
                  pl.BlockSpec((None, 2, gw, gw), lambda g: (l, g, 0, 0)),
                  pl.BlockSpec((2, n_ch, _LANES), lambda g: (0, 0, g)),
                  pl.BlockSpec((2, n_ch, _LANES), lambda g: (0, 0, g)),
                  pl.BlockSpec((None, 2, 2, _LANES, gw), lambda g: (l, 0, g, 0, 0)),
                  pl.BlockSpec((None, 2, 2, _LANES, gw), lambda g: (l, 0, g, 0, 0))],
        out_specs=pl.BlockSpec((2, n_ch, gw), lambda g: (g, 0, 0)),
        out_shape=sd((n_g, n_ch, gw), BF16),
        compiler_params=_cp("parallel"),
    )(u, toep, xre, xim, o_re, o_im)
    y = y.reshape(n_g, n_ch, ct, S5_GROUP).transpose(1, 2, 0, 3).reshape(m, w)
    tm = 256
    return pl.pallas_call(
        _s5_glu_kernel,
        grid=(rows // tm,),
        in_specs=[pl.BlockSpec((tm, w), lambda i: (i, 0)),
                  pl.BlockSpec((None, w, w), lambda i: (l, 0, 0)),
                  pl.BlockSpec((None, 1, w), lambda i: (l, 0, 0))],
        out_specs=pl.BlockSpec((tm, w), lambda i: (i, 0)),
        out_shape=sd((rows, w), BF16),
        scratch_shapes=[pltpu.VMEM((w, w), BF16)],
        compiler_params=_cp("arbitrary"),
    )(y, w_glu, b_glu3)


def _merge_kernel(xn_ref, ya_ref, yb_ref, yc_ref, yd_ref, wg0, wg1, wg2, wg3, bg0, bg1, bg2, bg3, wb_ref,
                  o_ref, wg_bf, wb_bf):
    wgs = (wg0, wg1, wg2, wg3)

    @pl.when(pl.program_id(1) == 0)
    def _():
        for n in range(4):
            wg_bf[n] = wgs[n][...].astype(BF16)
            wb_bf[n] = wb_ref[n].astype(BF16)

    xn = xn_ref[...]
    acc = None
    for n, (y_ref, bg) in enumerate(zip((ya_ref, yb_ref, yc_ref, yd_ref), (bg0, bg1, bg2, bg3))):
        gate = _sigmoid(jnp.dot(xn, wg_bf[n], preferred_element_type=F32) + bg[...])
        up = jnp.dot(y_ref[...], wb_bf[n], preferred_element_type=F32)
        acc = gate * up if acc is None else acc + gate * up
    o_ref[...] = acc.astype(o_ref.dtype)


def _merge(xn, ys, w_gate, b_gate3, w_branch, l, rows):
    d = xn.shape[1]
    w = ys[0].shape[1]
    tn = 256
    tm = _pick(rows, (1088, 1024, 512, 256))
    nb = d // tn
    wg = lambda n: pl.BlockSpec((None, d, tn), lambda j, i: (l, 0, n * nb + j))
    bg = lambda n: pl.BlockSpec((None, 1, tn), lambda j, i: (l, 0, n * nb + j))
    yspec = pl.BlockSpec((tm, w), lambda j, i: (i, 0))
    return pl.pallas_call(
        _merge_kernel,
        grid=(nb, rows // tm),
        in_specs=[pl.BlockSpec((tm, d), lambda j, i: (i, 0)), yspec, yspec, yspec, yspec,
                  wg(0), wg(1), wg(2), wg(3), bg(0), bg(1), bg(2), bg(3),
                  pl.BlockSpec((None, 4, w, tn), lambda j, i: (l, 0, 0, j))],
        out_specs=pl.BlockSpec((tm, tn), lambda j, i: (i, j)),
        out_shape=jax.ShapeDtypeStruct((rows, d), BF16),
        scratch_shapes=[pltpu.VMEM((4, d, tn), BF16), pltpu.VMEM((4, w, tn), BF16)],
        compiler_params=_cp("arbitrary", "arbitrary"),
    )(xn, *ys, w_gate, w_gate, w_gate, w_gate, b_gate3, b_gate3, b_gate3, b_gate3, w_branch)


def _router_kernel(x_ref, g_ref, sc_ref, sh_ref, wr_ref, br_ref, idx_o, gate_o, rank_o, cnt_o, hp_o, carry):
    tm = x_ref.shape[0]
    x = x_ref[...]
    h = x * lax.rsqrt(jnp.mean(x * x, axis=-1, keepdims=True) + NORM_EPS) * g_ref[...]
    h = h * (1.0 + sc_ref[...]) + sh_ref[...]
    hp_o[...] = _pack_pairs(h)

    @pl.when(pl.program_id(0) == 0)
    def _():
        carry[...] = jnp.zeros_like(carry)

    h_hi, h_lo = _split(h, 2)
    w_hi, w_lo = _split(wr_ref[...], 2)
    logits = (jnp.dot(h_hi, w_hi, preferred_element_type=F32) + jnp.dot(h_lo, w_hi, preferred_element_type=F32)
              + jnp.dot(h_hi, w_lo, preferred_element_type=F32)) + br_ref[...]
    lane = lax.broadcasted_iota(I32, (tm, _LANES), 1)
    vals, idxs, hots = [], [], []
    cur = logits
    for _ in range(TOP_K):
        mx = jnp.max(cur, axis=-1, keepdims=True)
        ix = jnp.min(jnp.where(cur == mx, lane, _LANES), axis=-1, keepdims=True)
        hot = lane == ix
        vals.append(mx)
        idxs.append(ix)
        hots.append(hot)
        cur = jnp.where(hot, -jnp.inf, cur)
    ex = [jnp.exp(v - vals[0]) for v in vals]
    inv = 1.0 / sum(ex)
    sel = sum(jnp.where(h, 1.0, 0.0) for h in hots)
    ri = lax.broadcasted_iota(I32, (tm, tm), 0)
    ci = lax.broadcasted_iota(I32, (tm, tm), 1)
    earlier = jnp.where(ci < ri, 1.0, 0.0).astype(BF16)
    before = jnp.dot(earlier, sel.astype(BF16), preferred_element_type=F32) + carry[...]
    idx_out = jnp.zeros((tm, _LANES), I32)
    gate_out = jnp.zeros((tm, _LANES), F32)
    rank_out = jnp.zeros((tm, _LANES), I32)
    for k in range(TOP_K):
        rk = jnp.sum(jnp.where(hots[k], before, 0.0), axis=-1, keepdims=True)
        idx_out = jnp.where(lane == k, idxs[k], idx_out)
        gate_out = jnp.where(lane == k, ex[k] * inv, gate_out)
        rank_out = jnp.where(lane == k, rk.astype(I32), rank_out)
    idx_o[...] = idx_out
    gate_o[...] = gate_out
    rank_o[...] = rank_out
    carry[...] = carry[...] + jnp.sum(sel, axis=0, keepdims=True)
    cnt_o[...] = jnp.broadcast_to(carry[...], cnt_o.shape)


def _router(x, gain3, mod, wr_pad, br_pad, l, dims, rows):
    d = x.shape[1]
    tm = 256
    seg = dims.seg_fn(tm)
    blk = pl.BlockSpec((tm, _LANES), lambda i: (i, 0))
    sd = jax.ShapeDtypeStruct
    return pl.pallas_call(
        _router_kernel,
        grid=(rows // tm,),
        in_specs=[pl.BlockSpec((tm, d), lambda i: (i, 0)),
                  pl.BlockSpec((None, 1, d), lambda i: (l, 0, 0)),
                  pl.BlockSpec((None, 1, d), lambda i: (seg(i), 0, 4)),
                  pl.BlockSpec((None, 1, d), lambda i: (seg(i), 0, 3)),
                  pl.BlockSpec((None, d, _LANES), lambda i: (l, 0, 0)),
                  pl.BlockSpec((None, 1, _LANES), lambda i: (l, 0, 0))],
        out_specs=[blk, blk, blk, pl.BlockSpec((8, _LANES), lambda i: (0, 0)),
                   pl.BlockSpec((tm, d // 2), lambda i: (i, 0))],
        out_shape=[sd((rows, _LANES), I32), sd((rows, _LANES), F32), sd((rows, _LANES), I32),
                   sd((8, _LANES), F32), sd((rows, d // 2), jnp.uint32)],
        scratch_shapes=[pltpu.VMEM((1, _LANES), F32)],
        compiler_params=_cp("arbitrary"),
    )(x, gain3, mod, mod, wr_pad, br_pad)


def _row_copy(src_hbm, row, dst, slot, sem):
    return pltpu.make_async_copy(src_hbm.at[pl.ds(row, 1), :], dst.at[pl.ds(slot, 1), :], sem)


def _issue_rows(idx_ref, n, src_hbm, dst, sem):
    for r in range(n):
        _row_copy(src_hbm, idx_ref[0, r], dst, r, sem).start()


def _expert_kernel(be_ref, first_ref, used_ref, tok_ref, tok_next_ref, h_hbm, wgu_ref, bgu_ref, wdn_ref,
                   bdn_ref, o_ref, xbuf0, xbuf1, sems, wgu_bf, wdn_bf):
    i = pl.program_id(0)
    n_used = used_ref[0]
    blk = xbuf0.shape[0]
    bufs = (xbuf0, xbuf1)

    @pl.when(i == 0)
    def _():
        _issue_rows(tok_ref, blk, h_hbm, xbuf0, sems.at[0])

    @pl.when(first_ref[i] == 1)
    def _():
        wgu_bf[...] = wgu_ref[...].astype(BF16)
        wdn_bf[...] = wdn_ref[...].astype(BF16)

    for par in range(2):
        cur, nxt = bufs[par], bufs[1 - par]
        mine = i % 2 == par

        @pl.when(mine & (i <= n_used))
        def _():
            pltpu.make_async_copy(h_hbm.at[pl.ds(0, blk), :], cur, sems.at[par]).wait()

        @pl.when(mine & (i < n_used))
        def _():
            _issue_rows(tok_next_ref, blk, h_hbm, nxt, sems.at[1 - par])
            x = jnp.concatenate(_unpack_pairs(cur[...]), axis=1).astype(BF16)
            gu = jnp.dot(x, wgu_bf[...], preferred_element_type=F32) + bgu_ref[...]
            g_lin = jnp.minimum(gu[:, :D_EXPERT], SWIGLU_LIMIT)
            u_lin = jnp.clip(gu[:, D_EXPERT:], -SWIGLU_LIMIT, SWIGLU_LIMIT)
            act = (u_lin + 1.0) * g_lin * _sigmoid(SWIGLU_ALPHA * g_lin)
            y = jnp.dot(act.astype(BF16), wdn_bf[...], preferred_element_type=F32) + bdn_ref[...]
            o_ref[...] = _pack_pairs(y)

    @pl.when(i >= n_used)
    def _():
        o_ref[...] = jnp.zeros_like(o_ref)


def _moe_experts(h, slot_tok3, block_e, first, n_used, w_gu, b_gu4, w_dn, b_dn4, l):
    n_blocks = slot_tok3.shape[0] - 1
    blk = slot_tok3.shape[2]
    dp = h.shape[1]
    d = 2 * dp
    de2 = w_gu.shape[3]
    grid_spec = pltpu.PrefetchScalarGridSpec(
        num_scalar_prefetch=3,
        grid=(n_blocks,),
        in_specs=[pl.BlockSpec((None, 1, blk), lambda i, be, fi, us: (i, 0, 0), memory_space=pltpu.SMEM),
                  pl.BlockSpec((None, 1, blk), lambda i, be, fi, us: (i + 1, 0, 0), memory_space=pltpu.SMEM),
                  pl.BlockSpec(memory_space=pl.ANY),
                  pl.BlockSpec((None, None, d, de2), lambda i, be, fi, us: (l, be[i], 0, 0)),
                  pl.BlockSpec((None, None, 1, de2), lambda i, be, fi, us: (l, be[i], 0, 0)),
                  pl.BlockSpec((None, None, de2 // 2, d), lambda i, be, fi, us: (l, be[i], 0, 0)),
                  pl.BlockSpec((None, None, 1, d), lambda i, be, fi, us: (l, be[i], 0, 0))],
        out_specs=pl.BlockSpec((blk, dp), lambda i, be, fi, us: (i, 0)),
        scratch_shapes=[pltpu.VMEM((blk, dp), jnp.uint32), pltpu.VMEM((blk, dp), jnp.uint32),
                        pltpu.SemaphoreType.DMA((2,)),
                        pltpu.VMEM((d, de2), BF16), pltpu.VMEM((de2 // 2, d), BF16)])
    return pl.pallas_call(
        _expert_kernel,
        grid_spec=grid_spec,
        out_shape=jax.ShapeDtypeStruct((n_blocks * blk, dp), jnp.uint32),
        compiler_params=_cp("arbitrary"),
    )(block_e, first, n_used, slot_tok3, slot_tok3, h, w_gu, b_gu4, w_dn, b_dn4)


def _combine_kernel(dest_ref, dest_next_ref, gate_ref, x_ref, g2_ref, ys_hbm, o_ref, buf, sems):
    tm = x_ref.shape[0]
    i = pl.program_id(0)
    slot = i % 2

    def issue(idx_ref, s):
        for r in range(tm):
            for k in range(TOP_K):
                _row_copy(ys_hbm, idx_ref[0, r * TOP_K + k], buf.at[s, k], r, sems.at[s]).start()

    @pl.when(i == 0)
    def _():
        issue(dest_ref, 0)

    @pl.when(i + 1 < pl.num_programs(0))
    def _():
        issue(dest_next_ref, 1 - slot)

    for k in range(TOP_K):
        pltpu.make_async_copy(ys_hbm.at[pl.ds(0, tm), :], buf.at[slot, k], sems.at[slot]).wait()
    gate = gate_ref[...]
    n = buf.shape[3]
    f_lo = f_hi = None
    for k in range(TOP_K):
        lo, hi = _unpack_pairs(buf[slot, k])
        gk = gate[:, k:k + 1]
        f_lo = gk * lo if f_lo is None else f_lo + gk * lo
        f_hi = gk * hi if f_hi is None else f_hi + gk * hi
    o_ref[:, :n] = x_ref[:, :n] + g2_ref[:, :n] * f_lo
    o_ref[:, n:] = x_ref[:, n:] + g2_ref[:, n:] * f_hi


def _moe_combine(x, ys, dest3, gate, mod, dims, rows):
    d = x.shape[1]
    tm = MOE_COMBINE_ROWS
    seg = dims.seg_fn(tm)
    return pl.pallas_call(
        _combine_kernel,
        grid=(rows // tm,),
        in_specs=[pl.BlockSpec((None, 1, tm * TOP_K), lambda i: (i, 0, 0), memory_space=pltpu.SMEM),
                  pl.BlockSpec((None, 1, tm * TOP_K), lambda i: (i + 1, 0, 0), memory_space=pltpu.SMEM),
                  pl.BlockSpec((tm, _LANES), lambda i: (i, 0)),
                  pl.BlockSpec((tm, d), lambda i: (i, 0)),
                  pl.BlockSpec((None, 1, d), lambda i: (seg(i), 0, 5)),
                  pl.BlockSpec(memory_space=pl.ANY)],
        out_specs=pl.BlockSpec((tm, d), lambda i: (i, 0)),
        out_shape=jax.ShapeDtypeStruct((rows, d), F32),
        scratch_shapes=[pltpu.VMEM((2, TOP_K, tm, d // 2), jnp.uint32), pltpu.SemaphoreType.DMA((2,))],
        compiler_params=_cp("arbitrary"),
    )(dest3, dest3, gate, x, mod, ys)


def _moe(x, mod, pw, l, dims, rows):
    idx, gate, rank, cnt, hp = _router(x, pw["norm2_g"], mod, pw["moe_wr"], pw["moe_br"], l, dims, rows)
    blk = MOE_BLOCK
    n_blocks = -(-(rows * TOP_K) // blk) + N_EXPERTS + 1
    counts = cnt[0, :N_EXPERTS].astype(I32)
    padded = (counts + blk - 1) // blk * blk
    pad_end = jnp.cumsum(padded)
    pad_start = pad_end - padded
    top_i = idx[:, :TOP_K]
    start_of = jnp.sum(jnp.where(top_i[..., None] == jnp.arange(N_EXPERTS, dtype=I32), pad_start, 0), axis=-1)
    dest = start_of + rank[:, :TOP_K]
    tok = jnp.broadcast_to(jnp.arange(rows, dtype=I32)[:, None], dest.shape)
    fill = jnp.arange((n_blocks + 1) * blk, dtype=I32) % rows
    slot_tok = fill.at[dest.reshape(-1)].set(tok.reshape(-1))
    block_start = jnp.arange(n_blocks, dtype=I32) * blk
    block_e = jnp.minimum(jnp.sum((pad_end[None, :] <= block_start[:, None]).astype(I32), axis=1), N_EXPERTS - 1)
    first = jnp.concatenate([jnp.ones((1,), I32), (block_e[1:] != block_e[:-1]).astype(I32)])
    n_used = (pad_end[-1:] // blk).astype(I32)
    ys = _moe_experts(hp, slot_tok.reshape(n_blocks + 1, 1, blk), block_e, first, n_used, pw["moe_w_gu"],
                      pw["moe_b_gu"], pw["moe_w_dn"], pw["moe_b_dn"], l)
    dest3 = dest.reshape(rows // MOE_COMBINE_ROWS, 1, MOE_COMBINE_ROWS * TOP_K)
    dest3 = jnp.concatenate([dest3, jnp.zeros_like(dest3[:1])], axis=0)
    return _moe_combine(x, ys, dest3, gate, mod, dims, rows)


def _prep_params(p, dims):
    n_l = p["w_in"].shape[0]
    w = dims.W
    o_mla = 3 * w + (3 * w + 384)
    o_s5 = o_mla + MLA_Q_RANK + MLA_KV_RANK + MLA_ROPE
    pw = dict(p)
    row3 = lambda a: a.reshape(a.shape[0], 1, -1)
    pw["norm1_g"], pw["norm2_g"] = row3(p["norm1_g"]), row3(p["norm2_g"])
    pw["b_mod"], pw["b_gate"] = row3(p["b_mod"]), row3(p["b_gate"])
    pw["w_in_mla"] = p["w_in"][:, :, o_mla:o_s5]
    pw["w_in_s5"] = p["w_in"][:, :, o_s5:]
    zeros = jnp.zeros_like(p["rw_w2"][:, 0])
    bdiag = lambda x: jnp.concatenate([jnp.concatenate([x[:, 0], zeros], axis=-1),
                                       jnp.concatenate([zeros, x[:, 1]], axis=-1)], axis=1)
    pw["rw_w2"], pw["rw_a2"] = bdiag(p["rw_w2"]), bdiag(p["rw_a2"])
    pw["rw_w0"], pw["rw_a0"] = row3(p["rw_w0"]), row3(p["rw_a0"])
    pw["rw_kk"], pw["rw_ka"] = row3(p["rw_kk"]), row3(p["rw_ka"])
    pw["rw_rk"], pw["rw_gn_g"], pw["rw_gn_b"] = row3(p["rw_rk"]), row3(p["rw_gn_g"]), row3(p["rw_gn_b"])
    pad_h = lambda x, n: jnp.pad(x, ((0, 0), (0, 0), (0, 0), (0, _LANES - n)))
    wq = p["mla_wq_up"].reshape(n_l, MLA_Q_RANK, MLA_HEADS, MLA_QK)
    pw["mla_wq"] = pad_h(wq, MLA_QK).reshape(n_l, MLA_Q_RANK, -1).astype(BF16)
    wkv = p["mla_wkv_up"].reshape(n_l, MLA_KV_RANK, MLA_HEADS, MLA_NOPE + MLA_V)
    pw["mla_wk"] = pad_h(wkv[..., :MLA_NOPE], MLA_NOPE).reshape(n_l, MLA_KV_RANK, -1).astype(BF16)
    pw["mla_wv"] = wkv[..., MLA_NOPE:].reshape(n_l, MLA_KV_RANK, -1).astype(BF16)
    pw["mla_place"] = jnp.asarray(np.arange(MLA_ROPE)[:, None] + MLA_NOPE == np.arange(_LANES)[None, :], BF16)
    pw["mla_gcq"], pw["mla_gckv"] = row3(p["mla_qn_g"]), row3(p["mla_kvn_g"])
    pad_g = lambda g: jnp.pad(g, ((0, 0), (0, _LANES - MLA_QK))).reshape(n_l, 1, _LANES)
    pw["mla_gq"] = pad_g(p["mla_qkn_q"] * (MLA_QK ** -0.5 * _LOG2E))
    pw["mla_gk"] = pad_g(p["mla_qkn_k"])
    pw["s5_b_glu"] = row3(p["s5_b_glu"])
    pw["moe_wr"] = jnp.pad(p["moe_wr"], ((0, 0), (0, 0), (0, _LANES - N_EXPERTS)))
    pw["moe_br"] = jnp.pad(p["moe_br"], ((0, 0), (0, _LANES - N_EXPERTS)), constant_values=-1e30)[:, None, :]
    pw["moe_b_gu"] = p["moe_b_gu"][:, :, None, :]
    pw["moe_b_dn"] = p["moe_b_dn"][:, :, None, :]
    return pw


def _residual_epilogue(acc, x, g):
    return x + g * acc


def _layer(xs, l, mod, pw, ropes, dims, need_ctx):
    m, d, w = dims.M, dims.D, dims.W
    rows = m if need_ctx else dims.BS
    tm_all = _pick(m, (1088, 1024, 512, 256))
    xn = _norm_mod(xs, pw["norm1_g"], l, mod, 0, 1, dims, m, BF16)
    f_na = _mm(xn, pw["w_in"], l, tm=tm_all, tn=768, col0=0, ncols=3 * w)
    f_rw = _mm(xn, pw["w_in"], l, tm=tm_all, tn=384, col0=3 * w, ncols=3 * w + 384)
    f_mla = _mm(xn, pw["w_in_mla"], l, tm=tm_all, tn=pw["w_in_mla"].shape[2])
    f_s5 = _mm(xn, pw["w_in_s5"], l, tm=tm_all, tn=w, out_dtype=BF16)

    qkv = _na_prep(f_na, pw["na_qn_g"][l], pw["na_kn_g"][l], dims)
    ya = _na_attention(qkv, pw["na_bias"], l, dims, need_ctx)

    r, v, kk, g, lw, kd, bb = _rw_prep(f_rw, pw, l, dims)
    yf, yb_ = _rw_scan(r, v, kk, lw, kd, bb, dims)
    yb = _rw_readout(yf, yb_, r, v, g, kd, pw, l, dims, rows)

    q, k, vv = _mla_prep(f_mla, pw, l, ropes, dims)
    yc = _mla_attention(q, k, vv, dims, need_ctx)

    yd = _s5_mixer(f_s5, pw["s5_mats"], pw["s5_w_glu"], pw["s5_b_glu"], l, dims, rows)

    merged = _merge(xn, (ya, yb, yc, yd), pw["w_gate"], pw["b_gate"], pw["w_branch"], l, rows)
    tm_seg = _pick(dims.S, (512, 256))
    seg = dims.seg_fn(tm_seg)
    tn = 1024
    xs = _mm(merged, pw["w_out"], l, tm=tm_seg, tn=tn, rows=rows, epilogue=_residual_epilogue,
             extras=((xs, pl.BlockSpec((tm_seg, tn), lambda j, i: (i, j))),
                     (mod, pl.BlockSpec((None, 1, tn), lambda j, i: (seg(i), 0, 2 * (d // tn) + j)))))
    return _moe(xs, mod, pw, l, dims, rows)


def kernel(x, c, ctx, c_ctx, w_mod, b_mod, norm1_g, norm2_g, w_in, w_gate, b_gate, w_branch, w_out, na_qn_g, na_kn_g, na_rpb, rw_mu, rw_w0, rw_w2, rw_a0, rw_a2, rw_g2, rw_kk, rw_ka, rw_rk, rw_gn_g, rw_gn_b, mla_qn_g, mla_wq_up, mla_kvn_g, mla_wkv_up, mla_qkn_q, mla_qkn_k, s5_a_re, s5_a_im, s5_log_dt, s5_b_re, s5_b_im, s5_c_re, s5_c_im, s5_d, s5_w_glu, s5_b_glu, moe_wr, moe_br, moe_w_gu, moe_b_gu, moe_w_dn, moe_b_dn):
    params = dict(w_mod=w_mod, b_mod=b_mod, norm1_g=norm1_g, norm2_g=norm2_g, w_in=w_in, w_gate=w_gate,
                  b_gate=b_gate, w_branch=w_branch, w_out=w_out, na_qn_g=na_qn_g, na_kn_g=na_kn_g,
                  na_rpb=na_rpb, rw_mu=rw_mu, rw_w0=rw_w0, rw_w2=rw_w2, rw_a0=rw_a0, rw_a2=rw_a2,
                  rw_g2=rw_g2, rw_kk=rw_kk, rw_ka=rw_ka, rw_rk=rw_rk, rw_gn_g=rw_gn_g, rw_gn_b=rw_gn_b,
                  mla_qn_g=mla_qn_g, mla_wq_up=mla_wq_up, mla_kvn_g=mla_kvn_g, mla_wkv_up=mla_wkv_up,
                  mla_qkn_q=mla_qkn_q, mla_qkn_k=mla_qkn_k, s5_a_re=s5_a_re, s5_a_im=s5_a_im,
                  s5_log_dt=s5_log_dt, s5_b_re=s5_b_re, s5_b_im=s5_b_im, s5_c_re=s5_c_re, s5_c_im=s5_c_im,
                  s5_d=s5_d, s5_w_glu=s5_w_glu, s5_b_glu=s5_b_glu, moe_wr=moe_wr, moe_br=moe_br,
                  moe_w_gu=moe_w_gu, moe_b_gu=moe_b_gu, moe_w_dn=moe_w_dn, moe_b_dn=moe_b_dn)
    b, s, d = x.shape
    c_len = ctx.shape[1]
    depth = w_mod.shape[0]
    dims = _Dims(b, s, c_len, d)
    assert b + 1 <= 8
    pw = _prep_params(params, dims)
    pw["na_bias"] = jax.vmap(lambda rpb: _na_bias_tables(rpb, s // GRID_W))(na_rpb)
    pw["s5_mats"] = jax.vmap(_s5_matrices)(s5_a_re, s5_a_im, s5_log_dt, s5_b_re, s5_b_im, s5_c_re, s5_c_im, s5_d)
    cond = jnp.concatenate([jax.nn.silu(c), jax.nn.silu(c_ctx)[None],
                            jnp.zeros((8 - b - 1, d), F32)], axis=0)
    ropes = _rope_tables(s, 256)
    xs = jnp.concatenate([x.reshape(b * s, d), ctx.reshape(b * c_len, d)], axis=0)
    for l in range(depth):
        mod = _mm(cond, pw["w_mod"], l, tm=8, tn=1024, bias=pw["b_mod"]).reshape(8, 1, 6 * d)
        xs = _layer(xs, l, mod, pw, ropes, dims, need_ctx=l < depth - 1)
    return xs[:b * s].reshape(b, s, d)
```

```python
import functools

import numpy as np
import jax
import jax.numpy as jnp
from jax import lax
from jax.experimental import pallas as pl
from jax.experimental.pallas import tpu as pltpu

F32 = jnp.float32
BF16 = jnp.bfloat16
I32 = jnp.int32

_VMEM_LIMIT_BYTES = 56 * 1024 * 1024
_LANES = 128
_LOG2E = 1.4426950408889634

NORM_EPS = 1e-6
GRID_W = 64
NA_HEADS = 8
NA_HEAD_DIM = 64
NA_WIN_R = 8
NA_WIN_C = 16
NA_QROWS = 4
NA_BAND_ROWS = NA_QROWS + NA_WIN_R - 1
NA_HEADS_TOGETHER = 2
RW_HEAD_DIM = 64
RW_GN_EPS = 64e-5
RW_CHUNK = 64
RW_PAIRS_PER_STEP = 4
MLA_HEADS = 8
MLA_NOPE = 64
MLA_ROPE = 32
MLA_QK = MLA_NOPE + MLA_ROPE
MLA_V = 64
MLA_Q_RANK = 512
MLA_KV_RANK = 256
ROPE_THETA = 10000.0
MLA_KEY_CHUNK = 256
MLA_Q_TILE = 1024
S5_GROUP = 16
S5_STATE = 64
S5_CHUNK = 16
N_EXPERTS = 32
TOP_K = 4
D_EXPERT = 512
SWIGLU_LIMIT = 7.0
SWIGLU_ALPHA = 1.702
MOE_BLOCK = 256
MERGE_TN = 512
MOE_COMBINE_ROWS = 128

_NT = (((1,), (1,)), ((), ()))
_TN = (((0,), (0,)), ((), ()))


def _cp(*sem):
    return pltpu.CompilerParams(dimension_semantics=sem, vmem_limit_bytes=_VMEM_LIMIT_BYTES)


def _dot(a, b):
    return jnp.dot(a.astype(BF16), b.astype(BF16), preferred_element_type=F32)


def _dot_nt(a, b):
    return lax.dot_general(a.astype(BF16), b.astype(BF16), _NT, preferred_element_type=F32)


def _dot_tn(a, b):
    return lax.dot_general(a.astype(BF16), b.astype(BF16), _TN, preferred_element_type=F32)


def _split(x, parts):
    out = []
    for _ in range(parts):
        h = x.astype(BF16)
        out.append(h)
        x = x - h.astype(F32)
    return out


def _dot01_right(x, m01, parts=2):
    return sum(jnp.dot(h, m01, preferred_element_type=F32) for h in _split(x, parts))


def _dot01_left(m01, x, parts=3):
    return sum(jnp.dot(m01, h, preferred_element_type=F32) for h in _split(x, parts))


def _pack_pairs(x):
    n = x.shape[1] // 2
    bits = pltpu.bitcast(x.astype(BF16).astype(F32), jnp.uint32)
    return (bits[:, :n] >> 16) | bits[:, n:]


def _unpack_pairs(w):
    return pltpu.bitcast(w << 16, F32), pltpu.bitcast(w & jnp.uint32(0xFFFF0000), F32)


def _sigmoid(x):
    return 1.0 / (1.0 + jnp.exp(-x))


def _softplus(x):
    return jnp.maximum(x, 0.0) + jnp.log(1.0 + jnp.exp(-jnp.abs(x)))


def _pick(n, cands):
    for c in cands:
        if n % c == 0:
            return c
    raise ValueError(f"no tile for {n} in {cands}")


def _head_blockdiag(width, head):
    i = np.arange(width)
    return jnp.asarray((i[:, None] // head) == (i[None, :] // head), BF16)


def _mm_kernel(*refs, has_bias, n_extra, epilogue):
    a_ref, w_ref = refs[0], refs[1]
    pos = 2
    bias_ref = None
    if has_bias:
        bias_ref = refs[pos]
        pos += 1
    extra = refs[pos:pos + n_extra]
    o_ref, wbf_ref = refs[pos + n_extra], refs[pos + n_extra + 1]

    @pl.when(pl.program_id(1) == 0)
    def _():
        wbf_ref[...] = w_ref[...].astype(BF16)

    acc = jnp.dot(a_ref[...].astype(BF16), wbf_ref[...], preferred_element_type=F32)
    if has_bias:
        acc = acc + bias_ref[...]
    if epilogue is not None:
        acc = epilogue(acc, *[e[...] for e in extra])
    o_ref[...] = acc.astype(o_ref.dtype)


def _mm(a, w, l, *, tm, tn, col0=0, ncols=None, rows=None, bias=None, extras=(), epilogue=None,
        out_dtype=F32):
    k = a.shape[1]
    rows = a.shape[0] if rows is None else rows
    ncols = w.shape[2] - col0 if ncols is None else ncols
    assert col0 % tn == 0 and ncols % tn == 0 and rows % tm == 0 and w.shape[1] == k
    c0 = col0 // tn
    in_specs = [pl.BlockSpec((tm, k), lambda j, i: (i, 0)),
                pl.BlockSpec((None, k, tn), lambda j, i: (l, 0, j + c0))]
    args = [a, w]
    if bias is not None:
        in_specs.append(pl.BlockSpec((None, 1, tn), lambda j, i: (l, 0, j + c0)))
        args.append(bias)
    for arr, spec in extras:
        in_specs.append(spec)
        args.append(arr)
    return pl.pallas_call(
        functools.partial(_mm_kernel, has_bias=bias is not None, n_extra=len(extras), epilogue=epilogue),
        grid=(ncols // tn, rows // tm),
        in_specs=in_specs,
        out_specs=pl.BlockSpec((tm, tn), lambda j, i: (i, j)),
        out_shape=jax.ShapeDtypeStruct((rows, ncols), out_dtype),
        scratch_shapes=[pltpu.VMEM((k, tn), BF16)],
        compiler_params=_cp("arbitrary", "arbitrary"),
    )(*args)


def _norm_mod_kernel(x_ref, g_ref, sc_ref, sh_ref, o_ref):
    x = x_ref[...]
    y = x * lax.rsqrt(jnp.mean(x * x, axis=-1, keepdims=True) + NORM_EPS) * g_ref[...]
    o_ref[...] = (y * (1.0 + sc_ref[...]) + sh_ref[...]).astype(o_ref.dtype)


def _norm_mod(x, gain3, l, mod, which_sh, which_sc, dims, rows, out_dtype):
    d = x.shape[1]
    tm = 256
    seg = dims.seg_fn(tm)
    return pl.pallas_call(
        _norm_mod_kernel,
        grid=(rows // tm,),
        in_specs=[pl.BlockSpec((tm, d), lambda i: (i, 0)),
                  pl.BlockSpec((None, 1, d), lambda i: (l, 0, 0)),
                  pl.BlockSpec((None, 1, d), lambda i: (seg(i), 0, which_sc)),
                  pl.BlockSpec((None, 1, d), lambda i: (seg(i), 0, which_sh))],
        out_specs=pl.BlockSpec((tm, d), lambda i: (i, 0)),
        out_shape=jax.ShapeDtypeStruct((rows, d), out_dtype),
        compiler_params=_cp("parallel"),
    )(x, gain3, mod, mod)


class _Dims:
    def __init__(self, b, s, c, d):
        self.B, self.S, self.C, self.D = b, s, c, d
        self.BS = b * s
        self.M = b * s + b * c
        self.W = d // 4

    def seg_fn(self, tm):
        assert self.S % tm == 0 and self.M % tm == 0
        s, b = self.S, self.B
        return lambda i: jnp.minimum((i * tm) // s, b)


def _attend(qs, segs_of):
    n = len(qs)
    m, l, acc = [None] * n, [None] * n, [None] * n
    for j in range(len(segs_of[0])):
        for i in range(n):
            k, v, bias = segs_of[i][j]
            s = lax.dot_general(qs[i], k, _NT, preferred_element_type=F32)
            if bias is not None:
                s = s + bias
            smax = jnp.max(s, axis=-1, keepdims=True)
            if j == 0:
                m[i] = smax
                p = jnp.exp2(s - smax)
                l[i] = jnp.sum(p, axis=-1, keepdims=True)
                acc[i] = jnp.dot(p.astype(BF16), v, preferred_element_type=F32)
            else:
                m_new = jnp.maximum(m[i], smax)
                alpha = jnp.exp2(m[i] - m_new)
                p = jnp.exp2(s - m_new)
                l[i] = alpha * l[i] + jnp.sum(p, axis=-1, keepdims=True)
                acc[i] = alpha * acc[i] + jnp.dot(p.astype(BF16), v, preferred_element_type=F32)
                m[i] = m_new
    return [acc[i] * (1.0 / l[i]) for i in range(n)]


def _half_masks(dtype):
    lane = lax.broadcasted_iota(I32, (1, _LANES), 1)
    lo = lane < 64
    return lo, jnp.where(lo, 1.0, 0.0).astype(dtype), jnp.where(lo, 0.0, 1.0).astype(dtype)


def _na_prep_kernel(f_ref, gq_ref, gk_ref, bd_ref, o_ref):
    w = gq_ref.shape[1]
    bd = bd_ref[...]

    def head_norm(x, g):
        ss = _dot01_right(x * x, bd)
        return x * lax.rsqrt(ss * (1.0 / NA_HEAD_DIM) + NORM_EPS) * g

    o_ref[:, 0:w] = head_norm(f_ref[:, 0:w], gq_ref[...]).astype(o_ref.dtype)
    o_ref[:, w:2 * w] = head_norm(f_ref[:, w:2 * w], gk_ref[...]).astype(o_ref.dtype)
    o_ref[:, 2 * w:3 * w] = f_ref[:, 2 * w:3 * w].astype(o_ref.dtype)


def _na_prep(f_na, qn_g, kn_g, dims):
    m, w3 = f_na.shape
    w = w3 // 3
    tm = 256
    gq = (jnp.tile(qn_g, NA_HEADS) * (NA_HEAD_DIM ** -0.5 * _LOG2E)).reshape(1, w)
    gk = jnp.tile(kn_g, NA_HEADS).reshape(1, w)
    return pl.pallas_call(
        _na_prep_kernel,
        grid=(m // tm,),
        in_specs=[pl.BlockSpec((tm, w3), lambda i: (i, 0)),
                  pl.BlockSpec((1, w), lambda i: (0, 0)),
                  pl.BlockSpec((1, w), lambda i: (0, 0)),
                  pl.BlockSpec((w, w), lambda i: (0, 0))],
        out_specs=pl.BlockSpec((tm, w3), lambda i: (i, 0)),
        out_shape=jax.ShapeDtypeStruct((m, w3), BF16),
        compiler_params=_cp("parallel"),
    )(f_na, gq, gk, _head_blockdiag(w, NA_HEAD_DIM))


def _na_bias_tables(rpb, rows):
    n_tiles = rows // NA_QROWS
    assert rows % NA_QROWS == 0 and rows >= NA_BAND_ROWS + NA_QROWS and NA_QROWS * 2 <= NA_WIN_R
    tabs = []
    for g in (0, 1, n_tiles - 1):
        r = g * NA_QROWS + np.arange(NA_QROWS)
        band0 = int(np.clip(g * NA_QROWS - NA_WIN_R // 2, 0, rows - NA_BAND_ROWS))
        key_row = band0 + np.arange(NA_BAND_ROWS)
        row_start = np.clip(r - NA_WIN_R // 2, 0, rows - NA_WIN_R)
        valid_r = (key_row[None, :] >= row_start[:, None]) & (key_row[None, :] < row_start[:, None] + NA_WIN_R)
        d_row = np.clip(key_row[None, :] - r[:, None] + NA_WIN_R - 1, 0, 2 * NA_WIN_R - 2)
        qc = np.arange(GRID_W)
        col_start = np.clip(qc - NA_WIN_C // 2, 0, GRID_W - NA_WIN_C)
        kc = np.arange(GRID_W)
        valid_c = (kc[None, :] >= col_start[:, None]) & (kc[None, :] < col_start[:, None] + NA_WIN_C)
        d_col = np.clip(kc[None, :] - qc[:, None] + NA_WIN_C - 1, 0, 2 * NA_WIN_C - 2)
        oh_r = jnp.asarray(d_row[:, :, None] == np.arange(2 * NA_WIN_R - 1), F32)
        oh_c = jnp.asarray(d_col[:, :, None] == np.arange(2 * NA_WIN_C - 1), F32)
        b = jnp.einsum('ajr,hrc->hajc', oh_r, rpb, precision=lax.Precision.HIGHEST)
        b = jnp.einsum('hajc,qkc->haqjk', b, oh_c, precision=lax.Precision.HIGHEST)
        valid = valid_r[:, None, :, None] & valid_c[None, :, None, :]
        b = jnp.where(valid[None], b, -jnp.inf)
        tabs.append(b.reshape(NA_HEADS, NA_QROWS * GRID_W, NA_BAND_ROWS * GRID_W))
    return (jnp.stack(tabs) * _LOG2E).astype(BF16)


def _na_attn_kernel(q_ref, k_ref, v_ref, kc_ref, vc_ref, bias_ref, o_ref, *, rows):
    g = pl.program_id(1)
    band0 = jnp.clip(g * NA_QROWS - NA_WIN_R // 2, 0, rows - NA_BAND_ROWS)
    off = pl.multiple_of(band0 * GRID_W, GRID_W)
    band = NA_BAND_ROWS * GRID_W
    lo, m_e, m_o = _half_masks(BF16)
    qs, segs = [], []
    for p in range(q_ref.shape[1] // _LANES):
        sl = slice(p * _LANES, (p + 1) * _LANES)
        qp = q_ref[:, sl]
        kb = k_ref[pl.ds(off, band), sl]
        vb = v_ref[pl.ds(off, band), sl]
        for half, msk in enumerate((m_e, m_o)):
            qs.append(qp * msk)
            segs.append([(kb, vb, bias_ref[2 * p + half].astype(F32)), (kc_ref[:, sl], vc_ref[:, sl], None)])
    outs = []
    for i in range(0, len(qs), NA_HEADS_TOGETHER):
        outs += _attend(qs[i:i + NA_HEADS_TOGETHER], segs[i:i + NA_HEADS_TOGETHER])
    for p in range(q_ref.shape[1] // _LANES):
        o_ref[:, p * _LANES:(p + 1) * _LANES] = jnp.where(lo, outs[2 * p], outs[2 * p + 1]).astype(o_ref.dtype)


def _na_ctx_kernel(q_ref, kc_ref, vc_ref, o_ref):
    lo, m_e, m_o = _half_masks(BF16)
    qs, segs = [], []
    for p in range(q_ref.shape[1] // _LANES):
        sl = slice(p * _LANES, (p + 1) * _LANES)
        for msk in (m_e, m_o):
            qs.append(q_ref[:, sl] * msk)
            segs.append([(kc_ref[:, sl], vc_ref[:, sl], None)])
    outs = _attend(qs, segs)
    for p in range(q_ref.shape[1] // _LANES):
        o_ref[:, p * _LANES:(p + 1) * _LANES] = jnp.where(lo, outs[2 * p], outs[2 * p + 1]).astype(o_ref.dtype)


def _na_attention(qkv, bias, l, dims, need_ctx):
    b, s, c, w = dims.B, dims.S, dims.C, dims.W
    rows = s // GRID_W
    n_tiles = rows // NA_QROWS
    tq = NA_QROWS * GRID_W
    band = NA_BAND_ROWS * GRID_W
    ctx_blk = dims.BS // c

    def cls(g):
        return (g > 0).astype(I32) + (g == n_tiles - 1).astype(I32)

    ya = pl.pallas_call(
        functools.partial(_na_attn_kernel, rows=rows),
        grid=(b, n_tiles),
        in_specs=[pl.BlockSpec((tq, w), lambda bi, g: (bi * n_tiles + g, 0)),
                  pl.BlockSpec((s, w), lambda bi, g: (bi, 1)),
                  pl.BlockSpec((s, w), lambda bi, g: (bi, 2)),
                  pl.BlockSpec((c, w), lambda bi, g: (ctx_blk + bi, 1)),
                  pl.BlockSpec((c, w), lambda bi, g: (ctx_blk + bi, 2)),
                  pl.BlockSpec((None, None, NA_HEADS, tq, band), lambda bi, g: (l, cls(g), 0, 0, 0))],
        out_specs=pl.BlockSpec((tq, w), lambda bi, g: (bi * n_tiles + g, 0)),
        out_shape=jax.ShapeDtypeStruct((dims.BS, w), BF16),
        compiler_params=_cp("parallel", "arbitrary"),
    )(qkv, qkv, qkv, qkv, qkv, bias)
    if not need_ctx:
        return ya
    yc = pl.pallas_call(
        _na_ctx_kernel,
        grid=(b,),
        in_specs=[pl.BlockSpec((c, w), lambda bi: (ctx_blk + bi, 0)),
                  pl.BlockSpec((c, w), lambda bi: (ctx_blk + bi, 1)),
                  pl.BlockSpec((c, w), lambda bi: (ctx_blk + bi, 2))],
        out_specs=pl.BlockSpec((c, w), lambda bi: (bi, 0)),
        out_shape=jax.ShapeDtypeStruct((b * c, w), BF16),
        compiler_params=_cp("parallel"),
    )(qkv, qkv, qkv)
    return jnp.concatenate([ya, yc], axis=0)


def _rope_tables(s, tm):
    n_freq = MLA_ROPE // 4
    inv_freq = ROPE_THETA ** (-jnp.arange(n_freq, dtype=F32) / n_freq)
    t = jnp.arange(s)
    row = (t // GRID_W).astype(F32)[:, None] * inv_freq
    col = (t % GRID_W).astype(F32)[:, None] * inv_freq
    zeros = jnp.zeros((s, n_freq), F32)

    def slab(row_x1, row_x2, col_x1, col_x2, fill):
        body = jnp.concatenate([row_x1, row_x2, col_x1, col_x2], axis=-1)
        return jnp.concatenate([jnp.full((s, MLA_NOPE), fill, F32), body,
                                jnp.zeros((s, _LANES - MLA_QK), F32)], axis=-1)

    cr, sr, cc, sc = jnp.cos(row), jnp.sin(row), jnp.cos(col), jnp.sin(col)
    c_tab = slab(cr, cr, cc, cc, 1.0)
    s1_tab = slab(-sr, zeros, -sc, zeros, 0.0)
    s2_tab = slab(zeros, sr, zeros, sc, 0.0)
    ident = jnp.concatenate([jnp.ones((tm, MLA_QK), F32), jnp.zeros((tm, _LANES - MLA_QK), F32)], axis=-1)
    zero = jnp.zeros((tm, _LANES), F32)
    return (jnp.concatenate([c_tab, ident]), jnp.concatenate([s1_tab, zero]),
            jnp.concatenate([s2_tab, zero]))


def _mla_prep_kernel(f_ref, wq_ref, wk_ref, wv_ref, place_ref, gcq_ref, gckv_ref, gq_ref, gk_ref,
                     ct_ref, s1_ref, s2_ref, q_o, k_o, v_o):
    cq = f_ref[:, 0:MLA_Q_RANK]
    ckv = f_ref[:, MLA_Q_RANK:MLA_Q_RANK + MLA_KV_RANK]
    kr = f_ref[:, MLA_Q_RANK + MLA_KV_RANK:MLA_Q_RANK + MLA_KV_RANK + MLA_ROPE]

    def rms(x, g):
        return x * lax.rsqrt(jnp.mean(x * x, axis=-1, keepdims=True) + NORM_EPS) * g

    cqn = rms(cq, gcq_ref[...]).astype(BF16)
    ckvn = rms(ckv, gckv_ref[...]).astype(BF16)
    q = jnp.dot(cqn, wq_ref[...], preferred_element_type=F32)
    kn = jnp.dot(ckvn, wk_ref[...], preferred_element_type=F32)
    v_o[...] = jnp.dot(ckvn, wv_ref[...], preferred_element_type=F32).astype(v_o.dtype)
    kr_slab = _dot01_right(kr, place_ref[...])
    ct, s1, s2 = ct_ref[...], s1_ref[...], s2_ref[...]

    def head(x, g):
        x = x * lax.rsqrt(jnp.sum(x * x, axis=-1, keepdims=True) * (1.0 / MLA_QK) + NORM_EPS) * g
        return x * ct + pltpu.roll(x, _LANES - MLA_ROPE // 4, 1) * s1 + pltpu.roll(x, MLA_ROPE // 4, 1) * s2

    for h in range(MLA_HEADS):
        sl = slice(h * _LANES, (h + 1) * _LANES)
        q_o[h] = head(q[:, sl], gq_ref[...]).astype(q_o.dtype)
        k_o[h] = head(kn[:, sl] + kr_slab, gk_ref[...]).astype(k_o.dtype)


def _mla_prep(f_mla, pw, l, ropes, dims):
    m = f_mla.shape[0]
    tm = 256
    n_lat = dims.S // tm
    bs = dims.BS

    def rope_blk(i):
        return jnp.where(i * tm < bs, i % n_lat, n_lat)

    hw = MLA_HEADS * _LANES
    full = lambda shape: pl.BlockSpec(shape, lambda i: tuple(0 for _ in shape))
    lyr = lambda shape: pl.BlockSpec((None,) + shape, lambda i: (l,) + tuple(0 for _ in shape))
    return pl.pallas_call(
        _mla_prep_kernel,
        grid=(m // tm,),
        in_specs=[pl.BlockSpec((tm, f_mla.shape[1]), lambda i: (i, 0)),
                  lyr((MLA_Q_RANK, hw)), lyr((MLA_KV_RANK, hw)), lyr((MLA_KV_RANK, MLA_HEADS * MLA_V)),
                  full((MLA_ROPE, _LANES)),
                  lyr((1, MLA_Q_RANK)), lyr((1, MLA_KV_RANK)), lyr((1, _LANES)), lyr((1, _LANES)),
                  pl.BlockSpec((tm, _LANES), lambda i: (rope_blk(i), 0)),
                  pl.BlockSpec((tm, _LANES), lambda i: (rope_blk(i), 0)),
                  pl.BlockSpec((tm, _LANES), lambda i: (rope_blk(i), 0))],
        out_specs=[pl.BlockSpec((MLA_HEADS, tm, _LANES), lambda i: (0, i, 0)),
                   pl.BlockSpec((MLA_HEADS, tm, _LANES), lambda i: (0, i, 0)),
                   pl.BlockSpec((tm, MLA_HEADS * MLA_V), lambda i: (i, 0))],
        out_shape=[jax.ShapeDtypeStruct((MLA_HEADS, m, _LANES), BF16),
                   jax.ShapeDtypeStruct((MLA_HEADS, m, _LANES), BF16),
                   jax.ShapeDtypeStruct((m, MLA_HEADS * MLA_V), BF16)],
        compiler_params=_cp("parallel"),
    )(f_mla, pw["mla_wq"], pw["mla_wk"], pw["mla_wv"], pw["mla_place"], pw["mla_gcq"], pw["mla_gckv"],
      pw["mla_gq"], pw["mla_gk"], *ropes)


def _mla_attn_kernel(qe_ref, qo_ref, kce_ref, kco_ref, kle_ref, klo_ref, vc_ref, vl_ref, o_ref):
    lo, _, _ = _half_masks(F32)
    n_chunks = kle_ref.shape[0] // MLA_KEY_CHUNK

    def segs(kc_ref, kl_ref):
        out = [(kc_ref[...], vc_ref[...], None)]
        for j in range(n_chunks):
            sl = slice(j * MLA_KEY_CHUNK, (j + 1) * MLA_KEY_CHUNK)
            out.append((kl_ref[sl, :], vl_ref[sl, :], None))
        return out

    o_e, o_o = _attend([qe_ref[...], qo_ref[...]], [segs(kce_ref, kle_ref), segs(kco_ref, klo_ref)])
    o_ref[...] = jnp.where(lo, o_e, o_o).astype(o_ref.dtype)


def _mla_ctx_kernel(qe_ref, qo_ref, kce_ref, kco_ref, vc_ref, o_ref):
    lo, _, _ = _half_masks(F32)
    o_e, o_o = _attend([qe_ref[...], qo_ref[...]],
                       [[(kce_ref[...], vc_ref[...], None)], [(kco_ref[...], vc_ref[...], None)]])
    o_ref[...] = jnp.where(lo, o_e, o_o).astype(o_ref.dtype)


def _mla_attention(q, k, v, dims, need_ctx):
    b, s, c, w = dims.B, dims.S, dims.C, dims.W
    tq = MLA_Q_TILE
    nq = s // tq
    ctx_blk = dims.BS // c
    pairs = MLA_HEADS // 2
    hq = lambda half: pl.BlockSpec((None, tq, _LANES), lambda bi, p, i: (2 * p + half, bi * nq + i, 0))
    hkc = lambda half: pl.BlockSpec((None, c, _LANES), lambda bi, p, i: (2 * p + half, ctx_blk + bi, 0))
    hkl = lambda half: pl.BlockSpec((None, s, _LANES), lambda bi, p, i: (2 * p + half, bi, 0))
    yc_x = pl.pallas_call(
        _mla_attn_kernel,
        grid=(b, pairs, nq),
        in_specs=[hq(0), hq(1), hkc(0), hkc(1), hkl(0), hkl(1),
                  pl.BlockSpec((c, _LANES), lambda bi, p, i: (ctx_blk + bi, p)),
                  pl.BlockSpec((s, _LANES), lambda bi, p, i: (bi, p))],
        out_specs=pl.BlockSpec((tq, _LANES), lambda bi, p, i: (bi * nq + i, p)),
        out_shape=jax.ShapeDtypeStruct((dims.BS, w), BF16),
        compiler_params=_cp("parallel", "parallel", "arbitrary"),
    )(q, q, k, k, k, k, v, v)
    if not need_ctx:
        return yc_x
    cq = lambda half: pl.BlockSpec((None, c, _LANES), lambda bi, p: (2 * p + half, ctx_blk + bi, 0))
    yc_s = pl.pallas_call(
        _mla_ctx_kernel,
        grid=(b, pairs),
        in_specs=[cq(0), cq(1), cq(0), cq(1),
                  pl.BlockSpec((c, _LANES), lambda bi, p: (ctx_blk + bi, p))],
        out_specs=pl.BlockSpec((c, _LANES), lambda bi, p: (bi, p)),
        out_shape=jax.ShapeDtypeStruct((b * c, w), BF16),
        compiler_params=_cp("parallel", "parallel"),
    )(q, q, k, k, v)
    return jnp.concatenate([yc_x, yc_s], axis=0)


def _rw_prep_kernel(f_ref, fp_ref, fn_ref, mu_ref, w0_ref, w2_ref, a0_ref, a2_ref, g2_ref, kkg_ref,
                    ka_ref, bd_ref, r_o, v_o, kk_o, g_o, lw_o, kd_o, bb_o, *, s, c, bs, tm):
    w = r_o.shape[1]
    row0 = pl.program_id(0) * tm
    is_lat = row0 < bs
    pos = jnp.where(is_lat, row0 % s, (row0 - bs) % c)
    seglen = jnp.where(is_lat, s, c)
    f = f_ref[...]
    prev_row = jnp.where(pos == 0, 0.0, fp_ref[7:8, :])
    next_row = jnp.where(pos + tm == seglen, 0.0, fn_ref[0:1, :])
    ridx = lax.broadcasted_iota(I32, (tm, 1), 0)
    prev = jnp.where(ridx == 0, prev_row, pltpu.roll(f, 1, 0))
    nxt = jnp.where(ridx == tm - 1, next_row, pltpu.roll(f, tm - 1, 0))
    fm = f + mu_ref[0:1, :] * (prev - f) + mu_ref[1:2, :] * (nxt - f)
    r, k, v = fm[:, 0:w], fm[:, w:2 * w], fm[:, 2 * w:3 * w]
    o = 3 * w
    w_low = jnp.tanh(fm[:, o:o + 128])
    a_low = fm[:, o + 128:o + 256]
    g_low = _sigmoid(fm[:, o + 256:o + 384])
    log_w = -_softplus(-(w0_ref[...] + _dot(w_low, w2_ref[...]))) - 0.5
    lw = -jnp.exp(log_w)
    a = _sigmoid(a0_ref[...] + _dot(a_low, a2_ref[...]))
    g_o[...] = _dot(g_low, g2_ref[...])
    kk = k * kkg_ref[...]
    kk = kk / jnp.maximum(jnp.sqrt(_dot01_right(kk * kk, bd_ref[...])), 1e-12)
    r_o[...] = r
    v_o[...] = v
    kk_o[...] = kk
    for d in range(2):
        ad = a[:, d * w:(d + 1) * w]
        lw_o[d] = lw[:, d * w:(d + 1) * w]
        kd_o[d] = k * (1.0 + (ad - 1.0) * ka_ref[...])
        bb_o[d] = ad * kk


def _rw_prep(f_rw, pw, l, dims):
    m, cols = f_rw.shape
    w = dims.W
    tm = 256
    assert dims.C % tm == 0 and dims.S % tm == 0
    nb8 = m // 8
    lyr = lambda shape: pl.BlockSpec((None,) + shape, lambda i: (l,) + tuple(0 for _ in shape))
    out_w = pl.BlockSpec((tm, w), lambda i: (i, 0))
    out_2w = pl.BlockSpec((2, tm, w), lambda i: (0, i, 0))
    sd = jax.ShapeDtypeStruct
    return pl.pallas_call(
        functools.partial(_rw_prep_kernel, s=dims.S, c=dims.C, bs=dims.BS, tm=tm),
        grid=(m // tm,),
        in_specs=[pl.BlockSpec((tm, cols), lambda i: (i, 0)),
                  pl.BlockSpec((8, cols), lambda i: (jnp.maximum(i * (tm // 8) - 1, 0), 0)),
                  pl.BlockSpec((8, cols), lambda i: (jnp.minimum((i + 1) * (tm // 8), nb8 - 1), 0)),
                  lyr((2, cols)), lyr((1, 2 * w)), lyr((128, 2 * w)), lyr((1, 2 * w)), lyr((128, 2 * w)),
                  lyr((128, w)), lyr((1, w)), lyr((1, w)),
                  pl.BlockSpec((w, w), lambda i: (0, 0))],
        out_specs=[out_w, out_w, out_w, out_w, out_2w, out_2w, out_2w],
        out_shape=[sd((m, w), F32)] * 4 + [sd((2, m, w), F32)] * 3,
        compiler_params=_cp("parallel"),
    )(f_rw, f_rw, f_rw, pw["rw_mu"], pw["rw_w0"], pw["rw_w2"], pw["rw_a0"], pw["rw_a2"], pw["rw_g2"],
      pw["rw_kk"], pw["rw_ka"], _head_blockdiag(w, RW_HEAD_DIM))


def _rw_chunk_kernel(rf, vf, kkf, lwf, kdf, bbf, rb, vb, kkb, lwb, kdb, bbb, yf_ref, yb_ref, h_ref):
    c = RW_CHUNK

    @pl.when(pl.program_id(2) == 0)
    def _():
        h_ref[...] = jnp.zeros_like(h_ref)

    ii = lax.broadcasted_iota(I32, (c, c), 0)
    jj = lax.broadcasted_iota(I32, (c, c), 1)
    tri_f = jnp.where(jj <= ii, 1.0, 0.0).astype(BF16)
    tri_b = jnp.where(jj >= ii, 1.0, 0.0).astype(BF16)
    _, m_e, m_o = _half_masks(F32)

    n2 = 2 * c
    i2 = lax.broadcasted_iota(I32, (n2, n2), 0)
    j2 = lax.broadcasted_iota(I32, (n2, n2), 1)
    same = (i2 // c) == (j2 // c)
    eye2 = jnp.where(i2 == j2, 1.0, 0.0)
    eye = lax.broadcasted_iota(I32, (_LANES, _LANES), 0) == lax.broadcasted_iota(I32, (_LANES, _LANES), 1)
    steps = int(np.log2(c)) - 1

    def stack(x):
        return jnp.concatenate([x * m_e, x * m_o], axis=0).astype(BF16)

    def feats(r_ref, kk_ref, lw_ref, kd_ref, bb_ref, v_ref, sl, tri, last, lag):
        lw = lw_ref[:, sl]
        cum = _dot01_left(tri, lw)
        tot = cum[last:last + 1, :]
        w_inv = jnp.exp(-cum)
        w_end = jnp.exp(tot - cum)
        bb, kd = bb_ref[:, sl], kd_ref[:, sl]
        lag = jnp.where(same, lag, -1)
        return dict(la=stack(-kk_ref[:, sl] * jnp.exp(cum - lw)), lr=stack(r_ref[:, sl] * jnp.exp(cum)),
                    rb=stack(bb * w_inv), rk=stack(kd * w_inv), vm=stack(v_ref[:, sl]),
                    bh=stack(bb * w_end), kh=stack(kd * w_end), wtot=jnp.exp(tot),
                    strict=lag > 0, incl=lag >= 0)

    sls = [slice(p * _LANES, (p + 1) * _LANES) for p in range(h_ref.shape[0])]
    f = []
    for sl in sls:
        f.append(feats(rf, kkf, lwf, kdf, bbf, vf, sl, tri_f, c - 1, i2 - j2))
        f.append(feats(rb, kkb, lwb, kdb, bbb, vb, sl, tri_b, 0, j2 - i2))
    un = range(len(f))
    nmat = [jnp.where(f[k]["strict"], _dot_nt(f[k]["la"], f[k]["rb"]), 0.0) for k in un]
    aak = [jnp.where(f[k]["strict"], _dot_nt(f[k]["la"], f[k]["rk"]), 0.0) for k in un]
    arb = [jnp.where(f[k]["incl"], _dot_nt(f[k]["lr"], f[k]["rb"]), 0.0) for k in un]
    ark = [jnp.where(f[k]["incl"], _dot_nt(f[k]["lr"], f[k]["rk"]), 0.0) for k in un]
    t_inv = [eye2 + nmat[k] for k in un]
    pw = nmat
    for _ in range(steps):
        pw = [_dot(pw[k], pw[k]) for k in un]
        t_inv = [t_inv[k] + _dot(t_inv[k], pw[k]) for k in un]
    x = [_dot(t_inv[k], f[k]["la"]) for k in un]
    av = [_dot(aak[k], f[k]["vm"]) for k in un]
    u = [_dot(t_inv[k], av[k]) for k in un]
    q = [f[k]["lr"].astype(F32) + _dot(arb[k], x[k]) for k in un]
    y0 = [_dot(arb[k], u[k]) + _dot(ark[k], f[k]["vm"]) for k in un]
    h = [h_ref[k // 2, k % 2] for k in un]
    yst = [_dot(q[k], h[k]) + y0[k] for k in un]
    for k in un:
        out_ref = yb_ref if k % 2 else yf_ref
        out_ref[:, sls[k // 2]] = yst[k][0:c] + yst[k][c:n2]
    trans = [jnp.where(eye, f[k]["wtot"], 0.0) + _dot_tn(f[k]["bh"], x[k]) for k in un]
    for k in un:
        h_ref[k // 2, k % 2] = (_dot(trans[k], h[k]) + _dot_tn(f[k]["bh"], u[k])
                                + _dot_tn(f[k]["kh"], f[k]["vm"]))


def _rw_scan(r, v, kk, lw, kd, bb, dims):
    b, s, c, w = dims.B, dims.S, dims.C, dims.W
    ch = RW_CHUNK
    n_c, n_s = c // ch, s // ch
    ctx0 = dims.BS // ch
    pp = RW_PAIRS_PER_STEP
    bw = pp * _LANES
    assert w % bw == 0

    def cf(bi, t):
        return jnp.where(t < n_c, ctx0 + bi * n_c + t, bi * n_s + (t - n_c))

    def cb(bi, t):
        return jnp.where(t < n_c, ctx0 + bi * n_c + (n_c - 1 - t), bi * n_s + (n_s - 1 - (t - n_c)))

    sh = lambda fn: pl.BlockSpec((ch, bw), lambda bi, p, t: (fn(bi, t), p))
    dr = lambda fn, d: pl.BlockSpec((None, ch, bw), lambda bi, p, t: (d, fn(bi, t), p))
    m = dims.M
    return pl.pallas_call(
        _rw_chunk_kernel,
        grid=(b, w // bw, n_c + n_s),
        in_specs=[sh(cf), sh(cf), sh(cf), dr(cf, 0), dr(cf, 0), dr(cf, 0),
                  sh(cb), sh(cb), sh(cb), dr(cb, 1), dr(cb, 1), dr(cb, 1)],
        out_specs=[sh(cf), sh(cb)],
        out_shape=[jax.ShapeDtypeStruct((m, w), F32)] * 2,
        scratch_shapes=[pltpu.VMEM((pp, 2, _LANES, _LANES), F32)],
        compiler_params=_cp("parallel", "parallel", "arbitrary"),
    )(r, v, kk, lw, kd, bb, r, v, kk, lw, kd, bb)


def _rw_readout_kernel(yf_ref, yb_ref, r_ref, v_ref, g_ref, kd_ref, rk_ref, gg_ref, gb_ref, bd_ref, o_ref):
    bd = bd_ref[...]
    inv = 1.0 / RW_HEAD_DIM
    y = yf_ref[...] + yb_ref[...]
    yc = y - _dot01_right(y, bd) * inv
    var = _dot01_right(yc * yc, bd) * inv
    yn = yc * lax.rsqrt(var + RW_GN_EPS)
    bonus = _dot01_right(r_ref[...] * (kd_ref[0] + kd_ref[1]) * rk_ref[...], bd) * v_ref[...]
    o_ref[...] = ((yn * gg_ref[...] + gb_ref[...] + bonus) * g_ref[...]).astype(o_ref.dtype)


def _rw_readout(yf, yb, r, v, g, kd, pw, l, dims, rows):
    w = dims.W
    tm = 256
    blk = pl.BlockSpec((tm, w), lambda i: (i, 0))
    lyr = pl.BlockSpec((None, 1, w), lambda i: (l, 0, 0))
    return pl.pallas_call(
        _rw_readout_kernel,
        grid=(rows // tm,),
        in_specs=[blk, blk, blk, blk, blk, pl.BlockSpec((2, tm, w), lambda i: (0, i, 0)), lyr, lyr, lyr,
                  pl.BlockSpec((w, w), lambda i: (0, 0))],
        out_specs=blk,
        out_shape=jax.ShapeDtypeStruct((rows, w), BF16),
        compiler_params=_cp("parallel"),
    )(yf, yb, r, v, g, kd, pw["rw_rk"], pw["rw_gn_g"], pw["rw_gn_b"], _head_blockdiag(w, RW_HEAD_DIM))


def _s5_matrices(a_re, a_im, log_dt, b_re, b_im, c_re, c_im, d_skip):
    hi = lax.Precision.HIGHEST
    n_g, n_p, n_j = b_re.shape[1], b_re.shape[2], b_re.shape[3]
    ct = S5_CHUNK
    dt = jnp.exp(log_dt)[..., None]
    mag = jnp.exp(a_re * dt)
    lam_re, lam_im = mag * jnp.cos(a_im * dt), mag * jnp.sin(a_im * dt)
    den = a_re * a_re + a_im * a_im
    f_re = ((lam_re - 1.0) * a_re + lam_im * a_im) / den
    f_im = (lam_im * a_re - (lam_re - 1.0) * a_im) / den
    bb_re = f_re[..., None] * b_re - f_im[..., None] * b_im
    bb_im = f_re[..., None] * b_im + f_im[..., None] * b_re
    p_re, p_im = [jnp.ones_like(lam_re)], [jnp.zeros_like(lam_im)]
    for _ in range(ct):
        p_re.append(p_re[-1] * lam_re - p_im[-1] * lam_im)
        p_im.append(p_re[-2] * lam_im + p_im[-1] * lam_re)
    p_re, p_im = jnp.stack(p_re), jnp.stack(p_im)
    cl_re = c_re[None] * p_re[:, :, :, None, :] - c_im[None] * p_im[:, :, :, None, :]
    cl_im = c_re[None] * p_im[:, :, :, None, :] + c_im[None] * p_re[:, :, :, None, :]
    kern = jnp.sum(cl_re[..., None] * bb_re[None, :, :, None] - cl_im[..., None] * bb_im[None, :, :, None],
                   axis=4)
    j_idx = np.arange(ct)[:, None]
    t_idx = np.arange(ct)[None, :]
    lag_f, lag_b = t_idx - j_idx, j_idx - t_idx
    sel_f = jnp.asarray(lag_f[:, :, None] == np.arange(ct + 1), F32)
    sel_b = jnp.asarray(lag_b[:, :, None] == np.arange(ct + 1), F32)
    toep = (jnp.einsum('jtn,ngim->gjmti', sel_f, kern[:, 0], precision=hi)
            + jnp.einsum('jtn,ngim->gjmti', sel_b, kern[:, 1], precision=hi))
    skip = (jnp.eye(ct)[None, :, None, :, None] * jnp.eye(n_j)[None, None, :, None, :]
            * d_skip.reshape(n_g, 1, 1, 1, n_j))
    toep = (toep + skip).reshape(n_g, ct * n_j, ct * n_j)

    def powers(z, d, first, reverse):
        z = z[first:first + ct, d]
        return z[::-1] if reverse else z

    def state_in(first, reverse, d):
        pr, pi = powers(p_re, d, first, reverse), powers(p_im, d, first, reverse)
        re = pr[..., None] * bb_re[d][None] - pi[..., None] * bb_im[d][None]
        im = pr[..., None] * bb_im[d][None] + pi[..., None] * bb_re[d][None]
        fix = lambda z: z.transpose(1, 0, 3, 2).reshape(n_g, ct * n_j, n_p)
        return fix(re), fix(im)

    def state_out(first, reverse, d):
        fix = lambda z: z.transpose(1, 3, 0, 2).reshape(n_g, n_p, ct * n_j)
        return fix(powers(cl_re, d, first, reverse)), fix(-powers(cl_im, d, first, reverse))

    rf_re, rf_im = state_in(0, True, 0)
    rb_re, rb_im = state_in(0, False, 1)
    of_re, of_im = state_out(1, False, 0)
    ob_re, ob_im = state_out(1, True, 1)
    half = (np.arange(n_g) % 2)[:, None, None]

    def lane_half(z):
        zero = jnp.zeros_like(z)
        return jnp.concatenate([jnp.where(half == 0, z, zero), jnp.where(half == 1, z, zero)], axis=-1)

    def row_half(z):
        zero = jnp.zeros_like(z)
        return jnp.concatenate([jnp.where(half == 0, z, zero), jnp.where(half == 1, z, zero)], axis=1)

    r_re = jnp.stack([lane_half(rf_re), lane_half(rb_re)]).astype(BF16)
    r_im = jnp.stack([lane_half(rf_im), lane_half(rb_im)]).astype(BF16)
    o_re = jnp.stack([row_half(of_re), row_half(ob_re)]).astype(BF16)
    o_im = jnp.stack([row_half(of_im), row_half(ob_im)]).astype(BF16)
    lam_c = (p_re[ct].reshape(2, 1, n_g * n_p), p_im[ct].reshape(2, 1, n_g * n_p))
    return toep.astype(BF16), r_re, r_im, o_re, o_im, lam_c


def _s5_z_kernel(u_ref, rre_ref, rim_ref, zre_o, zim_o):
    u0, u1 = u_ref[0].astype(BF16), u_ref[1].astype(BF16)
    for d in range(2):
        zre_o[d] = (jnp.dot(u0, rre_ref[d, 0], preferred_element_type=F32)
                    + jnp.dot(u1, rre_ref[d, 1], preferred_element_type=F32))
        zim_o[d] = (jnp.dot(u0, rim_ref[d, 0], preferred_element_type=F32)
                    + jnp.dot(u1, rim_ref[d, 1], preferred_element_type=F32))


def _s5_scan_kernel(zre_ref, zim_ref, lre_ref, lim_ref, xre_o, xim_o, *, n_b, n_s, n_c):
    lanes = zre_ref.shape[2]
    chains = [(d, b) for d in range(2) for b in range(n_b)]
    lam = [(lre_ref[d], lim_ref[d]) for d in range(2)]

    def run(base_of, length, carry):
        def body(t, carry):
            new = []
            for (d, b), (xr, xi) in zip(chains, carry):
                row = base_of(b) + (t if d == 0 else length - 1 - t)
                xre_o[d, pl.ds(row, 1), :] = xr
                xim_o[d, pl.ds(row, 1), :] = xi
                lr, li = lam[d]
                new.append((lr * xr - li * xi + zre_ref[d, pl.ds(row, 1), :],
                            lr * xi + li * xr + zim_ref[d, pl.ds(row, 1), :]))
            return tuple(new)
        return lax.fori_loop(0, length, body, carry)

    zero = jnp.zeros((1, lanes), F32)
    carry = tuple((zero, zero) for _ in chains)
    carry = run(lambda b: n_b * n_s + b * n_c, n_c, carry)
    run(lambda b: b * n_s, n_s, carry)


def _s5_y_kernel(u_ref, toep_ref, xre_ref, xim_ref, ore_ref, oim_ref, y_o):
    xr = [xre_ref[d].astype(BF16) for d in range(2)]
    xi = [xim_ref[d].astype(BF16) for d in range(2)]
    for g in range(2):
        y = jnp.dot(u_ref[g].astype(BF16), toep_ref[g], preferred_element_type=F32)
        for d in range(2):
            y = y + jnp.dot(xr[d], ore_ref[d, g], preferred_element_type=F32)
            y = y + jnp.dot(xi[d], oim_ref[d, g], preferred_element_type=F32)
        y_o[g] = y.astype(y_o.dtype)


def _s5_glu_kernel(y_ref, w_ref, b_ref, o_ref, wbf_ref):
    @pl.when(pl.program_id(0) == 0)
    def _():
        wbf_ref[...] = w_ref[...].astype(BF16)

    y = y_ref[...].astype(F32)
    y = 0.5 * y * (1.0 + jnp.tanh(0.7978845608028654 * (y + 0.044715 * y * y * y)))
    z = jnp.dot(y.astype(BF16), wbf_ref[...], preferred_element_type=F32) + b_ref[...]
    o_ref[...] = (y * _sigmoid(z)).astype(o_ref.dtype)


def _s5_mixer(f_s5, mats, w_glu, b_glu3, l, dims, rows):
    toep, r_re, r_im, o_re, o_im, (lam_re, lam_im) = mats
    m, w = f_s5.shape
    ct = S5_CHUNK
    n_g = w // S5_GROUP
    n_ch = m // ct
    gw = ct * S5_GROUP
    u = f_s5.reshape(n_ch, ct, n_g, S5_GROUP).transpose(2, 0, 1, 3).reshape(n_g, n_ch, gw)
    sl = n_g * S5_STATE
    sd = jax.ShapeDtypeStruct
    zre, zim = pl.pallas_call(
        _s5_z_kernel,
        grid=(n_g // 2,),
        in_specs=[pl.BlockSpec((2, n_ch, gw), lambda g: (g, 0, 0)),
                  pl.BlockSpec((None, 2, 2, gw, _LANES), lambda g: (l, 0, g, 0, 0)),
                  pl.BlockSpec((None, 2, 2, gw, _LANES), lambda g: (l, 0, g, 0, 0))],
        out_specs=[pl.BlockSpec((2, n_ch, _LANES), lambda g: (0, 0, g))] * 2,
        out_shape=[sd((2, n_ch, sl), F32)] * 2,
        compiler_params=_cp("parallel"),
    )(u, r_re, r_im)
    lb = 512
    xre, xim = pl.pallas_call(
        functools.partial(_s5_scan_kernel, n_b=dims.B, n_s=dims.S // ct, n_c=dims.C // ct),
        grid=(sl // lb,),
        in_specs=[pl.BlockSpec((2, n_ch, lb), lambda j: (0, 0, j))] * 2
                 + [pl.BlockSpec((None, 2, 1, lb), lambda j: (l, 0, 0, j))] * 2,
        out_specs=[pl.BlockSpec((2, n_ch, lb), lambda j: (0, 0, j))] * 2,
        out_shape=[sd((2, n_ch, sl), F32)] * 2,
        compiler_params=_cp("parallel"),
    )(zre, zim, lam_re, lam_im)
    y = pl.pallas_call(
        _s5_y_kernel,
        grid=(n_g // 2,),
        in_specs=[pl.BlockSpec((2, n_ch, gw), lambda g: (g, 0, 0)),
                  pl.BlockSpec((None, 2, gw, gw), lambda g: (l, g, 0, 0)),
                  pl.BlockSpec((2, n_ch, _LANES), lambda g: (0, 0, g)),
                  pl.BlockSpec((2, n_ch, _LANES), lambda g: (0, 0, g)),
                  pl.BlockSpec((None, 2, 2, _LANES, gw), lambda g: (l, 0, g, 0, 0)),
                  pl.BlockSpec((None, 2, 2, _LANES, gw), lambda g: (l, 0, g, 0, 0))],
        out_specs=pl.BlockSpec((2, n_ch, gw), lambda g: (g, 0, 0)),
        out_shape=sd((n_g, n_ch, gw), BF16),
        compiler_params=_cp("parallel"),
    )(u, toep, xre, xim, o_re, o_im)
    y = y.reshape(n_g, n_ch, ct, S5_GROUP).transpose(1, 2, 0, 3).reshape(m, w)
    tm = 256
    return pl.pallas_call(
        _s5_glu_kernel,
        grid=(rows // tm,),
        in_specs=[pl.BlockSpec((tm, w), lambda i: (i, 0)),
                  pl.BlockSpec((None, w, w), lambda i: (l, 0, 0)),
                  pl.BlockSpec((None, 1, w), lambda i: (l, 0, 0))],
        out_specs=pl.BlockSpec((tm, w), lambda i: (i, 0)),
        out_shape=sd((rows, w), BF16),
        scratch_shapes=[pltpu.VMEM((w, w), BF16)],
        compiler_params=_cp("arbitrary"),
    )(y, w_glu, b_glu3)


def _merge_kernel(xn_ref, ya_ref, yb_ref, yc_ref, yd_ref, wg0, wg1, wg2, wg3, bg0, bg1, bg2, bg3, wb_ref,
                  o_ref, wg_bf, wb_bf):
    wgs = (wg0, wg1, wg2, wg3)

    @pl.when(pl.program_id(1) == 0)
    def _():
        for n in range(4):
            wg_bf[n] = wgs[n][...].astype(BF16)
            wb_bf[n] = wb_ref[n].astype(BF16)

    xn = xn_ref[...]
    acc = None
    for n, (y_ref, bg) in enumerate(zip((ya_ref, yb_ref, yc_ref, yd_ref), (bg0, bg1, bg2, bg3))):
        gate = _sigmoid(jnp.dot(xn, wg_bf[n], preferred_element_type=F32) + bg[...])
        up = jnp.dot(y_ref[...], wb_bf[n], preferred_element_type=F32)
        acc = gate * up if acc is None else acc + gate * up
    o_ref[...] = acc.astype(o_ref.dtype)


def _merge(xn, ys, w_gate, b_gate3, w_branch, l, rows):
    d = xn.shape[1]
    w = ys[0].shape[1]
    tn = MERGE_TN
    tm = _pick(rows, (544, 512, 256))
    nb = d // tn
    once = pl.Buffered(1)
    wg = lambda n: pl.BlockSpec((None, d, tn), lambda j, i: (l, 0, n * nb + j), pipeline_mode=once)
    bg = lambda n: pl.BlockSpec((None, 1, tn), lambda j, i: (l, 0, n * nb + j))
    yspec = pl.BlockSpec((tm, w), lambda j, i: (i, 0))
    return pl.pallas_call(
        _merge_kernel,
        grid=(nb, rows // tm),
        in_specs=[pl.BlockSpec((tm, d), lambda j, i: (i, 0)), yspec, yspec, yspec, yspec,
                  wg(0), wg(1), wg(2), wg(3), bg(0), bg(1), bg(2), bg(3),
                  pl.BlockSpec((None, 4, w, tn), lambda j, i: (l, 0, 0, j), pipeline_mode=once)],
        out_specs=pl.BlockSpec((tm, tn), lambda j, i: (i, j)),
        out_shape=jax.ShapeDtypeStruct((rows, d), BF16),
        scratch_shapes=[pltpu.VMEM((4, d, tn), BF16), pltpu.VMEM((4, w, tn), BF16)],
        compiler_params=_cp("arbitrary", "arbitrary"),
    )(xn, *ys, w_gate, w_gate, w_gate, w_gate, b_gate3, b_gate3, b_gate3, b_gate3, w_branch)


def _router_kernel(x_ref, g_ref, sc_ref, sh_ref, wr_ref, br_ref, idx_o, gate_o, rank_o, cnt_o, hp_o, carry):
    tm = x_ref.shape[0]
    x = x_ref[...]
    h = x * lax.rsqrt(jnp.mean(x * x, axis=-1, keepdims=True) + NORM_EPS) * g_ref[...]
    h = h * (1.0 + sc_ref[...]) + sh_ref[...]
    hp_o[...] = _pack_pairs(h)

    @pl.when(pl.program_id(0) == 0)
    def _():
        carry[...] = jnp.zeros_like(carry)

    h_hi, h_lo = _split(h, 2)
    w_hi, w_lo = _split(wr_ref[...], 2)
    logits = (jnp.dot(h_hi, w_hi, preferred_element_type=F32) + jnp.dot(h_lo, w_hi, preferred_element_type=F32)
              + jnp.dot(h_hi, w_lo, preferred_element_type=F32)) + br_ref[...]
    lane = lax.broadcasted_iota(I32, (tm, _LANES), 1)
    vals, idxs, hots = [], [], []
    cur = logits
    for _ in range(TOP_K):
        mx = jnp.max(cur, axis=-1, keepdims=True)
        ix = jnp.min(jnp.where(cur == mx, lane, _LANES), axis=-1, keepdims=True)
        hot = lane == ix
        vals.append(mx)
        idxs.append(ix)
        hots.append(hot)
        cur = jnp.where(hot, -jnp.inf, cur)
    ex = [jnp.exp(v - vals[0]) for v in vals]
    inv = 1.0 / sum(ex)
    sel = sum(jnp.where(h, 1.0, 0.0) for h in hots)
    ri = lax.broadcasted_iota(I32, (tm, tm), 0)
    ci = lax.broadcasted_iota(I32, (tm, tm), 1)
    earlier = jnp.where(ci < ri, 1.0, 0.0).astype(BF16)
    before = jnp.dot(earlier, sel.astype(BF16), preferred_element_type=F32) + carry[...]
    idx_out = jnp.zeros((tm, _LANES), I32)
    gate_out = jnp.zeros((tm, _LANES), F32)
    rank_out = jnp.zeros((tm, _LANES), I32)
    for k in range(TOP_K):
        rk = jnp.sum(jnp.where(hots[k], before, 0.0), axis=-1, keepdims=True)
        idx_out = jnp.where(lane == k, idxs[k], idx_out)
        gate_out = jnp.where(lane == k, ex[k] * inv, gate_out)
        rank_out = jnp.where(lane == k, rk.astype(I32), rank_out)
    idx_o[...] = idx_out
    gate_o[...] = gate_out
    rank_o[...] = rank_out
    carry[...] = carry[...] + jnp.sum(sel, axis=0, keepdims=True)
    cnt_o[...] = jnp.broadcast_to(carry[...], cnt_o.shape)


def _router(x, gain3, mod, wr_pad, br_pad, l, dims, rows):
    d = x.shape[1]
    tm = 256
    seg = dims.seg_fn(tm)
    blk = pl.BlockSpec((tm, _LANES), lambda i: (i, 0))
    sd = jax.ShapeDtypeStruct
    return pl.pallas_call(
        _router_kernel,
        grid=(rows // tm,),
        in_specs=[pl.BlockSpec((tm, d), lambda i: (i, 0)),
                  pl.BlockSpec((None, 1, d), lambda i: (l, 0, 0)),
                  pl.BlockSpec((None, 1, d), lambda i: (seg(i), 0, 4)),
                  pl.BlockSpec((None, 1, d), lambda i: (seg(i), 0, 3)),
                  pl.BlockSpec((None, d, _LANES), lambda i: (l, 0, 0)),
                  pl.BlockSpec((None, 1, _LANES), lambda i: (l, 0, 0))],
        out_specs=[blk, blk, blk, pl.BlockSpec((8, _LANES), lambda i: (0, 0)),
                   pl.BlockSpec((tm, d // 2), lambda i: (i, 0))],
        out_shape=[sd((rows, _LANES), I32), sd((rows, _LANES), F32), sd((rows, _LANES), I32),
                   sd((8, _LANES), F32), sd((rows, d // 2), jnp.uint32)],
        scratch_shapes=[pltpu.VMEM((1, _LANES), F32)],
        compiler_params=_cp("arbitrary"),
    )(x, gain3, mod, mod, wr_pad, br_pad)


def _row_copy(src_hbm, row, dst, slot, sem):
    return pltpu.make_async_copy(src_hbm.at[pl.ds(row, 1), :], dst.at[pl.ds(slot, 1), :], sem)


def _issue_rows(idx_ref, n, src_hbm, dst, sem):
    for r in range(n):
        _row_copy(src_hbm, idx_ref[0, r], dst, r, sem).start()


def _expert_kernel(be_ref, first_ref, used_ref, tok_ref, tok_next_ref, h_hbm, wgu_ref, bgu_ref, wdn_ref,
                   bdn_ref, o_ref, xbuf0, xbuf1, sems, wgu_bf, wdn_bf):
    i = pl.program_id(0)
    n_used = used_ref[0]
    blk = xbuf0.shape[0]
    bufs = (xbuf0, xbuf1)

    @pl.when(i == 0)
    def _():
        _issue_rows(tok_ref, blk, h_hbm, xbuf0, sems.at[0])

    @pl.when(first_ref[i] == 1)
    def _():
        wgu_bf[...] = wgu_ref[...].astype(BF16)
        wdn_bf[...] = wdn_ref[...].astype(BF16)

    for par in range(2):
        cur, nxt = bufs[par], bufs[1 - par]
        mine = i % 2 == par

        @pl.when(mine & (i <= n_used))
        def _():
            pltpu.make_async_copy(h_hbm.at[pl.ds(0, blk), :], cur, sems.at[par]).wait()

        @pl.when(mine & (i < n_used))
        def _():
            _issue_rows(tok_next_ref, blk, h_hbm, nxt, sems.at[1 - par])
            x = jnp.concatenate(_unpack_pairs(cur[...]), axis=1).astype(BF16)
            gu = jnp.dot(x, wgu_bf[...], preferred_element_type=F32) + bgu_ref[...]
            g_lin = jnp.minimum(gu[:, :D_EXPERT], SWIGLU_LIMIT)
            u_lin = jnp.clip(gu[:, D_EXPERT:], -SWIGLU_LIMIT, SWIGLU_LIMIT)
            act = (u_lin + 1.0) * g_lin * _sigmoid(SWIGLU_ALPHA * g_lin)
            y = jnp.dot(act.astype(BF16), wdn_bf[...], preferred_element_type=F32) + bdn_ref[...]
            o_ref[...] = _pack_pairs(y)

    @pl.when(i >= n_used)
    def _():
        o_ref[...] = jnp.zeros_like(o_ref)


def _moe_experts(h, slot_tok3, block_e, first, n_used, w_gu, b_gu4, w_dn, b_dn4, l):
    n_blocks = slot_tok3.shape[0] - 1
    blk = slot_tok3.shape[2]
    dp = h.shape[1]
    d = 2 * dp
    de2 = w_gu.shape[3]
    grid_spec = pltpu.PrefetchScalarGridSpec(
        num_scalar_prefetch=3,
        grid=(n_blocks,),
        in_specs=[pl.BlockSpec((None, 1, blk), lambda i, be, fi, us: (i, 0, 0), memory_space=pltpu.SMEM),
                  pl.BlockSpec((None, 1, blk), lambda i, be, fi, us: (i + 1, 0, 0), memory_space=pltpu.SMEM),
                  pl.BlockSpec(memory_space=pl.ANY),
                  pl.BlockSpec((None, None, d, de2), lambda i, be, fi, us: (l, be[i], 0, 0)),
                  pl.BlockSpec((None, None, 1, de2), lambda i, be, fi, us: (l, be[i], 0, 0)),
                  pl.BlockSpec((None, None, de2 // 2, d), lambda i, be, fi, us: (l, be[i], 0, 0)),
                  pl.BlockSpec((None, None, 1, d), lambda i, be, fi, us: (l, be[i], 0, 0))],
        out_specs=pl.BlockSpec((blk, dp), lambda i, be, fi, us: (i, 0)),
        scratch_shapes=[pltpu.VMEM((blk, dp), jnp.uint32), pltpu.VMEM((blk, dp), jnp.uint32),
                        pltpu.SemaphoreType.DMA((2,)),
                        pltpu.VMEM((d, de2), BF16), pltpu.VMEM((de2 // 2, d), BF16)])
    return pl.pallas_call(
        _expert_kernel,
        grid_spec=grid_spec,
        out_shape=jax.ShapeDtypeStruct((n_blocks * blk, dp), jnp.uint32),
        compiler_params=_cp("arbitrary"),
    )(block_e, first, n_used, slot_tok3, slot_tok3, h, w_gu, b_gu4, w_dn, b_dn4)


def _combine_kernel(dest_ref, dest_next_ref, gate_ref, x_ref, g2_ref, ys_hbm, o_ref, buf, sems):
    tm = x_ref.shape[0]
    i = pl.program_id(0)
    slot = i % 2

    def issue(idx_ref, s):
        for r in range(tm):
            for k in range(TOP_K):
                _row_copy(ys_hbm, idx_ref[0, r * TOP_K + k], buf.at[s, k], r, sems.at[s]).start()

    @pl.when(i == 0)
    def _():
        issue(dest_ref, 0)

    @pl.when(i + 1 < pl.num_programs(0))
    def _():
        issue(dest_next_ref, 1 - slot)

    for k in range(TOP_K):
        pltpu.make_async_copy(ys_hbm.at[pl.ds(0, tm), :], buf.at[slot, k], sems.at[slot]).wait()
    gate = gate_ref[...]
    n = buf.shape[3]
    f_lo = f_hi = None
    for k in range(TOP_K):
        lo, hi = _unpack_pairs(buf[slot, k])
        gk = gate[:, k:k + 1]
        f_lo = gk * lo if f_lo is None else f_lo + gk * lo
        f_hi = gk * hi if f_hi is None else f_hi + gk * hi
    o_ref[:, :n] = x_ref[:, :n] + g2_ref[:, :n] * f_lo
    o_ref[:, n:] = x_ref[:, n:] + g2_ref[:, n:] * f_hi


def _moe_combine(x, ys, dest3, gate, mod, dims, rows):
    d = x.shape[1]
    tm = MOE_COMBINE_ROWS
    seg = dims.seg_fn(tm)
    return pl.pallas_call(
        _combine_kernel,
        grid=(rows // tm,),
        in_specs=[pl.BlockSpec((None, 1, tm * TOP_K), lambda i: (i, 0, 0), memory_space=pltpu.SMEM),
                  pl.BlockSpec((None, 1, tm * TOP_K), lambda i: (i + 1, 0, 0), memory_space=pltpu.SMEM),
                  pl.BlockSpec((tm, _LANES), lambda i: (i, 0)),
                  pl.BlockSpec((tm, d), lambda i: (i, 0)),
                  pl.BlockSpec((None, 1, d), lambda i: (seg(i), 0, 5)),
                  pl.BlockSpec(memory_space=pl.ANY)],
        out_specs=pl.BlockSpec((tm, d), lambda i: (i, 0)),
        out_shape=jax.ShapeDtypeStruct((rows, d), F32),
        scratch_shapes=[pltpu.VMEM((2, TOP_K, tm, d // 2), jnp.uint32), pltpu.SemaphoreType.DMA((2,))],
        compiler_params=_cp("arbitrary"),
    )(dest3, dest3, gate, x, mod, ys)


def _moe(x, mod, pw, l, dims, rows):
    idx, gate, rank, cnt, hp = _router(x, pw["norm2_g"], mod, pw["moe_wr"], pw["moe_br"], l, dims, rows)
    blk = MOE_BLOCK
    n_blocks = -(-(rows * TOP_K) // blk) + N_EXPERTS + 1
    counts = cnt[0, :N_EXPERTS].astype(I32)
    padded = (counts + blk - 1) // blk * blk
    pad_end = jnp.cumsum(padded)
    pad_start = pad_end - padded
    top_i = idx[:, :TOP_K]
    start_of = jnp.sum(jnp.where(top_i[..., None] == jnp.arange(N_EXPERTS, dtype=I32), pad_start, 0), axis=-1)
    dest = start_of + rank[:, :TOP_K]
    tok = jnp.broadcast_to(jnp.arange(rows, dtype=I32)[:, None], dest.shape)
    fill = jnp.arange((n_blocks + 1) * blk, dtype=I32) % rows
    slot_tok = fill.at[dest.reshape(-1)].set(tok.reshape(-1))
    block_start = jnp.arange(n_blocks, dtype=I32) * blk
    block_e = jnp.minimum(jnp.sum((pad_end[None, :] <= block_start[:, None]).astype(I32), axis=1), N_EXPERTS - 1)
    first = jnp.concatenate([jnp.ones((1,), I32), (block_e[1:] != block_e[:-1]).astype(I32)])
    n_used = (pad_end[-1:] // blk).astype(I32)
    ys = _moe_experts(hp, slot_tok.reshape(n_blocks + 1, 1, blk), block_e, first, n_used, pw["moe_w_gu"],
                      pw["moe_b_gu"], pw["moe_w_dn"], pw["moe_b_dn"], l)
    dest3 = dest.reshape(rows // MOE_COMBINE_ROWS, 1, MOE_COMBINE_ROWS * TOP_K)
    dest3 = jnp.concatenate([dest3, jnp.zeros_like(dest3[:1])], axis=0)
    return _moe_combine(x, ys, dest3, gate, mod, dims, rows)


def _prep_params(p, dims):
    n_l = p["w_in"].shape[0]
    w = dims.W
    o_mla = 3 * w + (3 * w + 384)
    o_s5 = o_mla + MLA_Q_RANK + MLA_KV_RANK + MLA_ROPE
    pw = dict(p)
    row3 = lambda a: a.reshape(a.shape[0], 1, -1)
    pw["norm1_g"], pw["norm2_g"] = row3(p["norm1_g"]), row3(p["norm2_g"])
    pw["b_mod"], pw["b_gate"] = row3(p["b_mod"]), row3(p["b_gate"])
    pw["w_in_mla"] = p["w_in"][:, :, o_mla:o_s5]
    pw["w_in_s5"] = p["w_in"][:, :, o_s5:]
    zeros = jnp.zeros_like(p["rw_w2"][:, 0])
    bdiag = lambda x: jnp.concatenate([jnp.concatenate([x[:, 0], zeros], axis=-1),
                                       jnp.concatenate([zeros, x[:, 1]], axis=-1)], axis=1)
    pw["rw_w2"], pw["rw_a2"] = bdiag(p["rw_w2"]), bdiag(p["rw_a2"])
    pw["rw_w0"], pw["rw_a0"] = row3(p["rw_w0"]), row3(p["rw_a0"])
    pw["rw_kk"], pw["rw_ka"] = row3(p["rw_kk"]), row3(p["rw_ka"])
    pw["rw_rk"], pw["rw_gn_g"], pw["rw_gn_b"] = row3(p["rw_rk"]), row3(p["rw_gn_g"]), row3(p["rw_gn_b"])
    pad_h = lambda x, n: jnp.pad(x, ((0, 0), (0, 0), (0, 0), (0, _LANES - n)))
    wq = p["mla_wq_up"].reshape(n_l, MLA_Q_RANK, MLA_HEADS, MLA_QK)
    pw["mla_wq"] = pad_h(wq, MLA_QK).reshape(n_l, MLA_Q_RANK, -1).astype(BF16)
    wkv = p["mla_wkv_up"].reshape(n_l, MLA_KV_RANK, MLA_HEADS, MLA_NOPE + MLA_V)
    pw["mla_wk"] = pad_h(wkv[..., :MLA_NOPE], MLA_NOPE).reshape(n_l, MLA_KV_RANK, -1).astype(BF16)
    pw["mla_wv"] = wkv[..., MLA_NOPE:].reshape(n_l, MLA_KV_RANK, -1).astype(BF16)
    pw["mla_place"] = jnp.asarray(np.arange(MLA_ROPE)[:, None] + MLA_NOPE == np.arange(_LANES)[None, :], BF16)
    pw["mla_gcq"], pw["mla_gckv"] = row3(p["mla_qn_g"]), row3(p["mla_kvn_g"])
    pad_g = lambda g: jnp.pad(g, ((0, 0), (0, _LANES - MLA_QK))).reshape(n_l, 1, _LANES)
    pw["mla_gq"] = pad_g(p["mla_qkn_q"] * (MLA_QK ** -0.5 * _LOG2E))
    pw["mla_gk"] = pad_g(p["mla_qkn_k"])
    pw["s5_b_glu"] = row3(p["s5_b_glu"])
    pw["moe_wr"] = jnp.pad(p["moe_wr"], ((0, 0), (0, 0), (0, _LANES - N_EXPERTS)))
    pw["moe_br"] = jnp.pad(p["moe_br"], ((0, 0), (0, _LANES - N_EXPERTS)), constant_values=-1e30)[:, None, :]
    pw["moe_b_gu"] = p["moe_b_gu"][:, :, None, :]
    pw["moe_b_dn"] = p["moe_b_dn"][:, :, None, :]
    return pw


def _residual_epilogue(acc, x, g):
    return x + g * acc


def _layer(xs, l, mod, pw, ropes, dims, need_ctx):
    m, d, w = dims.M, dims.D, dims.W
    rows = m if need_ctx else dims.BS
    tm_all = _pick(m, (1088, 1024, 512, 256))
    xn = _norm_mod(xs, pw["norm1_g"], l, mod, 0, 1, dims, m, BF16)
    f_na = _mm(xn, pw["w_in"], l, tm=tm_all, tn=768, col0=0, ncols=3 * w)
    f_rw = _mm(xn, pw["w_in"], l, tm=tm_all, tn=384, col0=3 * w, ncols=3 * w + 384)
    f_mla = _mm(xn, pw["w_in_mla"], l, tm=tm_all, tn=pw["w_in_mla"].shape[2])
    f_s5 = _mm(xn, pw["w_in_s5"], l, tm=tm_all, tn=w, out_dtype=BF16)

    qkv = _na_prep(f_na, pw["na_qn_g"][l], pw["na_kn_g"][l], dims)
    ya = _na_attention(qkv, pw["na_bias"], l, dims, need_ctx)

    r, v, kk, g, lw, kd, bb = _rw_prep(f_rw, pw, l, dims)
    yf, yb_ = _rw_scan(r, v, kk, lw, kd, bb, dims)
    yb = _rw_readout(yf, yb_, r, v, g, kd, pw, l, dims, rows)

    q, k, vv = _mla_prep(f_mla, pw, l, ropes, dims)
    yc = _mla_attention(q, k, vv, dims, need_ctx)

    yd = _s5_mixer(f_s5, pw["s5_mats"], pw["s5_w_glu"], pw["s5_b_glu"], l, dims, rows)

    merged = _merge(xn, (ya, yb, yc, yd), pw["w_gate"], pw["b_gate"], pw["w_branch"], l, rows)
    tm_seg = _pick(dims.S, (512, 256))
    seg = dims.seg_fn(tm_seg)
    tn = 1024
    xs = _mm(merged, pw["w_out"], l, tm=tm_seg, tn=tn, rows=rows, epilogue=_residual_epilogue,
             extras=((xs, pl.BlockSpec((tm_seg, tn), lambda j, i: (i, j))),
                     (mod, pl.BlockSpec((None, 1, tn), lambda j, i: (seg(i), 0, 2 * (d // tn) + j)))))
    return _moe(xs, mod, pw, l, dims, rows)


def kernel(x, c, ctx, c_ctx, w_mod, b_mod, norm1_g, norm2_g, w_in, w_gate, b_gate, w_branch, w_out, na_qn_g, na_kn_g, na_rpb, rw_mu, rw_w0, rw_w2, rw_a0, rw_a2, rw_g2, rw_kk, rw_ka, rw_rk, rw_gn_g, rw_gn_b, mla_qn_g, mla_wq_up, mla_kvn_g, mla_wkv_up, mla_qkn_q, mla_qkn_k, s5_a_re, s5_a_im, s5_log_dt, s5_b_re, s5_b_im, s5_c_re, s5_c_im, s5_d, s5_w_glu, s5_b_glu, moe_wr, moe_br, moe_w_gu, moe_b_gu, moe_w_dn, moe_b_dn):
    params = dict(w_mod=w_mod, b_mod=b_mod, norm1_g=norm1_g, norm2_g=norm2_g, w_in=w_in, w_gate=w_gate,
                  b_gate=b_gate, w_branch=w_branch, w_out=w_out, na_qn_g=na_qn_g, na_kn_g=na_kn_g,
                  na_rpb=na_rpb, rw_mu=rw_mu, rw_w0=rw_w0, rw_w2=rw_w2, rw_a0=rw_a0, rw_a2=rw_a2,
                  rw_g2=rw_g2, rw_kk=rw_kk, rw_ka=rw_ka, rw_rk=rw_rk, rw_gn_g=rw_gn_g, rw_gn_b=rw_gn_b,
                  mla_qn_g=mla_qn_g, mla_wq_up=mla_wq_up, mla_kvn_g=mla_kvn_g, mla_wkv_up=mla_wkv_up,
                  mla_qkn_q=mla_qkn_q, mla_qkn_k=mla_qkn_k, s5_a_re=s5_a_re, s5_a_im=s5_a_im,
                  s5_log_dt=s5_log_dt, s5_b_re=s5_b_re, s5_b_im=s5_b_im, s5_c_re=s5_c_re, s5_c_im=s5_c_im,
                  s5_d=s5_d, s5_w_glu=s5_w_glu, s5_b_glu=s5_b_glu, moe_wr=moe_wr, moe_br=moe_br,
                  moe_w_gu=moe_w_gu, moe_b_gu=moe_b_gu, moe_w_dn=moe_w_dn, moe_b_dn=moe_b_dn)
    b, s, d = x.shape
    c_len = ctx.shape[1]
    depth = w_mod.shape[0]
    dims = _Dims(b, s, c_len, d)
    assert b + 1 <= 8
    pw = _prep_params(params, dims)
    pw["na_bias"] = jax.vmap(lambda rpb: _na_bias_tables(rpb, s // GRID_W))(na_rpb)
    pw["s5_mats"] = jax.vmap(_s5_matrices)(s5_a_re, s5_a_im, s5_log_dt, s5_b_re, s5_b_im, s5_c_re, s5_c_im, s5_d)
    cond = jnp.concatenate([jax.nn.silu(c), jax.nn.silu(c_ctx)[None],
                            jnp.zeros((8 - b - 1, d), F32)], axis=0)
    ropes = _rope_tables(s, 256)
    xs = jnp.concatenate([x.reshape(b * s, d), ctx.reshape(b * c_len, d)], axis=0)
    for l in range(depth):
        mod = _mm(cond, pw["w_mod"], l, tm=8, tn=1024, bias=pw["b_mod"]).reshape(8, 1, 6 * d)
        xs = _layer(xs, l, mod, pw, ropes, dims, need_ctx=l < depth - 1)
    return xs[:b * s].reshape(b, s, d)
```

```python
import functools

import numpy as np
import jax
import jax.numpy as jnp
from jax import lax
from jax.experimental import pallas as pl
from jax.experimental.pallas import tpu as pltpu

F32 = jnp.float32
BF16 = jnp.bfloat16
I32 = jnp.int32

_VMEM_LIMIT_BYTES = 56 * 1024 * 1024
_LANES = 128
_LOG2E = 1.4426950408889634

NORM_EPS = 1e-6
GRID_W = 64
NA_HEADS = 8
NA_HEAD_DIM = 64
NA_WIN_R = 8
NA_WIN_C = 16
NA_QROWS = 4
NA_BAND_ROWS = NA_QROWS + NA_WIN_R - 1
NA_HEADS_TOGETHER = 8
RW_HEAD_DIM = 64
RW_GN_EPS = 64e-5
RW_CHUNK = 64
RW_PAIRS_PER_STEP = 4
MLA_HEADS = 8
MLA_NOPE = 64
MLA_ROPE = 32
MLA_QK = MLA_NOPE + MLA_ROPE
MLA_V = 64
MLA_Q_RANK = 512
MLA_KV_RANK = 256
ROPE_THETA = 10000.0
MLA_KEY_CHUNK = 256
MLA_Q_TILE = 1024
S5_GROUP = 16
S5_STATE = 64
S5_CHUNK = 16
N_EXPERTS = 32
TOP_K = 4
D_EXPERT = 512
SWIGLU_LIMIT = 7.0
SWIGLU_ALPHA = 1.702
MOE_BLOCK = 512
MOE_COMBINE_ROWS = 256

_NT = (((1,), (1,)), ((), ()))
_TN = (((0,), (0,)), ((), ()))


def _cp(*sem):
    return pltpu.CompilerParams(dimension_semantics=sem, vmem_limit_bytes=_VMEM_LIMIT_BYTES)


def _dot(a, b):
    return jnp.dot(a.astype(BF16), b.astype(BF16), preferred_element_type=F32)


def _dot_nt(a, b):
    return lax.dot_general(a.astype(BF16), b.astype(BF16), _NT, preferred_element_type=F32)


def _dot_tn(a, b):
    return lax.dot_general(a.astype(BF16), b.astype(BF16), _TN, preferred_element_type=F32)


def _split(x, parts):
    out = []
    for _ in range(parts):
        h = x.astype(BF16)
        out.append(h)
        x = x - h.astype(F32)
    return out


def _dot01_right(x, m01, parts=2):
    return sum(jnp.dot(h, m01, preferred_element_type=F32) for h in _split(x, parts))


def _dot01_left(m01, x, parts=3):
    return sum(jnp.dot(m01, h, preferred_element_type=F32) for h in _split(x, parts))


def _pack_pairs(x):
    n = x.shape[1] // 2
    bits = pltpu.bitcast(x.astype(BF16).astype(F32), jnp.uint32)
    return (bits[:, :n] >> 16) | bits[:, n:]


def _unpack_pairs(w):
    return pltpu.bitcast(w << 16, F32), pltpu.bitcast(w & jnp.uint32(0xFFFF0000), F32)


def _sigmoid(x):
    return 1.0 / (1.0 + jnp.exp(-x))


def _softplus(x):
    return jnp.maximum(x, 0.0) + jnp.log(1.0 + jnp.exp(-jnp.abs(x)))


def _pick(n, cands):
    for c in cands:
        if n % c == 0:
            return c
    raise ValueError(f"no tile for {n} in {cands}")


def _head_blockdiag(width, head):
    i = np.arange(width)
    return jnp.asarray((i[:, None] // head) == (i[None, :] // head), BF16)


def _mm_kernel(*refs, has_bias, n_extra, epilogue):
    a_ref, w_ref = refs[0], refs[1]
    pos = 2
    bias_ref = None
    if has_bias:
        bias_ref = refs[pos]
        pos += 1
    extra = refs[pos:pos + n_extra]
    o_ref, wbf_ref = refs[pos + n_extra], refs[pos + n_extra + 1]

    @pl.when(pl.program_id(1) == 0)
    def _():
        wbf_ref[...] = w_ref[...].astype(BF16)

    acc = jnp.dot(a_ref[...].astype(BF16), wbf_ref[...], preferred_element_type=F32)
    if has_bias:
        acc = acc + bias_ref[...]
    if epilogue is not None:
        acc = epilogue(acc, *[e[...] for e in extra])
    o_ref[...] = acc.astype(o_ref.dtype)


def _mm(a, w, l, *, tm, tn, col0=0, ncols=None, rows=None, bias=None, extras=(), epilogue=None,
        out_dtype=F32):
    k = a.shape[1]
    rows = a.shape[0] if rows is None else rows
    ncols = w.shape[2] - col0 if ncols is None else ncols
    assert col0 % tn == 0 and ncols % tn == 0 and rows % tm == 0 and w.shape[1] == k
    c0 = col0 // tn
    in_specs = [pl.BlockSpec((tm, k), lambda j, i: (i, 0)),
                pl.BlockSpec((None, k, tn), lambda j, i: (l, 0, j + c0))]
    args = [a, w]
    if bias is not None:
        in_specs.append(pl.BlockSpec((None, 1, tn), lambda j, i: (l, 0, j + c0)))
        args.append(bias)
    for arr, spec in extras:
        in_specs.append(spec)
        args.append(arr)
    return pl.pallas_call(
        functools.partial(_mm_kernel, has_bias=bias is not None, n_extra=len(extras), epilogue=epilogue),
        grid=(ncols // tn, rows // tm),
        in_specs=in_specs,
        out_specs=pl.BlockSpec((tm, tn), lambda j, i: (i, j)),
        out_shape=jax.ShapeDtypeStruct((rows, ncols), out_dtype),
        scratch_shapes=[pltpu.VMEM((k, tn), BF16)],
        compiler_params=_cp("arbitrary", "arbitrary"),
    )(*args)


def _norm_mod_kernel(x_ref, g_ref, sc_ref, sh_ref, o_ref):
    x = x_ref[...]
    y = x * lax.rsqrt(jnp.mean(x * x, axis=-1, keepdims=True) + NORM_EPS) * g_ref[...]
    o_ref[...] = (y * (1.0 + sc_ref[...]) + sh_ref[...]).astype(o_ref.dtype)


def _norm_mod(x, gain3, l, mod, which_sh, which_sc, dims, rows, out_dtype):
    d = x.shape[1]
    tm = 256
    seg = dims.seg_fn(tm)
    return pl.pallas_call(
        _norm_mod_kernel,
        grid=(rows // tm,),
        in_specs=[pl.BlockSpec((tm, d), lambda i: (i, 0)),
                  pl.BlockSpec((None, 1, d), lambda i: (l, 0, 0)),
                  pl.BlockSpec((None, 1, d), lambda i: (seg(i), 0, which_sc)),
                  pl.BlockSpec((None, 1, d), lambda i: (seg(i), 0, which_sh))],
        out_specs=pl.BlockSpec((tm, d), lambda i: (i, 0)),
        out_shape=jax.ShapeDtypeStruct((rows, d), out_dtype),
        compiler_params=_cp("parallel"),
    )(x, gain3, mod, mod)


class _Dims:
    def __init__(self, b, s, c, d):
        self.B, self.S, self.C, self.D = b, s, c, d
        self.BS = b * s
        self.M = b * s + b * c
        self.W = d // 4

    def seg_fn(self, tm):
        assert self.S % tm == 0 and self.M % tm == 0
        s, b = self.S, self.B
        return lambda i: jnp.minimum((i * tm) // s, b)


def _attend(qs, segs_of):
    n = len(qs)
    m, l, acc = [None] * n, [None] * n, [None] * n
    for j in range(len(segs_of[0])):
        for i in range(n):
            k, v, bias = segs_of[i][j]
            s = lax.dot_general(qs[i], k, _NT, preferred_element_type=F32)
            if bias is not None:
                s = s + bias
            smax = jnp.max(s, axis=-1, keepdims=True)
            if j == 0:
                m[i] = smax
                p = jnp.exp2(s - smax)
                l[i] = jnp.sum(p, axis=-1, keepdims=True)
                acc[i] = jnp.dot(p.astype(BF16), v, preferred_element_type=F32)
            else:
                m_new = jnp.maximum(m[i], smax)
                alpha = jnp.exp2(m[i] - m_new)
                p = jnp.exp2(s - m_new)
                l[i] = alpha * l[i] + jnp.sum(p, axis=-1, keepdims=True)
                acc[i] = alpha * acc[i] + jnp.dot(p.astype(BF16), v, preferred_element_type=F32)
                m[i] = m_new
    return [acc[i] * (1.0 / l[i]) for i in range(n)]


def _half_masks(dtype):
    lane = lax.broadcasted_iota(I32, (1, _LANES), 1)
    lo = lane < 64
    return lo, jnp.where(lo, 1.0, 0.0).astype(dtype), jnp.where(lo, 0.0, 1.0).astype(dtype)


def _na_prep_kernel(f_ref, gq_ref, gk_ref, bd_ref, o_ref):
    w = gq_ref.shape[1]
    bd = bd_ref[...]

    def head_norm(x, g):
        ss = _dot01_right(x * x, bd)
        return x * lax.rsqrt(ss * (1.0 / NA_HEAD_DIM) + NORM_EPS) * g

    o_ref[:, 0:w] = head_norm(f_ref[:, 0:w], gq_ref[...]).astype(o_ref.dtype)
    o_ref[:, w:2 * w] = head_norm(f_ref[:, w:2 * w], gk_ref[...]).astype(o_ref.dtype)
    o_ref[:, 2 * w:3 * w] = f_ref[:, 2 * w:3 * w].astype(o_ref.dtype)


def _na_prep(f_na, qn_g, kn_g, dims):
    m, w3 = f_na.shape
    w = w3 // 3
    tm = 256
    gq = (jnp.tile(qn_g, NA_HEADS) * (NA_HEAD_DIM ** -0.5 * _LOG2E)).reshape(1, w)
    gk = jnp.tile(kn_g, NA_HEADS).reshape(1, w)
    return pl.pallas_call(
        _na_prep_kernel,
        grid=(m // tm,),
        in_specs=[pl.BlockSpec((tm, w3), lambda i: (i, 0)),
                  pl.BlockSpec((1, w), lambda i: (0, 0)),
                  pl.BlockSpec((1, w), lambda i: (0, 0)),
                  pl.BlockSpec((w, w), lambda i: (0, 0))],
        out_specs=pl.BlockSpec((tm, w3), lambda i: (i, 0)),
        out_shape=jax.ShapeDtypeStruct((m, w3), BF16),
        compiler_params=_cp("parallel"),
    )(f_na, gq, gk, _head_blockdiag(w, NA_HEAD_DIM))


def _na_bias_tables(rpb, rows):
    n_tiles = rows // NA_QROWS
    assert rows % NA_QROWS == 0 and rows >= NA_BAND_ROWS + NA_QROWS and NA_QROWS * 2 <= NA_WIN_R
    tabs = []
    for g in (0, 1, n_tiles - 1):
        r = g * NA_QROWS + np.arange(NA_QROWS)
        band0 = int(np.clip(g * NA_QROWS - NA_WIN_R // 2, 0, rows - NA_BAND_ROWS))
        key_row = band0 + np.arange(NA_BAND_ROWS)
        row_start = np.clip(r - NA_WIN_R // 2, 0, rows - NA_WIN_R)
        valid_r = (key_row[None, :] >= row_start[:, None]) & (key_row[None, :] < row_start[:, None] + NA_WIN_R)
        d_row = np.clip(key_row[None, :] - r[:, None] + NA_WIN_R - 1, 0, 2 * NA_WIN_R - 2)
        qc = np.arange(GRID_W)
        col_start = np.clip(qc - NA_WIN_C // 2, 0, GRID_W - NA_WIN_C)
        kc = np.arange(GRID_W)
        valid_c = (kc[None, :] >= col_start[:, None]) & (kc[None, :] < col_start[:, None] + NA_WIN_C)
        d_col = np.clip(kc[None, :] - qc[:, None] + NA_WIN_C - 1, 0, 2 * NA_WIN_C - 2)
        oh_r = jnp.asarray(d_row[:, :, None] == np.arange(2 * NA_WIN_R - 1), F32)
        oh_c = jnp.asarray(d_col[:, :, None] == np.arange(2 * NA_WIN_C - 1), F32)
        b = jnp.einsum('ajr,hrc->hajc', oh_r, rpb, precision=lax.Precision.HIGHEST)
        b = jnp.einsum('hajc,qkc->haqjk', b, oh_c, precision=lax.Precision.HIGHEST)
        valid = valid_r[:, None, :, None] & valid_c[None, :, None, :]
        b = jnp.where(valid[None], b, -jnp.inf)
        tabs.append(b.reshape(NA_HEADS, NA_QROWS * GRID_W, NA_BAND_ROWS * GRID_W))
    return (jnp.stack(tabs) * _LOG2E).astype(BF16)


def _na_attn_kernel(q_ref, k_ref, v_ref, kc_ref, vc_ref, bias_ref, o_ref, *, rows):
    g = pl.program_id(1)
    band0 = jnp.clip(g * NA_QROWS - NA_WIN_R // 2, 0, rows - NA_BAND_ROWS)
    off = pl.multiple_of(band0 * GRID_W, GRID_W)
    band = NA_BAND_ROWS * GRID_W
    lo, m_e, m_o = _half_masks(BF16)
    qs, segs = [], []
    for p in range(q_ref.shape[1] // _LANES):
        sl = slice(p * _LANES, (p + 1) * _LANES)
        qp = q_ref[:, sl]
        kb = k_ref[pl.ds(off, band), sl]
        vb = v_ref[pl.ds(off, band), sl]
        for half, msk in enumerate((m_e, m_o)):
            qs.append(qp * msk)
            segs.append([(kb, vb, bias_ref[2 * p + half].astype(F32)), (kc_ref[:, sl], vc_ref[:, sl], None)])
    outs = []
    for i in range(0, len(qs), NA_HEADS_TOGETHER):
        outs += _attend(qs[i:i + NA_HEADS_TOGETHER], segs[i:i + NA_HEADS_TOGETHER])
    for p in range(q_ref.shape[1] // _LANES):
        o_ref[:, p * _LANES:(p + 1) * _LANES] = jnp.where(lo, outs[2 * p], outs[2 * p + 1]).astype(o_ref.dtype)


def _na_ctx_kernel(q_ref, kc_ref, vc_ref, o_ref):
    lo, m_e, m_o = _half_masks(BF16)
    qs, segs = [], []
    for p in range(q_ref.shape[1] // _LANES):
        sl = slice(p * _LANES, (p + 1) * _LANES)
        for msk in (m_e, m_o):
            qs.append(q_ref[:, sl] * msk)
            segs.append([(kc_ref[:, sl], vc_ref[:, sl], None)])
    outs = _attend(qs, segs)
    for p in range(q_ref.shape[1] // _LANES):
        o_ref[:, p * _LANES:(p + 1) * _LANES] = jnp.where(lo, outs[2 * p], outs[2 * p + 1]).astype(o_ref.dtype)


def _na_attention(qkv, bias, l, dims, need_ctx):
    b, s, c, w = dims.B, dims.S, dims.C, dims.W
    rows = s // GRID_W
    n_tiles = rows // NA_QROWS
    tq = NA_QROWS * GRID_W
    band = NA_BAND_ROWS * GRID_W
    ctx_blk = dims.BS // c

    def cls(g):
        return (g > 0).astype(I32) + (g == n_tiles - 1).astype(I32)

    ya = pl.pallas_call(
        functools.partial(_na_attn_kernel, rows=rows),
        grid=(b, n_tiles),
        in_specs=[pl.BlockSpec((tq, w), lambda bi, g: (bi * n_tiles + g, 0)),
                  pl.BlockSpec((s, w), lambda bi, g: (bi, 1)),
                  pl.BlockSpec((s, w), lambda bi, g: (bi, 2)),
                  pl.BlockSpec((c, w), lambda bi, g: (ctx_blk + bi, 1)),
                  pl.BlockSpec((c, w), lambda bi, g: (ctx_blk + bi, 2)),
                  pl.BlockSpec((None, None, NA_HEADS, tq, band), lambda bi, g: (l, cls(g), 0, 0, 0))],
        out_specs=pl.BlockSpec((tq, w), lambda bi, g: (bi * n_tiles + g, 0)),
        out_shape=jax.ShapeDtypeStruct((dims.BS, w), BF16),
        compiler_params=_cp("parallel", "arbitrary"),
    )(qkv, qkv, qkv, qkv, qkv, bias)
    if not need_ctx:
        return ya
    yc = pl.pallas_call(
        _na_ctx_kernel,
        grid=(b,),
        in_specs=[pl.BlockSpec((c, w), lambda bi: (ctx_blk + bi, 0)),
                  pl.BlockSpec((c, w), lambda bi: (ctx_blk + bi, 1)),
                  pl.BlockSpec((c, w), lambda bi: (ctx_blk + bi, 2))],
        out_specs=pl.BlockSpec((c, w), lambda bi: (bi, 0)),
        out_shape=jax.ShapeDtypeStruct((b * c, w), BF16),
        compiler_params=_cp("parallel"),
    )(qkv, qkv, qkv)
    return jnp.concatenate([ya, yc], axis=0)


def _rope_tables(s, tm):
    n_freq = MLA_ROPE // 4
    inv_freq = ROPE_THETA ** (-jnp.arange(n_freq, dtype=F32) / n_freq)
    t = jnp.arange(s)
    row = (t // GRID_W).astype(F32)[:, None] * inv_freq
    col = (t % GRID_W).astype(F32)[:, None] * inv_freq
    zeros = jnp.zeros((s, n_freq), F32)

    def slab(row_x1, row_x2, col_x1, col_x2, fill):
        body = jnp.concatenate([row_x1, row_x2, col_x1, col_x2], axis=-1)
        return jnp.concatenate([jnp.full((s, MLA_NOPE), fill, F32), body,
                                jnp.zeros((s, _LANES - MLA_QK), F32)], axis=-1)

    cr, sr, cc, sc = jnp.cos(row), jnp.sin(row), jnp.cos(col), jnp.sin(col)
    c_tab = slab(cr, cr, cc, cc, 1.0)
    s1_tab = slab(-sr, zeros, -sc, zeros, 0.0)
    s2_tab = slab(zeros, sr, zeros, sc, 0.0)
    ident = jnp.concatenate([jnp.ones((tm, MLA_QK), F32), jnp.zeros((tm, _LANES - MLA_QK), F32)], axis=-1)
    zero = jnp.zeros((tm, _LANES), F32)
    return (jnp.concatenate([c_tab, ident]), jnp.concatenate([s1_tab, zero]),
            jnp.concatenate([s2_tab, zero]))


def _mla_prep_kernel(f_ref, wq_ref, wk_ref, wv_ref, place_ref, gcq_ref, gckv_ref, gq_ref, gk_ref,
                     ct_ref, s1_ref, s2_ref, q_o, k_o, v_o):
    cq = f_ref[:, 0:MLA_Q_RANK]
    ckv = f_ref[:, MLA_Q_RANK:MLA_Q_RANK + MLA_KV_RANK]
    kr = f_ref[:, MLA_Q_RANK + MLA_KV_RANK:MLA_Q_RANK + MLA_KV_RANK + MLA_ROPE]

    def rms(x, g):
        return x * lax.rsqrt(jnp.mean(x * x, axis=-1, keepdims=True) + NORM_EPS) * g

    cqn = rms(cq, gcq_ref[...]).astype(BF16)
    ckvn = rms(ckv, gckv_ref[...]).astype(BF16)
    q = jnp.dot(cqn, wq_ref[...], preferred_element_type=F32)
    kn = jnp.dot(ckvn, wk_ref[...], preferred_element_type=F32)
    v_o[...] = jnp.dot(ckvn, wv_ref[...], preferred_element_type=F32).astype(v_o.dtype)
    kr_slab = _dot01_right(kr, place_ref[...])
    ct, s1, s2 = ct_ref[...], s1_ref[...], s2_ref[...]

    def head(x, g):
        x = x * lax.rsqrt(jnp.sum(x * x, axis=-1, keepdims=True) * (1.0 / MLA_QK) + NORM_EPS) * g
        return x * ct + pltpu.roll(x, _LANES - MLA_ROPE // 4, 1) * s1 + pltpu.roll(x, MLA_ROPE // 4, 1) * s2

    for h in range(MLA_HEADS):
        sl = slice(h * _LANES, (h + 1) * _LANES)
        q_o[h] = head(q[:, sl], gq_ref[...]).astype(q_o.dtype)
        k_o[h] = head(kn[:, sl] + kr_slab, gk_ref[...]).astype(k_o.dtype)


def _mla_prep(f_mla, pw, l, ropes, dims):
    m = f_mla.shape[0]
    tm = 256
    n_lat = dims.S // tm
    bs = dims.BS

    def rope_blk(i):
        return jnp.where(i * tm < bs, i % n_lat, n_lat)

    hw = MLA_HEADS * _LANES
    full = lambda shape: pl.BlockSpec(shape, lambda i: tuple(0 for _ in shape))
    lyr = lambda shape: pl.BlockSpec((None,) + shape, lambda i: (l,) + tuple(0 for _ in shape))
    return pl.pallas_call(
        _mla_prep_kernel,
        grid=(m // tm,),
        in_specs=[pl.BlockSpec((tm, f_mla.shape[1]), lambda i: (i, 0)),
                  lyr((MLA_Q_RANK, hw)), lyr((MLA_KV_RANK, hw)), lyr((MLA_KV_RANK, MLA_HEADS * MLA_V)),
                  full((MLA_ROPE, _LANES)),
                  lyr((1, MLA_Q_RANK)), lyr((1, MLA_KV_RANK)), lyr((1, _LANES)), lyr((1, _LANES)),
                  pl.BlockSpec((tm, _LANES), lambda i: (rope_blk(i), 0)),
                  pl.BlockSpec((tm, _LANES), lambda i: (rope_blk(i), 0)),
                  pl.BlockSpec((tm, _LANES), lambda i: (rope_blk(i), 0))],
        out_specs=[pl.BlockSpec((MLA_HEADS, tm, _LANES), lambda i: (0, i, 0)),
                   pl.BlockSpec((MLA_HEADS, tm, _LANES), lambda i: (0, i, 0)),
                   pl.BlockSpec((tm, MLA_HEADS * MLA_V), lambda i: (i, 0))],
        out_shape=[jax.ShapeDtypeStruct((MLA_HEADS, m, _LANES), BF16),
                   jax.ShapeDtypeStruct((MLA_HEADS, m, _LANES), BF16),
                   jax.ShapeDtypeStruct((m, MLA_HEADS * MLA_V), BF16)],
        compiler_params=_cp("parallel"),
    )(f_mla, pw["mla_wq"], pw["mla_wk"], pw["mla_wv"], pw["mla_place"], pw["mla_gcq"], pw["mla_gckv"],
      pw["mla_gq"], pw["mla_gk"], *ropes)


def _mla_attn_kernel(qe_ref, qo_ref, kce_ref, kco_ref, kle_ref, klo_ref, vc_ref, vl_ref, o_ref):
    lo, _, _ = _half_masks(F32)
    n_chunks = kle_ref.shape[0] // MLA_KEY_CHUNK

    def segs(kc_ref, kl_ref):
        out = [(kc_ref[...], vc_ref[...], None)]
        for j in range(n_chunks):
            sl = slice(j * MLA_KEY_CHUNK, (j + 1) * MLA_KEY_CHUNK)
            out.append((kl_ref[sl, :], vl_ref[sl, :], None))
        return out

    o_e, o_o = _attend([qe_ref[...], qo_ref[...]], [segs(kce_ref, kle_ref), segs(kco_ref, klo_ref)])
    o_ref[...] = jnp.where(lo, o_e, o_o).astype(o_ref.dtype)


def _mla_ctx_kernel(qe_ref, qo_ref, kce_ref, kco_ref, vc_ref, o_ref):
    lo, _, _ = _half_masks(F32)
    o_e, o_o = _attend([qe_ref[...], qo_ref[...]],
                       [[(kce_ref[...], vc_ref[...], None)], [(kco_ref[...], vc_ref[...], None)]])
    o_ref[...] = jnp.where(lo, o_e, o_o).astype(o_ref.dtype)


def _mla_attention(q, k, v, dims, need_ctx):
    b, s, c, w = dims.B, dims.S, dims.C, dims.W
    tq = MLA_Q_TILE
    nq = s // tq
    ctx_blk = dims.BS // c
    pairs = MLA_HEADS // 2
    hq = lambda half: pl.BlockSpec((None, tq, _LANES), lambda bi, p, i: (2 * p + half, bi * nq + i, 0))
    hkc = lambda half: pl.BlockSpec((None, c, _LANES), lambda bi, p, i: (2 * p + half, ctx_blk + bi, 0))
    hkl = lambda half: pl.BlockSpec((None, s, _LANES), lambda bi, p, i: (2 * p + half, bi, 0))
    yc_x = pl.pallas_call(
        _mla_attn_kernel,
        grid=(b, pairs, nq),
        in_specs=[hq(0), hq(1), hkc(0), hkc(1), hkl(0), hkl(1),
                  pl.BlockSpec((c, _LANES), lambda bi, p, i: (ctx_blk + bi, p)),
                  pl.BlockSpec((s, _LANES), lambda bi, p, i: (bi, p))],
        out_specs=pl.BlockSpec((tq, _LANES), lambda bi, p, i: (bi * nq + i, p)),
        out_shape=jax.ShapeDtypeStruct((dims.BS, w), BF16),
        compiler_params=_cp("parallel", "parallel", "arbitrary"),
    )(q, q, k, k, k, k, v, v)
    if not need_ctx:
        return yc_x
    cq = lambda half: pl.BlockSpec((None, c, _LANES), lambda bi, p: (2 * p + half, ctx_blk + bi, 0))
    yc_s = pl.pallas_call(
        _mla_ctx_kernel,
        grid=(b, pairs),
        in_specs=[cq(0), cq(1), cq(0), cq(1),
                  pl.BlockSpec((c, _LANES), lambda bi, p: (ctx_blk + bi, p))],
        out_specs=pl.BlockSpec((c, _LANES), lambda bi, p: (bi, p)),
        out_shape=jax.ShapeDtypeStruct((b * c, w), BF16),
        compiler_params=_cp("parallel", "parallel"),
    )(q, q, k, k, v)
    return jnp.concatenate([yc_x, yc_s], axis=0)


def _rw_prep_kernel(f_ref, fp_ref, fn_ref, mu_ref, w0_ref, w2_ref, a0_ref, a2_ref, g2_ref, kkg_ref,
                    ka_ref, bd_ref, r_o, v_o, kk_o, g_o, lw_o, kd_o, bb_o, *, s, c, bs, tm):
    w = r_o.shape[1]
    row0 = pl.program_id(0) * tm
    is_lat = row0 < bs
    pos = jnp.where(is_lat, row0 % s, (row0 - bs) % c)
    seglen = jnp.where(is_lat, s, c)
    f = f_ref[...]
    prev_row = jnp.where(pos == 0, 0.0, fp_ref[7:8, :])
    next_row = jnp.where(pos + tm == seglen, 0.0, fn_ref[0:1, :])
    ridx = lax.broadcasted_iota(I32, (tm, 1), 0)
    prev = jnp.where(ridx == 0, prev_row, pltpu.roll(f, 1, 0))
    nxt = jnp.where(ridx == tm - 1, next_row, pltpu.roll(f, tm - 1, 0))
    fm = f + mu_ref[0:1, :] * (prev - f) + mu_ref[1:2, :] * (nxt - f)
    r, k, v = fm[:, 0:w], fm[:, w:2 * w], fm[:, 2 * w:3 * w]
    o = 3 * w
    w_low = jnp.tanh(fm[:, o:o + 128])
    a_low = fm[:, o + 128:o + 256]
    g_low = _sigmoid(fm[:, o + 256:o + 384])
    log_w = -_softplus(-(w0_ref[...] + _dot(w_low, w2_ref[...]))) - 0.5
    lw = -jnp.exp(log_w)
    a = _sigmoid(a0_ref[...] + _dot(a_low, a2_ref[...]))
    g_o[...] = _dot(g_low, g2_ref[...])
    kk = k * kkg_ref[...]
    kk = kk / jnp.maximum(jnp.sqrt(_dot01_right(kk * kk, bd_ref[...])), 1e-12)
    r_o[...] = r
    v_o[...] = v
    kk_o[...] = kk
    for d in range(2):
        ad = a[:, d * w:(d + 1) * w]
        lw_o[d] = lw[:, d * w:(d + 1) * w]
        kd_o[d] = k * (1.0 + (ad - 1.0) * ka_ref[...])
        bb_o[d] = ad * kk


def _rw_prep(f_rw, pw, l, dims):
    m, cols = f_rw.shape
    w = dims.W
    tm = 256
    assert dims.C % tm == 0 and dims.S % tm == 0
    nb8 = m // 8
    lyr = lambda shape: pl.BlockSpec((None,) + shape, lambda i: (l,) + tuple(0 for _ in shape))
    out_w = pl.BlockSpec((tm, w), lambda i: (i, 0))
    out_2w = pl.BlockSpec((2, tm, w), lambda i: (0, i, 0))
    sd = jax.ShapeDtypeStruct
    return pl.pallas_call(
        functools.partial(_rw_prep_kernel, s=dims.S, c=dims.C, bs=dims.BS, tm=tm),
        grid=(m // tm,),
        in_specs=[pl.BlockSpec((tm, cols), lambda i: (i, 0)),
                  pl.BlockSpec((8, cols), lambda i: (jnp.maximum(i * (tm // 8) - 1, 0), 0)),
                  pl.BlockSpec((8, cols), lambda i: (jnp.minimum((i + 1) * (tm // 8), nb8 - 1), 0)),
                  lyr((2, cols)), lyr((1, 2 * w)), lyr((128, 2 * w)), lyr((1, 2 * w)), lyr((128, 2 * w)),
                  lyr((128, w)), lyr((1, w)), lyr((1, w)),
                  pl.BlockSpec((w, w), lambda i: (0, 0))],
        out_specs=[out_w, out_w, out_w, out_w, out_2w, out_2w, out_2w],
        out_shape=[sd((m, w), F32)] * 4 + [sd((2, m, w), F32)] * 3,
        compiler_params=_cp("parallel"),
    )(f_rw, f_rw, f_rw, pw["rw_mu"], pw["rw_w0"], pw["rw_w2"], pw["rw_a0"], pw["rw_a2"], pw["rw_g2"],
      pw["rw_kk"], pw["rw_ka"], _head_blockdiag(w, RW_HEAD_DIM))


def _rw_chunk_kernel(rf, vf, kkf, lwf, kdf, bbf, rb, vb, kkb, lwb, kdb, bbb, yf_ref, yb_ref, h_ref):
    c = RW_CHUNK

    @pl.when(pl.program_id(2) == 0)
    def _():
        h_ref[...] = jnp.zeros_like(h_ref)

    ii = lax.broadcasted_iota(I32, (c, c), 0)
    jj = lax.broadcasted_iota(I32, (c, c), 1)
    tri_f = jnp.where(jj <= ii, 1.0, 0.0).astype(BF16)
    tri_b = jnp.where(jj >= ii, 1.0, 0.0).astype(BF16)
    _, m_e, m_o = _half_masks(F32)

    n2 = 2 * c
    i2 = lax.broadcasted_iota(I32, (n2, n2), 0)
    j2 = lax.broadcasted_iota(I32, (n2, n2), 1)
    same = (i2 // c) == (j2 // c)
    eye2 = jnp.where(i2 == j2, 1.0, 0.0)
    eye = lax.broadcasted_iota(I32, (_LANES, _LANES), 0) == lax.broadcasted_iota(I32, (_LANES, _LANES), 1)
    steps = int(np.log2(c)) - 1

    def stack(x):
        return jnp.concatenate([x * m_e, x * m_o], axis=0).astype(BF16)

    def feats(r_ref, kk_ref, lw_ref, kd_ref, bb_ref, v_ref, sl, tri, last, lag):
        lw = lw_ref[:, sl]
        cum = _dot01_left(tri, lw)
        tot = cum[last:last + 1, :]
        w_inv = jnp.exp(-cum)
        w_end = jnp.exp(tot - cum)
        bb, kd = bb_ref[:, sl], kd_ref[:, sl]
        lag = jnp.where(same, lag, -1)
        return dict(la=stack(-kk_ref[:, sl] * jnp.exp(cum - lw)), lr=stack(r_ref[:, sl] * jnp.exp(cum)),
                    rb=stack(bb * w_inv), rk=stack(kd * w_inv), vm=stack(v_ref[:, sl]),
                    bh=stack(bb * w_end), kh=stack(kd * w_end), wtot=jnp.exp(tot),
                    strict=lag > 0, incl=lag >= 0)

    sls = [slice(p * _LANES, (p + 1) * _LANES) for p in range(h_ref.shape[0])]
    f = []
    for sl in sls:
        f.append(feats(rf, kkf, lwf, kdf, bbf, vf, sl, tri_f, c - 1, i2 - j2))
        f.append(feats(rb, kkb, lwb, kdb, bbb, vb, sl, tri_b, 0, j2 - i2))
    un = range(len(f))
    nmat = [jnp.where(f[k]["strict"], _dot_nt(f[k]["la"], f[k]["rb"]), 0.0) for k in un]
    aak = [jnp.where(f[k]["strict"], _dot_nt(f[k]["la"], f[k]["rk"]), 0.0) for k in un]
    arb = [jnp.where(f[k]["incl"], _dot_nt(f[k]["lr"], f[k]["rb"]), 0.0) for k in un]
    ark = [jnp.where(f[k]["incl"], _dot_nt(f[k]["lr"], f[k]["rk"]), 0.0) for k in un]
    t_inv = [eye2 + nmat[k] for k in un]
    pw = nmat
    for _ in range(steps):
        pw = [_dot(pw[k], pw[k]) for k in un]
        t_inv = [t_inv[k] + _dot(t_inv[k], pw[k]) for k in un]
    x = [_dot(t_inv[k], f[k]["la"]) for k in un]
    av = [_dot(aak[k], f[k]["vm"]) for k in un]
    u = [_dot(t_inv[k], av[k]) for k in un]
    q = [f[k]["lr"].astype(F32) + _dot(arb[k], x[k]) for k in un]
    y0 = [_dot(arb[k], u[k]) + _dot(ark[k], f[k]["vm"]) for k in un]
    h = [h_ref[k // 2, k % 2] for k in un]
    yst = [_dot(q[k], h[k]) + y0[k] for k in un]
    for k in un:
        out_ref = yb_ref if k % 2 else yf_ref
        out_ref[:, sls[k // 2]] = yst[k][0:c] + yst[k][c:n2]
    trans = [jnp.where(eye, f[k]["wtot"], 0.0) + _dot_tn(f[k]["bh"], x[k]) for k in un]
    for k in un:
        h_ref[k // 2, k % 2] = (_dot(trans[k], h[k]) + _dot_tn(f[k]["bh"], u[k])
                                + _dot_tn(f[k]["kh"], f[k]["vm"]))


def _rw_scan(r, v, kk, lw, kd, bb, dims):
    b, s, c, w = dims.B, dims.S, dims.C, dims.W
    ch = RW_CHUNK
    n_c, n_s = c // ch, s // ch
    ctx0 = dims.BS // ch
    pp = RW_PAIRS_PER_STEP
    bw = pp * _LANES
    assert w % bw == 0

    def cf(bi, t):
        return jnp.where(t < n_c, ctx0 + bi * n_c + t, bi * n_s + (t - n_c))

    def cb(bi, t):
        return jnp.where(t < n_c, ctx0 + bi * n_c + (n_c - 1 - t), bi * n_s + (n_s - 1 - (t - n_c)))

    sh = lambda fn: pl.BlockSpec((ch, bw), lambda bi, p, t: (fn(bi, t), p))
    dr = lambda fn, d: pl.BlockSpec((None, ch, bw), lambda bi, p, t: (d, fn(bi, t), p))
    m = dims.M
    return pl.pallas_call(
        _rw_chunk_kernel,
        grid=(b, w // bw, n_c + n_s),
        in_specs=[sh(cf), sh(cf), sh(cf), dr(cf, 0), dr(cf, 0), dr(cf, 0),
                  sh(cb), sh(cb), sh(cb), dr(cb, 1), dr(cb, 1), dr(cb, 1)],
        out_specs=[sh(cf), sh(cb)],
        out_shape=[jax.ShapeDtypeStruct((m, w), F32)] * 2,
        scratch_shapes=[pltpu.VMEM((pp, 2, _LANES, _LANES), F32)],
        compiler_params=_cp("parallel", "parallel", "arbitrary"),
    )(r, v, kk, lw, kd, bb, r, v, kk, lw, kd, bb)


def _rw_readout_kernel(yf_ref, yb_ref, r_ref, v_ref, g_ref, kd_ref, rk_ref, gg_ref, gb_ref, bd_ref, o_ref):
    bd = bd_ref[...]
    inv = 1.0 / RW_HEAD_DIM
    y = yf_ref[...] + yb_ref[...]
    yc = y - _dot01_right(y, bd) * inv
    var = _dot01_right(yc * yc, bd) * inv
    yn = yc * lax.rsqrt(var + RW_GN_EPS)
    bonus = _dot01_right(r_ref[...] * (kd_ref[0] + kd_ref[1]) * rk_ref[...], bd) * v_ref[...]
    o_ref[...] = ((yn * gg_ref[...] + gb_ref[...] + bonus) * g_ref[...]).astype(o_ref.dtype)


def _rw_readout(yf, yb, r, v, g, kd, pw, l, dims, rows):
    w = dims.W
    tm = 256
    blk = pl.BlockSpec((tm, w), lambda i: (i, 0))
    lyr = pl.BlockSpec((None, 1, w), lambda i: (l, 0, 0))
    return pl.pallas_call(
        _rw_readout_kernel,
        grid=(rows // tm,),
        in_specs=[blk, blk, blk, blk, blk, pl.BlockSpec((2, tm, w), lambda i: (0, i, 0)), lyr, lyr, lyr,
                  pl.BlockSpec((w, w), lambda i: (0, 0))],
        out_specs=blk,
        out_shape=jax.ShapeDtypeStruct((rows, w), BF16),
        compiler_params=_cp("parallel"),
    )(yf, yb, r, v, g, kd, pw["rw_rk"], pw["rw_gn_g"], pw["rw_gn_b"], _head_blockdiag(w, RW_HEAD_DIM))


def _s5_matrices(a_re, a_im, log_dt, b_re, b_im, c_re, c_im, d_skip):
    hi = lax.Precision.HIGHEST
    n_g, n_p, n_j = b_re.shape[1], b_re.shape[2], b_re.shape[3]
    ct = S5_CHUNK
    dt = jnp.exp(log_dt)[..., None]
    mag = jnp.exp(a_re * dt)
    lam_re, lam_im = mag * jnp.cos(a_im * dt), mag * jnp.sin(a_im * dt)
    den = a_re * a_re + a_im * a_im
    f_re = ((lam_re - 1.0) * a_re + lam_im * a_im) / den
    f_im = (lam_im * a_re - (lam_re - 1.0) * a_im) / den
    bb_re = f_re[..., None] * b_re - f_im[..., None] * b_im
    bb_im = f_re[..., None] * b_im + f_im[..., None] * b_re
    p_re, p_im = [jnp.ones_like(lam_re)], [jnp.zeros_like(lam_im)]
    for _ in range(ct):
        p_re.append(p_re[-1] * lam_re - p_im[-1] * lam_im)
        p_im.append(p_re[-2] * lam_im + p_im[-1] * lam_re)
    p_re, p_im = jnp.stack(p_re), jnp.stack(p_im)
    cl_re = c_re[None] * p_re[:, :, :, None, :] - c_im[None] * p_im[:, :, :, None, :]
    cl_im = c_re[None] * p_im[:, :, :, None, :] + c_im[None] * p_re[:, :, :, None, :]
    kern = jnp.sum(cl_re[..., None] * bb_re[None, :, :, None] - cl_im[..., None] * bb_im[None, :, :, None],
                   axis=4)
    j_idx = np.arange(ct)[:, None]
    t_idx = np.arange(ct)[None, :]
    lag_f, lag_b = t_idx - j_idx, j_idx - t_idx
    sel_f = jnp.asarray(lag_f[:, :, None] == np.arange(ct + 1), F32)
    sel_b = jnp.asarray(lag_b[:, :, None] == np.arange(ct + 1), F32)
    toep = (jnp.einsum('jtn,ngim->gjmti', sel_f, kern[:, 0], precision=hi)
            + jnp.einsum('jtn,ngim->gjmti', sel_b, kern[:, 1], precision=hi))
    skip = (jnp.eye(ct)[None, :, None, :, None] * jnp.eye(n_j)[None, None, :, None, :]
            * d_skip.reshape(n_g, 1, 1, 1, n_j))
    toep = (toep + skip).reshape(n_g, ct * n_j, ct * n_j)

    def powers(z, d, first, reverse):
        z = z[first:first + ct, d]
        return z[::-1] if reverse else z

    def state_in(first, reverse, d):
        pr, pi = powers(p_re, d, first, reverse), powers(p_im, d, first, reverse)
        re = pr[..., None] * bb_re[d][None] - pi[..., None] * bb_im[d][None]
        im = pr[..., None] * bb_im[d][None] + pi[..., None] * bb_re[d][None]
        fix = lambda z: z.transpose(1, 0, 3, 2).reshape(n_g, ct * n_j, n_p)
        return fix(re), fix(im)

    def state_out(first, reverse, d):
        fix = lambda z: z.transpose(1, 3, 0, 2).reshape(n_g, n_p, ct * n_j)
        return fix(powers(cl_re, d, first, reverse)), fix(-powers(cl_im, d, first, reverse))

    rf_re, rf_im = state_in(0, True, 0)
    rb_re, rb_im = state_in(0, False, 1)
    of_re, of_im = state_out(1, False, 0)
    ob_re, ob_im = state_out(1, True, 1)
    half = (np.arange(n_g) % 2)[:, None, None]

    def lane_half(z):
        zero = jnp.zeros_like(z)
        return jnp.concatenate([jnp.where(half == 0, z, zero), jnp.where(half == 1, z, zero)], axis=-1)

    def row_half(z):
        zero = jnp.zeros_like(z)
        return jnp.concatenate([jnp.where(half == 0, z, zero), jnp.where(half == 1, z, zero)], axis=1)

    r_re = jnp.stack([lane_half(rf_re), lane_half(rb_re)]).astype(BF16)
    r_im = jnp.stack([lane_half(rf_im), lane_half(rb_im)]).astype(BF16)
    o_re = jnp.stack([row_half(of_re), row_half(ob_re)]).astype(BF16)
    o_im = jnp.stack([row_half(of_im), row_half(ob_im)]).astype(BF16)
    lam_c = (p_re[ct].reshape(2, 1, n_g * n_p), p_im[ct].reshape(2, 1, n_g * n_p))
    return toep.astype(BF16), r_re, r_im, o_re, o_im, lam_c


def _s5_z_kernel(u_ref, rre_ref, rim_ref, zre_o, zim_o):
    u0, u1 = u_ref[0].astype(BF16), u_ref[1].astype(BF16)
    for d in range(2):
        zre_o[d] = (jnp.dot(u0, rre_ref[d, 0], preferred_element_type=F32)
                    + jnp.dot(u1, rre_ref[d, 1], preferred_element_type=F32))
        zim_o[d] = (jnp.dot(u0, rim_ref[d, 0], preferred_element_type=F32)
                    + jnp.dot(u1, rim_ref[d, 1], preferred_element_type=F32))


def _s5_scan_kernel(zre_ref, zim_ref, lre_ref, lim_ref, xre_o, xim_o, *, n_b, n_s, n_c):
    lanes = zre_ref.shape[2]
    chains = [(d, b) for d in range(2) for b in range(n_b)]
    lam = [(lre_ref[d], lim_ref[d]) for d in range(2)]

    def run(base_of, length, carry):
        def body(t, carry):
            new = []
            for (d, b), (xr, xi) in zip(chains, carry):
                row = base_of(b) + (t if d == 0 else length - 1 - t)
                xre_o[d, pl.ds(row, 1), :] = xr
                xim_o[d, pl.ds(row, 1), :] = xi
                lr, li = lam[d]
                new.append((lr * xr - li * xi + zre_ref[d, pl.ds(row, 1), :],
                            lr * xi + li * xr + zim_ref[d, pl.ds(row, 1), :]))
            return tuple(new)
        return lax.fori_loop(0, length, body, carry)

    zero = jnp.zeros((1, lanes), F32)
    carry = tuple((zero, zero) for _ in chains)
    carry = run(lambda b: n_b * n_s + b * n_c, n_c, carry)
    run(lambda b: b * n_s, n_s, carry)


def _s5_y_kernel(u_ref, toep_ref, xre_ref, xim_ref, ore_ref, oim_ref, y_o):
    xr = [xre_ref[d].astype(BF16) for d in range(2)]
    xi = [xim_ref[d].astype(BF16) for d in range(2)]
    for g in range(2):
        y = jnp.dot(u_ref[g].astype(BF16), toep_ref[g], preferred_element_type=F32)
        for d in range(2):
            y = y + jnp.dot(xr[d], ore_ref[d, g], preferred_element_type=F32)
            y = y + jnp.dot(xi[d], oim_ref[d, g], preferred_element_type=F32)
        y_o[g] = y.astype(y_o.dtype)


def _s5_glu_kernel(y_ref, w_ref, b_ref, o_ref, wbf_ref):
    @pl.when(pl.program_id(0) == 0)
    def _():
        wbf_ref[...] = w_ref[...].astype(BF16)

    y = y_ref[...].astype(F32)
    y = 0.5 * y * (1.0 + jnp.tanh(0.7978845608028654 * (y + 0.044715 * y * y * y)))
    z = jnp.dot(y.astype(BF16), wbf_ref[...], preferred_element_type=F32) + b_ref[...]
    o_ref[...] = (y * _sigmoid(z)).astype(o_ref.dtype)


def _s5_mixer(f_s5, mats, w_glu, b_glu3, l, dims, rows):
    toep, r_re, r_im, o_re, o_im, (lam_re, lam_im) = mats
    m, w = f_s5.shape
    ct = S5_CHUNK
    n_g = w // S5_GROUP
    n_ch = m // ct
    gw = ct * S5_GROUP
    u = f_s5.reshape(n_ch, ct, n_g, S5_GROUP).transpose(2, 0, 1, 3).reshape(n_g, n_ch, gw)
    sl = n_g * S5_STATE
    sd = jax.ShapeDtypeStruct
    zre, zim = pl.pallas_call(
        _s5_z_kernel,
        grid=(n_g // 2,),
        in_specs=[pl.BlockSpec((2, n_ch, gw), lambda g: (g, 0, 0)),
                  pl.BlockSpec((None, 2, 2, gw, _LANES), lambda g: (l, 0, g, 0, 0)),
                  pl.BlockSpec((None, 2, 2, gw, _LANES), lambda g: (l, 0, g, 0, 0))],
        out_specs=[pl.BlockSpec((2, n_ch, _LANES), lambda g: (0, 0, g))] * 2,
        out_shape=[sd((2, n_ch, sl), F32)] * 2,
        compiler_params=_cp("parallel"),
    )(u, r_re, r_im)
    lb = 512
    xre, xim = pl.pallas_call(
        functools.partial(_s5_scan_kernel, n_b=dims.B, n_s=dims.S // ct, n_c=dims.C // ct),
        grid=(sl // lb,),
        in_specs=[pl.BlockSpec((2, n_ch, lb), lambda j: (0, 0, j))] * 2
                 + [pl.BlockSpec((None, 2, 1, lb), lambda j: (l, 0, 0, j))] * 2,
        out_specs=[pl.BlockSpec((2, n_ch, lb), lambda j: (0, 0, j))] * 2,
        out_shape=[sd((2, n_ch, sl), F32)] * 2,
        compiler_params=_cp("parallel"),
    )(zre, zim, lam_re, lam_im)
    y = pl.pallas_call(
        _s5_y_kernel,
        grid=(n_g // 2,),
        in_specs=[pl.BlockSpec((2, n_ch, gw), lambda g: (g, 0, 0)),
                  pl.BlockSpec((None, 2, gw, gw), lambda g: (l, g, 0, 0)),
                  pl.BlockSpec((2, n_ch, _LANES), lambda g: (0, 0, g)),
                  pl.BlockSpec((2, n_ch, _LANES), lambda g: (0, 0, g)),
                  pl.BlockSpec((None, 2, 2, _LANES, gw), lambda g: (l, 0, g, 0, 0)),
                  pl.BlockSpec((None, 2, 2, _LANES, gw), lambda g: (l, 0, g, 0, 0))],
        out_specs=pl.BlockSpec((2, n_ch, gw), lambda g: (g, 0, 0)),
        out_shape=sd((n_g, n_ch, gw), BF16),
        compiler_params=_cp("parallel"),
    )(u, toep, xre, xim, o_re, o_im)
    y = y.reshape(n_g, n_ch, ct, S5_GROUP).transpose(1, 2, 0, 3).reshape(m, w)
    tm = 256
    return pl.pallas_call(
        _s5_glu_kernel,
        grid=(rows // tm,),
        in_specs=[pl.BlockSpec((tm, w), lambda i: (i, 0)),
                  pl.BlockSpec((None, w, w), lambda i: (l, 0, 0)),
                  pl.BlockSpec((None, 1, w), lambda i: (l, 0, 0))],
        out_specs=pl.BlockSpec((tm, w), lambda i: (i, 0)),
        out_shape=sd((rows, w), BF16),
        scratch_shapes=[pltpu.VMEM((w, w), BF16)],
        compiler_params=_cp("arbitrary"),
    )(y, w_glu, b_glu3)


def _merge_kernel(xn_ref, ya_ref, yb_ref, yc_ref, yd_ref, wg0, wg1, wg2, wg3, bg0, bg1, bg2, bg3, wb_ref,
                  o_ref, wg_bf, wb_bf):
    wgs = (wg0, wg1, wg2, wg3)

    @pl.when(pl.program_id(1) == 0)
    def _():
        for n in range(4):
            wg_bf[n] = wgs[n][...].astype(BF16)
            wb_bf[n] = wb_ref[n].astype(BF16)

    xn = xn_ref[...]
    acc = None
    for n, (y_ref, bg) in enumerate(zip((ya_ref, yb_ref, yc_ref, yd_ref), (bg0, bg1, bg2, bg3))):
        gate = _sigmoid(jnp.dot(xn, wg_bf[n], preferred_element_type=F32) + bg[...])
        up = jnp.dot(y_ref[...], wb_bf[n], preferred_element_type=F32)
        acc = gate * up if acc is None else acc + gate * up
    o_ref[...] = acc.astype(o_ref.dtype)


def _merge(xn, ys, w_gate, b_gate3, w_branch, l, rows):
    d = xn.shape[1]
    w = ys[0].shape[1]
    tn = 256
    tm = _pick(rows, (1088, 1024, 512, 256))
    nb = d // tn
    wg = lambda n: pl.BlockSpec((None, d, tn), lambda j, i: (l, 0, n * nb + j))
    bg = lambda n: pl.BlockSpec((None, 1, tn), lambda j, i: (l, 0, n * nb + j))
    yspec = pl.BlockSpec((tm, w), lambda j, i: (i, 0))
    return pl.pallas_call(
        _merge_kernel,
        grid=(nb, rows // tm),
        in_specs=[pl.BlockSpec((tm, d), lambda j, i: (i, 0)), yspec, yspec, yspec, yspec,
                  wg(0), wg(1), wg(2), wg(3), bg(0), bg(1), bg(2), bg(3),
                  pl.BlockSpec((None, 4, w, tn), lambda j, i: (l, 0, 0, j))],
        out_specs=pl.BlockSpec((tm, tn), lambda j, i: (i, j)),
        out_shape=jax.ShapeDtypeStruct((rows, d), BF16),
        scratch_shapes=[pltpu.VMEM((4, d, tn), BF16), pltpu.VMEM((4, w, tn), BF16)],
        compiler_params=_cp("arbitrary", "arbitrary"),
    )(xn, *ys, w_gate, w_gate, w_gate, w_gate, b_gate3, b_gate3, b_gate3, b_gate3, w_branch)


def _router_kernel(x_ref, g_ref, sc_ref, sh_ref, wr_ref, br_ref, idx_o, gate_o, rank_o, cnt_o, hp_o, carry):
    tm = x_ref.shape[0]
    x = x_ref[...]
    h = x * lax.rsqrt(jnp.mean(x * x, axis=-1, keepdims=True) + NORM_EPS) * g_ref[...]
    h = h * (1.0 + sc_ref[...]) + sh_ref[...]
    hp_o[...] = _pack_pairs(h)

    @pl.when(pl.program_id(0) == 0)
    def _():
        carry[...] = jnp.zeros_like(carry)

    h_hi, h_lo = _split(h, 2)
    w_hi, w_lo = _split(wr_ref[...], 2)
    logits = (jnp.dot(h_hi, w_hi, preferred_element_type=F32) + jnp.dot(h_lo, w_hi, preferred_element_type=F32)
              + jnp.dot(h_hi, w_lo, preferred_element_type=F32)) + br_ref[...]
    lane = lax.broadcasted_iota(I32, (tm, _LANES), 1)
    vals, idxs, hots = [], [], []
    cur = logits
    for _ in range(TOP_K):
        mx = jnp.max(cur, axis=-1, keepdims=True)
        ix = jnp.min(jnp.where(cur == mx, lane, _LANES), axis=-1, keepdims=True)
        hot = lane == ix
        vals.append(mx)
        idxs.append(ix)
        hots.append(hot)
        cur = jnp.where(hot, -jnp.inf, cur)
    ex = [jnp.exp(v - vals[0]) for v in vals]
    inv = 1.0 / sum(ex)
    sel = sum(jnp.where(h, 1.0, 0.0) for h in hots)
    ri = lax.broadcasted_iota(I32, (tm, tm), 0)
    ci = lax.broadcasted_iota(I32, (tm, tm), 1)
    earlier = jnp.where(ci < ri, 1.0, 0.0).astype(BF16)
    before = jnp.dot(earlier, sel.astype(BF16), preferred_element_type=F32) + carry[...]
    idx_out = jnp.zeros((tm, _LANES), I32)
    gate_out = jnp.zeros((tm, _LANES), F32)
    rank_out = jnp.zeros((tm, _LANES), I32)
    for k in range(TOP_K):
        rk = jnp.sum(jnp.where(hots[k], before, 0.0), axis=-1, keepdims=True)
        idx_out = jnp.where(lane == k, idxs[k], idx_out)
        gate_out = jnp.where(lane == k, ex[k] * inv, gate_out)
        rank_out = jnp.where(lane == k, rk.astype(I32), rank_out)
    idx_o[...] = idx_out
    gate_o[...] = gate_out
    rank_o[...] = rank_out
    carry[...] = carry[...] + jnp.sum(sel, axis=0, keepdims=True)
    cnt_o[...] = jnp.broadcast_to(carry[...], cnt_o.shape)


def _router(x, gain3, mod, wr_pad, br_pad, l, dims, rows):
    d = x.shape[1]
    tm = 256
    seg = dims.seg_fn(tm)
    blk = pl.BlockSpec((tm, _LANES), lambda i: (i, 0))
    sd = jax.ShapeDtypeStruct
    return pl.pallas_call(
        _router_kernel,
        grid=(rows // tm,),
        in_specs=[pl.BlockSpec((tm, d), lambda i: (i, 0)),
                  pl.BlockSpec((None, 1, d), lambda i: (l, 0, 0)),
                  pl.BlockSpec((None, 1, d), lambda i: (seg(i), 0, 4)),
                  pl.BlockSpec((None, 1, d), lambda i: (seg(i), 0, 3)),
                  pl.BlockSpec((None, d, _LANES), lambda i: (l, 0, 0)),
                  pl.BlockSpec((None, 1, _LANES), lambda i: (l, 0, 0))],
        out_specs=[blk, blk, blk, pl.BlockSpec((8, _LANES), lambda i: (0, 0)),
                   pl.BlockSpec((tm, d // 2), lambda i: (i, 0))],
        out_shape=[sd((rows, _LANES), I32), sd((rows, _LANES), F32), sd((rows, _LANES), I32),
                   sd((8, _LANES), F32), sd((rows, d // 2), jnp.uint32)],
        scratch_shapes=[pltpu.VMEM((1, _LANES), F32)],
        compiler_params=_cp("arbitrary"),
    )(x, gain3, mod, mod, wr_pad, br_pad)


def _row_copy(src_hbm, row, dst, slot, sem):
    return pltpu.make_async_copy(src_hbm.at[pl.ds(row, 1), :], dst.at[pl.ds(slot, 1), :], sem)


def _issue_rows(idx_ref, n, src_hbm, dst, sem):
    for r in range(n):
        _row_copy(src_hbm, idx_ref[0, r], dst, r, sem).start()


def _expert_kernel(be_ref, first_ref, used_ref, tok_ref, tok_next_ref, h_hbm, wgu_ref, bgu_ref, wdn_ref,
                   bdn_ref, o_ref, xbuf0, xbuf1, sems, wgu_bf, wdn_bf):
    i = pl.program_id(0)
    n_used = used_ref[0]
    blk = xbuf0.shape[0]
    bufs = (xbuf0, xbuf1)

    @pl.when(i == 0)
    def _():
        _issue_rows(tok_ref, blk, h_hbm, xbuf0, sems.at[0])

    @pl.when(first_ref[i] == 1)
    def _():
        wgu_bf[...] = wgu_ref[...].astype(BF16)
        wdn_bf[...] = wdn_ref[...].astype(BF16)

    for par in range(2):
        cur, nxt = bufs[par], bufs[1 - par]
        mine = i % 2 == par

        @pl.when(mine & (i <= n_used))
        def _():
            pltpu.make_async_copy(h_hbm.at[pl.ds(0, blk), :], cur, sems.at[par]).wait()

        @pl.when(mine & (i < n_used))
        def _():
            _issue_rows(tok_next_ref, blk, h_hbm, nxt, sems.at[1 - par])
            x = jnp.concatenate(_unpack_pairs(cur[...]), axis=1).astype(BF16)
            gu = jnp.dot(x, wgu_bf[...], preferred_element_type=F32) + bgu_ref[...]
            g_lin = jnp.minimum(gu[:, :D_EXPERT], SWIGLU_LIMIT)
            u_lin = jnp.clip(gu[:, D_EXPERT:], -SWIGLU_LIMIT, SWIGLU_LIMIT)
            act = (u_lin + 1.0) * g_lin * _sigmoid(SWIGLU_ALPHA * g_lin)
            y = jnp.dot(act.astype(BF16), wdn_bf[...], preferred_element_type=F32) + bdn_ref[...]
            o_ref[...] = _pack_pairs(y)

    @pl.when(i >= n_used)
    def _():
        o_ref[...] = jnp.zeros_like(o_ref)


def _moe_experts(h, slot_tok3, block_e, first, n_used, w_gu, b_gu4, w_dn, b_dn4, l):
    n_blocks = slot_tok3.shape[0] - 1
    blk = slot_tok3.shape[2]
    dp = h.shape[1]
    d = 2 * dp
    de2 = w_gu.shape[3]
    grid_spec = pltpu.PrefetchScalarGridSpec(
        num_scalar_prefetch=3,
        grid=(n_blocks,),
        in_specs=[pl.BlockSpec((None, 1, blk), lambda i, be, fi, us: (i, 0, 0), memory_space=pltpu.SMEM),
                  pl.BlockSpec((None, 1, blk), lambda i, be, fi, us: (i + 1, 0, 0), memory_space=pltpu.SMEM),
                  pl.BlockSpec(memory_space=pl.ANY),
                  pl.BlockSpec((None, None, d, de2), lambda i, be, fi, us: (l, be[i], 0, 0)),
                  pl.BlockSpec((None, None, 1, de2), lambda i, be, fi, us: (l, be[i], 0, 0)),
                  pl.BlockSpec((None, None, de2 // 2, d), lambda i, be, fi, us: (l, be[i], 0, 0)),
                  pl.BlockSpec((None, None, 1, d), lambda i, be, fi, us: (l, be[i], 0, 0))],
        out_specs=pl.BlockSpec((blk, dp), lambda i, be, fi, us: (i, 0)),
        scratch_shapes=[pltpu.VMEM((blk, dp), jnp.uint32), pltpu.VMEM((blk, dp), jnp.uint32),
                        pltpu.SemaphoreType.DMA((2,)),
                        pltpu.VMEM((d, de2), BF16), pltpu.VMEM((de2 // 2, d), BF16)])
    return pl.pallas_call(
        _expert_kernel,
        grid_spec=grid_spec,
        out_shape=jax.ShapeDtypeStruct((n_blocks * blk, dp), jnp.uint32),
        compiler_params=_cp("arbitrary"),
    )(block_e, first, n_used, slot_tok3, slot_tok3, h, w_gu, b_gu4, w_dn, b_dn4)


def _combine_kernel(dest_ref, dest_next_ref, gate_ref, x_ref, g2_ref, ys_hbm, o_ref, buf, sems):
    tm = x_ref.shape[0]
    i = pl.program_id(0)
    slot = i % 2

    def issue(idx_ref, s):
        for r in range(tm):
            for k in range(TOP_K):
                _row_copy(ys_hbm, idx_ref[0, r * TOP_K + k], buf.at[s, k], r, sems.at[s]).start()

    @pl.when(i == 0)
    def _():
        issue(dest_ref, 0)

    @pl.when(i + 1 < pl.num_programs(0))
    def _():
        issue(dest_next_ref, 1 - slot)

    for k in range(TOP_K):
        pltpu.make_async_copy(ys_hbm.at[pl.ds(0, tm), :], buf.at[slot, k], sems.at[slot]).wait()
    gate = gate_ref[...]
    n = buf.shape[3]
    f_lo = f_hi = None
    for k in range(TOP_K):
        lo, hi = _unpack_pairs(buf[slot, k])
        gk = gate[:, k:k + 1]
        f_lo = gk * lo if f_lo is None else f_lo + gk * lo
        f_hi = gk * hi if f_hi is None else f_hi + gk * hi
    o_ref[:, :n] = x_ref[:, :n] + g2_ref[:, :n] * f_lo
    o_ref[:, n:] = x_ref[:, n:] + g2_ref[:, n:] * f_hi


def _moe_combine(x, ys, dest3, gate, mod, dims, rows):
    d = x.shape[1]
    tm = MOE_COMBINE_ROWS
    seg = dims.seg_fn(tm)
    return pl.pallas_call(
        _combine_kernel,
        grid=(rows // tm,),
        in_specs=[pl.BlockSpec((None, 1, tm * TOP_K), lambda i: (i, 0, 0), memory_space=pltpu.SMEM),
                  pl.BlockSpec((None, 1, tm * TOP_K), lambda i: (i + 1, 0, 0), memory_space=pltpu.SMEM),
                  pl.BlockSpec((tm, _LANES), lambda i: (i, 0)),
                  pl.BlockSpec((tm, d), lambda i: (i, 0)),
                  pl.BlockSpec((None, 1, d), lambda i: (seg(i), 0, 5)),
                  pl.BlockSpec(memory_space=pl.ANY)],
        out_specs=pl.BlockSpec((tm, d), lambda i: (i, 0)),
        out_shape=jax.ShapeDtypeStruct((rows, d), F32),
        scratch_shapes=[pltpu.VMEM((2, TOP_K, tm, d // 2), jnp.uint32), pltpu.SemaphoreType.DMA((2,))],
        compiler_params=_cp("arbitrary"),
    )(dest3, dest3, gate, x, mod, ys)


def _moe(x, mod, pw, l, dims, rows):
    idx, gate, rank, cnt, hp = _router(x, pw["norm2_g"], mod, pw["moe_wr"], pw["moe_br"], l, dims, rows)
    blk = MOE_BLOCK
    n_blocks = -(-(rows * TOP_K) // blk) + N_EXPERTS + 1
    counts = cnt[0, :N_EXPERTS].astype(I32)
    padded = (counts + blk - 1) // blk * blk
    pad_end = jnp.cumsum(padded)
    pad_start = pad_end - padded
    top_i = idx[:, :TOP_K]
    start_of = jnp.sum(jnp.where(top_i[..., None] == jnp.arange(N_EXPERTS, dtype=I32), pad_start, 0), axis=-1)
    dest = start_of + rank[:, :TOP_K]
    tok = jnp.broadcast_to(jnp.arange(rows, dtype=I32)[:, None], dest.shape)
    fill = jnp.arange((n_blocks + 1) * blk, dtype=I32) % rows
    slot_tok = fill.at[dest.reshape(-1)].set(tok.reshape(-1))
    block_start = jnp.arange(n_blocks, dtype=I32) * blk
    block_e = jnp.minimum(jnp.sum((pad_end[None, :] <= block_start[:, None]).astype(I32), axis=1), N_EXPERTS - 1)
    first = jnp.concatenate([jnp.ones((1,), I32), (block_e[1:] != block_e[:-1]).astype(I32)])
    n_used = (pad_end[-1:] // blk).astype(I32)
    ys = _moe_experts(hp, slot_tok.reshape(n_blocks + 1, 1, blk), block_e, first, n_used, pw["moe_w_gu"],
                      pw["moe_b_gu"], pw["moe_w_dn"], pw["moe_b_dn"], l)
    dest3 = dest.reshape(rows // MOE_COMBINE_ROWS, 1, MOE_COMBINE_ROWS * TOP_K)
    dest3 = jnp.concatenate([dest3, jnp.zeros_like(dest3[:1])], axis=0)
    return _moe_combine(x, ys, dest3, gate, mod, dims, rows)


def _prep_params(p, dims):
    n_l = p["w_in"].shape[0]
    w = dims.W
    o_mla = 3 * w + (3 * w + 384)
    o_s5 = o_mla + MLA_Q_RANK + MLA_KV_RANK + MLA_ROPE
    pw = dict(p)
    row3 = lambda a: a.reshape(a.shape[0], 1, -1)
    pw["norm1_g"], pw["norm2_g"] = row3(p["norm1_g"]), row3(p["norm2_g"])
    pw["b_mod"], pw["b_gate"] = row3(p["b_mod"]), row3(p["b_gate"])
    pw["w_in_mla"] = p["w_in"][:, :, o_mla:o_s5]
    pw["w_in_s5"] = p["w_in"][:, :, o_s5:]
    zeros = jnp.zeros_like(p["rw_w2"][:, 0])
    bdiag = lambda x: jnp.concatenate([jnp.concatenate([x[:, 0], zeros], axis=-1),
                                       jnp.concatenate([zeros, x[:, 1]], axis=-1)], axis=1)
    pw["rw_w2"], pw["rw_a2"] = bdiag(p["rw_w2"]), bdiag(p["rw_a2"])
    pw["rw_w0"], pw["rw_a0"] = row3(p["rw_w0"]), row3(p["rw_a0"])
    pw["rw_kk"], pw["rw_ka"] = row3(p["rw_kk"]), row3(p["rw_ka"])
    pw["rw_rk"], pw["rw_gn_g"], pw["rw_gn_b"] = row3(p["rw_rk"]), row3(p["rw_gn_g"]), row3(p["rw_gn_b"])
    pad_h = lambda x, n: jnp.pad(x, ((0, 0), (0, 0), (0, 0), (0, _LANES - n)))
    wq = p["mla_wq_up"].reshape(n_l, MLA_Q_RANK, MLA_HEADS, MLA_QK)
    pw["mla_wq"] = pad_h(wq, MLA_QK).reshape(n_l, MLA_Q_RANK, -1).astype(BF16)
    wkv = p["mla_wkv_up"].reshape(n_l, MLA_KV_RANK, MLA_HEADS, MLA_NOPE + MLA_V)
    pw["mla_wk"] = pad_h(wkv[..., :MLA_NOPE], MLA_NOPE).reshape(n_l, MLA_KV_RANK, -1).astype(BF16)
    pw["mla_wv"] = wkv[..., MLA_NOPE:].reshape(n_l, MLA_KV_RANK, -1).astype(BF16)
    pw["mla_place"] = jnp.asarray(np.arange(MLA_ROPE)[:, None] + MLA_NOPE == np.arange(_LANES)[None, :], BF16)
    pw["mla_gcq"], pw["mla_gckv"] = row3(p["mla_qn_g"]), row3(p["mla_kvn_g"])
    pad_g = lambda g: jnp.pad(g, ((0, 0), (0, _LANES - MLA_QK))).reshape(n_l, 1, _LANES)
    pw["mla_gq"] = pad_g(p["mla_qkn_q"] * (MLA_QK ** -0.5 * _LOG2E))
    pw["mla_gk"] = pad_g(p["mla_qkn_k"])
    pw["s5_b_glu"] = row3(p["s5_b_glu"])
    pw["moe_wr"] = jnp.pad(p["moe_wr"], ((0, 0), (0, 0), (0, _LANES - N_EXPERTS)))
    pw["moe_br"] = jnp.pad(p["moe_br"], ((0, 0), (0, _LANES - N_EXPERTS)), constant_values=-1e30)[:, None, :]
    pw["moe_b_gu"] = p["moe_b_gu"][:, :, None, :]
    pw["moe_b_dn"] = p["moe_b_dn"][:, :, None, :]
    return pw


def _residual_epilogue(acc, x, g):
    return x + g * acc


def _layer(xs, l, mod, pw, ropes, dims, need_ctx):
    m, d, w = dims.M, dims.D, dims.W
    rows = m if need_ctx else dims.BS
    tm_all = _pick(m, (1088, 1024, 512, 256))
    xn = _norm_mod(xs, pw["norm1_g"], l, mod, 0, 1, dims, m, BF16)
    f_na = _mm(xn, pw["w_in"], l, tm=tm_all, tn=768, col0=0, ncols=3 * w)
    f_rw = _mm(xn, pw["w_in"], l, tm=tm_all, tn=384, col0=3 * w, ncols=3 * w + 384)
    f_mla = _mm(xn, pw["w_in_mla"], l, tm=tm_all, tn=pw["w_in_mla"].shape[2])
    f_s5 = _mm(xn, pw["w_in_s5"], l, tm=tm_all, tn=w, out_dtype=BF16)

    qkv = _na_prep(f_na, pw["na_qn_g"][l], pw["na_kn_g"][l], dims)
    ya = _na_attention(qkv, pw["na_bias"], l, dims, need_ctx)

    r, v, kk, g, lw, kd, bb = _rw_prep(f_rw, pw, l, dims)
    yf, yb_ = _rw_scan(r, v, kk, lw, kd, bb, dims)
    yb = _rw_readout(yf, yb_, r, v, g, kd, pw, l, dims, rows)

    q, k, vv = _mla_prep(f_mla, pw, l, ropes, dims)
    yc = _mla_attention(q, k, vv, dims, need_ctx)

    yd = _s5_mixer(f_s5, pw["s5_mats"], pw["s5_w_glu"], pw["s5_b_glu"], l, dims, rows)

    merged = _merge(xn, (ya, yb, yc, yd), pw["w_gate"], pw["b_gate"], pw["w_branch"], l, rows)
    tm_seg = _pick(dims.S, (512, 256))
    seg = dims.seg_fn(tm_seg)
    tn = 1024
    xs = _mm(merged, pw["w_out"], l, tm=tm_seg, tn=tn, rows=rows, epilogue=_residual_epilogue,
             extras=((xs, pl.BlockSpec((tm_seg, tn), lambda j, i: (i, j))),
                     (mod, pl.BlockSpec((None, 1, tn), lambda j, i: (seg(i), 0, 2 * (d // tn) + j)))))
    return _moe(xs, mod, pw, l, dims, rows)


def kernel(x, c, ctx, c_ctx, w_mod, b_mod, norm1_g, norm2_g, w_in, w_gate, b_gate, w_branch, w_out, na_qn_g, na_kn_g, na_rpb, rw_mu, rw_w0, rw_w2, rw_a0, rw_a2, rw_g2, rw_kk, rw_ka, rw_rk, rw_gn_g, rw_gn_b, mla_qn_g, mla_wq_up, mla_kvn_g, mla_wkv_up, mla_qkn_q, mla_qkn_k, s5_a_re, s5_a_im, s5_log_dt, s5_b_re, s5_b_im, s5_c_re, s5_c_im, s5_d, s5_w_glu, s5_b_glu, moe_wr, moe_br, moe_w_gu, moe_b_gu, moe_w_dn, moe_b_dn):
    params = dict(w_mod=w_mod, b_mod=b_mod, norm1_g=norm1_g, norm2_g=norm2_g, w_in=w_in, w_gate=w_gate,
                  b_gate=b_gate, w_branch=w_branch, w_out=w_out, na_qn_g=na_qn_g, na_kn_g=na_kn_g,
                  na_rpb=na_rpb, rw_mu=rw_mu, rw_w0=rw_w0, rw_w2=rw_w2, rw_a0=rw_a0, rw_a2=rw_a2,
                  rw_g2=rw_g2, rw_kk=rw_kk, rw_ka=rw_ka, rw_rk=rw_rk, rw_gn_g=rw_gn_g, rw_gn_b=rw_gn_b,
                  mla_qn_g=mla_qn_g, mla_wq_up=mla_wq_up, mla_kvn_g=mla_kvn_g, mla_wkv_up=mla_wkv_up,
                  mla_qkn_q=mla_qkn_q, mla_qkn_k=mla_qkn_k, s5_a_re=s5_a_re, s5_a_im=s5_a_im,
                  s5_log_dt=s5_log_dt, s5_b_re=s5_b_re, s5_b_im=s5_b_im, s5_c_re=s5_c_re, s5_c_im=s5_c_im,
                  s5_d=s5_d, s5_w_glu=s5_w_glu, s5_b_glu=s5_b_glu, moe_wr=moe_wr, moe_br=moe_br,
                  moe_w_gu=moe_w_gu, moe_b_gu=moe_b_gu, moe_w_dn=moe_w_dn, moe_b_dn=moe_b_dn)
    b, s, d = x.shape
    c_len = ctx.shape[1]
    depth = w_mod.shape[0]
    dims = _Dims(b, s, c_len, d)
    assert b + 1 <= 8
    pw = _prep_params(params, dims)
    pw["na_bias"] = jax.vmap(lambda rpb: _na_bias_tables(rpb, s // GRID_W))(na_rpb)
    pw["s5_mats"] = jax.vmap(_s5_matrices)(s5_a_re, s5_a_im, s5_log_dt, s5_b_re, s5_b_im, s5_c_re, s5_c_im, s5_d)
    cond = jnp.concatenate([jax.nn.silu(c), jax.nn.silu(c_ctx)[None],
                            jnp.zeros((8 - b - 1, d), F32)], axis=0)
    ropes = _rope_tables(s, 256)
    xs = jnp.concatenate([x.reshape(b * s, d), ctx.reshape(b * c_len, d)], axis=0)
    for l in range(depth):
        mod = _mm(cond, pw["w_mod"], l, tm=8, tn=1024, bias=pw["b_mod"]).reshape(8, 1, 6 * d)
        xs = _layer(xs, l, mod, pw, ropes, dims, need_ctx=l < depth - 1)
    return xs[:b * s].reshape(b, s, d)
```
